```python
import math
import jax
import jax.numpy as jnp
from jax import lax
import numpy as np

D_MODEL = 1024
BATCH = 4
SEQ = 4096
DEPTH = 2
DEC_BATCH = 32
DEC_SEQ = 8
PAST_LEN = 8192
PAGE_SIZE = 128

N_EVEN = (DEPTH + 1) // 2
N_ODD = DEPTH // 2
MIX_W = D_MODEL // 2

H_A = 8
HD_A = MIX_W // H_A
MOBA_BLOCK = 256
MOBA_TOPK = 3
MOBA_Q_BLOCK = 4
H_B = 4
DK_B = MIX_W // H_B
DV_B = MIX_W // H_B
H_C = 4
DK_C = MIX_W // H_C
DV_C = MIX_W // H_C
CONV_W = 4
GDN_CONV_DIM = 2 * H_C * DK_C + H_C * DV_C
H_D = 8
HD_D = MIX_W // H_D
SB_Q_BLOCK = 128

CHUNK = 64
EVEN_IN = 3 * H_A * HD_A + 2 * H_B * DK_B + 2 * H_B * DV_B
EVEN_MIX = H_A * HD_A + H_B * DV_B
ODD_IN = GDN_CONV_DIM + 2 * H_C + H_C * DV_C + 3 * H_D * HD_D
ODD_MIX = H_C * DV_C + H_D * HD_D

D_FF = 3584
N_EXPERTS = 8
TOP_K = 2
D_FF_E = 3584

DEEPNORM_ALPHA = (2 * DEPTH) ** 0.25
DEEPNORM_BETA = (8 * DEPTH) ** -0.25
LN_EPS = 1e-5
RMS_EPS = 1e-6
F32 = jnp.float32

kernel_name = 'hybrid_moba_hgrn2_gdn_stickbreak_step'


def _split(h, sizes):
    out, start = [], 0
    for s in sizes:
        out.append(h[..., start:start + s])
        start += s
    return out


def _layernorm(x, g, b):
    xf = x.astype(F32)
    xc = xf - jnp.mean(xf, -1, keepdims=True)
    var = jnp.mean(xc * xc, -1, keepdims=True)
    return (xc * lax.rsqrt(var + LN_EPS) * g.astype(F32) + b.astype(F32)).astype(x.dtype)


def _rmsnorm(x, g):
    xf = x.astype(F32)
    return xf * lax.rsqrt(jnp.mean(xf * xf, -1, keepdims=True) + RMS_EPS) * g.astype(F32)


def _l2norm(x):
    xf = x.astype(F32)
    return xf * lax.rsqrt(jnp.sum(xf * xf, -1, keepdims=True) + RMS_EPS)


def _alibi_slopes(n_heads):
    return jnp.exp2(-8.0 * jnp.arange(1, n_heads + 1, dtype=F32) / n_heads)


def _gather_pages(pool, page_table):
    pages = pool[page_table]
    b, n, p = pages.shape[:3]
    return pages.reshape((b, n * p) + pages.shape[3:])


def _map_query_blocks(fn, q, q_pos, block):
    b, l = q.shape[:2]
    qb = min(block, l)
    n = -(-l // qb)
    pad = n * qb - l
    qp = jnp.pad(q, [(0, 0), (0, pad)] + [(0, 0)] * (q.ndim - 2))
    pp = jnp.pad(q_pos, (0, pad), mode='edge')
    qs = jnp.moveaxis(qp.reshape((b, n, qb) + q.shape[2:]), 1, 0)
    out = jnp.moveaxis(lax.map(fn, (qs, pp.reshape(n, qb))), 0, 1)
    return out.reshape((b, n * qb) + out.shape[3:])[:, :l]


def _chunked(a, c):
    b, l = a.shape[:2]
    n = -(-l // c)
    a = jnp.pad(a.astype(F32), [(0, 0), (0, n * c - l)] + [(0, 0)] * (a.ndim - 2))
    return jnp.moveaxis(a.reshape((b, n, c) + a.shape[2:]), 1, 0)


def _unchunk(o, l):
    o = jnp.moveaxis(o, 0, 1)
    return o.reshape((o.shape[0], o.shape[1] * o.shape[2]) + o.shape[3:])[:, :l]


def _moba_attention(q, k, v, q_pos):
    b, lk, h, dh = k.shape
    nb = -(-lk // MOBA_BLOCK)
    pad = [(0, 0), (0, nb * MOBA_BLOCK - lk), (0, 0), (0, 0)]
    kb = jnp.pad(k, pad).reshape(b, nb, MOBA_BLOCK, h, dh)
    vb = jnp.pad(v, pad).reshape(b, nb, MOBA_BLOCK, h, dh)
    k_mean = jnp.mean(kb.astype(F32), axis=2)
    n_sel = min(MOBA_TOPK, nb)
    slopes = _alibi_slopes(h)[:, None, None, None]
    scale = dh ** -0.5
    b_ix = jnp.arange(b)[:, None, None, None]
    h_ix = jnp.arange(h)[None, :, None, None]
    offs = jnp.arange(MOBA_BLOCK)
    blk_ids = jnp.arange(nb)

    def attend(args):
        qc, pc = args
        nq = pc.shape[0]
        own = pc // MOBA_BLOCK
        gate = jnp.einsum('bqhd,bnhd->bhqn', qc.astype(F32), k_mean)
        gate = jnp.where(blk_ids[None, :] < own[:, None], gate, -jnp.inf)
        _, top = lax.top_k(gate, n_sel)
        own_b = jnp.broadcast_to(own[None, None, :, None], (b, h, nq, 1))
        blocks = jnp.concatenate([top, own_b.astype(top.dtype)], axis=-1)
        keep = jnp.concatenate([top < own[:, None], jnp.ones((b, h, nq, 1), bool)], axis=-1)
        ks = kb[b_ix, blocks, :, h_ix]
        vs = vb[b_ix, blocks, :, h_ix]
        dist = pc[:, None, None] - (blocks[..., None] * MOBA_BLOCK + offs)
        s = jnp.einsum('bqhd,bhqnkd->bhqnk', qc, ks).astype(F32) * scale - slopes * dist.astype(F32)
        s = jnp.where(keep[..., None] & (dist >= 0), s, -jnp.inf)
        p = jax.nn.softmax(s.reshape(b, h, nq, -1), axis=-1).reshape(s.shape)
        return jnp.einsum('bhqnk,bhqnkd->bqhd', p.astype(vs.dtype), vs)

    return _map_query_blocks(attend, q, q_pos, MOBA_Q_BLOCK)


def _stick_breaking(q, k, v, q_pos):
    lk, dh = k.shape[1], k.shape[-1]
    kpos = jnp.arange(lk)
    scale = dh ** -0.5

    def attend(args):
        qc, pc = args
        z = jnp.einsum('bqhd,bkhd->bhqk', qc, k).astype(F32) * scale
        before = kpos[None, :] < pc[:, None]
        log_keep = jnp.where(before, jax.nn.log_sigmoid(-z), 0.0)
        later = lax.cumsum(log_keep, axis=3, reverse=True) - log_keep
        w = jnp.where(before, jnp.exp(jax.nn.log_sigmoid(z) + later), 0.0)
        return jnp.einsum('bhqk,bkhd->bqhd', w.astype(v.dtype), v)

    return _map_query_blocks(attend, q, q_pos, SB_Q_BLOCK)


def _hgrn2_recurrence(q, log_f, k, v, s0):
    l = q.shape[1]
    c = min(CHUNK, l)
    incl = jnp.tril(jnp.ones((c, c), bool))[:, :, None]

    def step(S, inp):
        qh, gh, kh, vh = (a.transpose(0, 2, 1, 3) for a in inp)
        cg = jnp.cumsum(gh, axis=2)
        diff = cg[:, :, :, None, :] - cg[:, :, None, :, :]
        decay = jnp.exp(jnp.where(incl, diff, -jnp.inf))
        a = jnp.einsum('bhtd,bhsd,bhtsd->bhts', qh, kh, decay)
        o = (qh * jnp.exp(cg)) @ S + a @ vh
        last = cg[:, :, -1]
        S = S * jnp.exp(last)[..., None] + jnp.einsum('bhsk,bhsv->bhkv', kh * jnp.exp(last[:, :, None] - cg), vh)
        return S, o.transpose(0, 2, 1, 3)

    S, o = lax.scan(step, s0.astype(F32), tuple(_chunked(a, c) for a in (q, log_f, k, v)))
    return _unchunk(o, l), S


def _gated_delta_rule(q, k, v, g, beta, s0):
    l, dv = q.shape[1], v.shape[-1]
    c = min(CHUNK, l)
    incl = jnp.tril(jnp.ones((c, c), bool))
    strict = jnp.tril(jnp.ones((c, c), bool), -1)
    eye = jnp.eye(c, dtype=F32)

    def step(S, inp):
        qc, kc, vc, gc, bc = inp
        qh, kh, vh = (a.transpose(0, 2, 1, 3) for a in (qc, kc, vc))
        cg = jnp.cumsum(gc, axis=1).transpose(0, 2, 1)
        bh = bc.transpose(0, 2, 1)[..., None]
        decay = jnp.exp(jnp.where(incl, cg[..., :, None] - cg[..., None, :], -jnp.inf))
        kk = jnp.einsum('bhtd,bhsd->bhts', kh, kh) * decay * bh
        tri = eye + jnp.where(strict, kk, 0.0)
        rhs = jnp.concatenate([vh * bh, kh * bh * jnp.exp(cg)[..., None]], axis=-1)
        sol = lax.linalg.triangular_solve(tri, rhs, left_side=True, lower=True, unit_diagonal=True)
        u = sol[..., :dv] - sol[..., dv:] @ S
        qk = jnp.einsum('bhtd,bhsd->bhts', qh, kh) * decay
        o = (qh * jnp.exp(cg)[..., None]) @ S + qk @ u
        last = cg[..., -1:]
        S = S * jnp.exp(last)[..., None] + jnp.einsum('bhsk,bhsv->bhkv', kh * jnp.exp(last - cg)[..., None], u)
        return S, o.transpose(0, 2, 1, 3)

    S, o = lax.scan(step, s0.astype(F32), tuple(_chunked(a, c) for a in (q, k, v, g, beta)))
    return _unchunk(o, l), S


def _causal_conv(x, buf, w):
    l = x.shape[1]
    xx = jnp.concatenate([buf.astype(x.dtype), x], axis=1)
    y = xx[:, 0:l] * w[0]
    for i in range(1, CONV_W):
        y = y + xx[:, i:i + l] * w[i]
    return jax.nn.silu(y), xx[:, -(CONV_W - 1):]


def _even_mixer(x, pos, k_past, v_past, s0, w_in, w_out, lb, g_norm):
    b, l, _ = x.shape
    qa, ka, va, qb, fb, ib, gb = _split(x @ w_in, (H_A * HD_A,) * 3 + (H_B * DK_B, H_B * DK_B, H_B * DV_B, H_B * DV_B))
    qa, ka, va = (t.reshape(b, l, H_A, HD_A) for t in (qa, ka, va))
    k_all = ka if k_past is None else jnp.concatenate([k_past.astype(ka.dtype), ka], axis=1)
    v_all = va if v_past is None else jnp.concatenate([v_past.astype(va.dtype), va], axis=1)
    o_a = _moba_attention(qa, k_all, v_all, pos)
    zf = fb.reshape(b, l, H_B, DK_B).astype(F32)
    lb = lb.reshape(H_B, DK_B)
    log_f = jnp.log(lb + (1.0 - lb) * jax.nn.sigmoid(zf))
    k_b = (1.0 - lb) * jax.nn.sigmoid(-zf)
    o_b, s_new = _hgrn2_recurrence(jax.nn.silu(qb.reshape(b, l, H_B, DK_B)), log_f, k_b, ib.reshape(b, l, H_B, DV_B), s0)
    o_b = _rmsnorm(o_b, g_norm) * jax.nn.sigmoid(gb.reshape(b, l, H_B, DV_B).astype(F32))
    mixed = jnp.concatenate([o_a.reshape(b, l, -1), o_b.reshape(b, l, -1).astype(x.dtype)], axis=-1)
    return mixed @ w_out, ka, va, s_new.astype(s0.dtype)


def _odd_mixer(x, pos, k_past, v_past, s0, conv_buf, w_in, w_out, conv_w, a_log, dt_bias, g_norm):
    b, l, _ = x.shape
    qkv_c, a_c, b_c, g_c, q_d, k_d, v_d = _split(x @ w_in, (GDN_CONV_DIM, H_C, H_C, H_C * DV_C) + (H_D * HD_D,) * 3)
    qkv_c, new_buf = _causal_conv(qkv_c, conv_buf, conv_w)
    q_c, k_c, v_c = _split(qkv_c, (H_C * DK_C, H_C * DK_C, H_C * DV_C))
    q_c = _l2norm(q_c.reshape(b, l, H_C, DK_C)) * DK_C ** -0.5
    k_c = _l2norm(k_c.reshape(b, l, H_C, DK_C))
    log_a = -jnp.exp(a_log.astype(F32)) * jax.nn.softplus(a_c.astype(F32) + dt_bias.astype(F32))
    beta = jax.nn.sigmoid(b_c.astype(F32))
    o_c, s_new = _gated_delta_rule(q_c, k_c, v_c.reshape(b, l, H_C, DV_C), log_a, beta, s0)
    o_c = _rmsnorm(o_c, g_norm) * jax.nn.silu(g_c.reshape(b, l, H_C, DV_C).astype(F32))
    q_d, k_d, v_d = (t.reshape(b, l, H_D, HD_D) for t in (q_d, k_d, v_d))
    k_all = k_d if k_past is None else jnp.concatenate([k_past.astype(k_d.dtype), k_d], axis=1)
    v_all = v_d if v_past is None else jnp.concatenate([v_past.astype(v_d.dtype), v_d], axis=1)
    o_d = _stick_breaking(q_d, k_all, v_all, pos)
    mixed = jnp.concatenate([o_c.reshape(b, l, -1).astype(x.dtype), o_d.reshape(b, l, -1)], axis=-1)
    return mixed @ w_out, k_d, v_d, s_new.astype(s0.dtype), new_buf


def _swiglu(x, wg, wu, wd):
    return (jax.nn.silu(x @ wg) * (x @ wu)) @ wd


def _moe(x, router, wg, wu, wd):
    logits = (x @ router).astype(F32)
    top_val, top_idx = lax.top_k(logits, TOP_K)
    gates = jax.nn.softmax(top_val, axis=-1)
    comb = jnp.sum(jax.nn.one_hot(top_idx, N_EXPERTS, dtype=F32) * gates[..., None], axis=-2).astype(x.dtype)
    y = jnp.zeros_like(x)
    for e in range(N_EXPERTS):
        y = y + comb[..., e:e + 1] * _swiglu(x, wg[e], wu[e], wd[e])
    return y


def _layer_stack(x, pos, k_moba_pool, v_moba_pool, s_hgrn, s_gdn, gdn_buf, k_sb_pool, v_sb_pool, page_table, weights):
    (w_in_even, w_out_even, hgrn_lb, hgrn_norm, w_in_odd, w_out_odd, gdn_conv_w, gdn_a_log, gdn_dt_bias,
     gdn_norm, ln1_g, ln1_b, ln2_g, ln2_b, ffn_wg, ffn_wu, ffn_wd, router, moe_wg, moe_wu, moe_wd) = weights
    lower_bounds = jnp.cumsum(jax.nn.softmax(hgrn_lb.astype(F32), axis=0), axis=0)
    ka_l, va_l, sb_l, sc_l, cb_l, kd_l, vd_l = [], [], [], [], [], [], []
    for layer in range(DEPTH):
        i = layer // 2
        if layer % 2 == 0:
            k_past = None if page_table is None else _gather_pages(k_moba_pool[i], page_table)
            v_past = None if page_table is None else _gather_pages(v_moba_pool[i], page_table)
            mix, k_new, v_new, s_new = _even_mixer(x, pos, k_past, v_past, s_hgrn[i], w_in_even[i], w_out_even[i],
                                                   lower_bounds[i], hgrn_norm[i])
            ka_l.append(k_new)
            va_l.append(v_new)
            sb_l.append(s_new)
        else:
            k_past = None if page_table is None else _gather_pages(k_sb_pool[i], page_table)
            v_past = None if page_table is None else _gather_pages(v_sb_pool[i], page_table)
            mix, k_new, v_new, s_new, buf_new = _odd_mixer(x, pos, k_past, v_past, s_gdn[i], gdn_buf[i], w_in_odd[i],
                                                           w_out_odd[i], gdn_conv_w[i], gdn_a_log[i], gdn_dt_bias[i],
                                                           gdn_norm[i])
            kd_l.append(k_new)
            vd_l.append(v_new)
            sc_l.append(s_new)
            cb_l.append(buf_new)
        x = _layernorm(DEEPNORM_ALPHA * x + mix, ln1_g[layer], ln1_b[layer])
        if layer % 2 == 0:
            ffn = _swiglu(x, ffn_wg[i], ffn_wu[i], ffn_wd[i])
        else:
            ffn = _moe(x, router[i], moe_wg[i], moe_wu[i], moe_wd[i])
        x = _layernorm(DEEPNORM_ALPHA * x + ffn, ln2_g[layer], ln2_b[layer])
    return (x, jnp.stack(ka_l), jnp.stack(va_l), jnp.stack(sb_l), jnp.stack(sc_l), jnp.stack(cb_l),
            jnp.stack(kd_l), jnp.stack(vd_l))


def setup_inputs(seed: int = 0) -> dict:
    key = jax.random.key(seed)
    keys = iter(jax.random.split(key, 48))

    def normal(shape, scale):
        return jax.random.normal(next(keys), shape, F32) * scale

    n_pages = PAST_LEN // PAGE_SIZE
    n_used = DEC_BATCH * n_pages
    n_pool = n_used + (n_used + 3) // 4
    page_table = jax.random.permutation(next(keys), n_pool)[:n_used].reshape(DEC_BATCH, n_pages).astype(jnp.int32)
    dt = jnp.exp(jax.random.uniform(next(keys), (N_ODD, H_C), F32, math.log(1e-3), math.log(1e-1)))
    a_log = jnp.log(jax.random.uniform(next(keys), (N_ODD, H_C), F32, 1.0, 16.0))
    return {
        'x_prompt': normal((BATCH, SEQ, D_MODEL), 1.0),
        'x_sample': normal((DEC_BATCH, DEC_SEQ, D_MODEL), 1.0),
        'cache_k_moba': normal((N_EVEN, n_pool, PAGE_SIZE, H_A, HD_A), 1.0),
        'cache_v_moba': normal((N_EVEN, n_pool, PAGE_SIZE, H_A, HD_A), 1.0),
        'state_hgrn': normal((N_EVEN, DEC_BATCH, H_B, DK_B, DV_B), 0.5),
        'state_gdn': normal((N_ODD, DEC_BATCH, H_C, DK_C, DV_C), 0.1),
        'state_gdn_conv': normal((N_ODD, DEC_BATCH, CONV_W - 1, GDN_CONV_DIM), 1.0),
        'cache_k_sb': normal((N_ODD, n_pool, PAGE_SIZE, H_D, HD_D), 1.0),
        'cache_v_sb': normal((N_ODD, n_pool, PAGE_SIZE, H_D, HD_D), 1.0),
        'page_table': page_table,
        'w_in_even': normal((N_EVEN, D_MODEL, EVEN_IN), D_MODEL ** -0.5),
        'w_out_even': normal((N_EVEN, EVEN_MIX, D_MODEL), DEEPNORM_BETA * EVEN_MIX ** -0.5),
        'hgrn_lb': normal((N_EVEN + 1, H_B * DK_B), 0.5),
        'hgrn_norm': 1.0 + normal((N_EVEN, DV_B), 0.02),
        'w_in_odd': normal((N_ODD, D_MODEL, ODD_IN), D_MODEL ** -0.5),
        'w_out_odd': normal((N_ODD, ODD_MIX, D_MODEL), DEEPNORM_BETA * ODD_MIX ** -0.5),
        'gdn_conv_w': normal((N_ODD, CONV_W, GDN_CONV_DIM), CONV_W ** -0.5),
        'gdn_a_log': a_log,
        'gdn_dt_bias': dt + jnp.log(-jnp.expm1(-dt)),
        'gdn_norm': 1.0 + normal((N_ODD, DV_C), 0.02),
        'ln1_g': 1.0 + normal((DEPTH, D_MODEL), 0.02),
        'ln1_b': normal((DEPTH, D_MODEL), 0.02),
        'ln2_g': 1.0 + normal((DEPTH, D_MODEL), 0.02),
        'ln2_b': normal((DEPTH, D_MODEL), 0.02),
        'ffn_wg': normal((N_EVEN, D_MODEL, D_FF), D_MODEL ** -0.5),
        'ffn_wu': normal((N_EVEN, D_MODEL, D_FF), D_MODEL ** -0.5),
        'ffn_wd': normal((N_EVEN, D_FF, D_MODEL), DEEPNORM_BETA * D_FF ** -0.5),
        'router': normal((N_ODD, D_MODEL, N_EXPERTS), D_MODEL ** -0.5),
        'moe_wg': normal((N_ODD, N_EXPERTS, D_MODEL, D_FF_E), D_MODEL ** -0.5),
        'moe_wu': normal((N_ODD, N_EXPERTS, D_MODEL, D_FF_E), D_MODEL ** -0.5),
        'moe_wd': normal((N_ODD, N_EXPERTS, D_FF_E, D_MODEL), DEEPNORM_BETA * D_FF_E ** -0.5),
    }


def reference(x_prompt, x_sample, cache_k_moba, cache_v_moba, state_hgrn, state_gdn, state_gdn_conv, cache_k_sb,
              cache_v_sb, page_table, w_in_even, w_out_even, hgrn_lb, hgrn_norm, w_in_odd, w_out_odd, gdn_conv_w,
              gdn_a_log, gdn_dt_bias, gdn_norm, ln1_g, ln1_b, ln2_g, ln2_b, ffn_wg, ffn_wu, ffn_wd, router, moe_wg,
              moe_wu, moe_wd):
    weights = (w_in_even, w_out_even, hgrn_lb, hgrn_norm, w_in_odd, w_out_odd, gdn_conv_w, gdn_a_log, gdn_dt_bias,
               gdn_norm, ln1_g, ln1_b, ln2_g, ln2_b, ffn_wg, ffn_wu, ffn_wd, router, moe_wg, moe_wu, moe_wd)
    bp, lp = x_prompt.shape[0], x_prompt.shape[1]
    dt = x_prompt.dtype
    pos_p = jnp.arange(lp, dtype=jnp.int32)
    (y_p, ka_p, va_p, hg_p, gd_p, cv_p, kd_p, vd_p) = _layer_stack(
        x_prompt, pos_p, None, None,
        jnp.zeros((N_EVEN, bp, H_B, DK_B, DV_B), dt), jnp.zeros((N_ODD, bp, H_C, DK_C, DV_C), dt),
        jnp.zeros((N_ODD, bp, CONV_W - 1, GDN_CONV_DIM), dt), None, None, None, weights)
    pos_s = PAST_LEN + jnp.arange(x_sample.shape[1], dtype=jnp.int32)
    (y_s, ka_s, va_s, hg_s, gd_s, cv_s, kd_s, vd_s) = _layer_stack(
        x_sample, pos_s, cache_k_moba, cache_v_moba, state_hgrn, state_gdn, state_gdn_conv, cache_k_sb, cache_v_sb,
        page_table, weights)
    return (y_p, y_s, ka_p, va_p, hg_p, gd_p, cv_p, kd_p, vd_p, ka_s, va_s, hg_s, gd_s, cv_s, kd_s, vd_s)
```

```python
import functools

import jax
import jax.numpy as jnp
from jax import lax
from jax.experimental import pallas as pl
from jax.experimental.pallas import tpu as pltpu

F32 = jnp.float32
BF16 = jnp.bfloat16

H_A, H_B, H_C, H_D = 8, 4, 4, 8
MOBA_BLOCK = 256
MOBA_TOPK = 3
CONV_W = 4
N_EXPERTS = 8
DEPTH = 2
DEEPNORM_ALPHA = (2 * DEPTH) ** 0.25
LN_EPS = 1e-5
RMS_EPS = 1e-6
NEG = -1e30
LANES = 128
VMEM_LIMIT = 56 * 1024 * 1024


def _cparams(sem):
    return pltpu.CompilerParams(dimension_semantics=sem, vmem_limit_bytes=VMEM_LIMIT)


def _layernorm(y, g, b):
    mu = jnp.mean(y, axis=-1, keepdims=True)
    yc = y - mu
    var = jnp.mean(yc * yc, axis=-1, keepdims=True)
    return yc * lax.rsqrt(var + LN_EPS) * g + b


def _split3(x):
    hi = x.astype(BF16)
    r = x - hi.astype(F32)
    mid = r.astype(BF16)
    lo = (r - mid.astype(F32)).astype(BF16)
    return hi, mid, lo


def _dot(a, b):
    return jnp.dot(a, b, preferred_element_type=F32)


def _dot_nt(a, b):
    return lax.dot_general(a, b, (((1,), (1,)), ((), ())), preferred_element_type=F32)


def _dot_tn(a, b):
    return lax.dot_general(a, b, (((0,), (0,)), ((), ())), preferred_element_type=F32)


def _dot_hi(a, b, dot=_dot):
    a0, a1, a2 = _split3(a)
    b0, b1, b2 = _split3(b)
    return (dot(a0, b0) + (dot(a0, b1) + dot(a1, b0))
            + (dot(a1, b1) + dot(a0, b2) + dot(a2, b0)))


def _dot_exact_rhs(a, b_bf16):
    a0, a1, a2 = _split3(a)
    return _dot(a0, b_bf16) + _dot(a1, b_bf16) + _dot(a2, b_bf16)


def _dot_exact_lhs(a_bf16, b):
    b0, b1, b2 = _split3(b)
    return _dot(a_bf16, b0) + _dot(a_bf16, b1) + _dot(a_bf16, b2)


def _mm_kernel(*refs, n_in, ln):
    a_refs, w_refs, rest = refs[:n_in], refs[n_in:2 * n_in], refs[2 * n_in:]
    acc = None
    for a, w in zip(a_refs, w_refs):
        d = _dot(a[...].astype(BF16), w[...])
        acc = d if acc is None else acc + d
    if ln:
        res_ref, g_ref, b_ref, o_ref = rest
        o_ref[...] = _layernorm(DEEPNORM_ALPHA * res_ref[...] + acc, g_ref[...], b_ref[...])
    else:
        (o_ref,) = rest
        o_ref[...] = acc


def _row_tile(m):
    for t in (512, 256, 128, 64, 32, 16, 8):
        if m % t == 0:
            return t
    raise ValueError(m)


def _matmul(a_list, w_list, *, tn=None, ln_args=None):
    m = a_list[0].shape[0]
    n = w_list[0].shape[1]
    tm = _row_tile(m)
    ln = ln_args is not None
    tn = n if (ln or tn is None) else tn
    assert n % tn == 0
    in_specs = [pl.BlockSpec((tm, a.shape[1]), lambda i, j: (i, 0)) for a in a_list]
    in_specs += [pl.BlockSpec((w.shape[0], tn), lambda i, j: (0, j)) for w in w_list]
    args = list(a_list) + list(w_list)
    if ln:
        res, g, b = ln_args
        in_specs += [pl.BlockSpec((tm, n), lambda i, j: (i, 0)),
                     pl.BlockSpec((1, n), lambda i, j: (0, 0)),
                     pl.BlockSpec((1, n), lambda i, j: (0, 0))]
        args += [res, g.reshape(1, n), b.reshape(1, n)]
    return pl.pallas_call(
        functools.partial(_mm_kernel, n_in=len(a_list), ln=ln),
        grid=(m // tm, n // tn),
        in_specs=in_specs,
        out_specs=pl.BlockSpec((tm, tn), lambda i, j: (i, j)),
        out_shape=jax.ShapeDtypeStruct((m, n), F32),
        compiler_params=_cparams(("parallel", "arbitrary")),
        name="matmul_ln" if ln else "matmul",
    )(*args)


def _ffn_kernel(*refs, use_comb):
    if use_comb:
        x_ref, comb_ref, wg_ref, wu_ref, wd_ref, g_ref, b_ref, o_ref, xb_ref, acc_ref = refs
    else:
        x_ref, wg_ref, wu_ref, wd_ref, g_ref, b_ref, o_ref, xb_ref, acc_ref = refs
    e, j = pl.program_id(1), pl.program_id(2)
    first = jnp.logical_and(e == 0, j == 0)
    last = jnp.logical_and(e == pl.num_programs(1) - 1, j == pl.num_programs(2) - 1)

    @pl.when(first)
    def _():
        xb_ref[...] = x_ref[...].astype(BF16)
        acc_ref[...] = jnp.zeros_like(acc_ref)

    xb = xb_ref[...]
    h = jax.nn.silu(_dot(xb, wg_ref[0])) * _dot(xb, wu_ref[0])
    if use_comb:
        lane = lax.broadcasted_iota(jnp.int32, comb_ref.shape, 1)
        h = h * jnp.sum(jnp.where(lane == e, comb_ref[...], 0.0), axis=-1, keepdims=True)
    acc_ref[...] += _dot(h.astype(BF16), wd_ref[0])

    @pl.when(last)
    def _():
        o_ref[...] = _layernorm(DEEPNORM_ALPHA * x_ref[...] + acc_ref[...], g_ref[...], b_ref[...])


def _ffn(x, comb, wg, wu, wd, g, b, *, tf=512):
    m, d = x.shape
    n_e, _, ff = wg.shape
    tm = _row_tile(m)
    assert ff % tf == 0
    use_comb = comb is not None
    in_specs = [pl.BlockSpec((tm, d), lambda i, e, j: (i, 0))]
    args = [x]
    if use_comb:
        in_specs.append(pl.BlockSpec((tm, comb.shape[1]), lambda i, e, j: (i, 0)))
        args.append(comb)
    in_specs += [pl.BlockSpec((1, d, tf), lambda i, e, j: (e, 0, j)),
                 pl.BlockSpec((1, d, tf), lambda i, e, j: (e, 0, j)),
                 pl.BlockSpec((1, tf, d), lambda i, e, j: (e, j, 0)),
                 pl.BlockSpec((1, d), lambda i, e, j: (0, 0)),
                 pl.BlockSpec((1, d), lambda i, e, j: (0, 0))]
    args += [wg, wu, wd, g.reshape(1, d), b.reshape(1, d)]
    return pl.pallas_call(
        functools.partial(_ffn_kernel, use_comb=use_comb),
        grid=(m // tm, n_e, ff // tf),
        in_specs=in_specs,
        out_specs=pl.BlockSpec((tm, d), lambda i, e, j: (i, 0)),
        out_shape=jax.ShapeDtypeStruct((m, d), F32),
        scratch_shapes=[pltpu.VMEM((tm, d), BF16), pltpu.VMEM((tm, d), F32)],
        compiler_params=_cparams(("parallel", "arbitrary", "arbitrary")),
        name="moe_ffn" if use_comb else "ffn",
    )(*args)


def _router_kernel(x_ref, r_ref, o_ref):
    logits = _dot_hi(x_ref[...], r_ref[...])
    lane = lax.broadcasted_iota(jnp.int32, logits.shape, 1)
    logits = jnp.where(lane < N_EXPERTS, logits, -jnp.inf)
    m1 = jnp.max(logits, axis=-1, keepdims=True)
    i1 = jnp.min(jnp.where(logits == m1, lane, LANES), axis=-1, keepdims=True)
    rest = jnp.where(lane == i1, -jnp.inf, logits)
    m2 = jnp.max(rest, axis=-1, keepdims=True)
    i2 = jnp.min(jnp.where(rest == m2, lane, LANES), axis=-1, keepdims=True)
    e2 = jnp.exp(m2 - m1)
    den = 1.0 + e2
    o_ref[...] = jnp.where(lane == i1, 1.0 / den, 0.0) + jnp.where(lane == i2, e2 / den, 0.0)


def _router(x, router):
    m, d = x.shape
    tm = _row_tile(m)
    r = jnp.pad(router, ((0, 0), (0, LANES - router.shape[1])))
    return pl.pallas_call(
        _router_kernel,
        grid=(m // tm,),
        in_specs=[pl.BlockSpec((tm, d), lambda i: (i, 0)), pl.BlockSpec((d, LANES), lambda i: (0, 0))],
        out_specs=pl.BlockSpec((tm, LANES), lambda i: (i, 0)),
        out_shape=jax.ShapeDtypeStruct((m, LANES), F32),
        compiler_params=_cparams(("parallel",)),
        name="router",
    )(x, r)


def _moba_kernel(slope_ref, q_ref, k_ref, v_ref, o_ref, kb_ref, vb_ref, km_ref, *, tq, nb, q_offset, hd):
    i = pl.program_id(2)
    blk = MOBA_BLOCK

    @pl.when(i == 0)
    def _():
        kb_ref[...] = k_ref[...].astype(BF16)
        vb_ref[...] = v_ref[...].astype(BF16)
        km_ref[...] = jnp.zeros_like(km_ref)
        for j in range(nb):
            km_ref[j:j + 1, :] = jnp.sum(k_ref[j * blk:(j + 1) * blk, :], axis=0, keepdims=True) * (1.0 / blk)

    q0 = q_offset + i * tq
    own = q0 // blk
    q = q_ref[...]
    lane = lax.broadcasted_iota(jnp.int32, (tq, LANES), 1)
    row_pos = q0 + lax.broadcasted_iota(jnp.int32, (tq, blk), 0)
    col = lax.broadcasted_iota(jnp.int32, (tq, blk), 1)
    scale = hd ** -0.5
    slope_lane = slope_ref[...]
    km = km_ref[...]

    qb, sel, slope = [], [], []
    for a in range(2):
        in_head = jnp.logical_and(lane >= a * hd, lane < (a + 1) * hd)
        qa = jnp.where(in_head, q, 0.0)
        gate = _dot_hi(qa, km, dot=_dot_nt)
        g = jnp.where(lane < own, gate, -jnp.inf)
        s_a = jnp.zeros((tq, LANES), F32)
        for _ in range(MOBA_TOPK):
            m = jnp.max(g, axis=-1, keepdims=True)
            idx = jnp.min(jnp.where(g == m, lane, LANES), axis=-1, keepdims=True)
            hit = jnp.logical_and(lane == idx, m > -jnp.inf)
            s_a = jnp.where(hit, 1.0, s_a)
            g = jnp.where(hit, -jnp.inf, g)
        qb.append(qa.astype(BF16))
        sel.append(s_a)
        slope.append(slope_lane[:, a * hd:a * hd + 1])

    def body(t, carry):
        m0, m1, l0, l1, acc = carry
        j = own - t
        start = pl.multiple_of(j * blk, blk)
        kj = kb_ref[pl.ds(start, blk), :]
        vj = vb_ref[pl.ds(start, blk), :]
        dist = row_pos - (j * blk + col)
        distf = dist.astype(F32)
        out = []
        for a, (m_a, l_a) in enumerate(((m0, l0), (m1, l1))):
            s = _dot_nt(qb[a], kj) * scale - slope[a] * distf
            picked = jnp.max(jnp.where(lane == j, sel[a], 0.0), axis=-1, keepdims=True)
            valid = jnp.logical_and(dist >= 0, jnp.logical_or(picked > 0.0, t == 0))
            s = jnp.where(valid, s, NEG)
            m_new = jnp.maximum(m_a, jnp.max(s, axis=-1, keepdims=True))
            alpha = jnp.exp(m_a - m_new)
            p = jnp.exp(s - m_new)
            l_new = alpha * l_a + jnp.sum(p, axis=-1, keepdims=True)
            out.append((m_new, l_new, alpha, _dot(p.astype(BF16), vj)))
        first = lane < hd
        acc = jnp.where(first, out[0][2], out[1][2]) * acc + jnp.where(first, out[0][3], out[1][3])
        return out[0][0], out[1][0], out[0][1], out[1][1], acc

    init = (jnp.full((tq, 1), NEG, F32), jnp.full((tq, 1), NEG, F32),
            jnp.zeros((tq, 1), F32), jnp.zeros((tq, 1), F32), jnp.zeros((tq, LANES), F32))
    _, _, l0, l1, acc = lax.fori_loop(0, own + 1, body, init)
    o_ref[...] = acc / jnp.where(lane < hd, l0, l1)


def _alibi_slopes_lanes(n_heads, hd):
    slopes = jnp.exp2(-8.0 * jnp.arange(1, n_heads + 1, dtype=F32) / n_heads)
    return jnp.repeat(slopes, hd).reshape(1, n_heads * hd)


def _attn_call(kernel, q, k, v, *, batch, q_offset, n_heads, extra_in=(), extra_specs=(), extra_scratch=(), name):
    lq, lk = q.shape[0] // batch, k.shape[0] // batch
    hd = q.shape[1] // n_heads
    assert 2 * hd == LANES and lk % MOBA_BLOCK == 0
    tq = min(MOBA_BLOCK, lq)
    assert lq % tq == 0 and MOBA_BLOCK % tq == 0 and q_offset % tq == 0
    nq = lq // tq
    return pl.pallas_call(
        functools.partial(kernel, tq=tq, q_offset=q_offset, hd=hd),
        grid=(batch, n_heads // 2, nq),
        in_specs=list(extra_specs) + [
            pl.BlockSpec((tq, LANES), lambda b, h, i: (b * nq + i, h)),
            pl.BlockSpec((lk, LANES), lambda b, h, i: (b, h)),
            pl.BlockSpec((lk, LANES), lambda b, h, i: (b, h))],
        out_specs=pl.BlockSpec((tq, LANES), lambda b, h, i: (b * nq + i, h)),
        out_shape=jax.ShapeDtypeStruct(q.shape, F32),
        scratch_shapes=[pltpu.VMEM((lk, LANES), BF16), pltpu.VMEM((lk, LANES), BF16)] + list(extra_scratch),
        compiler_params=_cparams(("parallel", "parallel", "arbitrary")),
        name=name,
    )(*extra_in, q, k, v)


def _moba(q, k, v, *, batch, q_offset):
    nb = k.shape[0] // batch // MOBA_BLOCK
    assert nb <= LANES
    slopes = _alibi_slopes_lanes(H_A, q.shape[1] // H_A)
    return _attn_call(
        functools.partial(_moba_kernel, nb=nb), q, k, v, batch=batch, q_offset=q_offset, n_heads=H_A,
        extra_in=(slopes,), extra_specs=(pl.BlockSpec((1, LANES), lambda b, h, i: (0, h)),),
        extra_scratch=(pltpu.VMEM((LANES, LANES), F32),), name="moba")


def _stick_kernel(q_ref, k_ref, v_ref, o_ref, kb_ref, vb_ref, *, tq, q_offset, hd):
    i = pl.program_id(2)
    blk = MOBA_BLOCK

    @pl.when(i == 0)
    def _():
        kb_ref[...] = k_ref[...].astype(BF16)
        vb_ref[...] = v_ref[...].astype(BF16)

    q0 = q_offset + i * tq
    own = q0 // blk
    q = q_ref[...]
    lane = lax.broadcasted_iota(jnp.int32, (tq, LANES), 1)
    row_pos = q0 + lax.broadcasted_iota(jnp.int32, (tq, blk), 0)
    col = lax.broadcasted_iota(jnp.int32, (tq, blk), 1)
    scale = hd ** -0.5
    u = jnp.where(lax.broadcasted_iota(jnp.int32, (blk, blk), 0) > lax.broadcasted_iota(jnp.int32, (blk, blk), 1),
                  1.0, 0.0).astype(BF16)
    qb = []
    for a in range(2):
        in_head = jnp.logical_and(lane >= a * hd, lane < (a + 1) * hd)
        qb.append(jnp.where(in_head, q, 0.0).astype(BF16))

    def body(t, carry):
        c0, c1, acc = carry
        j = own - t
        start = pl.multiple_of(j * blk, blk)
        kj = kb_ref[pl.ds(start, blk), :]
        vj = vb_ref[pl.ds(start, blk), :]
        before = (j * blk + col) < row_pos
        out = []
        for a, c_a in enumerate((c0, c1)):
            z = _dot_nt(qb[a], kj) * scale
            sp = jnp.maximum(z, 0.0) + jnp.log(1.0 + jnp.exp(-jnp.abs(z)))
            log_keep = jnp.where(before, -sp, 0.0)
            hi = log_keep.astype(BF16)
            lo = (log_keep - hi.astype(F32)).astype(BF16)
            later = _dot(hi, u) + _dot(lo, u)
            w = jnp.where(before, jnp.exp(z - sp + later + c_a), 0.0)
            out.append((c_a + jnp.sum(log_keep, axis=-1, keepdims=True), _dot(w.astype(BF16), vj)))
        acc = acc + jnp.where(lane < hd, out[0][1], out[1][1])
        return out[0][0], out[1][0], acc

    init = (jnp.zeros((tq, 1), F32), jnp.zeros((tq, 1), F32), jnp.zeros((tq, LANES), F32))
    _, _, acc = lax.fori_loop(0, own + 1, body, init)
    o_ref[...] = acc


def _stick(q, k, v, *, batch, q_offset):
    return _attn_call(_stick_kernel, q, k, v, batch=batch, q_offset=q_offset, n_heads=H_D, name="stick")


CHUNK = 64
SUB = 16
SEQ_BLOCK = 512
EXP_CLAMP = 80.0


def _dot_mid(a, b, dot=_dot):
    a0 = a.astype(BF16)
    a1 = (a - a0.astype(F32)).astype(BF16)
    b0 = b.astype(BF16)
    b1 = (b - b0.astype(F32)).astype(BF16)
    return dot(a0, b0) + (dot(a0, b1) + dot(a1, b0))


def _incl_lower(c):
    r = lax.broadcasted_iota(jnp.int32, (c, c), 0)
    s = lax.broadcasted_iota(jnp.int32, (c, c), 1)
    return r, s


def _hgrn_kernel(q_ref, f_ref, i_ref, g_ref, lb_ref, gn_ref, s0_ref, o_ref, s_ref, st_ref, *, c, n_chunks, n_heads,
                 layer):
    l = pl.program_id(1)
    dk = LANES

    @pl.when(l == 0)
    def _():
        st_ref[...] = s0_ref[0]

    r, s = _incl_lower(c)
    tri = jnp.where(s <= r, 1.0, 0.0).astype(BF16)
    sub = min(SUB, c)
    r_sub = lax.broadcasted_iota(jnp.int32, (sub, c), 0)
    s_sub = lax.broadcasted_iota(jnp.int32, (sub, c), 1)
    gn = gn_ref[...]
    e = jnp.exp(lb_ref[...] - jnp.max(lb_ref[...], axis=0, keepdims=True))
    lb_all = jnp.sum(e[:layer + 1], axis=0, keepdims=True) / jnp.sum(e, axis=0, keepdims=True)

    def chunk(n, carry):
        rows = pl.ds(pl.multiple_of(n * c, c), c)
        for h in range(n_heads):
            cols = slice(h * dk, (h + 1) * dk)
            lb = lb_all[:, cols]
            z = f_ref[rows, cols]
            f = lb + (1.0 - lb) * jax.nn.sigmoid(z)
            k = (1.0 - lb) * jax.nn.sigmoid(-z)
            q = jax.nn.silu(q_ref[rows, cols])
            v = i_ref[rows, cols]
            cg = _dot_exact_lhs(tri, jnp.log(f))
            vb = v.astype(BF16)
            st = st_ref[h]
            intra = []
            for b in range(c // sub):
                lo, hi = b * sub, (b + 1) * sub
                ref_pt = cg[lo - 1:lo] if b else jnp.zeros((1, dk), F32)
                qs = q[lo:hi] * jnp.exp(cg[lo:hi] - ref_pt)
                ks = k * jnp.exp(jnp.minimum(ref_pt - cg, EXP_CLAMP))
                a = _dot_nt(qs.astype(BF16), ks.astype(BF16))
                a = jnp.where(s_sub <= r_sub + lo, a, 0.0)
                intra.append(_dot(a.astype(BF16), vb))
            o = _dot_nt((q * jnp.exp(cg)).astype(BF16), st.astype(BF16)) + jnp.concatenate(intra, axis=0)
            last = cg[c - 1:c]
            kt = k * jnp.exp(last - cg)
            st_ref[h] = st * jnp.exp(last) + _dot_tn(vb, kt.astype(BF16))
            o = o * lax.rsqrt(jnp.mean(o * o, axis=-1, keepdims=True) + RMS_EPS) * gn
            o_ref[rows, cols] = o * jax.nn.sigmoid(g_ref[rows, cols])
        return carry

    lax.fori_loop(0, n_chunks, chunk, 0)

    @pl.when(l == pl.num_programs(1) - 1)
    def _():
        s_ref[0] = st_ref[...]


def _hgrn(hg, lb, gn, s0, *, batch, layer=0):
    t, width = hg.shape
    seq = t // batch
    w = width // 4
    n_heads = w // LANES
    c = min(CHUNK, seq)
    lblk = min(SEQ_BLOCK, seq)
    assert seq % lblk == 0 and lblk % c == 0 and c % min(SUB, c) == 0
    nl = seq // lblk
    part = lambda p: pl.BlockSpec((lblk, w), lambda b, l: (b * nl + l, p))
    state = pl.BlockSpec((1, n_heads, LANES, LANES), lambda b, l: (b, 0, 0, 0))
    o, st = pl.pallas_call(
        functools.partial(_hgrn_kernel, c=c, n_chunks=lblk // c, n_heads=n_heads, layer=layer),
        grid=(batch, nl),
        in_specs=[part(0), part(1), part(2), part(3),
                  pl.BlockSpec(lb.shape, lambda b, l: (0, 0)), pl.BlockSpec((1, LANES), lambda b, l: (0, 0)), state],
        out_specs=[pl.BlockSpec((lblk, w), lambda b, l: (b * nl + l, 0)), state],
        out_shape=[jax.ShapeDtypeStruct((t, w), F32), jax.ShapeDtypeStruct(s0.shape, F32)],
        scratch_shapes=[pltpu.VMEM((n_heads, LANES, LANES), F32)],
        compiler_params=_cparams(("parallel", "arbitrary")),
        name="hgrn2",
    )(hg, hg, hg, hg, lb, gn.reshape(1, LANES), jnp.swapaxes(s0, -1, -2))
    return o, jnp.swapaxes(st, -1, -2)


def _unit_lower_inverse(lm, c):
    bs = min(SUB, c)
    r, s = _incl_lower(c)
    eye = jnp.where(r == s, 1.0, 0.0)
    shift = bs.bit_length() - 1
    same = jnp.right_shift(r, shift) == jnp.right_shift(s, shift)

    def neumann(n, order):
        inv, pw, k = eye + n, n, 2
        while k < order:
            pw = _dot_mid(pw, pw)
            inv = inv + _dot_mid(inv, pw)
            k *= 2
        return inv

    inv_d = neumann(-jnp.where(same, lm, 0.0), bs)
    if c == bs:
        return inv_d
    m = _dot_mid(inv_d, jnp.where(same, 0.0, lm))
    return _dot_mid(neumann(-m, c // bs), inv_d)


def _gdn_kernel(x_ref, ab_ref, g_ref, cw_ref, al_ref, dtb_ref, gn_ref, s0_ref, cb_ref, o_ref, s_ref, nb_ref,
                xx_ref, y_ref, st_ref, *, c, n_chunks, n_heads, lblk):
    l = pl.program_id(1)
    dk = LANES
    halo = 8
    w = n_heads * dk

    @pl.when(l == 0)
    def _():
        st_ref[...] = s0_ref[0]
        xx_ref[halo - (CONV_W - 1):halo, :] = cb_ref[0]

    xx_ref[halo:halo + lblk, :] = x_ref[...]
    y = xx_ref[halo:halo + lblk, :] * cw_ref[CONV_W - 1:CONV_W, :]
    for i in range(CONV_W - 1):
        off = halo - (CONV_W - 1) + i
        y = y + xx_ref[off:off + lblk, :] * cw_ref[i:i + 1, :]
    y_ref[...] = jax.nn.silu(y)
    tail = xx_ref[halo + lblk - (CONV_W - 1):halo + lblk, :]
    xx_ref[halo - (CONV_W - 1):halo, :] = tail

    r, s = _incl_lower(c)
    tri = jnp.where(s <= r, 1.0, 0.0).astype(BF16)
    gn = gn_ref[...]

    def l2n(t):
        return t * lax.rsqrt(jnp.sum(t * t, axis=-1, keepdims=True) + RMS_EPS)

    def chunk(n, carry):
        rows = pl.ds(pl.multiple_of(n * c, c), c)
        ab = ab_ref[rows, :]
        pre = ab + dtb_ref[...]
        log_a = -jnp.exp(al_ref[...]) * (jnp.maximum(pre, 0.0) + jnp.log(1.0 + jnp.exp(-jnp.abs(pre))))
        beta_all = jax.nn.sigmoid(ab)
        cg_all = _dot_exact_lhs(tri, log_a)
        cg_t = cg_all.T
        for h in range(n_heads):
            q = l2n(y_ref[rows, h * dk:(h + 1) * dk]) * dk ** -0.5
            k = l2n(y_ref[rows, w + h * dk:w + (h + 1) * dk])
            v = y_ref[rows, 2 * w + h * dk:2 * w + (h + 1) * dk]
            cg = cg_all[:, h:h + 1]
            beta = beta_all[:, n_heads + h:n_heads + h + 1]
            decay = jnp.exp(jnp.where(s <= r, cg - cg_t[h:h + 1, :c], NEG))
            kk = _dot_mid(k, k, dot=_dot_nt) * decay * beta
            t_inv = _unit_lower_inverse(jnp.where(s < r, kk, 0.0), c)
            e_cg = jnp.exp(cg)
            sol_v = _dot_mid(t_inv, v * beta)
            sol_k = _dot_mid(t_inv, k * (beta * e_cg))
            st = st_ref[h]
            stb = st.astype(BF16)
            u = sol_v - _dot(sol_k.astype(BF16), stb)
            qk = _dot_nt(q.astype(BF16), k.astype(BF16)) * decay
            o = _dot((q * e_cg).astype(BF16), stb) + _dot(qk.astype(BF16), u.astype(BF16))
            last = cg[c - 1:c]
            kt = k * jnp.exp(last - cg)
            st_ref[h] = st * jnp.exp(last) + _dot_tn(kt.astype(BF16), u.astype(BF16))
            o = o * lax.rsqrt(jnp.mean(o * o, axis=-1, keepdims=True) + RMS_EPS) * gn
            o_ref[rows, h * dk:(h + 1) * dk] = o * jax.nn.silu(g_ref[rows, h * dk:(h + 1) * dk])
        return carry

    lax.fori_loop(0, n_chunks, chunk, 0)

    @pl.when(l == pl.num_programs(1) - 1)
    def _():
        s_ref[0] = st_ref[...]
        nb_ref[0] = tail


def _gdn(qkv, ab, g, conv_w, a_log, dt_bias, gn, s0, conv_buf, *, batch):
    t, width = qkv.shape
    seq = t // batch
    w = width // 3
    n_heads = w // LANES
    c = min(CHUNK, seq)
    lblk = min(SEQ_BLOCK, seq)
    assert seq % lblk == 0 and lblk % c == 0 and lblk >= CONV_W - 1
    nl = seq // lblk
    pad_lane = lambda vec: jnp.pad(vec, (0, LANES - vec.shape[0])).reshape(1, LANES)
    rows = lambda width_: pl.BlockSpec((lblk, width_), lambda b, l: (b * nl + l, 0))
    const = lambda shape: pl.BlockSpec(shape, lambda b, l: (0,) * len(shape))
    state = pl.BlockSpec((1, n_heads, LANES, LANES), lambda b, l: (b, 0, 0, 0))
    buf = pl.BlockSpec((1, CONV_W - 1, width), lambda b, l: (b, 0, 0))
    return pl.pallas_call(
        functools.partial(_gdn_kernel, c=c, n_chunks=lblk // c, n_heads=n_heads, lblk=lblk),
        grid=(batch, nl),
        in_specs=[rows(width), rows(LANES), rows(w), const((CONV_W, width)), const((1, LANES)), const((1, LANES)),
                  const((1, LANES)), state, buf],
        out_specs=[rows(w), state, buf],
        out_shape=[jax.ShapeDtypeStruct((t, w), F32), jax.ShapeDtypeStruct(s0.shape, F32),
                   jax.ShapeDtypeStruct(conv_buf.shape, F32)],
        scratch_shapes=[pltpu.VMEM((lblk + 8, width), F32), pltpu.VMEM((lblk, width), F32),
                        pltpu.VMEM((n_heads, LANES, LANES), F32)],
        compiler_params=_cparams(("parallel", "arbitrary")),
        name="gated_deltanet",
    )(qkv, ab, g, conv_w, pad_lane(a_log), pad_lane(dt_bias), gn.reshape(1, LANES), s0, conv_buf)


def _gather_kernel(pt_ref, pool_ref, new_ref, o_ref, *, n_pages, n_new):
    p = pl.program_id(1)

    @pl.when(p < n_pages)
    def _():
        o_ref[...] = pool_ref[0]

    @pl.when(p >= n_pages)
    def _():
        o_ref[...] = jnp.zeros_like(o_ref)

    @pl.when(p == n_pages)
    def _():
        o_ref[0:n_new, :] = new_ref[...]


def _gather_pages(pool, page_table, new):
    batch, n_pages = page_table.shape
    page, width = pool.shape[1:]
    n_new = new.shape[0] // batch
    assert n_new <= page and n_new % 8 == 0 and (n_pages * page) % MOBA_BLOCK == 0 and MOBA_BLOCK % page == 0
    steps = n_pages + MOBA_BLOCK // page
    return pl.pallas_call(
        functools.partial(_gather_kernel, n_pages=n_pages, n_new=n_new),
        grid_spec=pltpu.PrefetchScalarGridSpec(
            num_scalar_prefetch=1,
            grid=(batch, steps),
            in_specs=[pl.BlockSpec((1, page, width), lambda b, p, pt: (pt[b, jnp.minimum(p, n_pages - 1)], 0, 0)),
                      pl.BlockSpec((n_new, width), lambda b, p, pt: (b, 0))],
            out_specs=pl.BlockSpec((page, width), lambda b, p, pt: (b * steps + p, 0))),
        out_shape=jax.ShapeDtypeStruct((batch * steps * page, width), F32),
        compiler_params=_cparams(("parallel", "arbitrary")),
        name="gather_pages",
    )(page_table, pool, new)


def _layer_stack(x, batch, q_offset, past, states, w):
    s_hgrn, s_gdn, conv_buf = states
    mm = lambda a, wt: _matmul([a], [wt], tn=min(512, wt.shape[1]))

    def keys(pool, new):
        if past is None:
            return new
        return _gather_pages(pool.reshape(pool.shape[0], pool.shape[1], -1), past[4], new)

    q_a, k_a, v_a = (mm(x, wt) for wt in w["in_a"])
    hg = mm(x, w["in_hgrn"])
    o_a = _moba(q_a, keys(past and past[0], k_a), keys(past and past[1], v_a), batch=batch, q_offset=q_offset)
    o_b, s_hgrn_new = _hgrn(hg, w["hgrn_lb"], w["hgrn_norm"], s_hgrn, batch=batch, layer=0)
    x = _matmul([o_a, o_b], w["out_even"], ln_args=(x, w["ln1_g"][0], w["ln1_b"][0]))
    x = _ffn(x, None, w["ffn_wg"], w["ffn_wu"], w["ffn_wd"], w["ln2_g"][0], w["ln2_b"][0])

    qkv_c, ab, g_c = mm(x, w["in_qkv_c"]), mm(x, w["in_ab"]), mm(x, w["in_g_c"])
    q_d, k_d, v_d = (mm(x, wt) for wt in w["in_d"])
    o_c, s_gdn_new, conv_new = _gdn(qkv_c, ab, g_c, w["conv_w"], w["a_log"], w["dt_bias"], w["gdn_norm"],
                                    s_gdn, conv_buf, batch=batch)
    o_d = _stick(q_d, keys(past and past[2], k_d), keys(past and past[3], v_d), batch=batch, q_offset=q_offset)
    x = _matmul([o_c, o_d], w["out_odd"], ln_args=(x, w["ln1_g"][1], w["ln1_b"][1]))
    comb = _router(x, w["router"])
    x = _ffn(x, comb, w["moe_wg"], w["moe_wu"], w["moe_wd"], w["ln2_g"][1], w["ln2_b"][1])
    return x, k_a, v_a, s_hgrn_new, s_gdn_new, conv_new, k_d, v_d


def kernel(x_prompt, x_sample, cache_k_moba, cache_v_moba, state_hgrn, state_gdn, state_gdn_conv, cache_k_sb,
           cache_v_sb, page_table, w_in_even, w_out_even, hgrn_lb, hgrn_norm, w_in_odd, w_out_odd, gdn_conv_w,
           gdn_a_log, gdn_dt_bias, gdn_norm, ln1_g, ln1_b, ln2_g, ln2_b, ffn_wg, ffn_wu, ffn_wd, router, moe_wg,
           moe_wu, moe_wd):
    assert w_in_even.shape[0] == 1 and w_in_odd.shape[0] == 1
    bp, lp, d = x_prompt.shape
    bs, ls, _ = x_sample.shape
    mix = d // 2
    hd_a, hd_d = mix // H_A, mix // H_D
    conv_dim = gdn_conv_w.shape[-1]
    bf = lambda t: t.astype(BF16)
    cols = lambda wt, lo, n: bf(wt[:, lo:lo + n])

    wie, wio = w_in_even[0], w_in_odd[0]
    g_lo = conv_dim + 2 * H_C
    d_lo = g_lo + mix
    w = {
        "in_a": [cols(wie, i * mix, mix) for i in range(3)],
        "in_hgrn": cols(wie, 3 * mix, 4 * mix),
        "out_even": [bf(w_out_even[0][:mix]), bf(w_out_even[0][mix:])],
        "hgrn_lb": hgrn_lb, "hgrn_norm": hgrn_norm[0],
        "in_qkv_c": cols(wio, 0, conv_dim),
        "in_ab": bf(jnp.pad(wio[:, conv_dim:g_lo], ((0, 0), (0, LANES - 2 * H_C)))),
        "in_g_c": cols(wio, g_lo, mix),
        "in_d": [cols(wio, d_lo + i * mix, mix) for i in range(3)],
        "out_odd": [bf(w_out_odd[0][:mix]), bf(w_out_odd[0][mix:])],
        "conv_w": gdn_conv_w[0], "a_log": gdn_a_log[0], "dt_bias": gdn_dt_bias[0], "gdn_norm": gdn_norm[0],
        "ln1_g": ln1_g, "ln1_b": ln1_b, "ln2_g": ln2_g, "ln2_b": ln2_b,
        "ffn_wg": bf(ffn_wg), "ffn_wu": bf(ffn_wu), "ffn_wd": bf(ffn_wd),
        "router": router[0], "moe_wg": bf(moe_wg[0]), "moe_wu": bf(moe_wu[0]), "moe_wd": bf(moe_wd[0]),
    }

    def run(x, batch, q_offset, past, states):
        seq = x.shape[1]
        y, k_a, v_a, s_h, s_g, cv, k_d, v_d = _layer_stack(x.reshape(batch * seq, d), batch, q_offset, past, states, w)
        return (y.reshape(batch, seq, d), k_a.reshape(1, batch, seq, H_A, hd_a), v_a.reshape(1, batch, seq, H_A, hd_a),
                s_h[None], s_g[None], cv[None], k_d.reshape(1, batch, seq, H_D, hd_d),
                v_d.reshape(1, batch, seq, H_D, hd_d))

    zeros_p = (jnp.zeros((bp,) + state_hgrn.shape[2:], F32), jnp.zeros((bp,) + state_gdn.shape[2:], F32),
               jnp.zeros((bp,) + state_gdn_conv.shape[2:], F32))
    out_p = run(x_prompt, bp, 0, None, zeros_p)
    past_len = page_table.shape[1] * cache_k_moba.shape[2]
    past = (cache_k_moba[0], cache_v_moba[0], cache_k_sb[0], cache_v_sb[0], page_table)
    out_s = run(x_sample, bs, past_len, past, (state_hgrn[0], state_gdn[0], state_gdn_conv[0]))
    return (out_p[0], out_s[0]) + out_p[1:] + out_s[1:]
```

```python
import functools

import jax
import jax.numpy as jnp
from jax import lax
from jax.experimental import pallas as pl
from jax.experimental.pallas import tpu as pltpu

F32 = jnp.float32
BF16 = jnp.bfloat16

H_A, H_B, H_C, H_D = 8, 4, 4, 8
MOBA_BLOCK = 256
MOBA_TOPK = 3
CONV_W = 4
N_EXPERTS = 8
DEPTH = 2
DEEPNORM_ALPHA = (2 * DEPTH) ** 0.25
LN_EPS = 1e-5
RMS_EPS = 1e-6
NEG = -1e30
LANES = 128
VMEM_LIMIT = 56 * 1024 * 1024


def _cparams(sem):
    return pltpu.CompilerParams(dimension_semantics=sem, vmem_limit_bytes=VMEM_LIMIT)


def _layernorm(y, g, b):
    mu = jnp.mean(y, axis=-1, keepdims=True)
    yc = y - mu
    var = jnp.mean(yc * yc, axis=-1, keepdims=True)
    return yc * lax.rsqrt(var + LN_EPS) * g + b


def _split3(x):
    hi = x.astype(BF16)
    r = x - hi.astype(F32)
    mid = r.astype(BF16)
    lo = (r - mid.astype(F32)).astype(BF16)
    return hi, mid, lo


def _dot(a, b):
    return jnp.dot(a, b, preferred_element_type=F32)


def _dot_nt(a, b):
    return lax.dot_general(a, b, (((1,), (1,)), ((), ())), preferred_element_type=F32)


def _dot_tn(a, b):
    return lax.dot_general(a, b, (((0,), (0,)), ((), ())), preferred_element_type=F32)


def _dot_hi(a, b, dot=_dot):
    a0, a1, a2 = _split3(a)
    b0, b1, b2 = _split3(b)
    return (dot(a0, b0) + (dot(a0, b1) + dot(a1, b0))
            + (dot(a1, b1) + dot(a0, b2) + dot(a2, b0)))


def _dot_exact_rhs(a, b_bf16):
    a0, a1, a2 = _split3(a)
    return _dot(a0, b_bf16) + _dot(a1, b_bf16) + _dot(a2, b_bf16)


def _dot_exact_lhs(a_bf16, b):
    b0, b1, b2 = _split3(b)
    return _dot(a_bf16, b0) + _dot(a_bf16, b1) + _dot(a_bf16, b2)


def _mm_kernel(*refs, n_in, ln):
    a_refs, w_refs, rest = refs[:n_in], refs[n_in:2 * n_in], refs[2 * n_in:]
    acc = None
    for a, w in zip(a_refs, w_refs):
        d = _dot(a[...].astype(BF16), w[...])
        acc = d if acc is None else acc + d
    if ln:
        res_ref, g_ref, b_ref, o_ref = rest
        o_ref[...] = _layernorm(DEEPNORM_ALPHA * res_ref[...] + acc, g_ref[...], b_ref[...])
    else:
        (o_ref,) = rest
        o_ref[...] = acc


def _row_tile(m):
    for t in (512, 256, 128, 64, 32, 16, 8):
        if m % t == 0:
            return t
    raise ValueError(m)


def _matmul(a_list, w_list, *, tn=None, ln_args=None):
    m = a_list[0].shape[0]
    n = w_list[0].shape[1]
    tm = _row_tile(m)
    ln = ln_args is not None
    tn = n if (ln or tn is None) else tn
    assert n % tn == 0
    in_specs = [pl.BlockSpec((tm, a.shape[1]), lambda i, j: (i, 0)) for a in a_list]
    in_specs += [pl.BlockSpec((w.shape[0], tn), lambda i, j: (0, j)) for w in w_list]
    args = list(a_list) + list(w_list)
    if ln:
        res, g, b = ln_args
        in_specs += [pl.BlockSpec((tm, n), lambda i, j: (i, 0)),
                     pl.BlockSpec((1, n), lambda i, j: (0, 0)),
                     pl.BlockSpec((1, n), lambda i, j: (0, 0))]
        args += [res, g.reshape(1, n), b.reshape(1, n)]
    return pl.pallas_call(
        functools.partial(_mm_kernel, n_in=len(a_list), ln=ln),
        grid=(m // tm, n // tn),
        in_specs=in_specs,
        out_specs=pl.BlockSpec((tm, tn), lambda i, j: (i, j)),
        out_shape=jax.ShapeDtypeStruct((m, n), F32),
        compiler_params=_cparams(("parallel", "arbitrary")),
        name="matmul_ln" if ln else "matmul",
    )(*args)


def _ffn_kernel(*refs, use_comb):
    if use_comb:
        x_ref, comb_ref, wg_ref, wu_ref, wd_ref, g_ref, b_ref, o_ref, xb_ref, acc_ref = refs
    else:
        x_ref, wg_ref, wu_ref, wd_ref, g_ref, b_ref, o_ref, xb_ref, acc_ref = refs
    e, j = pl.program_id(1), pl.program_id(2)
    first = jnp.logical_and(e == 0, j == 0)
    last = jnp.logical_and(e == pl.num_programs(1) - 1, j == pl.num_programs(2) - 1)

    @pl.when(first)
    def _():
        xb_ref[...] = x_ref[...].astype(BF16)
        acc_ref[...] = jnp.zeros_like(acc_ref)

    xb = xb_ref[...]
    h = jax.nn.silu(_dot(xb, wg_ref[0])) * _dot(xb, wu_ref[0])
    if use_comb:
        lane = lax.broadcasted_iota(jnp.int32, comb_ref.shape, 1)
        h = h * jnp.sum(jnp.where(lane == e, comb_ref[...], 0.0), axis=-1, keepdims=True)
    acc_ref[...] += _dot(h.astype(BF16), wd_ref[0])

    @pl.when(last)
    def _():
        o_ref[...] = _layernorm(DEEPNORM_ALPHA * x_ref[...] + acc_ref[...], g_ref[...], b_ref[...])


def _ffn(x, comb, wg, wu, wd, g, b, *, tf=512):
    m, d = x.shape
    n_e, _, ff = wg.shape
    tm = _row_tile(m)
    assert ff % tf == 0
    use_comb = comb is not None
    in_specs = [pl.BlockSpec((tm, d), lambda i, e, j: (i, 0))]
    args = [x]
    if use_comb:
        in_specs.append(pl.BlockSpec((tm, comb.shape[1]), lambda i, e, j: (i, 0)))
        args.append(comb)
    in_specs += [pl.BlockSpec((1, d, tf), lambda i, e, j: (e, 0, j)),
                 pl.BlockSpec((1, d, tf), lambda i, e, j: (e, 0, j)),
                 pl.BlockSpec((1, tf, d), lambda i, e, j: (e, j, 0)),
                 pl.BlockSpec((1, d), lambda i, e, j: (0, 0)),
                 pl.BlockSpec((1, d), lambda i, e, j: (0, 0))]
    args += [wg, wu, wd, g.reshape(1, d), b.reshape(1, d)]
    return pl.pallas_call(
        functools.partial(_ffn_kernel, use_comb=use_comb),
        grid=(m // tm, n_e, ff // tf),
        in_specs=in_specs,
        out_specs=pl.BlockSpec((tm, d), lambda i, e, j: (i, 0)),
        out_shape=jax.ShapeDtypeStruct((m, d), F32),
        scratch_shapes=[pltpu.VMEM((tm, d), BF16), pltpu.VMEM((tm, d), F32)],
        compiler_params=_cparams(("parallel", "arbitrary", "arbitrary")),
        name="moe_ffn" if use_comb else "ffn",
    )(*args)


def _router_kernel(x_ref, r_ref, o_ref):
    logits = _dot_hi(x_ref[...], r_ref[...])
    lane = lax.broadcasted_iota(jnp.int32, logits.shape, 1)
    logits = jnp.where(lane < N_EXPERTS, logits, -jnp.inf)
    m1 = jnp.max(logits, axis=-1, keepdims=True)
    i1 = jnp.min(jnp.where(logits == m1, lane, LANES), axis=-1, keepdims=True)
    rest = jnp.where(lane == i1, -jnp.inf, logits)
    m2 = jnp.max(rest, axis=-1, keepdims=True)
    i2 = jnp.min(jnp.where(rest == m2, lane, LANES), axis=-1, keepdims=True)
    e2 = jnp.exp(m2 - m1)
    den = 1.0 + e2
    o_ref[...] = jnp.where(lane == i1, 1.0 / den, 0.0) + jnp.where(lane == i2, e2 / den, 0.0)


def _router(x, router):
    m, d = x.shape
    tm = _row_tile(m)
    r = jnp.pad(router, ((0, 0), (0, LANES - router.shape[1])))
    return pl.pallas_call(
        _router_kernel,
        grid=(m // tm,),
        in_specs=[pl.BlockSpec((tm, d), lambda i: (i, 0)), pl.BlockSpec((d, LANES), lambda i: (0, 0))],
        out_specs=pl.BlockSpec((tm, LANES), lambda i: (i, 0)),
        out_shape=jax.ShapeDtypeStruct((m, LANES), F32),
        compiler_params=_cparams(("parallel",)),
        name="router",
    )(x, r)


def _topk_lanes(g, idx, axis=-1):
    sel = jnp.zeros(g.shape, F32)
    for _ in range(MOBA_TOPK):
        m = jnp.max(g, axis=axis, keepdims=True)
        first = jnp.min(jnp.where(g == m, idx, LANES), axis=axis, keepdims=True)
        hit = jnp.logical_and(idx == first, m > -jnp.inf)
        sel = jnp.where(hit, 1.0, sel)
        g = jnp.where(hit, -jnp.inf, g)
    return sel


def _moba_kernel(slope_ref, q_ref, k_ref, v_ref, o_ref, kb_ref, vt_ref, km_ref, m_ref, l_ref, acc_ref, sel_ref,
                 *, tq, nb, q_offset, hd):
    i = pl.program_id(2)
    blk = MOBA_BLOCK

    @pl.when(i == 0)
    def _():
        kb_ref[...] = k_ref[...].astype(BF16)
        km_ref[...] = jnp.zeros_like(km_ref)
        for j in range(nb):
            rows = slice(j * blk, (j + 1) * blk)
            vt_ref[j] = v_ref[rows, :].T.astype(BF16)
            km_ref[j:j + 1, :] = jnp.sum(k_ref[rows, :], axis=0, keepdims=True) * (1.0 / blk)

    q0 = q_offset + i * tq
    own = q0 // blk
    q = q_ref[...]
    lane = lax.broadcasted_iota(jnp.int32, (tq, LANES), 1)
    q2 = jnp.concatenate([jnp.where(lane < hd, q, 0.0), jnp.where(lane >= hd, q, 0.0)], axis=0)
    gate_t = _dot_hi(km_ref[...], q2, dot=_dot_nt)
    blk_id = lax.broadcasted_iota(jnp.int32, (LANES, 2 * tq), 0)
    sel_ref[...] = _topk_lanes(jnp.where(blk_id < own, gate_t, -jnp.inf), blk_id, axis=0)
    qb = (q2 * hd ** -0.5).astype(BF16)
    c2 = lax.broadcasted_iota(jnp.int32, (blk, 2 * tq), 1)
    d0 = (jnp.where(c2 >= tq, c2 - tq, c2) - lax.broadcasted_iota(jnp.int32, (blk, 2 * tq), 0)).astype(F32)
    slope_lane = slope_ref[...]
    c1 = lax.broadcasted_iota(jnp.int32, (1, 2 * tq), 1)
    slope = jnp.where(c1 < tq, slope_lane[:, 0:1], slope_lane[:, hd:hd + 1])
    slope_d0 = slope * d0
    first = lax.broadcasted_iota(jnp.int32, (2 * hd, tq), 0) < hd
    m_ref[...] = jnp.full(m_ref.shape, NEG, F32)
    l_ref[...] = jnp.zeros_like(l_ref)
    acc_ref[...] = jnp.zeros_like(acc_ref)

    def tile(j, diagonal):
        off = (q0 - j * blk).astype(F32)
        s = _dot_nt(kb_ref[pl.ds(pl.multiple_of(j * blk, blk), blk), :], qb) - (slope_d0 + slope * off)
        if diagonal:
            s = jnp.where(d0 + off >= 0.0, s, NEG)
        else:
            s = jnp.where(sel_ref[pl.ds(j, 1), :] > 0.0, s, NEG)
        m_old = m_ref[...]
        m_new = jnp.maximum(m_old, jnp.max(s, axis=0, keepdims=True))
        alpha = jnp.exp(m_old - m_new)
        p = jnp.exp(s - m_new)
        m_ref[...] = m_new
        l_ref[...] = alpha * l_ref[...] + jnp.sum(p, axis=0, keepdims=True)
        pv = _dot(vt_ref[j], p.astype(BF16))
        acc_ref[...] = (jnp.where(first, alpha[:, :tq], alpha[:, tq:]) * acc_ref[...]
                        + jnp.where(first, pv[:, :tq], pv[:, tq:]))

    tile(own, True)

    def body(t, carry):
        tile(own - t, False)
        return carry

    lax.fori_loop(1, own + 1, body, 0)
    l = l_ref[...]
    o_ref[...] = (acc_ref[...] / jnp.where(first, l[:, :tq], l[:, tq:])).T


def _alibi_slopes_lanes(n_heads, hd):
    slopes = jnp.exp2(-8.0 * jnp.arange(1, n_heads + 1, dtype=F32) / n_heads)
    return jnp.repeat(slopes, hd).reshape(1, n_heads * hd)


def _attn_call(kernel, q, k, v, *, batch, q_offset, n_heads, extra_in=(), extra_specs=(), extra_scratch=(), name):
    lq, lk = q.shape[0] // batch, k.shape[0] // batch
    hd = q.shape[1] // n_heads
    assert 2 * hd == LANES and lk % MOBA_BLOCK == 0
    tq = min(MOBA_BLOCK, lq)
    assert lq % tq == 0 and MOBA_BLOCK % tq == 0 and q_offset % tq == 0
    nq = lq // tq
    return pl.pallas_call(
        functools.partial(kernel, tq=tq, q_offset=q_offset, hd=hd),
        grid=(batch, n_heads // 2, nq),
        in_specs=list(extra_specs) + [
            pl.BlockSpec((tq, LANES), lambda b, h, i: (b * nq + i, h)),
            pl.BlockSpec((lk, LANES), lambda b, h, i: (b, h)),
            pl.BlockSpec((lk, LANES), lambda b, h, i: (b, h))],
        out_specs=pl.BlockSpec((tq, LANES), lambda b, h, i: (b * nq + i, h)),
        out_shape=jax.ShapeDtypeStruct(q.shape, F32),
        scratch_shapes=list(extra_scratch),
        compiler_params=_cparams(("parallel", "parallel", "arbitrary")),
        name=name,
    )(*extra_in, q, k, v)


def _moba(q, k, v, *, batch, q_offset):
    nb = k.shape[0] // batch // MOBA_BLOCK
    assert nb <= LANES
    slopes = _alibi_slopes_lanes(H_A, q.shape[1] // H_A)
    tq = min(MOBA_BLOCK, q.shape[0] // batch)
    return _attn_call(
        functools.partial(_moba_kernel, nb=nb), q, k, v, batch=batch, q_offset=q_offset, n_heads=H_A,
        extra_in=(slopes,), extra_specs=(pl.BlockSpec((1, LANES), lambda b, h, i: (0, h)),),
        extra_scratch=(pltpu.VMEM((nb * MOBA_BLOCK, LANES), BF16), pltpu.VMEM((nb, LANES, MOBA_BLOCK), BF16),
                       pltpu.VMEM((LANES, LANES), F32), pltpu.VMEM((1, 2 * tq), F32), pltpu.VMEM((1, 2 * tq), F32),
                       pltpu.VMEM((LANES, tq), F32), pltpu.VMEM((LANES, 2 * tq), F32)), name="moba")


def _softplus(z):
    return jnp.maximum(z, 0.0) + jnp.log(1.0 + jnp.exp(-jnp.abs(z)))


def _stick_kernel(q_ref, k_ref, v_ref, o_ref, kb_ref, vb_ref, c_ref, acc_ref, *, tq, q_offset, hd):
    i = pl.program_id(2)
    blk = MOBA_BLOCK

    @pl.when(i == 0)
    def _():
        kb_ref[...] = k_ref[...].astype(BF16)
        vb_ref[...] = v_ref[...].astype(BF16)

    q0 = q_offset + i * tq
    own = q0 // blk
    q = q_ref[...] * hd ** -0.5
    lane = lax.broadcasted_iota(jnp.int32, (tq, LANES), 1)
    qb = jnp.concatenate([jnp.where(lane < hd, q, 0.0), jnp.where(lane >= hd, q, 0.0)], axis=0).astype(BF16)
    r2 = lax.broadcasted_iota(jnp.int32, (2 * tq, blk), 0)
    d0 = jnp.where(r2 >= tq, r2 - tq, r2) - lax.broadcasted_iota(jnp.int32, (2 * tq, blk), 1)
    u = jnp.where(lax.broadcasted_iota(jnp.int32, (blk, blk), 0) > lax.broadcasted_iota(jnp.int32, (blk, blk), 1),
                  1.0, 0.0).astype(BF16)
    c_ref[...] = jnp.zeros_like(c_ref)
    acc_ref[...] = jnp.zeros_like(acc_ref)

    def tile(j, diagonal):
        start = pl.multiple_of(j * blk, blk)
        z = _dot_nt(qb, kb_ref[pl.ds(start, blk), :])
        log_keep = -_softplus(z)
        log_beta = z + log_keep
        if diagonal:
            before = d0 + (q0 - j * blk) > 0
            log_keep = jnp.where(before, log_keep, 0.0)
        hi = log_keep.astype(BF16)
        lo = (log_keep - hi.astype(F32)).astype(BF16)
        later = _dot(jnp.concatenate([hi, lo], axis=0), u)
        later = later[:2 * tq] + later[2 * tq:]
        c = c_ref[...]
        w = jnp.exp(log_beta + later + c)
        if diagonal:
            w = jnp.where(before, w, 0.0)
        c_ref[...] = c + later[:, 0:1] + log_keep[:, 0:1]
        pv = _dot(w.astype(BF16), vb_ref[pl.ds(start, blk), :])
        acc_ref[...] += jnp.where(lane < hd, pv[:tq], pv[tq:])

    tile(own, True)

    def body(t, carry):
        tile(own - t, False)
        return carry

    lax.fori_loop(1, own + 1, body, 0)
    o_ref[...] = acc_ref[...]


def _stick(q, k, v, *, batch, q_offset):
    tq = min(MOBA_BLOCK, q.shape[0] // batch)
    lk = k.shape[0] // batch
    return _attn_call(_stick_kernel, q, k, v, batch=batch, q_offset=q_offset, n_heads=H_D,
                      extra_scratch=(pltpu.VMEM((lk, LANES), BF16), pltpu.VMEM((lk, LANES), BF16),
                                     pltpu.VMEM((2 * tq, 1), F32), pltpu.VMEM((tq, LANES), F32)), name="stick")


CHUNK = 64
SUB = 16
SEQ_BLOCK = 512
EXP_CLAMP = 80.0


def _dot_mid(a, b, dot=_dot):
    a0 = a.astype(BF16)
    a1 = (a - a0.astype(F32)).astype(BF16)
    b0 = b.astype(BF16)
    b1 = (b - b0.astype(F32)).astype(BF16)
    return dot(a0, b0) + (dot(a0, b1) + dot(a1, b0))


def _incl_lower(c):
    r = lax.broadcasted_iota(jnp.int32, (c, c), 0)
    s = lax.broadcasted_iota(jnp.int32, (c, c), 1)
    return r, s


def _hgrn_kernel(q_ref, f_ref, i_ref, g_ref, lb_ref, gn_ref, s0_ref, o_ref, s_ref, st_ref, *, c, n_chunks, n_heads,
                 layer):
    l = pl.program_id(1)
    dk = LANES

    @pl.when(l == 0)
    def _():
        st_ref[...] = s0_ref[0]

    r, s = _incl_lower(c)
    tri = jnp.where(s <= r, 1.0, 0.0).astype(BF16)
    sub = min(SUB, c)
    r_sub = lax.broadcasted_iota(jnp.int32, (sub, c), 0)
    s_sub = lax.broadcasted_iota(jnp.int32, (sub, c), 1)
    gn = gn_ref[...]
    e = jnp.exp(lb_ref[...] - jnp.max(lb_ref[...], axis=0, keepdims=True))
    lb_all = jnp.sum(e[:layer + 1], axis=0, keepdims=True) / jnp.sum(e, axis=0, keepdims=True)

    def chunk(n, carry):
        rows = pl.ds(pl.multiple_of(n * c, c), c)
        for h in range(n_heads):
            cols = slice(h * dk, (h + 1) * dk)
            lb = lb_all[:, cols]
            z = f_ref[rows, cols]
            f = lb + (1.0 - lb) * jax.nn.sigmoid(z)
            k = (1.0 - lb) * jax.nn.sigmoid(-z)
            q = jax.nn.silu(q_ref[rows, cols])
            v = i_ref[rows, cols]
            cg = _dot_exact_lhs(tri, jnp.log(f))
            vb = v.astype(BF16)
            st = st_ref[h]
            intra = []
            for b in range(c // sub):
                lo, hi = b * sub, (b + 1) * sub
                ref_pt = cg[lo - 1:lo] if b else jnp.zeros((1, dk), F32)
                qs = q[lo:hi] * jnp.exp(cg[lo:hi] - ref_pt)
                ks = k * jnp.exp(jnp.minimum(ref_pt - cg, EXP_CLAMP))
                a = _dot_nt(qs.astype(BF16), ks.astype(BF16))
                a = jnp.where(s_sub <= r_sub + lo, a, 0.0)
                intra.append(_dot(a.astype(BF16), vb))
            o = _dot_nt((q * jnp.exp(cg)).astype(BF16), st.astype(BF16)) + jnp.concatenate(intra, axis=0)
            last = cg[c - 1:c]
            kt = k * jnp.exp(last - cg)
            st_ref[h] = st * jnp.exp(last) + _dot_tn(vb, kt.astype(BF16))
            o = o * lax.rsqrt(jnp.mean(o * o, axis=-1, keepdims=True) + RMS_EPS) * gn
            o_ref[rows, cols] = o * jax.nn.sigmoid(g_ref[rows, cols])
        return carry

    lax.fori_loop(0, n_chunks, chunk, 0)

    @pl.when(l == pl.num_programs(1) - 1)
    def _():
        s_ref[0] = st_ref[...]


def _hgrn(hg, lb, gn, s0, *, batch, layer=0):
    t, width = hg.shape
    seq = t // batch
    w = width // 4
    n_heads = w // LANES
    c = min(CHUNK, seq)
    lblk = min(SEQ_BLOCK, seq)
    assert seq % lblk == 0 and lblk % c == 0 and c % min(SUB, c) == 0
    nl = seq // lblk
    part = lambda p: pl.BlockSpec((lblk, w), lambda b, l: (b * nl + l, p))
    state = pl.BlockSpec((1, n_heads, LANES, LANES), lambda b, l: (b, 0, 0, 0))
    o, st = pl.pallas_call(
        functools.partial(_hgrn_kernel, c=c, n_chunks=lblk // c, n_heads=n_heads, layer=layer),
        grid=(batch, nl),
        in_specs=[part(0), part(1), part(2), part(3),
                  pl.BlockSpec(lb.shape, lambda b, l: (0, 0)), pl.BlockSpec((1, LANES), lambda b, l: (0, 0)), state],
        out_specs=[pl.BlockSpec((lblk, w), lambda b, l: (b * nl + l, 0)), state],
        out_shape=[jax.ShapeDtypeStruct((t, w), F32), jax.ShapeDtypeStruct(s0.shape, F32)],
        scratch_shapes=[pltpu.VMEM((n_heads, LANES, LANES), F32)],
        compiler_params=_cparams(("parallel", "arbitrary")),
        name="hgrn2",
    )(hg, hg, hg, hg, lb, gn.reshape(1, LANES), jnp.swapaxes(s0, -1, -2))
    return o, jnp.swapaxes(st, -1, -2)


def _unit_lower_inverse(lm, c):
    bs = min(SUB, c)
    r, s = _incl_lower(c)
    eye = jnp.where(r == s, 1.0, 0.0)
    shift = bs.bit_length() - 1
    same = jnp.right_shift(r, shift) == jnp.right_shift(s, shift)

    def neumann(n, order):
        inv, pw, k = eye + n, n, 2
        while k < order:
            pw = _dot_mid(pw, pw)
            inv = inv + _dot_mid(inv, pw)
            k *= 2
        return inv

    inv_d = neumann(-jnp.where(same, lm, 0.0), bs)
    if c == bs:
        return inv_d
    m = _dot_mid(inv_d, jnp.where(same, 0.0, lm))
    return _dot_mid(neumann(-m, c // bs), inv_d)


def _gdn_kernel(x_ref, ab_ref, g_ref, cw_ref, al_ref, dtb_ref, gn_ref, s0_ref, cb_ref, o_ref, s_ref, nb_ref,
                xx_ref, y_ref, st_ref, *, c, n_chunks, n_heads, lblk):
    l = pl.program_id(1)
    dk = LANES
    halo = 8
    w = n_heads * dk

    @pl.when(l == 0)
    def _():
        st_ref[...] = s0_ref[0]
        xx_ref[halo - (CONV_W - 1):halo, :] = cb_ref[0]

    xx_ref[halo:halo + lblk, :] = x_ref[...]
    y = xx_ref[halo:halo + lblk, :] * cw_ref[CONV_W - 1:CONV_W, :]
    for i in range(CONV_W - 1):
        off = halo - (CONV_W - 1) + i
        y = y + xx_ref[off:off + lblk, :] * cw_ref[i:i + 1, :]
    y_ref[...] = jax.nn.silu(y)
    tail = xx_ref[halo + lblk - (CONV_W - 1):halo + lblk, :]
    xx_ref[halo - (CONV_W - 1):halo, :] = tail

    r, s = _incl_lower(c)
    tri = jnp.where(s <= r, 1.0, 0.0).astype(BF16)
    gn = gn_ref[...]

    def l2n(t):
        return t * lax.rsqrt(jnp.sum(t * t, axis=-1, keepdims=True) + RMS_EPS)

    def chunk(n, carry):
        rows = pl.ds(pl.multiple_of(n * c, c), c)
        ab = ab_ref[rows, :]
        pre = ab + dtb_ref[...]
        log_a = -jnp.exp(al_ref[...]) * (jnp.maximum(pre, 0.0) + jnp.log(1.0 + jnp.exp(-jnp.abs(pre))))
        beta_all = jax.nn.sigmoid(ab)
        cg_all = _dot_exact_lhs(tri, log_a)
        cg_t = cg_all.T
        for h in range(n_heads):
            q = l2n(y_ref[rows, h * dk:(h + 1) * dk]) * dk ** -0.5
            k = l2n(y_ref[rows, w + h * dk:w + (h + 1) * dk])
            v = y_ref[rows, 2 * w + h * dk:2 * w + (h + 1) * dk]
            cg = cg_all[:, h:h + 1]
            beta = beta_all[:, n_heads + h:n_heads + h + 1]
            decay = jnp.exp(jnp.where(s <= r, cg - cg_t[h:h + 1, :c], NEG))
            kk = _dot_mid(k, k, dot=_dot_nt) * decay * beta
            t_inv = _unit_lower_inverse(jnp.where(s < r, kk, 0.0), c)
            e_cg = jnp.exp(cg)
            sol_v = _dot_mid(t_inv, v * beta)
            sol_k = _dot_mid(t_inv, k * (beta * e_cg))
            st = st_ref[h]
            stb = st.astype(BF16)
            u = sol_v - _dot(sol_k.astype(BF16), stb)
            qk = _dot_nt(q.astype(BF16), k.astype(BF16)) * decay
            o = _dot((q * e_cg).astype(BF16), stb) + _dot(qk.astype(BF16), u.astype(BF16))
            last = cg[c - 1:c]
            kt = k * jnp.exp(last - cg)
            st_ref[h] = st * jnp.exp(last) + _dot_tn(kt.astype(BF16), u.astype(BF16))
            o = o * lax.rsqrt(jnp.mean(o * o, axis=-1, keepdims=True) + RMS_EPS) * gn
            o_ref[rows, h * dk:(h + 1) * dk] = o * jax.nn.silu(g_ref[rows, h * dk:(h + 1) * dk])
        return carry

    lax.fori_loop(0, n_chunks, chunk, 0)

    @pl.when(l == pl.num_programs(1) - 1)
    def _():
        s_ref[0] = st_ref[...]
        nb_ref[0] = tail


def _gdn(qkv, ab, g, conv_w, a_log, dt_bias, gn, s0, conv_buf, *, batch):
    t, width = qkv.shape
    seq = t // batch
    w = width // 3
    n_heads = w // LANES
    c = min(CHUNK, seq)
    lblk = min(SEQ_BLOCK, seq)
    assert seq % lblk == 0 and lblk % c == 0 and lblk >= CONV_W - 1
    nl = seq // lblk
    pad_lane = lambda vec: jnp.pad(vec, (0, LANES - vec.shape[0])).reshape(1, LANES)
    rows = lambda width_: pl.BlockSpec((lblk, width_), lambda b, l: (b * nl + l, 0))
    const = lambda shape: pl.BlockSpec(shape, lambda b, l: (0,) * len(shape))
    state = pl.BlockSpec((1, n_heads, LANES, LANES), lambda b, l: (b, 0, 0, 0))
    buf = pl.BlockSpec((1, CONV_W - 1, width), lambda b, l: (b, 0, 0))
    return pl.pallas_call(
        functools.partial(_gdn_kernel, c=c, n_chunks=lblk // c, n_heads=n_heads, lblk=lblk),
        grid=(batch, nl),
        in_specs=[rows(width), rows(LANES), rows(w), const((CONV_W, width)), const((1, LANES)), const((1, LANES)),
                  const((1, LANES)), state, buf],
        out_specs=[rows(w), state, buf],
        out_shape=[jax.ShapeDtypeStruct((t, w), F32), jax.ShapeDtypeStruct(s0.shape, F32),
                   jax.ShapeDtypeStruct(conv_buf.shape, F32)],
        scratch_shapes=[pltpu.VMEM((lblk + 8, width), F32), pltpu.VMEM((lblk, width), F32),
                        pltpu.VMEM((n_heads, LANES, LANES), F32)],
        compiler_params=_cparams(("parallel", "arbitrary")),
        name="gated_deltanet",
    )(qkv, ab, g, conv_w, pad_lane(a_log), pad_lane(dt_bias), gn.reshape(1, LANES), s0, conv_buf)


PAGES_PER_STEP = 4


def _head_fold(pv, n_heads, nq, hd):
    return jnp.concatenate([pv[h * nq:(h + 1) * nq, h * hd:(h + 1) * hd] for h in range(n_heads)], axis=0)


def _row_ids(rows, nq):
    r = lax.broadcasted_iota(jnp.int32, (rows, 1), 0)
    return r // nq, r % nq


def _stick_paged_kernel(pt_ref, q_ref, qbd_ref, kn_ref, vn_ref, *rest, pps, n_heads, nq, hd):
    k_refs, v_refs = rest[:pps], rest[pps:2 * pps]
    o_ref, c_ref, acc_ref = rest[2 * pps:]
    s = pl.program_id(1)
    rows = n_heads * nq
    scale = hd ** -0.5
    width = k_refs[0].shape[2] * n_heads
    row_h, row_q = _row_ids(rows, nq)

    def strict_later(n):
        return jnp.where(lax.broadcasted_iota(jnp.int32, (n, n), 0) > lax.broadcasted_iota(jnp.int32, (n, n), 1),
                         1.0, 0.0).astype(BF16)

    def weights(z, mask, carry, u):
        sp = _softplus(z)
        log_keep = jnp.where(mask, -sp, 0.0)
        hi = log_keep.astype(BF16)
        lo = (log_keep - hi.astype(F32)).astype(BF16)
        later = _dot(hi, u) + _dot(lo, u)
        w = jnp.where(mask, jnp.exp(z - sp + later + carry), 0.0)
        return w.astype(BF16), carry + jnp.sum(log_keep, axis=-1, keepdims=True)

    @pl.when(s == 0)
    def _():
        n = kn_ref.shape[1]
        z = _dot_nt(qbd_ref[0].astype(BF16), kn_ref[0].astype(BF16)) * scale
        col = lax.broadcasted_iota(jnp.int32, (rows, n), 1)
        w, c = weights(z, col < row_q, jnp.zeros((rows, 1), F32), strict_later(n))
        c_ref[...] = c
        acc_ref[...] = _head_fold(_dot(w, vn_ref[0].astype(BF16)), n_heads, nq, hd)

    @pl.when(s > 0)
    def _():
        qb = q_ref[0].astype(BF16)
        grp = 2 * LANES
        u = strict_later(grp)
        match = lax.broadcasted_iota(jnp.int32, (rows, grp), 1) % n_heads == row_h
        c = c_ref[...]
        acc = acc_ref[...]
        for r in reversed(range(pps)):
            k2 = k_refs[r][0, 0].reshape(width, hd).astype(BF16)
            v2 = v_refs[r][0, 0].reshape(width, hd).astype(BF16)
            z = _dot_nt(qb, k2) * scale
            for g in reversed(range(width // grp)):
                w, c = weights(z[:, g * grp:(g + 1) * grp], match, c, u)
                acc = acc + _dot(w, v2[g * grp:(g + 1) * grp])
        c_ref[...] = c
        acc_ref[...] = acc

    @pl.when(s == pl.num_programs(1) - 1)
    def _():
        o_ref[0] = acc_ref[...]


def _moba_paged_kernel(pt_ref, q_ref, qbd_ref, kn_ref, vn_ref, *rest, pps, n_heads, nq, hd, past_len):
    k_refs, v_refs = rest[:pps], rest[pps:2 * pps]
    o_ref, mo_ref, lo_ref, acco_ref, m_ref, l_ref, acc_ref, km_ref = rest[2 * pps:]
    s = pl.program_id(1)
    rows = n_heads * nq
    scale = hd ** -0.5
    page = k_refs[0].shape[2]
    width = page * n_heads
    ppb = MOBA_BLOCK // page
    row_h, row_q = _row_ids(rows, nq)
    slope = jnp.exp2(-8.0 * (row_h + 1).astype(F32) / n_heads)
    lane = lax.broadcasted_iota(jnp.int32, (rows, LANES), 1)

    @pl.when(s == 0)
    def _():
        n = kn_ref.shape[1]
        col = lax.broadcasted_iota(jnp.int32, (rows, n), 1)
        sc = _dot_nt(qbd_ref[0].astype(BF16), kn_ref[0].astype(BF16)) * scale - slope * (row_q - col).astype(F32)
        sc = jnp.where(col <= row_q, sc, NEG)
        m = jnp.max(sc, axis=-1, keepdims=True)
        p = jnp.exp(sc - m)
        mo_ref[...] = m
        lo_ref[...] = jnp.sum(p, axis=-1, keepdims=True)
        acco_ref[...] = _head_fold(_dot(p.astype(BF16), vn_ref[0].astype(BF16)), n_heads, nq, hd)
        m_ref[...] = jnp.full(m_ref.shape, NEG, F32)
        l_ref[...] = jnp.zeros_like(l_ref)
        km_ref[...] = jnp.zeros_like(km_ref)

    @pl.when(s > 0)
    def _():
        qb = q_ref[0].astype(BF16)
        lane_w = lax.broadcasted_iota(jnp.int32, (rows, width), 1)
        match = lane_w % n_heads == row_h
        key_in_page = lane_w // n_heads
        m_all, l_all = m_ref[...], l_ref[...]
        for bi in range(pps // ppb):
            blk = (s - 1) * (pps // ppb) + bi
            ksum = jnp.zeros((n_heads, hd), F32)
            scs = []
            for r in range(ppb):
                kp = k_refs[bi * ppb + r][0, 0]
                ksum = ksum + jnp.sum(kp, axis=0)
                dist = past_len + row_q - (blk * MOBA_BLOCK + r * page + key_in_page)
                sc = _dot_nt(qb, kp.reshape(width, hd).astype(BF16)) * scale - slope * dist.astype(F32)
                scs.append(jnp.where(match, sc, NEG))
            m_b = functools.reduce(jnp.maximum, [jnp.max(sc, axis=-1, keepdims=True) for sc in scs])
            l_b = jnp.zeros((rows, 1), F32)
            acc_b = jnp.zeros((rows, hd), F32)
            for r in range(ppb):
                p = jnp.exp(scs[r] - m_b)
                l_b = l_b + jnp.sum(p, axis=-1, keepdims=True)
                acc_b = acc_b + _dot(p.astype(BF16), v_refs[bi * ppb + r][0, 0].reshape(width, hd).astype(BF16))
            m_all = jnp.where(lane == blk, m_b, m_all)
            l_all = jnp.where(lane == blk, l_b, l_all)
            acc_ref[blk] = acc_b
            kmean = ksum * (1.0 / MOBA_BLOCK)
            for h in range(n_heads):
                km_ref[h, pl.ds(blk, 1), :] = kmean[h:h + 1, :]
        m_ref[...] = m_all
        l_ref[...] = l_all

    @pl.when(s == pl.num_programs(1) - 1)
    def _():
        n_blocks = past_len // MOBA_BLOCK
        q = q_ref[0]
        gate = jnp.concatenate([_dot_hi(q[h * nq:(h + 1) * nq], km_ref[h], dot=_dot_nt) for h in range(n_heads)], axis=0)
        sel = _topk_lanes(jnp.where(lane < n_blocks, gate, -jnp.inf), lane) > 0.0
        m_all, l_all = m_ref[...], l_ref[...]
        m_o = mo_ref[...]
        m_tot = jnp.maximum(m_o, jnp.max(jnp.where(sel, m_all, NEG), axis=-1, keepdims=True))
        wgt = jnp.where(sel, jnp.exp(m_all - m_tot), 0.0)
        a_o = jnp.exp(m_o - m_tot)
        l_tot = lo_ref[...] * a_o + jnp.sum(wgt * l_all, axis=-1, keepdims=True)
        acc = acco_ref[...] * a_o
        for b in range(n_blocks):
            acc = acc + wgt[:, b:b + 1] * acc_ref[b]
        o_ref[0] = acc / l_tot


def _paged_attention(kind, q, k_new, v_new, k_pool, v_pool, page_table, *, n_heads):
    batch, n_pages = page_table.shape
    page, hd = k_pool.shape[2], k_pool.shape[4]
    nq = q.shape[0] // batch
    rows = n_heads * nq
    pps = PAGES_PER_STEP
    past_len = n_pages * page
    assert n_pages % pps == 0 and past_len % MOBA_BLOCK == 0 and MOBA_BLOCK % page == 0 and pps % (MOBA_BLOCK // page) == 0
    assert nq <= page and past_len // MOBA_BLOCK <= LANES and (2 * LANES) % n_heads == 0
    n_groups = n_pages // pps
    q4 = q.reshape(batch, nq, n_heads, hd).transpose(0, 2, 1, 3)
    q_cat = q4.reshape(batch, rows, hd)
    q_bd = (q4[:, :, :, None, :] * jnp.eye(n_heads, dtype=F32)[None, :, None, :, None]).reshape(batch, rows, n_heads * hd)
    pad_new = lambda t: jnp.pad(t.reshape(batch, nq, n_heads * hd), ((0, 0), (0, page - nq), (0, 0)))

    if kind == "stick":
        group = lambda s: n_groups - jnp.maximum(s, 1)
        body = functools.partial(_stick_paged_kernel, pps=pps, n_heads=n_heads, nq=nq, hd=hd)
        scratch = [pltpu.VMEM((rows, 1), F32), pltpu.VMEM((rows, hd), F32)]
    else:
        group = lambda s: jnp.maximum(s, 1) - 1
        body = functools.partial(_moba_paged_kernel, pps=pps, n_heads=n_heads, nq=nq, hd=hd, past_len=past_len)
        scratch = [pltpu.VMEM((rows, 1), F32), pltpu.VMEM((rows, 1), F32), pltpu.VMEM((rows, hd), F32),
                   pltpu.VMEM((rows, LANES), F32), pltpu.VMEM((rows, LANES), F32),
                   pltpu.VMEM((past_len // MOBA_BLOCK, rows, hd), F32), pltpu.VMEM((n_heads, LANES, hd), F32)]

    def page_spec(r):
        return pl.BlockSpec((1, 1, page, n_heads, hd), lambda b, s, pt: (0, pt[b, group(s) * pps + r], 0, 0, 0))

    per_seq = lambda shape: pl.BlockSpec((1,) + shape, lambda b, s, pt: (b, 0, 0))
    out = pl.pallas_call(
        body,
        grid_spec=pltpu.PrefetchScalarGridSpec(
            num_scalar_prefetch=1,
            grid=(batch, n_groups + 1),
            in_specs=[per_seq((rows, hd)), per_seq((rows, n_heads * hd)), per_seq((page, n_heads * hd)),
                      per_seq((page, n_heads * hd))] + [page_spec(r) for r in range(pps)] * 2,
            out_specs=per_seq((rows, hd)),
            scratch_shapes=scratch),
        out_shape=jax.ShapeDtypeStruct((batch, rows, hd), F32),
        compiler_params=_cparams(("parallel", "arbitrary")),
        name=kind + "_paged",
    )(page_table, q_cat, q_bd, pad_new(k_new), pad_new(v_new), *([k_pool] * pps), *([v_pool] * pps))
    return out.reshape(batch, n_heads, nq, hd).transpose(0, 2, 1, 3).reshape(batch * nq, n_heads * hd)


def _layer_stack(x, batch, past, states, w):
    s_hgrn, s_gdn, conv_buf = states
    mm = lambda a, wt: _matmul([a], [wt], tn=min(512, wt.shape[1]))

    def attend(kind, q, k, v, pools, n_heads):
        if past is None:
            return (_moba if kind == "moba" else _stick)(q, k, v, batch=batch, q_offset=0)
        return _paged_attention(kind, q, k, v, *pools, past[4], n_heads=n_heads)

    q_a, k_a, v_a = (mm(x, wt) for wt in w["in_a"])
    hg = mm(x, w["in_hgrn"])
    o_a = attend("moba", q_a, k_a, v_a, past and past[0:2], H_A)
    o_b, s_hgrn_new = _hgrn(hg, w["hgrn_lb"], w["hgrn_norm"], s_hgrn, batch=batch, layer=0)
    x = _matmul([o_a, o_b], w["out_even"], ln_args=(x, w["ln1_g"][0], w["ln1_b"][0]))
    x = _ffn(x, None, w["ffn_wg"], w["ffn_wu"], w["ffn_wd"], w["ln2_g"][0], w["ln2_b"][0])

    qkv_c, ab, g_c = mm(x, w["in_qkv_c"]), mm(x, w["in_ab"]), mm(x, w["in_g_c"])
    q_d, k_d, v_d = (mm(x, wt) for wt in w["in_d"])
    o_c, s_gdn_new, conv_new = _gdn(qkv_c, ab, g_c, w["conv_w"], w["a_log"], w["dt_bias"], w["gdn_norm"],
                                    s_gdn, conv_buf, batch=batch)
    o_d = attend("stick", q_d, k_d, v_d, past and past[2:4], H_D)
    x = _matmul([o_c, o_d], w["out_odd"], ln_args=(x, w["ln1_g"][1], w["ln1_b"][1]))
    comb = _router(x, w["router"])
    x = _ffn(x, comb, w["moe_wg"], w["moe_wu"], w["moe_wd"], w["ln2_g"][1], w["ln2_b"][1])
    return x, k_a, v_a, s_hgrn_new, s_gdn_new, conv_new, k_d, v_d


def kernel(x_prompt, x_sample, cache_k_moba, cache_v_moba, state_hgrn, state_gdn, state_gdn_conv, cache_k_sb,
           cache_v_sb, page_table, w_in_even, w_out_even, hgrn_lb, hgrn_norm, w_in_odd, w_out_odd, gdn_conv_w,
           gdn_a_log, gdn_dt_bias, gdn_norm, ln1_g, ln1_b, ln2_g, ln2_b, ffn_wg, ffn_wu, ffn_wd, router, moe_wg,
           moe_wu, moe_wd):
    assert w_in_even.shape[0] == 1 and w_in_odd.shape[0] == 1
    bp, lp, d = x_prompt.shape
    bs, ls, _ = x_sample.shape
    mix = d // 2
    hd_a, hd_d = mix // H_A, mix // H_D
    conv_dim = gdn_conv_w.shape[-1]
    bf = lambda t: t.astype(BF16)
    cols = lambda wt, lo, n: bf(wt[:, lo:lo + n])

    wie, wio = w_in_even[0], w_in_odd[0]
    g_lo = conv_dim + 2 * H_C
    d_lo = g_lo + mix
    w = {
        "in_a": [cols(wie, i * mix, mix) for i in range(3)],
        "in_hgrn": cols(wie, 3 * mix, 4 * mix),
        "out_even": [bf(w_out_even[0][:mix]), bf(w_out_even[0][mix:])],
        "hgrn_lb": hgrn_lb, "hgrn_norm": hgrn_norm[0],
        "in_qkv_c": cols(wio, 0, conv_dim),
        "in_ab": bf(jnp.pad(wio[:, conv_dim:g_lo], ((0, 0), (0, LANES - 2 * H_C)))),
        "in_g_c": cols(wio, g_lo, mix),
        "in_d": [cols(wio, d_lo + i * mix, mix) for i in range(3)],
        "out_odd": [bf(w_out_odd[0][:mix]), bf(w_out_odd[0][mix:])],
        "conv_w": gdn_conv_w[0], "a_log": gdn_a_log[0], "dt_bias": gdn_dt_bias[0], "gdn_norm": gdn_norm[0],
        "ln1_g": ln1_g, "ln1_b": ln1_b, "ln2_g": ln2_g, "ln2_b": ln2_b,
        "ffn_wg": bf(ffn_wg), "ffn_wu": bf(ffn_wu), "ffn_wd": bf(ffn_wd),
        "router": router[0], "moe_wg": bf(moe_wg[0]), "moe_wu": bf(moe_wu[0]), "moe_wd": bf(moe_wd[0]),
    }

    def run(x, batch, past, states):
        seq = x.shape[1]
        y, k_a, v_a, s_h, s_g, cv, k_d, v_d = _layer_stack(x.reshape(batch * seq, d), batch, past, states, w)
        return (y.reshape(batch, seq, d), k_a.reshape(1, batch, seq, H_A, hd_a), v_a.reshape(1, batch, seq, H_A, hd_a),
                s_h[None], s_g[None], cv[None], k_d.reshape(1, batch, seq, H_D, hd_d),
                v_d.reshape(1, batch, seq, H_D, hd_d))

    zeros_p = (jnp.zeros((bp,) + state_hgrn.shape[2:], F32), jnp.zeros((bp,) + state_gdn.shape[2:], F32),
               jnp.zeros((bp,) + state_gdn_conv.shape[2:], F32))
    out_p = run(x_prompt, bp, None, zeros_p)
    past = (cache_k_moba, cache_v_moba, cache_k_sb, cache_v_sb, page_table)
    out_s = run(x_sample, bs, past, (state_hgrn[0], state_gdn[0], state_gdn_conv[0]))
    return (out_p[0], out_s[0]) + out_p[1:] + out_s[1:]
```

```python
import functools

import jax
import jax.numpy as jnp
from jax import lax
from jax.experimental import pallas as pl
from jax.experimental.pallas import tpu as pltpu

F32 = jnp.float32
BF16 = jnp.bfloat16

H_A, H_B, H_C, H_D = 8, 4, 4, 8
MOBA_BLOCK = 256
MOBA_TOPK = 3
CONV_W = 4
N_EXPERTS = 8
DEPTH = 2
DEEPNORM_ALPHA = (2 * DEPTH) ** 0.25
LN_EPS = 1e-5
RMS_EPS = 1e-6
NEG = -1e30
LANES = 128
VMEM_LIMIT = 56 * 1024 * 1024


def _cparams(sem):
    return pltpu.CompilerParams(dimension_semantics=sem, vmem_limit_bytes=VMEM_LIMIT)


def _layernorm(y, g, b):
    mu = jnp.mean(y, axis=-1, keepdims=True)
    yc = y - mu
    var = jnp.mean(yc * yc, axis=-1, keepdims=True)
    return yc * lax.rsqrt(var + LN_EPS) * g + b


def _split3(x):
    hi = x.astype(BF16)
    r = x - hi.astype(F32)
    mid = r.astype(BF16)
    lo = (r - mid.astype(F32)).astype(BF16)
    return hi, mid, lo


def _dot(a, b):
    return jnp.dot(a, b, preferred_element_type=F32)


def _dot_nt(a, b):
    return lax.dot_general(a, b, (((1,), (1,)), ((), ())), preferred_element_type=F32)


def _dot_tn(a, b):
    return lax.dot_general(a, b, (((0,), (0,)), ((), ())), preferred_element_type=F32)


def _dot_hi(a, b, dot=_dot):
    a0, a1, a2 = _split3(a)
    b0, b1, b2 = _split3(b)
    return (dot(a0, b0) + (dot(a0, b1) + dot(a1, b0))
            + (dot(a1, b1) + dot(a0, b2) + dot(a2, b0)))


def _dot_exact_rhs(a, b_bf16):
    a0, a1, a2 = _split3(a)
    return _dot(a0, b_bf16) + _dot(a1, b_bf16) + _dot(a2, b_bf16)


def _dot_exact_lhs(a_bf16, b):
    b0, b1, b2 = _split3(b)
    return _dot(a_bf16, b0) + _dot(a_bf16, b1) + _dot(a_bf16, b2)


def _mm_kernel(*refs, n_in, ln):
    a_refs, w_refs, rest = refs[:n_in], refs[n_in:2 * n_in], refs[2 * n_in:]
    acc = None
    for a, w in zip(a_refs, w_refs):
        d = _dot(a[...].astype(BF16), w[...])
        acc = d if acc is None else acc + d
    if ln:
        res_ref, g_ref, b_ref, o_ref = rest
        o_ref[...] = _layernorm(DEEPNORM_ALPHA * res_ref[...] + acc, g_ref[...], b_ref[...])
    else:
        (o_ref,) = rest
        o_ref[...] = acc


def _row_tile(m):
    for t in (512, 256, 128, 64, 32, 16, 8):
        if m % t == 0:
            return t
    raise ValueError(m)


def _matmul(a_list, w_list, *, tn=None, ln_args=None):
    m = a_list[0].shape[0]
    n = w_list[0].shape[1]
    tm = _row_tile(m)
    ln = ln_args is not None
    tn = n if (ln or tn is None) else tn
    assert n % tn == 0
    in_specs = [pl.BlockSpec((tm, a.shape[1]), lambda i, j: (i, 0)) for a in a_list]
    in_specs += [pl.BlockSpec((w.shape[0], tn), lambda i, j: (0, j)) for w in w_list]
    args = list(a_list) + list(w_list)
    if ln:
        res, g, b = ln_args
        in_specs += [pl.BlockSpec((tm, n), lambda i, j: (i, 0)),
                     pl.BlockSpec((1, n), lambda i, j: (0, 0)),
                     pl.BlockSpec((1, n), lambda i, j: (0, 0))]
        args += [res, g.reshape(1, n), b.reshape(1, n)]
    return pl.pallas_call(
        functools.partial(_mm_kernel, n_in=len(a_list), ln=ln),
        grid=(m // tm, n // tn),
        in_specs=in_specs,
        out_specs=pl.BlockSpec((tm, tn), lambda i, j: (i, j)),
        out_shape=jax.ShapeDtypeStruct((m, n), F32),
        compiler_params=_cparams(("parallel", "arbitrary")),
        name="matmul_ln" if ln else "matmul",
    )(*args)


def _ffn_kernel(*refs, use_comb):
    if use_comb:
        x_ref, comb_ref, wg_ref, wu_ref, wd_ref, g_ref, b_ref, o_ref, xb_ref, acc_ref = refs
    else:
        x_ref, wg_ref, wu_ref, wd_ref, g_ref, b_ref, o_ref, xb_ref, acc_ref = refs
    e, j = pl.program_id(1), pl.program_id(2)
    first = jnp.logical_and(e == 0, j == 0)
    last = jnp.logical_and(e == pl.num_programs(1) - 1, j == pl.num_programs(2) - 1)

    @pl.when(first)
    def _():
        xb_ref[...] = x_ref[...].astype(BF16)
        acc_ref[...] = jnp.zeros_like(acc_ref)

    xb = xb_ref[...]
    h = jax.nn.silu(_dot(xb, wg_ref[0])) * _dot(xb, wu_ref[0])
    if use_comb:
        lane = lax.broadcasted_iota(jnp.int32, comb_ref.shape, 1)
        h = h * jnp.sum(jnp.where(lane == e, comb_ref[...], 0.0), axis=-1, keepdims=True)
    acc_ref[...] += _dot(h.astype(BF16), wd_ref[0])

    @pl.when(last)
    def _():
        o_ref[...] = _layernorm(DEEPNORM_ALPHA * x_ref[...] + acc_ref[...], g_ref[...], b_ref[...])


def _ff_tile(ff):
    for t in (896, 512, 256, LANES):
        if ff % t == 0:
            return t
    raise ValueError(ff)


def _ffn(x, comb, wg, wu, wd, g, b):
    m, d = x.shape
    n_e, _, ff = wg.shape
    tm = _row_tile(m)
    tf = _ff_tile(ff)
    use_comb = comb is not None
    in_specs = [pl.BlockSpec((tm, d), lambda i, e, j: (i, 0))]
    args = [x]
    if use_comb:
        in_specs.append(pl.BlockSpec((tm, comb.shape[1]), lambda i, e, j: (i, 0)))
        args.append(comb)
    in_specs += [pl.BlockSpec((1, d, tf), lambda i, e, j: (e, 0, j)),
                 pl.BlockSpec((1, d, tf), lambda i, e, j: (e, 0, j)),
                 pl.BlockSpec((1, tf, d), lambda i, e, j: (e, j, 0)),
                 pl.BlockSpec((1, d), lambda i, e, j: (0, 0)),
                 pl.BlockSpec((1, d), lambda i, e, j: (0, 0))]
    args += [wg, wu, wd, g.reshape(1, d), b.reshape(1, d)]
    return pl.pallas_call(
        functools.partial(_ffn_kernel, use_comb=use_comb),
        grid=(m // tm, n_e, ff // tf),
        in_specs=in_specs,
        out_specs=pl.BlockSpec((tm, d), lambda i, e, j: (i, 0)),
        out_shape=jax.ShapeDtypeStruct((m, d), F32),
        scratch_shapes=[pltpu.VMEM((tm, d), BF16), pltpu.VMEM((tm, d), F32)],
        compiler_params=_cparams(("parallel", "arbitrary", "arbitrary")),
        name="moe_ffn" if use_comb else "ffn",
    )(*args)


MOE_TOKENS = 1024
MOE_ROWS = 128


def _moe_kernel(x_ref, comb_ref, wg_ref, wu_ref, wd_ref, g_ref, b_ref, o_ref,
                xb_ref, acc_ref, xc_ref, yc_ref, rank_ref, rank_t_ref, comb_t_ref):
    e, j = pl.program_id(1), pl.program_id(2)
    tm, d = x_ref.shape
    r = MOE_ROWS
    lane = lax.broadcasted_iota(jnp.int32, (tm, LANES), 1)

    @pl.when(jnp.logical_and(e == 0, j == 0))
    def _():
        xb_ref[...] = x_ref[...].astype(BF16)
        acc_ref[...] = jnp.zeros_like(acc_ref)
        routed = jnp.where(comb_ref[...] > 0.0, 1.0, 0.0).astype(BF16)
        earlier = jnp.where(lax.broadcasted_iota(jnp.int32, (tm, tm), 1) < lax.broadcasted_iota(jnp.int32, (tm, tm), 0),
                            1.0, 0.0).astype(BF16)
        rank = _dot(earlier, routed)
        rank_ref[...] = rank
        rank_t_ref[...] = rank.T
        comb_t_ref[...] = comb_ref[...].T

    gate_row = comb_t_ref[pl.ds(e, 1), :]
    n_pass = (jnp.sum(jnp.where(gate_row > 0.0, 1, 0)) + (r - 1)) // r

    @pl.when(j == 0)
    def _():
        rank_row = rank_t_ref[pl.ds(e, 1), :]

        def compact(k, carry):
            slot = (k * r + lax.broadcasted_iota(jnp.int32, (r, tm), 0)).astype(F32)
            pick = jnp.where(jnp.logical_and(rank_row == slot, gate_row > 0.0), 1.0, 0.0).astype(BF16)
            rows = pl.ds(pl.multiple_of(k * r, r), r)
            xc_ref[rows, :] = _dot(pick, xb_ref[...]).astype(BF16)
            yc_ref[rows, :] = jnp.zeros((r, d), F32)
            return carry

        lax.fori_loop(0, n_pass, compact, 0)

    def expert(start, n_rows):
        rows = pl.ds(pl.multiple_of(start, r), n_rows)
        xk = xc_ref[rows, :]
        h = jax.nn.silu(_dot(xk, wg_ref[0])) * _dot(xk, wu_ref[0])
        yc_ref[rows, :] += _dot(h.astype(BF16), wd_ref[0])

    def expert_pair(k, carry):
        expert(k * (2 * r), 2 * r)
        return carry

    lax.fori_loop(0, n_pass // 2, expert_pair, 0)

    @pl.when(n_pass % 2 == 1)
    def _():
        expert((n_pass - 1) * r, r)

    @pl.when(j == pl.num_programs(2) - 1)
    def _():
        rank_col = jnp.sum(jnp.where(lane == e, rank_ref[...], 0.0), axis=-1, keepdims=True)
        gate_col = jnp.sum(jnp.where(lane == e, comb_ref[...], 0.0), axis=-1, keepdims=True)

        def place(k, carry):
            slot = (k * r + lax.broadcasted_iota(jnp.int32, (tm, r), 1)).astype(F32)
            put = jnp.where(jnp.logical_and(rank_col == slot, gate_col > 0.0), 1.0, 0.0).astype(BF16)
            rows = pl.ds(pl.multiple_of(k * r, r), r)
            acc_ref[...] += gate_col * _dot(put, yc_ref[rows, :].astype(BF16))
            return carry

        lax.fori_loop(0, n_pass, place, 0)

    @pl.when(jnp.logical_and(e == pl.num_programs(1) - 1, j == pl.num_programs(2) - 1))
    def _():
        o_ref[...] = _layernorm(DEEPNORM_ALPHA * x_ref[...] + acc_ref[...], g_ref[...], b_ref[...])


def _moe(x, comb, wg, wu, wd, g, b):
    m, d = x.shape
    n_e, _, ff = wg.shape
    tm = MOE_TOKENS
    if m % tm != 0:
        return _ffn(x, comb, wg, wu, wd, g, b)
    tf = _ff_tile(ff)
    assert tm % (2 * MOE_ROWS) == 0 and comb.shape[1] == LANES and n_e <= LANES
    return pl.pallas_call(
        _moe_kernel,
        grid=(m // tm, n_e, ff // tf),
        in_specs=[pl.BlockSpec((tm, d), lambda i, e, j: (i, 0)),
                  pl.BlockSpec((tm, LANES), lambda i, e, j: (i, 0)),
                  pl.BlockSpec((1, d, tf), lambda i, e, j: (e, 0, j)),
                  pl.BlockSpec((1, d, tf), lambda i, e, j: (e, 0, j)),
                  pl.BlockSpec((1, tf, d), lambda i, e, j: (e, j, 0)),
                  pl.BlockSpec((1, d), lambda i, e, j: (0, 0)),
                  pl.BlockSpec((1, d), lambda i, e, j: (0, 0))],
        out_specs=pl.BlockSpec((tm, d), lambda i, e, j: (i, 0)),
        out_shape=jax.ShapeDtypeStruct((m, d), F32),
        scratch_shapes=[pltpu.VMEM((tm, d), BF16), pltpu.VMEM((tm, d), F32), pltpu.VMEM((tm, d), BF16),
                        pltpu.VMEM((tm, d), F32), pltpu.VMEM((tm, LANES), F32), pltpu.VMEM((LANES, tm), F32),
                        pltpu.VMEM((LANES, tm), F32)],
        compiler_params=_cparams(("parallel", "arbitrary", "arbitrary")),
        name="moe_sparse",
    )(x, comb, wg, wu, wd, g.reshape(1, d), b.reshape(1, d))


def _router_kernel(x_ref, r_ref, o_ref):
    logits = _dot_hi(x_ref[...], r_ref[...])
    lane = lax.broadcasted_iota(jnp.int32, logits.shape, 1)
    logits = jnp.where(lane < N_EXPERTS, logits, -jnp.inf)
    m1 = jnp.max(logits, axis=-1, keepdims=True)
    i1 = jnp.min(jnp.where(logits == m1, lane, LANES), axis=-1, keepdims=True)
    rest = jnp.where(lane == i1, -jnp.inf, logits)
    m2 = jnp.max(rest, axis=-1, keepdims=True)
    i2 = jnp.min(jnp.where(rest == m2, lane, LANES), axis=-1, keepdims=True)
    e2 = jnp.exp(m2 - m1)
    den = 1.0 + e2
    o_ref[...] = jnp.where(lane == i1, 1.0 / den, 0.0) + jnp.where(lane == i2, e2 / den, 0.0)


def _router(x, router):
    m, d = x.shape
    tm = _row_tile(m)
    r = jnp.pad(router, ((0, 0), (0, LANES - router.shape[1])))
    return pl.pallas_call(
        _router_kernel,
        grid=(m // tm,),
        in_specs=[pl.BlockSpec((tm, d), lambda i: (i, 0)), pl.BlockSpec((d, LANES), lambda i: (0, 0))],
        out_specs=pl.BlockSpec((tm, LANES), lambda i: (i, 0)),
        out_shape=jax.ShapeDtypeStruct((m, LANES), F32),
        compiler_params=_cparams(("parallel",)),
        name="router",
    )(x, r)


def _topk_lanes(g, idx, axis=-1):
    sel = jnp.zeros(g.shape, F32)
    for _ in range(MOBA_TOPK):
        m = jnp.max(g, axis=axis, keepdims=True)
        first = jnp.min(jnp.where(g == m, idx, LANES), axis=axis, keepdims=True)
        hit = jnp.logical_and(idx == first, m > -jnp.inf)
        sel = jnp.where(hit, 1.0, sel)
        g = jnp.where(hit, -jnp.inf, g)
    return sel


def _moba_kernel(slope_ref, q_ref, k_ref, v_ref, o_ref, kb_ref, vt_ref, km_ref, m_ref, l_ref, acc_ref, sel_ref,
                 *, tq, nb, q_offset, hd):
    i = pl.program_id(2)
    blk = MOBA_BLOCK

    @pl.when(i == 0)
    def _():
        kb_ref[...] = k_ref[...].astype(BF16)
        km_ref[...] = jnp.zeros_like(km_ref)
        for j in range(nb):
            rows = slice(j * blk, (j + 1) * blk)
            vt_ref[j] = v_ref[rows, :].T.astype(BF16)
            km_ref[j:j + 1, :] = jnp.sum(k_ref[rows, :], axis=0, keepdims=True) * (1.0 / blk)

    q0 = q_offset + i * tq
    own = q0 // blk
    q = q_ref[...]
    lane = lax.broadcasted_iota(jnp.int32, (tq, LANES), 1)
    q2 = jnp.concatenate([jnp.where(lane < hd, q, 0.0), jnp.where(lane >= hd, q, 0.0)], axis=0)
    gate_t = _dot_hi(km_ref[...], q2, dot=_dot_nt)
    blk_id = lax.broadcasted_iota(jnp.int32, (LANES, 2 * tq), 0)
    sel_ref[...] = _topk_lanes(jnp.where(blk_id < own, gate_t, -jnp.inf), blk_id, axis=0)
    qb = (q2 * hd ** -0.5).astype(BF16)
    c2 = lax.broadcasted_iota(jnp.int32, (blk, 2 * tq), 1)
    d0 = (jnp.where(c2 >= tq, c2 - tq, c2) - lax.broadcasted_iota(jnp.int32, (blk, 2 * tq), 0)).astype(F32)
    slope_lane = slope_ref[...]
    c1 = lax.broadcasted_iota(jnp.int32, (1, 2 * tq), 1)
    slope = jnp.where(c1 < tq, slope_lane[:, 0:1], slope_lane[:, hd:hd + 1])
    slope_d0 = slope * d0
    first = lax.broadcasted_iota(jnp.int32, (2 * hd, tq), 0) < hd
    m_ref[...] = jnp.full(m_ref.shape, NEG, F32)
    l_ref[...] = jnp.zeros_like(l_ref)
    acc_ref[...] = jnp.zeros_like(acc_ref)

    def tile(j, diagonal):
        off = (q0 - j * blk).astype(F32)
        s = _dot_nt(kb_ref[pl.ds(pl.multiple_of(j * blk, blk), blk), :], qb) - (slope_d0 + slope * off)
        if diagonal:
            s = jnp.where(d0 + off >= 0.0, s, NEG)
        else:
            s = jnp.where(sel_ref[pl.ds(j, 1), :] > 0.0, s, NEG)
        m_old = m_ref[...]
        m_new = jnp.maximum(m_old, jnp.max(s, axis=0, keepdims=True))
        alpha = jnp.exp(m_old - m_new)
        p = jnp.exp(s - m_new)
        m_ref[...] = m_new
        l_ref[...] = alpha * l_ref[...] + jnp.sum(p, axis=0, keepdims=True)
        pv = _dot(vt_ref[j], p.astype(BF16))
        acc_ref[...] = (jnp.where(first, alpha[:, :tq], alpha[:, tq:]) * acc_ref[...]
                        + jnp.where(first, pv[:, :tq], pv[:, tq:]))

    tile(own, True)

    def body(t, carry):
        tile(own - t, False)
        return carry

    lax.fori_loop(1, own + 1, body, 0)
    l = l_ref[...]
    o_ref[...] = (acc_ref[...] / jnp.where(first, l[:, :tq], l[:, tq:])).T


def _alibi_slopes_lanes(n_heads, hd):
    slopes = jnp.exp2(-8.0 * jnp.arange(1, n_heads + 1, dtype=F32) / n_heads)
    return jnp.repeat(slopes, hd).reshape(1, n_heads * hd)


def _attn_call(kernel, q, k, v, *, batch, q_offset, n_heads, extra_in=(), extra_specs=(), extra_scratch=(), name):
    lq, lk = q.shape[0] // batch, k.shape[0] // batch
    hd = q.shape[1] // n_heads
    assert 2 * hd == LANES and lk % MOBA_BLOCK == 0
    tq = min(MOBA_BLOCK, lq)
    assert lq % tq == 0 and MOBA_BLOCK % tq == 0 and q_offset % tq == 0
    nq = lq // tq
    return pl.pallas_call(
        functools.partial(kernel, tq=tq, q_offset=q_offset, hd=hd),
        grid=(batch, n_heads // 2, nq),
        in_specs=list(extra_specs) + [
            pl.BlockSpec((tq, LANES), lambda b, h, i: (b * nq + i, h)),
            pl.BlockSpec((lk, LANES), lambda b, h, i: (b, h)),
            pl.BlockSpec((lk, LANES), lambda b, h, i: (b, h))],
        out_specs=pl.BlockSpec((tq, LANES), lambda b, h, i: (b * nq + i, h)),
        out_shape=jax.ShapeDtypeStruct(q.shape, F32),
        scratch_shapes=list(extra_scratch),
        compiler_params=_cparams(("parallel", "parallel", "arbitrary")),
        name=name,
    )(*extra_in, q, k, v)


def _moba(q, k, v, *, batch, q_offset):
    nb = k.shape[0] // batch // MOBA_BLOCK
    assert nb <= LANES
    slopes = _alibi_slopes_lanes(H_A, q.shape[1] // H_A)
    tq = min(MOBA_BLOCK, q.shape[0] // batch)
    return _attn_call(
        functools.partial(_moba_kernel, nb=nb), q, k, v, batch=batch, q_offset=q_offset, n_heads=H_A,
        extra_in=(slopes,), extra_specs=(pl.BlockSpec((1, LANES), lambda b, h, i: (0, h)),),
        extra_scratch=(pltpu.VMEM((nb * MOBA_BLOCK, LANES), BF16), pltpu.VMEM((nb, LANES, MOBA_BLOCK), BF16),
                       pltpu.VMEM((LANES, LANES), F32), pltpu.VMEM((1, 2 * tq), F32), pltpu.VMEM((1, 2 * tq), F32),
                       pltpu.VMEM((LANES, tq), F32), pltpu.VMEM((LANES, 2 * tq), F32)), name="moba")


def _softplus(z):
    return jnp.maximum(z, 0.0) + jnp.log(1.0 + jnp.exp(-jnp.abs(z)))


def _stick_kernel(q_ref, k_ref, v_ref, o_ref, kb_ref, vb_ref, c_ref, acc_ref, *, tq, q_offset, hd):
    i = pl.program_id(2)
    blk = MOBA_BLOCK

    @pl.when(i == 0)
    def _():
        kb_ref[...] = k_ref[...].astype(BF16)
        vb_ref[...] = v_ref[...].astype(BF16)

    q0 = q_offset + i * tq
    own = q0 // blk
    q = q_ref[...] * hd ** -0.5
    lane = lax.broadcasted_iota(jnp.int32, (tq, LANES), 1)
    qb = jnp.concatenate([jnp.where(lane < hd, q, 0.0), jnp.where(lane >= hd, q, 0.0)], axis=0).astype(BF16)
    r2 = lax.broadcasted_iota(jnp.int32, (2 * tq, blk), 0)
    d0 = jnp.where(r2 >= tq, r2 - tq, r2) - lax.broadcasted_iota(jnp.int32, (2 * tq, blk), 1)
    u = jnp.where(lax.broadcasted_iota(jnp.int32, (blk, blk), 0) > lax.broadcasted_iota(jnp.int32, (blk, blk), 1),
                  1.0, 0.0).astype(BF16)
    c_ref[...] = jnp.zeros_like(c_ref)
    acc_ref[...] = jnp.zeros_like(acc_ref)

    def tile(j, diagonal):
        start = pl.multiple_of(j * blk, blk)
        z = _dot_nt(qb, kb_ref[pl.ds(start, blk), :])
        log_keep = -_softplus(z)
        log_beta = z + log_keep
        if diagonal:
            before = d0 + (q0 - j * blk) > 0
            log_keep = jnp.where(before, log_keep, 0.0)
        hi = log_keep.astype(BF16)
        lo = (log_keep - hi.astype(F32)).astype(BF16)
        later = _dot(jnp.concatenate([hi, lo], axis=0), u)
        later = later[:2 * tq] + later[2 * tq:]
        c = c_ref[...]
        w = jnp.exp(log_beta + later + c)
        if diagonal:
            w = jnp.where(before, w, 0.0)
        c_ref[...] = c + later[:, 0:1] + log_keep[:, 0:1]
        pv = _dot(w.astype(BF16), vb_ref[pl.ds(start, blk), :])
        acc_ref[...] += jnp.where(lane < hd, pv[:tq], pv[tq:])

    tile(own, True)

    def body(t, carry):
        tile(own - t, False)
        return carry

    lax.fori_loop(1, own + 1, body, 0)
    o_ref[...] = acc_ref[...]


def _stick(q, k, v, *, batch, q_offset):
    tq = min(MOBA_BLOCK, q.shape[0] // batch)
    lk = k.shape[0] // batch
    return _attn_call(_stick_kernel, q, k, v, batch=batch, q_offset=q_offset, n_heads=H_D,
                      extra_scratch=(pltpu.VMEM((lk, LANES), BF16), pltpu.VMEM((lk, LANES), BF16),
                                     pltpu.VMEM((2 * tq, 1), F32), pltpu.VMEM((tq, LANES), F32)), name="stick")


CHUNK = 64
SUB = 16
SEQ_BLOCK = 512
EXP_CLAMP = 80.0


def _dot_mid(a, b, dot=_dot):
    a0 = a.astype(BF16)
    a1 = (a - a0.astype(F32)).astype(BF16)
    b0 = b.astype(BF16)
    b1 = (b - b0.astype(F32)).astype(BF16)
    return dot(a0, b0) + (dot(a0, b1) + dot(a1, b0))


def _incl_lower(c):
    r = lax.broadcasted_iota(jnp.int32, (c, c), 0)
    s = lax.broadcasted_iota(jnp.int32, (c, c), 1)
    return r, s


def _hgrn_kernel(q_ref, f_ref, i_ref, g_ref, lb_ref, gn_ref, s0_ref, o_ref, s_ref, st_ref, *, c, n_chunks, n_heads,
                 layer):
    l = pl.program_id(1)
    dk = LANES

    @pl.when(l == 0)
    def _():
        st_ref[...] = s0_ref[0]

    r, s = _incl_lower(c)
    tri = jnp.where(s <= r, 1.0, 0.0).astype(BF16)
    sub = min(SUB, c)
    r_sub = lax.broadcasted_iota(jnp.int32, (sub, c), 0)
    s_sub = lax.broadcasted_iota(jnp.int32, (sub, c), 1)
    gn = gn_ref[...]
    e = jnp.exp(lb_ref[...] - jnp.max(lb_ref[...], axis=0, keepdims=True))
    lb_all = jnp.sum(e[:layer + 1], axis=0, keepdims=True) / jnp.sum(e, axis=0, keepdims=True)

    def chunk(n, carry):
        rows = pl.ds(pl.multiple_of(n * c, c), c)
        for h in range(n_heads):
            cols = slice(h * dk, (h + 1) * dk)
            lb = lb_all[:, cols]
            z = f_ref[rows, cols]
            f = lb + (1.0 - lb) * jax.nn.sigmoid(z)
            k = (1.0 - lb) * jax.nn.sigmoid(-z)
            q = jax.nn.silu(q_ref[rows, cols])
            v = i_ref[rows, cols]
            cg = _dot_exact_lhs(tri, jnp.log(f))
            vb = v.astype(BF16)
            st = st_ref[h]
            intra = []
            for b in range(c // sub):
                lo, hi = b * sub, (b + 1) * sub
                ref_pt = cg[lo - 1:lo] if b else jnp.zeros((1, dk), F32)
                qs = q[lo:hi] * jnp.exp(cg[lo:hi] - ref_pt)
                ks = k * jnp.exp(jnp.minimum(ref_pt - cg, EXP_CLAMP))
                a = _dot_nt(qs.astype(BF16), ks.astype(BF16))
                a = jnp.where(s_sub <= r_sub + lo, a, 0.0)
                intra.append(_dot(a.astype(BF16), vb))
            o = _dot_nt((q * jnp.exp(cg)).astype(BF16), st.astype(BF16)) + jnp.concatenate(intra, axis=0)
            last = cg[c - 1:c]
            kt = k * jnp.exp(last - cg)
            st_ref[h] = st * jnp.exp(last) + _dot_tn(vb, kt.astype(BF16))
            o = o * lax.rsqrt(jnp.mean(o * o, axis=-1, keepdims=True) + RMS_EPS) * gn
            o_ref[rows, cols] = o * jax.nn.sigmoid(g_ref[rows, cols])
        return carry

    lax.fori_loop(0, n_chunks, chunk, 0)

    @pl.when(l == pl.num_programs(1) - 1)
    def _():
        s_ref[0] = st_ref[...]


def _hgrn(hg, lb, gn, s0, *, batch, layer=0):
    t, width = hg.shape
    seq = t // batch
    w = width // 4
    n_heads = w // LANES
    c = min(CHUNK, seq)
    lblk = min(SEQ_BLOCK, seq)
    assert seq % lblk == 0 and lblk % c == 0 and c % min(SUB, c) == 0
    nl = seq // lblk
    part = lambda p: pl.BlockSpec((lblk, w), lambda b, l: (b * nl + l, p))
    state = pl.BlockSpec((1, n_heads, LANES, LANES), lambda b, l: (b, 0, 0, 0))
    o, st = pl.pallas_call(
        functools.partial(_hgrn_kernel, c=c, n_chunks=lblk // c, n_heads=n_heads, layer=layer),
        grid=(batch, nl),
        in_specs=[part(0), part(1), part(2), part(3),
                  pl.BlockSpec(lb.shape, lambda b, l: (0, 0)), pl.BlockSpec((1, LANES), lambda b, l: (0, 0)), state],
        out_specs=[pl.BlockSpec((lblk, w), lambda b, l: (b * nl + l, 0)), state],
        out_shape=[jax.ShapeDtypeStruct((t, w), F32), jax.ShapeDtypeStruct(s0.shape, F32)],
        scratch_shapes=[pltpu.VMEM((n_heads, LANES, LANES), F32)],
        compiler_params=_cparams(("parallel", "arbitrary")),
        name="hgrn2",
    )(hg, hg, hg, hg, lb, gn.reshape(1, LANES), jnp.swapaxes(s0, -1, -2))
    return o, jnp.swapaxes(st, -1, -2)


def _unit_lower_inverse(lm, c):
    bs = min(SUB, c)
    r, s = _incl_lower(c)
    eye = jnp.where(r == s, 1.0, 0.0)
    shift = bs.bit_length() - 1
    same = jnp.right_shift(r, shift) == jnp.right_shift(s, shift)

    def neumann(n, order):
        inv, pw, k = eye + n, n, 2
        while k < order:
            pw = _dot_mid(pw, pw)
            inv = inv + _dot_mid(inv, pw)
            k *= 2
        return inv

    inv_d = neumann(-jnp.where(same, lm, 0.0), bs)
    if c == bs:
        return inv_d
    m = _dot_mid(inv_d, jnp.where(same, 0.0, lm))
    return _dot_mid(neumann(-m, c // bs), inv_d)


def _gdn_kernel(x_ref, ab_ref, g_ref, cw_ref, al_ref, dtb_ref, gn_ref, s0_ref, cb_ref, o_ref, s_ref, nb_ref,
                xx_ref, y_ref, st_ref, *, c, n_chunks, n_heads, lblk):
    l = pl.program_id(1)
    dk = LANES
    halo = 8
    w = n_heads * dk

    @pl.when(l == 0)
    def _():
        st_ref[...] = s0_ref[0]
        xx_ref[halo - (CONV_W - 1):halo, :] = cb_ref[0]

    xx_ref[halo:halo + lblk, :] = x_ref[...]
    y = xx_ref[halo:halo + lblk, :] * cw_ref[CONV_W - 1:CONV_W, :]
    for i in range(CONV_W - 1):
        off = halo - (CONV_W - 1) + i
        y = y + xx_ref[off:off + lblk, :] * cw_ref[i:i + 1, :]
    y_ref[...] = jax.nn.silu(y)
    tail = xx_ref[halo + lblk - (CONV_W - 1):halo + lblk, :]
    xx_ref[halo - (CONV_W - 1):halo, :] = tail

    r, s = _incl_lower(c)
    tri = jnp.where(s <= r, 1.0, 0.0).astype(BF16)
    gn = gn_ref[...]

    def l2n(t):
        return t * lax.rsqrt(jnp.sum(t * t, axis=-1, keepdims=True) + RMS_EPS)

    def chunk(n, carry):
        rows = pl.ds(pl.multiple_of(n * c, c), c)
        ab = ab_ref[rows, :]
        pre = ab + dtb_ref[...]
        log_a = -jnp.exp(al_ref[...]) * (jnp.maximum(pre, 0.0) + jnp.log(1.0 + jnp.exp(-jnp.abs(pre))))
        beta_all = jax.nn.sigmoid(ab)
        cg_all = _dot_exact_lhs(tri, log_a)
        cg_t = cg_all.T
        for h in range(n_heads):
            q = l2n(y_ref[rows, h * dk:(h + 1) * dk]) * dk ** -0.5
            k = l2n(y_ref[rows, w + h * dk:w + (h + 1) * dk])
            v = y_ref[rows, 2 * w + h * dk:2 * w + (h + 1) * dk]
            cg = cg_all[:, h:h + 1]
            beta = beta_all[:, n_heads + h:n_heads + h + 1]
            decay = jnp.exp(jnp.where(s <= r, cg - cg_t[h:h + 1, :c], NEG))
            kk = _dot_mid(k, k, dot=_dot_nt) * decay * beta
            t_inv = _unit_lower_inverse(jnp.where(s < r, kk, 0.0), c)
            e_cg = jnp.exp(cg)
            sol_v = _dot_mid(t_inv, v * beta)
            sol_k = _dot_mid(t_inv, k * (beta * e_cg))
            st = st_ref[h]
            stb = st.astype(BF16)
            u = sol_v - _dot(sol_k.astype(BF16), stb)
            qk = _dot_nt(q.astype(BF16), k.astype(BF16)) * decay
            o = _dot((q * e_cg).astype(BF16), stb) + _dot(qk.astype(BF16), u.astype(BF16))
            last = cg[c - 1:c]
            kt = k * jnp.exp(last - cg)
            st_ref[h] = st * jnp.exp(last) + _dot_tn(kt.astype(BF16), u.astype(BF16))
            o = o * lax.rsqrt(jnp.mean(o * o, axis=-1, keepdims=True) + RMS_EPS) * gn
            o_ref[rows, h * dk:(h + 1) * dk] = o * jax.nn.silu(g_ref[rows, h * dk:(h + 1) * dk])
        return carry

    lax.fori_loop(0, n_chunks, chunk, 0)

    @pl.when(l == pl.num_programs(1) - 1)
    def _():
        s_ref[0] = st_ref[...]
        nb_ref[0] = tail


def _gdn(qkv, ab, g, conv_w, a_log, dt_bias, gn, s0, conv_buf, *, batch):
    t, width = qkv.shape
    seq = t // batch
    w = width // 3
    n_heads = w // LANES
    c = min(CHUNK, seq)
    lblk = min(SEQ_BLOCK, seq)
    assert seq % lblk == 0 and lblk % c == 0 and lblk >= CONV_W - 1
    nl = seq // lblk
    pad_lane = lambda vec: jnp.pad(vec, (0, LANES - vec.shape[0])).reshape(1, LANES)
    rows = lambda width_: pl.BlockSpec((lblk, width_), lambda b, l: (b * nl + l, 0))
    const = lambda shape: pl.BlockSpec(shape, lambda b, l: (0,) * len(shape))
    state = pl.BlockSpec((1, n_heads, LANES, LANES), lambda b, l: (b, 0, 0, 0))
    buf = pl.BlockSpec((1, CONV_W - 1, width), lambda b, l: (b, 0, 0))
    return pl.pallas_call(
        functools.partial(_gdn_kernel, c=c, n_chunks=lblk // c, n_heads=n_heads, lblk=lblk),
        grid=(batch, nl),
        in_specs=[rows(width), rows(LANES), rows(w), const((CONV_W, width)), const((1, LANES)), const((1, LANES)),
                  const((1, LANES)), state, buf],
        out_specs=[rows(w), state, buf],
        out_shape=[jax.ShapeDtypeStruct((t, w), F32), jax.ShapeDtypeStruct(s0.shape, F32),
                   jax.ShapeDtypeStruct(conv_buf.shape, F32)],
        scratch_shapes=[pltpu.VMEM((lblk + 8, width), F32), pltpu.VMEM((lblk, width), F32),
                        pltpu.VMEM((n_heads, LANES, LANES), F32)],
        compiler_params=_cparams(("parallel", "arbitrary")),
        name="gated_deltanet",
    )(qkv, ab, g, conv_w, pad_lane(a_log), pad_lane(dt_bias), gn.reshape(1, LANES), s0, conv_buf)


PAGES_PER_STEP = 4


def _head_fold(pv, n_heads, nq, hd):
    return jnp.concatenate([pv[h * nq:(h + 1) * nq, h * hd:(h + 1) * hd] for h in range(n_heads)], axis=0)


def _row_ids(rows, nq):
    r = lax.broadcasted_iota(jnp.int32, (rows, 1), 0)
    return r // nq, r % nq


def _page_group(refs, first, count):
    return jnp.concatenate([refs[first + r][0, 0].reshape(-1, refs[first + r].shape[-1]) for r in range(count)], axis=1)


def _stick_paged_kernel(pt_ref, qbd_ref, kn_ref, vn_ref, *rest, pps, n_heads, nq, hd):
    k_refs, v_refs = rest[:pps], rest[pps:2 * pps]
    o_ref, c_ref, acc_ref = rest[2 * pps:]
    s = pl.program_id(1)
    rows = n_heads * nq
    page = k_refs[0].shape[-1]
    ppg = MOBA_BLOCK // page
    _, row_q = _row_ids(rows, nq)
    qbd = (qbd_ref[0] * hd ** -0.5).astype(BF16)

    def strict_later(n):
        return jnp.where(lax.broadcasted_iota(jnp.int32, (n, n), 0) > lax.broadcasted_iota(jnp.int32, (n, n), 1),
                         1.0, 0.0).astype(BF16)

    def weights(z, mask, carry, u):
        log_keep = -_softplus(z)
        log_beta = z + log_keep
        if mask is not None:
            log_keep = jnp.where(mask, log_keep, 0.0)
        hi = log_keep.astype(BF16)
        lo = (log_keep - hi.astype(F32)).astype(BF16)
        later = _dot(hi, u) + _dot(lo, u)
        w = jnp.exp(log_beta + later + carry)
        if mask is not None:
            w = jnp.where(mask, w, 0.0)
        return w.astype(BF16), carry + jnp.sum(log_keep, axis=-1, keepdims=True)

    @pl.when(s == 0)
    def _():
        n = kn_ref.shape[1]
        z = _dot_nt(qbd, kn_ref[0].astype(BF16))
        col = lax.broadcasted_iota(jnp.int32, (rows, n), 1)
        w, c = weights(z, col < row_q, jnp.zeros((rows, 1), F32), strict_later(n))
        c_ref[...] = c
        acc_ref[...] = _dot(w, vn_ref[0].astype(BF16))

    @pl.when(s > 0)
    def _():
        u = strict_later(ppg * page)
        c = c_ref[...]
        acc = acc_ref[...]
        for g in reversed(range(pps // ppg)):
            z = _dot(qbd, _page_group(k_refs, g * ppg, ppg).astype(BF16))
            w, c = weights(z, None, c, u)
            acc = acc + _dot_nt(w, _page_group(v_refs, g * ppg, ppg).astype(BF16))
        c_ref[...] = c
        acc_ref[...] = acc

    @pl.when(s == pl.num_programs(1) - 1)
    def _():
        o_ref[0] = _head_fold(acc_ref[...], n_heads, nq, hd)


def _moba_paged_kernel(pt_ref, qbd_ref, kn_ref, vn_ref, *rest, pps, n_heads, nq, hd, past_len):
    k_refs, v_refs = rest[:pps], rest[pps:2 * pps]
    o_ref, mo_ref, lo_ref, acco_ref, m_ref, l_ref, acc_ref, gate_ref = rest[2 * pps:]
    s = pl.program_id(1)
    rows = n_heads * nq
    page = k_refs[0].shape[-1]
    ppb = MOBA_BLOCK // page
    row_h, row_q = _row_ids(rows, nq)
    slope = jnp.exp2(-8.0 * (row_h + 1).astype(F32) / n_heads)
    lane = lax.broadcasted_iota(jnp.int32, (rows, LANES), 1)
    q_f32 = qbd_ref[0]
    qbd = (q_f32 * hd ** -0.5).astype(BF16)

    @pl.when(s == 0)
    def _():
        n = kn_ref.shape[1]
        col = lax.broadcasted_iota(jnp.int32, (rows, n), 1)
        sc = _dot_nt(qbd, kn_ref[0].astype(BF16)) - slope * (row_q - col).astype(F32)
        sc = jnp.where(col <= row_q, sc, NEG)
        m = jnp.max(sc, axis=-1, keepdims=True)
        p = jnp.exp(sc - m)
        mo_ref[...] = m
        lo_ref[...] = jnp.sum(p, axis=-1, keepdims=True)
        acco_ref[...] = _head_fold(_dot(p.astype(BF16), vn_ref[0].astype(BF16)), n_heads, nq, hd)
        m_ref[...] = jnp.full(m_ref.shape, NEG, F32)
        l_ref[...] = jnp.zeros_like(l_ref)
        gate_ref[...] = jnp.zeros_like(gate_ref)

    @pl.when(s > 0)
    def _():
        key = lax.broadcasted_iota(jnp.int32, (rows, MOBA_BLOCK), 1)
        m_all, l_all, gate_all = m_ref[...], l_ref[...], gate_ref[...]
        for bi in range(pps // ppb):
            blk = (s - 1) * (pps // ppb) + bi
            kf = _page_group(k_refs, bi * ppb, ppb)
            k_mean = jnp.broadcast_to(jnp.sum(kf, axis=-1, keepdims=True) * (1.0 / MOBA_BLOCK), (kf.shape[0], LANES))
            gate = _dot_mid(q_f32, k_mean)
            dist = past_len + row_q - (blk * MOBA_BLOCK + key)
            sc = _dot(qbd, kf.astype(BF16)) - slope * dist.astype(F32)
            m_b = jnp.max(sc, axis=-1, keepdims=True)
            p = jnp.exp(sc - m_b)
            l_b = jnp.sum(p, axis=-1, keepdims=True)
            pv = _dot_nt(p.astype(BF16), _page_group(v_refs, bi * ppb, ppb).astype(BF16))
            acc_ref[blk] = _head_fold(pv, n_heads, nq, hd)
            m_all = jnp.where(lane == blk, m_b, m_all)
            l_all = jnp.where(lane == blk, l_b, l_all)
            gate_all = jnp.where(lane == blk, gate, gate_all)
        m_ref[...] = m_all
        l_ref[...] = l_all
        gate_ref[...] = gate_all

    @pl.when(s == pl.num_programs(1) - 1)
    def _():
        n_blocks = past_len // MOBA_BLOCK
        sel = _topk_lanes(jnp.where(lane < n_blocks, gate_ref[...], -jnp.inf), lane) > 0.0
        m_all, l_all = m_ref[...], l_ref[...]
        m_o = mo_ref[...]
        m_tot = jnp.maximum(m_o, jnp.max(jnp.where(sel, m_all, NEG), axis=-1, keepdims=True))
        wgt = jnp.where(sel, jnp.exp(m_all - m_tot), 0.0)
        a_o = jnp.exp(m_o - m_tot)
        l_tot = lo_ref[...] * a_o + jnp.sum(wgt * l_all, axis=-1, keepdims=True)
        acc = acco_ref[...] * a_o
        for b in range(n_blocks):
            acc = acc + wgt[:, b:b + 1] * acc_ref[b]
        o_ref[0] = acc / l_tot


def _paged_attention(kind, q, k_new, v_new, k_pool, v_pool, page_table, *, n_heads):
    batch, n_pages = page_table.shape
    page, hd = k_pool.shape[2], k_pool.shape[4]
    nq = q.shape[0] // batch
    rows = n_heads * nq
    pps = PAGES_PER_STEP
    past_len = n_pages * page
    assert n_pages % pps == 0 and past_len % MOBA_BLOCK == 0 and MOBA_BLOCK % page == 0 and pps % (MOBA_BLOCK // page) == 0
    assert nq <= page and past_len // MOBA_BLOCK <= LANES and (2 * LANES) % n_heads == 0
    n_groups = n_pages // pps
    q4 = q.reshape(batch, nq, n_heads, hd).transpose(0, 2, 1, 3)
    q_bd = (q4[:, :, :, None, :] * jnp.eye(n_heads, dtype=F32)[None, :, None, :, None]).reshape(batch, rows, n_heads * hd)
    pad_new = lambda t: jnp.pad(t.reshape(batch, nq, n_heads * hd), ((0, 0), (0, page - nq), (0, 0)))
    k_t, v_t = (jnp.transpose(t, (0, 1, 3, 4, 2)) for t in (k_pool, v_pool))

    if kind == "stick":
        group = lambda s: n_groups - jnp.maximum(s, 1)
        body = functools.partial(_stick_paged_kernel, pps=pps, n_heads=n_heads, nq=nq, hd=hd)
        scratch = [pltpu.VMEM((rows, 1), F32), pltpu.VMEM((rows, n_heads * hd), F32)]
    else:
        group = lambda s: jnp.maximum(s, 1) - 1
        body = functools.partial(_moba_paged_kernel, pps=pps, n_heads=n_heads, nq=nq, hd=hd, past_len=past_len)
        scratch = [pltpu.VMEM((rows, 1), F32), pltpu.VMEM((rows, 1), F32), pltpu.VMEM((rows, hd), F32),
                   pltpu.VMEM((rows, LANES), F32), pltpu.VMEM((rows, LANES), F32),
                   pltpu.VMEM((past_len // MOBA_BLOCK, rows, hd), F32), pltpu.VMEM((rows, LANES), F32)]

    def page_spec(r):
        return pl.BlockSpec((1, 1, n_heads, hd, page), lambda b, s, pt: (0, pt[b, group(s) * pps + r], 0, 0, 0))

    per_seq = lambda shape: pl.BlockSpec((1,) + shape, lambda b, s, pt: (b, 0, 0))
    out = pl.pallas_call(
        body,
        grid_spec=pltpu.PrefetchScalarGridSpec(
            num_scalar_prefetch=1,
            grid=(batch, n_groups + 1),
            in_specs=[per_seq((rows, n_heads * hd)), per_seq((page, n_heads * hd)),
                      per_seq((page, n_heads * hd))] + [page_spec(r) for r in range(pps)] * 2,
            out_specs=per_seq((rows, hd)),
            scratch_shapes=scratch),
        out_shape=jax.ShapeDtypeStruct((batch, rows, hd), F32),
        compiler_params=_cparams(("parallel", "arbitrary")),
        name=kind + "_paged",
    )(page_table, q_bd, pad_new(k_new), pad_new(v_new), *([k_t] * pps), *([v_t] * pps))
    return out.reshape(batch, n_heads, nq, hd).transpose(0, 2, 1, 3).reshape(batch * nq, n_heads * hd)


def _layer_stack(x, batch, past, states, w):
    s_hgrn, s_gdn, conv_buf = states
    mm = lambda a, wt: _matmul([a], [wt], tn=min(512, wt.shape[1]))

    def attend(kind, q, k, v, pools, n_heads):
        if past is None:
            return (_moba if kind == "moba" else _stick)(q, k, v, batch=batch, q_offset=0)
        return _paged_attention(kind, q, k, v, *pools, past[4], n_heads=n_heads)

    q_a, k_a, v_a = (mm(x, wt) for wt in w["in_a"])
    hg = mm(x, w["in_hgrn"])
    o_a = attend("moba", q_a, k_a, v_a, past and past[0:2], H_A)
    o_b, s_hgrn_new = _hgrn(hg, w["hgrn_lb"], w["hgrn_norm"], s_hgrn, batch=batch, layer=0)
    x = _matmul([o_a, o_b], w["out_even"], ln_args=(x, w["ln1_g"][0], w["ln1_b"][0]))
    x = _ffn(x, None, w["ffn_wg"], w["ffn_wu"], w["ffn_wd"], w["ln2_g"][0], w["ln2_b"][0])

    qkv_c, ab, g_c = mm(x, w["in_qkv_c"]), mm(x, w["in_ab"]), mm(x, w["in_g_c"])
    q_d, k_d, v_d = (mm(x, wt) for wt in w["in_d"])
    o_c, s_gdn_new, conv_new = _gdn(qkv_c, ab, g_c, w["conv_w"], w["a_log"], w["dt_bias"], w["gdn_norm"],
                                    s_gdn, conv_buf, batch=batch)
    o_d = attend("stick", q_d, k_d, v_d, past and past[2:4], H_D)
    x = _matmul([o_c, o_d], w["out_odd"], ln_args=(x, w["ln1_g"][1], w["ln1_b"][1]))
    comb = _router(x, w["router"])
    x = _moe(x, comb, w["moe_wg"], w["moe_wu"], w["moe_wd"], w["ln2_g"][1], w["ln2_b"][1])
    return x, k_a, v_a, s_hgrn_new, s_gdn_new, conv_new, k_d, v_d


def kernel(x_prompt, x_sample, cache_k_moba, cache_v_moba, state_hgrn, state_gdn, state_gdn_conv, cache_k_sb,
           cache_v_sb, page_table, w_in_even, w_out_even, hgrn_lb, hgrn_norm, w_in_odd, w_out_odd, gdn_conv_w,
           gdn_a_log, gdn_dt_bias, gdn_norm, ln1_g, ln1_b, ln2_g, ln2_b, ffn_wg, ffn_wu, ffn_wd, router, moe_wg,
           moe_wu, moe_wd):
    assert w_in_even.shape[0] == 1 and w_in_odd.shape[0] == 1
    bp, lp, d = x_prompt.shape
    bs, ls, _ = x_sample.shape
    mix = d // 2
    hd_a, hd_d = mix // H_A, mix // H_D
    conv_dim = gdn_conv_w.shape[-1]
    bf = lambda t: t.astype(BF16)
    cols = lambda wt, lo, n: bf(wt[:, lo:lo + n])

    wie, wio = w_in_even[0], w_in_odd[0]
    g_lo = conv_dim + 2 * H_C
    d_lo = g_lo + mix
    w = {
        "in_a": [cols(wie, i * mix, mix) for i in range(3)],
        "in_hgrn": cols(wie, 3 * mix, 4 * mix),
        "out_even": [bf(w_out_even[0][:mix]), bf(w_out_even[0][mix:])],
        "hgrn_lb": hgrn_lb, "hgrn_norm": hgrn_norm[0],
        "in_qkv_c": cols(wio, 0, conv_dim),
        "in_ab": bf(jnp.pad(wio[:, conv_dim:g_lo], ((0, 0), (0, LANES - 2 * H_C)))),
        "in_g_c": cols(wio, g_lo, mix),
        "in_d": [cols(wio, d_lo + i * mix, mix) for i in range(3)],
        "out_odd": [bf(w_out_odd[0][:mix]), bf(w_out_odd[0][mix:])],
        "conv_w": gdn_conv_w[0], "a_log": gdn_a_log[0], "dt_bias": gdn_dt_bias[0], "gdn_norm": gdn_norm[0],
        "ln1_g": ln1_g, "ln1_b": ln1_b, "ln2_g": ln2_g, "ln2_b": ln2_b,
        "ffn_wg": bf(ffn_wg), "ffn_wu": bf(ffn_wu), "ffn_wd": bf(ffn_wd),
        "router": router[0], "moe_wg": bf(moe_wg[0]), "moe_wu": bf(moe_wu[0]), "moe_wd": bf(moe_wd[0]),
    }

    def run(x, batch, past, states):
        seq = x.shape[1]
        y, k_a, v_a, s_h, s_g, cv, k_d, v_d = _layer_stack(x.reshape(batch * seq, d), batch, past, states, w)
        return (y.reshape(batch, seq, d), k_a.reshape(1, batch, seq, H_A, hd_a), v_a.reshape(1, batch, seq, H_A, hd_a),
                s_h[None], s_g[None], cv[None], k_d.reshape(1, batch, seq, H_D, hd_d),
                v_d.reshape(1, batch, seq, H_D, hd_d))

    zeros_p = (jnp.zeros((bp,) + state_hgrn.shape[2:], F32), jnp.zeros((bp,) + state_gdn.shape[2:], F32),
               jnp.zeros((bp,) + state_gdn_conv.shape[2:], F32))
    out_p = run(x_prompt, bp, None, zeros_p)
    past = (cache_k_moba, cache_v_moba, cache_k_sb, cache_v_sb, page_table)
    out_s = run(x_sample, bs, past, (state_hgrn[0], state_gdn[0], state_gdn_conv[0]))
    return (out_p[0], out_s[0]) + out_p[1:] + out_s[1:]
```

```python
import functools

import jax
import jax.numpy as jnp
from jax import lax
from jax.experimental import pallas as pl
from jax.experimental.pallas import tpu as pltpu

F32 = jnp.float32
BF16 = jnp.bfloat16

H_A, H_B, H_C, H_D = 8, 4, 4, 8
MOBA_BLOCK = 256
MOBA_TOPK = 3
CONV_W = 4
N_EXPERTS = 8
DEPTH = 2
DEEPNORM_ALPHA = (2 * DEPTH) ** 0.25
LN_EPS = 1e-5
RMS_EPS = 1e-6
NEG = -1e30
LANES = 128
VMEM_LIMIT = 56 * 1024 * 1024


def _cparams(sem):
    return pltpu.CompilerParams(dimension_semantics=sem, vmem_limit_bytes=VMEM_LIMIT)


def _layernorm(y, g, b):
    mu = jnp.mean(y, axis=-1, keepdims=True)
    yc = y - mu
    var = jnp.mean(yc * yc, axis=-1, keepdims=True)
    return yc * lax.rsqrt(var + LN_EPS) * g + b


def _split3(x):
    hi = x.astype(BF16)
    r = x - hi.astype(F32)
    mid = r.astype(BF16)
    lo = (r - mid.astype(F32)).astype(BF16)
    return hi, mid, lo


def _dot(a, b):
    return jnp.dot(a, b, preferred_element_type=F32)


def _dot_nt(a, b):
    return lax.dot_general(a, b, (((1,), (1,)), ((), ())), preferred_element_type=F32)


def _dot_tn(a, b):
    return lax.dot_general(a, b, (((0,), (0,)), ((), ())), preferred_element_type=F32)


def _dot_hi(a, b, dot=_dot):
    a0, a1, a2 = _split3(a)
    b0, b1, b2 = _split3(b)
    return (dot(a0, b0) + (dot(a0, b1) + dot(a1, b0))
            + (dot(a1, b1) + dot(a0, b2) + dot(a2, b0)))


def _dot_exact_rhs(a, b_bf16):
    a0, a1, a2 = _split3(a)
    return _dot(a0, b_bf16) + _dot(a1, b_bf16) + _dot(a2, b_bf16)


def _dot_exact_lhs(a_bf16, b):
    b0, b1, b2 = _split3(b)
    return _dot(a_bf16, b0) + _dot(a_bf16, b1) + _dot(a_bf16, b2)


def _mm_kernel(*refs, n_in, ln):
    a_refs, w_refs, rest = refs[:n_in], refs[n_in:2 * n_in], refs[2 * n_in:]
    acc = None
    for a, w in zip(a_refs, w_refs):
        d = _dot(a[...].astype(BF16), w[...])
        acc = d if acc is None else acc + d
    if ln:
        res_ref, g_ref, b_ref, o_ref = rest
        o_ref[...] = _layernorm(DEEPNORM_ALPHA * res_ref[...] + acc, g_ref[...], b_ref[...])
    else:
        (o_ref,) = rest
        o_ref[...] = acc


def _row_tile(m, largest=512):
    for t in (1024, 512, 256, 128, 64, 32, 16, 8):
        if t <= largest and m % t == 0:
            return t
    raise ValueError(m)


def _proj_kernel(x_ref, *refs):
    n = len(refs) // 2
    xb = x_ref[...].astype(BF16)
    for w_ref, o_ref in zip(refs[:n], refs[n:]):
        o_ref[...] = _dot(xb, w_ref[...])


def _project(x, w_list):
    m, d = x.shape
    tm = _row_tile(m)
    return pl.pallas_call(
        _proj_kernel,
        grid=(m // tm,),
        in_specs=[pl.BlockSpec((tm, d), lambda i: (i, 0))] + [pl.BlockSpec(w.shape, lambda i: (0, 0)) for w in w_list],
        out_specs=[pl.BlockSpec((tm, w.shape[1]), lambda i: (i, 0)) for w in w_list],
        out_shape=[jax.ShapeDtypeStruct((m, w.shape[1]), F32) for w in w_list],
        compiler_params=_cparams(("parallel",)),
        name="in_proj",
    )(x, *w_list)


def _matmul(a_list, w_list, *, tn=None, ln_args=None):
    m = a_list[0].shape[0]
    n = w_list[0].shape[1]
    tm = _row_tile(m, largest=1024)
    ln = ln_args is not None
    tn = n if (ln or tn is None) else tn
    assert n % tn == 0
    in_specs = [pl.BlockSpec((tm, a.shape[1]), lambda i, j: (i, 0)) for a in a_list]
    in_specs += [pl.BlockSpec((w.shape[0], tn), lambda i, j: (0, j)) for w in w_list]
    args = list(a_list) + list(w_list)
    if ln:
        res, g, b = ln_args
        in_specs += [pl.BlockSpec((tm, n), lambda i, j: (i, 0)),
                     pl.BlockSpec((1, n), lambda i, j: (0, 0)),
                     pl.BlockSpec((1, n), lambda i, j: (0, 0))]
        args += [res, g.reshape(1, n), b.reshape(1, n)]
    return pl.pallas_call(
        functools.partial(_mm_kernel, n_in=len(a_list), ln=ln),
        grid=(m // tm, n // tn),
        in_specs=in_specs,
        out_specs=pl.BlockSpec((tm, tn), lambda i, j: (i, j)),
        out_shape=jax.ShapeDtypeStruct((m, n), F32),
        compiler_params=_cparams(("parallel", "arbitrary")),
        name="matmul_ln" if ln else "matmul",
    )(*args)


def _ffn_kernel(*refs, use_comb):
    if use_comb:
        x_ref, comb_ref, wg_ref, wu_ref, wd_ref, g_ref, b_ref, o_ref, xb_ref, acc_ref = refs
    else:
        x_ref, wg_ref, wu_ref, wd_ref, g_ref, b_ref, o_ref, xb_ref, acc_ref = refs
    e, j = pl.program_id(1), pl.program_id(2)
    first = jnp.logical_and(e == 0, j == 0)
    last = jnp.logical_and(e == pl.num_programs(1) - 1, j == pl.num_programs(2) - 1)

    @pl.when(first)
    def _():
        xb_ref[...] = x_ref[...].astype(BF16)
        acc_ref[...] = jnp.zeros_like(acc_ref)

    xb = xb_ref[...]
    h = jax.nn.silu(_dot(xb, wg_ref[0])) * _dot(xb, wu_ref[0])
    if use_comb:
        lane = lax.broadcasted_iota(jnp.int32, comb_ref.shape, 1)
        h = h * jnp.sum(jnp.where(lane == e, comb_ref[...], 0.0), axis=-1, keepdims=True)
    acc_ref[...] += _dot(h.astype(BF16), wd_ref[0])

    @pl.when(last)
    def _():
        o_ref[...] = _layernorm(DEEPNORM_ALPHA * x_ref[...] + acc_ref[...], g_ref[...], b_ref[...])


def _ff_tile(ff):
    for t in (896, 512, 256, LANES):
        if ff % t == 0:
            return t
    raise ValueError(ff)


def _ffn(x, comb, wg, wu, wd, g, b):
    m, d = x.shape
    n_e, _, ff = wg.shape
    tm = _row_tile(m)
    tf = _ff_tile(ff)
    use_comb = comb is not None
    in_specs = [pl.BlockSpec((tm, d), lambda i, e, j: (i, 0))]
    args = [x]
    if use_comb:
        in_specs.append(pl.BlockSpec((tm, comb.shape[1]), lambda i, e, j: (i, 0)))
        args.append(comb)
    in_specs += [pl.BlockSpec((1, d, tf), lambda i, e, j: (e, 0, j)),
                 pl.BlockSpec((1, d, tf), lambda i, e, j: (e, 0, j)),
                 pl.BlockSpec((1, tf, d), lambda i, e, j: (e, j, 0)),
                 pl.BlockSpec((1, d), lambda i, e, j: (0, 0)),
                 pl.BlockSpec((1, d), lambda i, e, j: (0, 0))]
    args += [wg, wu, wd, g.reshape(1, d), b.reshape(1, d)]
    return pl.pallas_call(
        functools.partial(_ffn_kernel, use_comb=use_comb),
        grid=(m // tm, n_e, ff // tf),
        in_specs=in_specs,
        out_specs=pl.BlockSpec((tm, d), lambda i, e, j: (i, 0)),
        out_shape=jax.ShapeDtypeStruct((m, d), F32),
        scratch_shapes=[pltpu.VMEM((tm, d), BF16), pltpu.VMEM((tm, d), F32)],
        compiler_params=_cparams(("parallel", "arbitrary", "arbitrary")),
        name="moe_ffn" if use_comb else "ffn",
    )(*args)


MOE_TOKENS = 1024
MOE_ROWS = 128


def _moe_kernel(x_ref, comb_ref, wg_ref, wu_ref, wd_ref, g_ref, b_ref, o_ref,
                xb_ref, acc_ref, xc_ref, yc_ref, rank_ref, rank_t_ref, comb_t_ref):
    e, j = pl.program_id(1), pl.program_id(2)
    tm, d = x_ref.shape
    r = MOE_ROWS
    lane = lax.broadcasted_iota(jnp.int32, (tm, LANES), 1)

    @pl.when(jnp.logical_and(e == 0, j == 0))
    def _():
        xb_ref[...] = x_ref[...].astype(BF16)
        acc_ref[...] = jnp.zeros_like(acc_ref)
        routed = jnp.where(comb_ref[...] > 0.0, 1.0, 0.0).astype(BF16)
        earlier = jnp.where(lax.broadcasted_iota(jnp.int32, (tm, tm), 1) < lax.broadcasted_iota(jnp.int32, (tm, tm), 0),
                            1.0, 0.0).astype(BF16)
        rank = _dot(earlier, routed)
        rank_ref[...] = rank
        rank_t_ref[...] = rank.T
        comb_t_ref[...] = comb_ref[...].T

    gate_row = comb_t_ref[pl.ds(e, 1), :]
    n_pass = (jnp.sum(jnp.where(gate_row > 0.0, 1, 0)) + (r - 1)) // r

    @pl.when(j == 0)
    def _():
        rank_row = rank_t_ref[pl.ds(e, 1), :]

        def compact(k, carry):
            slot = (k * r + lax.broadcasted_iota(jnp.int32, (r, tm), 0)).astype(F32)
            pick = jnp.where(jnp.logical_and(rank_row == slot, gate_row > 0.0), 1.0, 0.0).astype(BF16)
            rows = pl.ds(pl.multiple_of(k * r, r), r)
            xc_ref[rows, :] = _dot(pick, xb_ref[...]).astype(BF16)
            yc_ref[rows, :] = jnp.zeros((r, d), F32)
            return carry

        lax.fori_loop(0, n_pass, compact, 0)

    def expert(start, n_rows):
        rows = pl.ds(pl.multiple_of(start, r), n_rows)
        xk = xc_ref[rows, :]
        h = jax.nn.silu(_dot(xk, wg_ref[0])) * _dot(xk, wu_ref[0])
        yc_ref[rows, :] += _dot(h.astype(BF16), wd_ref[0])

    def expert_pair(k, carry):
        expert(k * (2 * r), 2 * r)
        return carry

    lax.fori_loop(0, n_pass // 2, expert_pair, 0)

    @pl.when(n_pass % 2 == 1)
    def _():
        expert((n_pass - 1) * r, r)

    @pl.when(j == pl.num_programs(2) - 1)
    def _():
        rank_col = jnp.sum(jnp.where(lane == e, rank_ref[...], 0.0), axis=-1, keepdims=True)
        gate_col = jnp.sum(jnp.where(lane == e, comb_ref[...], 0.0), axis=-1, keepdims=True)

        def place(k, carry):
            slot = (k * r + lax.broadcasted_iota(jnp.int32, (tm, r), 1)).astype(F32)
            put = jnp.where(jnp.logical_and(rank_col == slot, gate_col > 0.0), 1.0, 0.0).astype(BF16)
            rows = pl.ds(pl.multiple_of(k * r, r), r)
            acc_ref[...] += gate_col * _dot(put, yc_ref[rows, :].astype(BF16))
            return carry

        lax.fori_loop(0, n_pass, place, 0)

    @pl.when(jnp.logical_and(e == pl.num_programs(1) - 1, j == pl.num_programs(2) - 1))
    def _():
        o_ref[...] = _layernorm(DEEPNORM_ALPHA * x_ref[...] + acc_ref[...], g_ref[...], b_ref[...])


def _moe(x, comb, wg, wu, wd, g, b):
    m, d = x.shape
    n_e, _, ff = wg.shape
    tm = MOE_TOKENS
    if m % tm != 0:
        return _ffn(x, comb, wg, wu, wd, g, b)
    tf = _ff_tile(ff)
    assert tm % (2 * MOE_ROWS) == 0 and comb.shape[1] == LANES and n_e <= LANES
    return pl.pallas_call(
        _moe_kernel,
        grid=(m // tm, n_e, ff // tf),
        in_specs=[pl.BlockSpec((tm, d), lambda i, e, j: (i, 0)),
                  pl.BlockSpec((tm, LANES), lambda i, e, j: (i, 0)),
                  pl.BlockSpec((1, d, tf), lambda i, e, j: (e, 0, j)),
                  pl.BlockSpec((1, d, tf), lambda i, e, j: (e, 0, j)),
                  pl.BlockSpec((1, tf, d), lambda i, e, j: (e, j, 0)),
                  pl.BlockSpec((1, d), lambda i, e, j: (0, 0)),
                  pl.BlockSpec((1, d), lambda i, e, j: (0, 0))],
        out_specs=pl.BlockSpec((tm, d), lambda i, e, j: (i, 0)),
        out_shape=jax.ShapeDtypeStruct((m, d), F32),
        scratch_shapes=[pltpu.VMEM((tm, d), BF16), pltpu.VMEM((tm, d), F32), pltpu.VMEM((tm, d), BF16),
                        pltpu.VMEM((tm, d), F32), pltpu.VMEM((tm, LANES), F32), pltpu.VMEM((LANES, tm), F32),
                        pltpu.VMEM((LANES, tm), F32)],
        compiler_params=_cparams(("parallel", "arbitrary", "arbitrary")),
        name="moe_sparse",
    )(x, comb, wg, wu, wd, g.reshape(1, d), b.reshape(1, d))


def _router_kernel(x_ref, r_ref, o_ref):
    logits = _dot_hi(x_ref[...], r_ref[...])
    lane = lax.broadcasted_iota(jnp.int32, logits.shape, 1)
    logits = jnp.where(lane < N_EXPERTS, logits, -jnp.inf)
    m1 = jnp.max(logits, axis=-1, keepdims=True)
    i1 = jnp.min(jnp.where(logits == m1, lane, LANES), axis=-1, keepdims=True)
    rest = jnp.where(lane == i1, -jnp.inf, logits)
    m2 = jnp.max(rest, axis=-1, keepdims=True)
    i2 = jnp.min(jnp.where(rest == m2, lane, LANES), axis=-1, keepdims=True)
    e2 = jnp.exp(m2 - m1)
    den = 1.0 + e2
    o_ref[...] = jnp.where(lane == i1, 1.0 / den, 0.0) + jnp.where(lane == i2, e2 / den, 0.0)


def _router(x, router):
    m, d = x.shape
    tm = _row_tile(m)
    r = jnp.pad(router, ((0, 0), (0, LANES - router.shape[1])))
    return pl.pallas_call(
        _router_kernel,
        grid=(m // tm,),
        in_specs=[pl.BlockSpec((tm, d), lambda i: (i, 0)), pl.BlockSpec((d, LANES), lambda i: (0, 0))],
        out_specs=pl.BlockSpec((tm, LANES), lambda i: (i, 0)),
        out_shape=jax.ShapeDtypeStruct((m, LANES), F32),
        compiler_params=_cparams(("parallel",)),
        name="router",
    )(x, r)


def _topk_lanes(g, idx, axis=-1):
    sel = jnp.zeros(g.shape, F32)
    for _ in range(MOBA_TOPK):
        m = jnp.max(g, axis=axis, keepdims=True)
        first = jnp.min(jnp.where(g == m, idx, LANES), axis=axis, keepdims=True)
        hit = jnp.logical_and(idx == first, m > -jnp.inf)
        sel = jnp.where(hit, 1.0, sel)
        g = jnp.where(hit, -jnp.inf, g)
    return sel


def _moba_kernel(slope_ref, q_ref, k_ref, v_ref, o_ref, kb_ref, vt_ref, km_ref, m_ref, l_ref, acc_ref, sel_ref,
                 *, tq, nb, q_offset, hd):
    i = pl.program_id(2)
    blk = MOBA_BLOCK

    @pl.when(i == 0)
    def _():
        kb_ref[...] = k_ref[...].astype(BF16)
        km_ref[...] = jnp.zeros_like(km_ref)
        for j in range(nb):
            rows = slice(j * blk, (j + 1) * blk)
            vt_ref[j] = v_ref[rows, :].T.astype(BF16)
            km_ref[j:j + 1, :] = jnp.sum(k_ref[rows, :], axis=0, keepdims=True) * (1.0 / blk)

    q0 = q_offset + i * tq
    own = q0 // blk
    q = q_ref[...]
    lane = lax.broadcasted_iota(jnp.int32, (tq, LANES), 1)
    q2 = jnp.concatenate([jnp.where(lane < hd, q, 0.0), jnp.where(lane >= hd, q, 0.0)], axis=0)
    gate_t = _dot_mid(km_ref[...], q2, dot=_dot_nt)
    blk_id = lax.broadcasted_iota(jnp.int32, (LANES, 2 * tq), 0)
    sel_ref[...] = _topk_lanes(jnp.where(blk_id < own, gate_t, -jnp.inf), blk_id, axis=0)
    qb = (q2 * hd ** -0.5).astype(BF16)
    c2 = lax.broadcasted_iota(jnp.int32, (blk, 2 * tq), 1)
    d0 = (jnp.where(c2 >= tq, c2 - tq, c2) - lax.broadcasted_iota(jnp.int32, (blk, 2 * tq), 0)).astype(F32)
    slope_lane = slope_ref[...]
    c1 = lax.broadcasted_iota(jnp.int32, (1, 2 * tq), 1)
    slope = jnp.where(c1 < tq, slope_lane[:, 0:1], slope_lane[:, hd:hd + 1])
    slope_d0 = slope * d0
    first = lax.broadcasted_iota(jnp.int32, (2 * hd, tq), 0) < hd
    m_ref[...] = jnp.full(m_ref.shape, NEG, F32)
    l_ref[...] = jnp.zeros_like(l_ref)
    acc_ref[...] = jnp.zeros_like(acc_ref)

    def tile(j, diagonal):
        off = (q0 - j * blk).astype(F32)
        s = _dot_nt(kb_ref[pl.ds(pl.multiple_of(j * blk, blk), blk), :], qb) - (slope_d0 + slope * off)
        if diagonal:
            s = jnp.where(d0 + off >= 0.0, s, NEG)
        else:
            s = jnp.where(sel_ref[pl.ds(j, 1), :] > 0.0, s, NEG)
        m_old = m_ref[...]
        m_new = jnp.maximum(m_old, jnp.max(s, axis=0, keepdims=True))
        alpha = jnp.exp(m_old - m_new)
        p = jnp.exp(s - m_new)
        m_ref[...] = m_new
        l_ref[...] = alpha * l_ref[...] + jnp.sum(p, axis=0, keepdims=True)
        pv = _dot(vt_ref[j], p.astype(BF16))
        acc_ref[...] = (jnp.where(first, alpha[:, :tq], alpha[:, tq:]) * acc_ref[...]
                        + jnp.where(first, pv[:, :tq], pv[:, tq:]))

    tile(own, True)

    def body(t, carry):
        tile(own - t, False)
        return carry

    lax.fori_loop(1, own + 1, body, 0)
    l = l_ref[...]
    o_ref[...] = (acc_ref[...] / jnp.where(first, l[:, :tq], l[:, tq:])).T


def _alibi_slopes_lanes(n_heads, hd):
    slopes = jnp.exp2(-8.0 * jnp.arange(1, n_heads + 1, dtype=F32) / n_heads)
    return jnp.repeat(slopes, hd).reshape(1, n_heads * hd)


def _attn_call(kernel, q, k, v, *, batch, q_offset, n_heads, extra_in=(), extra_specs=(), extra_scratch=(), name):
    lq, lk = q.shape[0] // batch, k.shape[0] // batch
    hd = q.shape[1] // n_heads
    assert 2 * hd == LANES and lk % MOBA_BLOCK == 0
    tq = min(MOBA_BLOCK, lq)
    assert lq % tq == 0 and MOBA_BLOCK % tq == 0 and q_offset % tq == 0
    nq = lq // tq
    return pl.pallas_call(
        functools.partial(kernel, tq=tq, q_offset=q_offset, hd=hd),
        grid=(batch, n_heads // 2, nq),
        in_specs=list(extra_specs) + [
            pl.BlockSpec((tq, LANES), lambda b, h, i: (b * nq + i, h)),
            pl.BlockSpec((lk, LANES), lambda b, h, i: (b, h)),
            pl.BlockSpec((lk, LANES), lambda b, h, i: (b, h))],
        out_specs=pl.BlockSpec((tq, LANES), lambda b, h, i: (b * nq + i, h)),
        out_shape=jax.ShapeDtypeStruct(q.shape, F32),
        scratch_shapes=list(extra_scratch),
        compiler_params=_cparams(("parallel", "parallel", "arbitrary")),
        name=name,
    )(*extra_in, q, k, v)


def _moba(q, k, v, *, batch, q_offset):
    nb = k.shape[0] // batch // MOBA_BLOCK
    assert nb <= LANES
    slopes = _alibi_slopes_lanes(H_A, q.shape[1] // H_A)
    tq = min(MOBA_BLOCK, q.shape[0] // batch)
    return _attn_call(
        functools.partial(_moba_kernel, nb=nb), q, k, v, batch=batch, q_offset=q_offset, n_heads=H_A,
        extra_in=(slopes,), extra_specs=(pl.BlockSpec((1, LANES), lambda b, h, i: (0, h)),),
        extra_scratch=(pltpu.VMEM((nb * MOBA_BLOCK, LANES), BF16), pltpu.VMEM((nb, LANES, MOBA_BLOCK), BF16),
                       pltpu.VMEM((LANES, LANES), F32), pltpu.VMEM((1, 2 * tq), F32), pltpu.VMEM((1, 2 * tq), F32),
                       pltpu.VMEM((LANES, tq), F32), pltpu.VMEM((LANES, 2 * tq), F32)), name="moba")


EXP_UNDERFLOW = -110.0


def _softplus(z):
    return jnp.maximum(z, 0.0) + jnp.log(1.0 + jnp.exp(-jnp.abs(z)))


def _stick_kernel(q_ref, k_ref, v_ref, o_ref, kb_ref, vb_ref, c_ref, acc_ref, *, tq, q_offset, hd):
    i = pl.program_id(2)
    blk = MOBA_BLOCK

    @pl.when(i == 0)
    def _():
        kb_ref[...] = k_ref[...].astype(BF16)
        vb_ref[...] = v_ref[...].astype(BF16)

    q0 = q_offset + i * tq
    own = q0 // blk
    q = q_ref[...] * hd ** -0.5
    lane = lax.broadcasted_iota(jnp.int32, (tq, LANES), 1)
    qb = jnp.concatenate([jnp.where(lane < hd, q, 0.0), jnp.where(lane >= hd, q, 0.0)], axis=0).astype(BF16)
    r2 = lax.broadcasted_iota(jnp.int32, (2 * tq, blk), 0)
    d0 = jnp.where(r2 >= tq, r2 - tq, r2) - lax.broadcasted_iota(jnp.int32, (2 * tq, blk), 1)
    u = jnp.where(lax.broadcasted_iota(jnp.int32, (blk, blk), 0) > lax.broadcasted_iota(jnp.int32, (blk, blk), 1),
                  1.0, 0.0).astype(BF16)
    c_ref[...] = jnp.zeros_like(c_ref)
    acc_ref[...] = jnp.zeros_like(acc_ref)

    def tile(j, diagonal):
        start = pl.multiple_of(j * blk, blk)
        z = _dot_nt(qb, kb_ref[pl.ds(start, blk), :])
        log_keep = -_softplus(z)
        log_beta = z + log_keep
        if diagonal:
            before = d0 + (q0 - j * blk) > 0
            log_keep = jnp.where(before, log_keep, 0.0)
        hi = log_keep.astype(BF16)
        lo = (log_keep - hi.astype(F32)).astype(BF16)
        later = _dot(jnp.concatenate([hi, lo], axis=0), u)
        later = later[:2 * tq] + later[2 * tq:]
        c = c_ref[...]
        w = jnp.exp(log_beta + later + c)
        if diagonal:
            w = jnp.where(before, w, 0.0)
        c_ref[...] = c + later[:, 0:1] + log_keep[:, 0:1]
        pv = _dot(w.astype(BF16), vb_ref[pl.ds(start, blk), :])
        acc_ref[...] += jnp.where(lane < hd, pv[:tq], pv[tq:])

    tile(own, True)

    def more(carry):
        t, c_max = carry
        return jnp.logical_and(t <= own, c_max > EXP_UNDERFLOW)

    def body(carry):
        t, _ = carry
        tile(own - t, False)
        return t + 1, jnp.max(c_ref[...])

    lax.while_loop(more, body, (jnp.int32(1), jnp.max(c_ref[...])))
    o_ref[...] = acc_ref[...]


def _stick(q, k, v, *, batch, q_offset):
    tq = min(MOBA_BLOCK, q.shape[0] // batch)
    lk = k.shape[0] // batch
    return _attn_call(_stick_kernel, q, k, v, batch=batch, q_offset=q_offset, n_heads=H_D,
                      extra_scratch=(pltpu.VMEM((lk, LANES), BF16), pltpu.VMEM((lk, LANES), BF16),
                                     pltpu.VMEM((2 * tq, 1), F32), pltpu.VMEM((tq, LANES), F32)), name="stick")


CHUNK = 64
SUB = 16
SEQ_BLOCK = 512
EXP_CLAMP = 80.0


def _dot_mid(a, b, dot=_dot):
    a0 = a.astype(BF16)
    a1 = (a - a0.astype(F32)).astype(BF16)
    b0 = b.astype(BF16)
    b1 = (b - b0.astype(F32)).astype(BF16)
    return dot(a0, b0) + (dot(a0, b1) + dot(a1, b0))


def _incl_lower(c):
    r = lax.broadcasted_iota(jnp.int32, (c, c), 0)
    s = lax.broadcasted_iota(jnp.int32, (c, c), 1)
    return r, s


def _hgrn_kernel(q_ref, f_ref, i_ref, g_ref, lb_ref, gn_ref, s0_ref, o_ref, s_ref, st_ref, *, c, n_chunks, n_heads,
                 layer):
    l = pl.program_id(1)
    dk = LANES

    w = n_heads * dk

    @pl.when(l == 0)
    def _():
        for h in range(n_heads):
            st_ref[h * dk:(h + 1) * dk, :] = s0_ref[0, h]

    r, s = _incl_lower(c)
    tri = jnp.where(s <= r, 1.0, 0.0).astype(BF16)
    sub = min(SUB, c)
    n = n_heads * c
    shift_c, shift_s = c.bit_length() - 1, sub.bit_length() - 1
    ri = lax.broadcasted_iota(jnp.int32, (n_heads * sub, n), 0)
    ci = lax.broadcasted_iota(jnp.int32, (n_heads * sub, n), 1)
    pair = jnp.right_shift(ri, shift_s) == jnp.right_shift(ci, shift_c)
    t_loc, s_loc = jnp.bitwise_and(ri, sub - 1), jnp.bitwise_and(ci, c - 1)
    own_cols = (jnp.right_shift(lax.broadcasted_iota(jnp.int32, (n, w), 0), shift_c)
                == lax.broadcasted_iota(jnp.int32, (n, w), 1) // dk)
    gn = gn_ref[...]
    e = jnp.exp(lb_ref[...] - jnp.max(lb_ref[...], axis=0, keepdims=True))
    lb = jnp.sum(e[:layer + 1], axis=0, keepdims=True) / jnp.sum(e, axis=0, keepdims=True)

    def stack(t):
        return jnp.concatenate([t[:, h * dk:(h + 1) * dk] for h in range(n_heads)], axis=0)

    def chunk(i, carry):
        rows = pl.ds(pl.multiple_of(i * c, c), c)
        z = f_ref[rows, :]
        f = lb + (1.0 - lb) * jax.nn.sigmoid(z)
        k = (1.0 - lb) * jax.nn.sigmoid(-z)
        q = jax.nn.silu(q_ref[rows, :])
        cg = _dot_exact_lhs(tri, jnp.log(f))
        v4 = stack(i_ref[rows, :]).astype(BF16)
        st = st_ref[...]
        intra = []
        for b in range(c // sub):
            lo, hi = b * sub, (b + 1) * sub
            ref_pt = cg[lo - 1:lo] if b else jnp.zeros((1, w), F32)
            qs = stack(q[lo:hi] * jnp.exp(cg[lo:hi] - ref_pt))
            ks = stack(k * jnp.exp(jnp.minimum(ref_pt - cg, EXP_CLAMP)))
            a = _dot_nt(qs.astype(BF16), ks.astype(BF16))
            a = jnp.where(jnp.logical_and(pair, s_loc <= t_loc + lo), a, 0.0)
            intra.append(_dot(a.astype(BF16), v4))
        from_state = _dot_nt(stack(q * jnp.exp(cg)).astype(BF16), st.astype(BF16))
        last = cg[c - 1:c]
        kt = stack(k * jnp.exp(last - cg))
        v_wide = jnp.where(own_cols, jnp.concatenate([v4] * n_heads, axis=1), 0.0)
        e_last = jnp.concatenate([jnp.broadcast_to(jnp.exp(last[:, h * dk:(h + 1) * dk]), (dk, dk))
                                  for h in range(n_heads)], axis=0)
        st_ref[...] = st * e_last + _dot_tn(v_wide, kt.astype(BF16))
        for h in range(n_heads):
            o = from_state[h * c:(h + 1) * c, h * dk:(h + 1) * dk] + jnp.concatenate(
                [part[h * sub:(h + 1) * sub] for part in intra], axis=0)
            o = o * lax.rsqrt(jnp.mean(o * o, axis=-1, keepdims=True) + RMS_EPS) * gn
            o_ref[rows, h * dk:(h + 1) * dk] = o * jax.nn.sigmoid(g_ref[rows, h * dk:(h + 1) * dk])
        return carry

    lax.fori_loop(0, n_chunks, chunk, 0)

    @pl.when(l == pl.num_programs(1) - 1)
    def _():
        for h in range(n_heads):
            s_ref[0, h] = st_ref[h * dk:(h + 1) * dk, :]


def _hgrn(hg, lb, gn, s0, *, batch, layer=0):
    t, width = hg.shape
    seq = t // batch
    w = width // 4
    n_heads = w // LANES
    c = min(CHUNK, seq)
    lblk = min(SEQ_BLOCK, seq)
    assert seq % lblk == 0 and lblk % c == 0 and c % min(SUB, c) == 0
    nl = seq // lblk
    part = lambda p: pl.BlockSpec((lblk, w), lambda b, l: (b * nl + l, p))
    state = pl.BlockSpec((1, n_heads, LANES, LANES), lambda b, l: (b, 0, 0, 0))
    o, st = pl.pallas_call(
        functools.partial(_hgrn_kernel, c=c, n_chunks=lblk // c, n_heads=n_heads, layer=layer),
        grid=(batch, nl),
        in_specs=[part(0), part(1), part(2), part(3),
                  pl.BlockSpec(lb.shape, lambda b, l: (0, 0)), pl.BlockSpec((1, LANES), lambda b, l: (0, 0)), state],
        out_specs=[pl.BlockSpec((lblk, w), lambda b, l: (b * nl + l, 0)), state],
        out_shape=[jax.ShapeDtypeStruct((t, w), F32), jax.ShapeDtypeStruct(s0.shape, F32)],
        scratch_shapes=[pltpu.VMEM((n_heads * LANES, LANES), F32)],
        compiler_params=_cparams(("parallel", "arbitrary")),
        name="hgrn2",
    )(hg, hg, hg, hg, lb, gn.reshape(1, LANES), jnp.swapaxes(s0, -1, -2))
    return o, jnp.swapaxes(st, -1, -2)


def _unit_lower_inverse(lm, n, period):
    bs = min(SUB, period)
    r, s = _incl_lower(n)
    eye = jnp.where(r == s, 1.0, 0.0)
    shift = bs.bit_length() - 1
    same = jnp.right_shift(r, shift) == jnp.right_shift(s, shift)

    def neumann(a, order):
        inv, pw, k = eye + a, a, 2
        while k < order:
            pw = _dot_mid(pw, pw)
            inv = inv + _dot_mid(inv, pw)
            k *= 2
        return inv

    inv_d = neumann(-jnp.where(same, lm, 0.0), bs)
    if period == bs:
        return inv_d
    m = _dot_mid(inv_d, jnp.where(same, 0.0, lm))
    return _dot_mid(neumann(-m, period // bs), inv_d)


def _gdn_kernel(x_ref, ab_ref, g_ref, cw_ref, al_ref, dtb_ref, gn_ref, s0_ref, cb_ref, o_ref, s_ref, nb_ref,
                xx_ref, y_ref, st_ref, *, c, n_chunks, n_heads, lblk):
    l = pl.program_id(1)
    dk = LANES
    halo = 8
    w = n_heads * dk

    @pl.when(l == 0)
    def _():
        for h in range(n_heads):
            st_ref[:, h * dk:(h + 1) * dk] = s0_ref[0, h]
        xx_ref[halo - (CONV_W - 1):halo, :] = cb_ref[0]

    xx_ref[halo:halo + lblk, :] = x_ref[...]
    y = xx_ref[halo:halo + lblk, :] * cw_ref[CONV_W - 1:CONV_W, :]
    for i in range(CONV_W - 1):
        off = halo - (CONV_W - 1) + i
        y = y + xx_ref[off:off + lblk, :] * cw_ref[i:i + 1, :]
    y_ref[...] = jax.nn.silu(y)
    tail = xx_ref[halo + lblk - (CONV_W - 1):halo + lblk, :]
    xx_ref[halo - (CONV_W - 1):halo, :] = tail

    r, s = _incl_lower(c)
    tri = jnp.where(s <= r, 1.0, 0.0).astype(BF16)
    n = n_heads * c
    rr, ss = _incl_lower(n)
    shift = c.bit_length() - 1
    same_head = jnp.right_shift(rr, shift) == jnp.right_shift(ss, shift)
    incl = jnp.logical_and(same_head, ss <= rr)
    strict = jnp.logical_and(same_head, ss < rr)
    own_cols = (jnp.right_shift(lax.broadcasted_iota(jnp.int32, (n, w), 0), shift)
                == lax.broadcasted_iota(jnp.int32, (n, w), 1) // dk)
    gn = gn_ref[...]

    def l2n(t):
        return t * lax.rsqrt(jnp.sum(t * t, axis=-1, keepdims=True) + RMS_EPS)

    def stack(f):
        return jnp.concatenate([f(h) for h in range(n_heads)], axis=0)

    def own_block(t):
        return stack(lambda h: t[h * c:(h + 1) * c, h * dk:(h + 1) * dk])

    def chunk(i, carry):
        rows = pl.ds(pl.multiple_of(i * c, c), c)
        ab = ab_ref[rows, :]
        pre = ab + dtb_ref[...]
        log_a = -jnp.exp(al_ref[...]) * (jnp.maximum(pre, 0.0) + jnp.log(1.0 + jnp.exp(-jnp.abs(pre))))
        beta_all = jax.nn.sigmoid(ab)
        cg_all = _dot_exact_lhs(tri, log_a)
        cg_t = cg_all.T
        q = stack(lambda h: l2n(y_ref[rows, h * dk:(h + 1) * dk])) * dk ** -0.5
        k = stack(lambda h: l2n(y_ref[rows, w + h * dk:w + (h + 1) * dk]))
        v = stack(lambda h: y_ref[rows, 2 * w + h * dk:2 * w + (h + 1) * dk])
        cg = stack(lambda h: cg_all[:, h:h + 1])
        beta = stack(lambda h: beta_all[:, n_heads + h:n_heads + h + 1])
        cg_row = jnp.concatenate([cg_t[h:h + 1, :c] for h in range(n_heads)], axis=1)
        last = stack(lambda h: jnp.broadcast_to(cg_all[c - 1:c, h:h + 1], (c, 1)))
        decay = jnp.exp(jnp.where(incl, cg - cg_row, NEG))
        kk = _dot_mid(k, k, dot=_dot_nt) * decay * beta
        t_inv = _unit_lower_inverse(jnp.where(strict, kk, 0.0), n, c)
        e_cg = jnp.exp(cg)
        sol = _dot_mid(t_inv, jnp.concatenate([v * beta, k * (beta * e_cg)], axis=1))
        st = st_ref[...]
        stb = st.astype(BF16)
        u = sol[:, :dk] - own_block(_dot(sol[:, dk:].astype(BF16), stb))
        qk = _dot_nt(q.astype(BF16), k.astype(BF16)) * decay
        o = own_block(_dot((q * e_cg).astype(BF16), stb)) + _dot(qk.astype(BF16), u.astype(BF16))
        kt = k * jnp.exp(last - cg)
        u_wide = jnp.where(own_cols, jnp.concatenate([u] * n_heads, axis=1), 0.0)
        e_last = jnp.concatenate([jnp.broadcast_to(jnp.exp(cg_all[c - 1:c, h:h + 1]), (1, dk)) for h in range(n_heads)],
                                 axis=1)
        st_ref[...] = st * e_last + _dot_tn(kt.astype(BF16), u_wide.astype(BF16))
        o = o * lax.rsqrt(jnp.mean(o * o, axis=-1, keepdims=True) + RMS_EPS) * gn
        for h in range(n_heads):
            o_ref[rows, h * dk:(h + 1) * dk] = o[h * c:(h + 1) * c] * jax.nn.silu(g_ref[rows, h * dk:(h + 1) * dk])
        return carry

    lax.fori_loop(0, n_chunks, chunk, 0)

    @pl.when(l == pl.num_programs(1) - 1)
    def _():
        for h in range(n_heads):
            s_ref[0, h] = st_ref[:, h * dk:(h + 1) * dk]
        nb_ref[0] = tail


def _gdn(qkv, ab, g, conv_w, a_log, dt_bias, gn, s0, conv_buf, *, batch):
    t, width = qkv.shape
    seq = t // batch
    w = width // 3
    n_heads = w // LANES
    c = min(CHUNK, seq)
    lblk = min(SEQ_BLOCK, seq)
    assert seq % lblk == 0 and lblk % c == 0 and lblk >= CONV_W - 1
    nl = seq // lblk
    pad_lane = lambda vec: jnp.pad(vec, (0, LANES - vec.shape[0])).reshape(1, LANES)
    rows = lambda width_: pl.BlockSpec((lblk, width_), lambda b, l: (b * nl + l, 0))
    const = lambda shape: pl.BlockSpec(shape, lambda b, l: (0,) * len(shape))
    state = pl.BlockSpec((1, n_heads, LANES, LANES), lambda b, l: (b, 0, 0, 0))
    buf = pl.BlockSpec((1, CONV_W - 1, width), lambda b, l: (b, 0, 0))
    return pl.pallas_call(
        functools.partial(_gdn_kernel, c=c, n_chunks=lblk // c, n_heads=n_heads, lblk=lblk),
        grid=(batch, nl),
        in_specs=[rows(width), rows(LANES), rows(w), const((CONV_W, width)), const((1, LANES)), const((1, LANES)),
                  const((1, LANES)), state, buf],
        out_specs=[rows(w), state, buf],
        out_shape=[jax.ShapeDtypeStruct((t, w), F32), jax.ShapeDtypeStruct(s0.shape, F32),
                   jax.ShapeDtypeStruct(conv_buf.shape, F32)],
        scratch_shapes=[pltpu.VMEM((lblk + 8, width), F32), pltpu.VMEM((lblk, width), F32),
                        pltpu.VMEM((LANES, w), F32)],
        compiler_params=_cparams(("parallel", "arbitrary")),
        name="gated_deltanet",
    )(qkv, ab, g, conv_w, pad_lane(a_log), pad_lane(dt_bias), gn.reshape(1, LANES), s0, conv_buf)


PAGES_PER_STEP = 8


def _head_fold(pv, n_heads, nq, hd):
    return jnp.concatenate([pv[h * nq:(h + 1) * nq, h * hd:(h + 1) * hd] for h in range(n_heads)], axis=0)


def _row_ids(rows, nq):
    r = lax.broadcasted_iota(jnp.int32, (rows, 1), 0)
    return r // nq, r % nq


def _page_group(refs, first, count):
    return jnp.concatenate([refs[first + r][0, 0].reshape(-1, refs[first + r].shape[-1]) for r in range(count)], axis=1)


def _stick_paged_kernel(pt_ref, qbd_ref, kn_ref, vn_ref, *rest, pps, n_heads, nq, hd):
    k_refs, v_refs = rest[:pps], rest[pps:2 * pps]
    o_ref, c_ref, acc_ref = rest[2 * pps:]
    s = pl.program_id(1)
    rows = n_heads * nq
    page = k_refs[0].shape[-1]
    ppg = MOBA_BLOCK // page
    _, row_q = _row_ids(rows, nq)
    qbd = (qbd_ref[0] * hd ** -0.5).astype(BF16)

    def strict_later(n):
        return jnp.where(lax.broadcasted_iota(jnp.int32, (n, n), 0) > lax.broadcasted_iota(jnp.int32, (n, n), 1),
                         1.0, 0.0).astype(BF16)

    def weights(z, mask, carry, u):
        log_keep = -_softplus(z)
        log_beta = z + log_keep
        if mask is not None:
            log_keep = jnp.where(mask, log_keep, 0.0)
        hi = log_keep.astype(BF16)
        lo = (log_keep - hi.astype(F32)).astype(BF16)
        later = _dot(hi, u) + _dot(lo, u)
        w = jnp.exp(log_beta + later + carry)
        if mask is not None:
            w = jnp.where(mask, w, 0.0)
        return w.astype(BF16), carry + jnp.sum(log_keep, axis=-1, keepdims=True)

    @pl.when(s == 0)
    def _():
        n = kn_ref.shape[1]
        z = _dot_nt(qbd, kn_ref[0].astype(BF16))
        col = lax.broadcasted_iota(jnp.int32, (rows, n), 1)
        w, c = weights(z, col < row_q, jnp.zeros((rows, 1), F32), strict_later(n))
        c_ref[...] = c
        acc_ref[...] = _dot(w, vn_ref[0].astype(BF16))

    @pl.when(jnp.logical_and(s > 0, jnp.max(c_ref[...]) > EXP_UNDERFLOW))
    def _():
        u = strict_later(ppg * page)
        c = c_ref[...]
        acc = acc_ref[...]
        for g in reversed(range(pps // ppg)):
            z = _dot(qbd, _page_group(k_refs, g * ppg, ppg).astype(BF16))
            w, c = weights(z, None, c, u)
            acc = acc + _dot_nt(w, _page_group(v_refs, g * ppg, ppg).astype(BF16))
        c_ref[...] = c
        acc_ref[...] = acc

    @pl.when(s == pl.num_programs(1) - 1)
    def _():
        o_ref[0] = _head_fold(acc_ref[...], n_heads, nq, hd)


def _moba_paged_kernel(pt_ref, qbd_ref, kn_ref, vn_ref, *rest, pps, n_heads, nq, hd, past_len):
    k_refs, v_refs = rest[:pps], rest[pps:2 * pps]
    o_ref, mo_ref, lo_ref, acco_ref, m_ref, l_ref, acc_ref, gate_ref = rest[2 * pps:]
    s = pl.program_id(1)
    rows = n_heads * nq
    page = k_refs[0].shape[-1]
    ppb = MOBA_BLOCK // page
    row_h, row_q = _row_ids(rows, nq)
    slope = jnp.exp2(-8.0 * (row_h + 1).astype(F32) / n_heads)
    lane = lax.broadcasted_iota(jnp.int32, (rows, LANES), 1)
    q_f32 = qbd_ref[0]
    qbd = (q_f32 * hd ** -0.5).astype(BF16)

    @pl.when(s == 0)
    def _():
        n = kn_ref.shape[1]
        col = lax.broadcasted_iota(jnp.int32, (rows, n), 1)
        sc = _dot_nt(qbd, kn_ref[0].astype(BF16)) - slope * (row_q - col).astype(F32)
        sc = jnp.where(col <= row_q, sc, NEG)
        m = jnp.max(sc, axis=-1, keepdims=True)
        p = jnp.exp(sc - m)
        mo_ref[...] = m
        lo_ref[...] = jnp.sum(p, axis=-1, keepdims=True)
        acco_ref[...] = _head_fold(_dot(p.astype(BF16), vn_ref[0].astype(BF16)), n_heads, nq, hd)
        m_ref[...] = jnp.full(m_ref.shape, NEG, F32)
        l_ref[...] = jnp.zeros_like(l_ref)
        gate_ref[...] = jnp.zeros_like(gate_ref)

    @pl.when(s > 0)
    def _():
        key = lax.broadcasted_iota(jnp.int32, (rows, MOBA_BLOCK), 1)
        m_all, l_all, gate_all = m_ref[...], l_ref[...], gate_ref[...]
        for bi in range(pps // ppb):
            blk = (s - 1) * (pps // ppb) + bi
            kf = _page_group(k_refs, bi * ppb, ppb)
            k_mean = jnp.broadcast_to(jnp.sum(kf, axis=-1, keepdims=True) * (1.0 / MOBA_BLOCK), (kf.shape[0], LANES))
            gate = _dot_mid(q_f32, k_mean)
            dist = past_len + row_q - (blk * MOBA_BLOCK + key)
            sc = _dot(qbd, kf.astype(BF16)) - slope * dist.astype(F32)
            m_b = jnp.max(sc, axis=-1, keepdims=True)
            p = jnp.exp(sc - m_b)
            l_b = jnp.sum(p, axis=-1, keepdims=True)
            pv = _dot_nt(p.astype(BF16), _page_group(v_refs, bi * ppb, ppb).astype(BF16))
            acc_ref[blk] = _head_fold(pv, n_heads, nq, hd)
            m_all = jnp.where(lane == blk, m_b, m_all)
            l_all = jnp.where(lane == blk, l_b, l_all)
            gate_all = jnp.where(lane == blk, gate, gate_all)
        m_ref[...] = m_all
        l_ref[...] = l_all
        gate_ref[...] = gate_all

    @pl.when(s == pl.num_programs(1) - 1)
    def _():
        n_blocks = past_len // MOBA_BLOCK
        sel = _topk_lanes(jnp.where(lane < n_blocks, gate_ref[...], -jnp.inf), lane) > 0.0
        m_all, l_all = m_ref[...], l_ref[...]
        m_o = mo_ref[...]
        m_tot = jnp.maximum(m_o, jnp.max(jnp.where(sel, m_all, NEG), axis=-1, keepdims=True))
        wgt = jnp.where(sel, jnp.exp(m_all - m_tot), 0.0)
        a_o = jnp.exp(m_o - m_tot)
        l_tot = lo_ref[...] * a_o + jnp.sum(wgt * l_all, axis=-1, keepdims=True)
        acc = acco_ref[...] * a_o
        for b in range(n_blocks):
            acc = acc + wgt[:, b:b + 1] * acc_ref[b]
        o_ref[0] = acc / l_tot


def _paged_attention(kind, q, k_new, v_new, k_pool, v_pool, page_table, *, n_heads):
    batch, n_pages = page_table.shape
    page, hd = k_pool.shape[2], k_pool.shape[4]
    nq = q.shape[0] // batch
    rows = n_heads * nq
    pps = PAGES_PER_STEP
    past_len = n_pages * page
    assert n_pages % pps == 0 and past_len % MOBA_BLOCK == 0 and MOBA_BLOCK % page == 0 and pps % (MOBA_BLOCK // page) == 0
    assert nq <= page and past_len // MOBA_BLOCK <= LANES and (2 * LANES) % n_heads == 0
    n_groups = n_pages // pps
    q4 = q.reshape(batch, nq, n_heads, hd).transpose(0, 2, 1, 3)
    q_bd = (q4[:, :, :, None, :] * jnp.eye(n_heads, dtype=F32)[None, :, None, :, None]).reshape(batch, rows, n_heads * hd)
    pad_new = lambda t: jnp.pad(t.reshape(batch, nq, n_heads * hd), ((0, 0), (0, page - nq), (0, 0)))
    k_t, v_t = (jnp.transpose(t, (0, 1, 3, 4, 2)) for t in (k_pool, v_pool))

    if kind == "stick":
        group = lambda s: n_groups - jnp.maximum(s, 1)
        body = functools.partial(_stick_paged_kernel, pps=pps, n_heads=n_heads, nq=nq, hd=hd)
        scratch = [pltpu.VMEM((rows, 1), F32), pltpu.VMEM((rows, n_heads * hd), F32)]
    else:
        group = lambda s: jnp.maximum(s, 1) - 1
        body = functools.partial(_moba_paged_kernel, pps=pps, n_heads=n_heads, nq=nq, hd=hd, past_len=past_len)
        scratch = [pltpu.VMEM((rows, 1), F32), pltpu.VMEM((rows, 1), F32), pltpu.VMEM((rows, hd), F32),
                   pltpu.VMEM((rows, LANES), F32), pltpu.VMEM((rows, LANES), F32),
                   pltpu.VMEM((past_len // MOBA_BLOCK, rows, hd), F32), pltpu.VMEM((rows, LANES), F32)]

    def page_spec(r):
        return pl.BlockSpec((1, 1, n_heads, hd, page), lambda b, s, pt: (0, pt[b, group(s) * pps + r], 0, 0, 0))

    per_seq = lambda shape: pl.BlockSpec((1,) + shape, lambda b, s, pt: (b, 0, 0))
    out = pl.pallas_call(
        body,
        grid_spec=pltpu.PrefetchScalarGridSpec(
            num_scalar_prefetch=1,
            grid=(batch, n_groups + 1),
            in_specs=[per_seq((rows, n_heads * hd)), per_seq((page, n_heads * hd)),
                      per_seq((page, n_heads * hd))] + [page_spec(r) for r in range(pps)] * 2,
            out_specs=per_seq((rows, hd)),
            scratch_shapes=scratch),
        out_shape=jax.ShapeDtypeStruct((batch, rows, hd), F32),
        compiler_params=_cparams(("parallel", "arbitrary")),
        name=kind + "_paged",
    )(page_table, q_bd, pad_new(k_new), pad_new(v_new), *([k_t] * pps), *([v_t] * pps))
    return out.reshape(batch, n_heads, nq, hd).transpose(0, 2, 1, 3).reshape(batch * nq, n_heads * hd)


def _layer_stack(x, batch, past, states, w):
    s_hgrn, s_gdn, conv_buf = states

    def attend(kind, q, k, v, pools, n_heads):
        if past is None:
            return (_moba if kind == "moba" else _stick)(q, k, v, batch=batch, q_offset=0)
        return _paged_attention(kind, q, k, v, *pools, past[4], n_heads=n_heads)

    q_a, k_a, v_a, hg = _project(x, w["in_a"] + [w["in_hgrn"]])
    o_a = attend("moba", q_a, k_a, v_a, past and past[0:2], H_A)
    o_b, s_hgrn_new = _hgrn(hg, w["hgrn_lb"], w["hgrn_norm"], s_hgrn, batch=batch, layer=0)
    x = _matmul([o_a, o_b], w["out_even"], ln_args=(x, w["ln1_g"][0], w["ln1_b"][0]))
    x = _ffn(x, None, w["ffn_wg"], w["ffn_wu"], w["ffn_wd"], w["ln2_g"][0], w["ln2_b"][0])

    qkv_c, ab, g_c, q_d, k_d, v_d = _project(x, [w["in_qkv_c"], w["in_ab"], w["in_g_c"]] + w["in_d"])
    o_c, s_gdn_new, conv_new = _gdn(qkv_c, ab, g_c, w["conv_w"], w["a_log"], w["dt_bias"], w["gdn_norm"],
                                    s_gdn, conv_buf, batch=batch)
    o_d = attend("stick", q_d, k_d, v_d, past and past[2:4], H_D)
    x = _matmul([o_c, o_d], w["out_odd"], ln_args=(x, w["ln1_g"][1], w["ln1_b"][1]))
    comb = _router(x, w["router"])
    x = _moe(x, comb, w["moe_wg"], w["moe_wu"], w["moe_wd"], w["ln2_g"][1], w["ln2_b"][1])
    return x, k_a, v_a, s_hgrn_new, s_gdn_new, conv_new, k_d, v_d


def kernel(x_prompt, x_sample, cache_k_moba, cache_v_moba, state_hgrn, state_gdn, state_gdn_conv, cache_k_sb,
           cache_v_sb, page_table, w_in_even, w_out_even, hgrn_lb, hgrn_norm, w_in_odd, w_out_odd, gdn_conv_w,
           gdn_a_log, gdn_dt_bias, gdn_norm, ln1_g, ln1_b, ln2_g, ln2_b, ffn_wg, ffn_wu, ffn_wd, router, moe_wg,
           moe_wu, moe_wd):
    assert w_in_even.shape[0] == 1 and w_in_odd.shape[0] == 1
    bp, lp, d = x_prompt.shape
    bs, ls, _ = x_sample.shape
    mix = d // 2
    hd_a, hd_d = mix // H_A, mix // H_D
    conv_dim = gdn_conv_w.shape[-1]
    bf = lambda t: t.astype(BF16)
    cols = lambda wt, lo, n: bf(wt[:, lo:lo + n])

    wie, wio = w_in_even[0], w_in_odd[0]
    g_lo = conv_dim + 2 * H_C
    d_lo = g_lo + mix
    w = {
        "in_a": [cols(wie, i * mix, mix) for i in range(3)],
        "in_hgrn": cols(wie, 3 * mix, 4 * mix),
        "out_even": [bf(w_out_even[0][:mix]), bf(w_out_even[0][mix:])],
        "hgrn_lb": hgrn_lb, "hgrn_norm": hgrn_norm[0],
        "in_qkv_c": cols(wio, 0, conv_dim),
        "in_ab": bf(jnp.pad(wio[:, conv_dim:g_lo], ((0, 0), (0, LANES - 2 * H_C)))),
        "in_g_c": cols(wio, g_lo, mix),
        "in_d": [cols(wio, d_lo + i * mix, mix) for i in range(3)],
        "out_odd": [bf(w_out_odd[0][:mix]), bf(w_out_odd[0][mix:])],
        "conv_w": gdn_conv_w[0], "a_log": gdn_a_log[0], "dt_bias": gdn_dt_bias[0], "gdn_norm": gdn_norm[0],
        "ln1_g": ln1_g, "ln1_b": ln1_b, "ln2_g": ln2_g, "ln2_b": ln2_b,
        "ffn_wg": bf(ffn_wg), "ffn_wu": bf(ffn_wu), "ffn_wd": bf(ffn_wd),
        "router": router[0], "moe_wg": bf(moe_wg[0]), "moe_wu": bf(moe_wu[0]), "moe_wd": bf(moe_wd[0]),
    }

    def run(x, batch, past, states):
        seq = x.shape[1]
        y, k_a, v_a, s_h, s_g, cv, k_d, v_d = _layer_stack(x.reshape(batch * seq, d), batch, past, states, w)
        return (y.reshape(batch, seq, d), k_a.reshape(1, batch, seq, H_A, hd_a), v_a.reshape(1, batch, seq, H_A, hd_a),
                s_h[None], s_g[None], cv[None], k_d.reshape(1, batch, seq, H_D, hd_d),
                v_d.reshape(1, batch, seq, H_D, hd_d))

    zeros_p = (jnp.zeros((bp,) + state_hgrn.shape[2:], F32), jnp.zeros((bp,) + state_gdn.shape[2:], F32),
               jnp.zeros((bp,) + state_gdn_conv.shape[2:], F32))
    out_p = run(x_prompt, bp, None, zeros_p)
    past = (cache_k_moba, cache_v_moba, cache_k_sb, cache_v_sb, page_table)
    out_s = run(x_sample, bs, past, (state_hgrn[0], state_gdn[0], state_gdn_conv[0]))
    return (out_p[0], out_s[0]) + out_p[1:] + out_s[1:]
```

```python
import functools

import jax
import jax.numpy as jnp
from jax import lax
from jax.experimental import pallas as pl
from jax.experimental.pallas import tpu as pltpu

F32 = jnp.float32
BF16 = jnp.bfloat16

H_A, H_B, H_C, H_D = 8, 4, 4, 8
MOBA_BLOCK = 256
MOBA_TOPK = 3
CONV_W = 4
N_EXPERTS = 8
DEPTH = 2
DEEPNORM_ALPHA = (2 * DEPTH) ** 0.25
LN_EPS = 1e-5
RMS_EPS = 1e-6
NEG = -1e30
LANES = 128
VMEM_LIMIT = 56 * 1024 * 1024


def _cparams(sem):
    return pltpu.CompilerParams(dimension_semantics=sem, vmem_limit_bytes=VMEM_LIMIT)


def _layernorm(y, g, b):
    mu = jnp.mean(y, axis=-1, keepdims=True)
    yc = y - mu
    var = jnp.mean(yc * yc, axis=-1, keepdims=True)
    return yc * lax.rsqrt(var + LN_EPS) * g + b


def _split3(x):
    hi = x.astype(BF16)
    r = x - hi.astype(F32)
    mid = r.astype(BF16)
    lo = (r - mid.astype(F32)).astype(BF16)
    return hi, mid, lo


def _dot(a, b):
    return jnp.dot(a, b, preferred_element_type=F32)


def _dot_nt(a, b):
    return lax.dot_general(a, b, (((1,), (1,)), ((), ())), preferred_element_type=F32)


def _dot_tn(a, b):
    return lax.dot_general(a, b, (((0,), (0,)), ((), ())), preferred_element_type=F32)


def _dot_hi(a, b, dot=_dot):
    a0, a1, a2 = _split3(a)
    b0, b1, b2 = _split3(b)
    return (dot(a0, b0) + (dot(a0, b1) + dot(a1, b0))
            + (dot(a1, b1) + dot(a0, b2) + dot(a2, b0)))


def _dot_exact_rhs(a, b_bf16):
    a0, a1, a2 = _split3(a)
    return _dot(a0, b_bf16) + _dot(a1, b_bf16) + _dot(a2, b_bf16)


def _dot_exact_lhs(a_bf16, b):
    b0, b1, b2 = _split3(b)
    return _dot(a_bf16, b0) + _dot(a_bf16, b1) + _dot(a_bf16, b2)


def _mm_kernel(*refs, n_in, ln):
    a_refs, w_refs, rest = refs[:n_in], refs[n_in:2 * n_in], refs[2 * n_in:]
    acc = None
    for a, w in zip(a_refs, w_refs):
        d = _dot(a[...].astype(BF16), w[...])
        acc = d if acc is None else acc + d
    if ln:
        res_ref, g_ref, b_ref, o_ref = rest
        o_ref[...] = _layernorm(DEEPNORM_ALPHA * res_ref[...] + acc, g_ref[...], b_ref[...])
    else:
        (o_ref,) = rest
        o_ref[...] = acc


def _row_tile(m, largest=512):
    for t in (1024, 512, 256, 128, 64, 32, 16, 8):
        if t <= largest and m % t == 0:
            return t
    raise ValueError(m)


def _proj_kernel(x_ref, *refs):
    n = len(refs) // 2
    xb = x_ref[...].astype(BF16)
    for w_ref, o_ref in zip(refs[:n], refs[n:]):
        o_ref[...] = _dot(xb, w_ref[...])


def _project(x, w_list):
    m, d = x.shape
    tm = _row_tile(m)
    return pl.pallas_call(
        _proj_kernel,
        grid=(m // tm,),
        in_specs=[pl.BlockSpec((tm, d), lambda i: (i, 0))] + [pl.BlockSpec(w.shape, lambda i: (0, 0)) for w in w_list],
        out_specs=[pl.BlockSpec((tm, w.shape[1]), lambda i: (i, 0)) for w in w_list],
        out_shape=[jax.ShapeDtypeStruct((m, w.shape[1]), F32) for w in w_list],
        compiler_params=_cparams(("parallel",)),
        name="in_proj",
    )(x, *w_list)


def _matmul(a_list, w_list, *, tn=None, ln_args=None):
    m = a_list[0].shape[0]
    n = w_list[0].shape[1]
    tm = _row_tile(m, largest=1024)
    ln = ln_args is not None
    tn = n if (ln or tn is None) else tn
    assert n % tn == 0
    in_specs = [pl.BlockSpec((tm, a.shape[1]), lambda i, j: (i, 0)) for a in a_list]
    in_specs += [pl.BlockSpec((w.shape[0], tn), lambda i, j: (0, j)) for w in w_list]
    args = list(a_list) + list(w_list)
    if ln:
        res, g, b = ln_args
        in_specs += [pl.BlockSpec((tm, n), lambda i, j: (i, 0)),
                     pl.BlockSpec((1, n), lambda i, j: (0, 0)),
                     pl.BlockSpec((1, n), lambda i, j: (0, 0))]
        args += [res, g.reshape(1, n), b.reshape(1, n)]
    return pl.pallas_call(
        functools.partial(_mm_kernel, n_in=len(a_list), ln=ln),
        grid=(m // tm, n // tn),
        in_specs=in_specs,
        out_specs=pl.BlockSpec((tm, tn), lambda i, j: (i, j)),
        out_shape=jax.ShapeDtypeStruct((m, n), F32),
        compiler_params=_cparams(("parallel", "arbitrary")),
        name="matmul_ln" if ln else "matmul",
    )(*args)


def _ffn_kernel(*refs, use_comb):
    if use_comb:
        x_ref, comb_ref, wg_ref, wu_ref, wd_ref, g_ref, b_ref, o_ref, xb_ref, acc_ref = refs
    else:
        x_ref, wg_ref, wu_ref, wd_ref, g_ref, b_ref, o_ref, xb_ref, acc_ref = refs
    e, j = pl.program_id(1), pl.program_id(2)
    first = jnp.logical_and(e == 0, j == 0)
    last = jnp.logical_and(e == pl.num_programs(1) - 1, j == pl.num_programs(2) - 1)

    @pl.when(first)
    def _():
        xb_ref[...] = x_ref[...].astype(BF16)
        acc_ref[...] = jnp.zeros_like(acc_ref)

    xb = xb_ref[...]
    h = jax.nn.silu(_dot(xb, wg_ref[0])) * _dot(xb, wu_ref[0])
    if use_comb:
        lane = lax.broadcasted_iota(jnp.int32, comb_ref.shape, 1)
        h = h * jnp.sum(jnp.where(lane == e, comb_ref[...], 0.0), axis=-1, keepdims=True)
    acc_ref[...] += _dot(h.astype(BF16), wd_ref[0])

    @pl.when(last)
    def _():
        o_ref[...] = _layernorm(DEEPNORM_ALPHA * x_ref[...] + acc_ref[...], g_ref[...], b_ref[...])


def _ff_tile(ff):
    for t in (896, 512, 256, LANES):
        if ff % t == 0:
            return t
    raise ValueError(ff)


def _ffn(x, comb, wg, wu, wd, g, b):
    m, d = x.shape
    n_e, _, ff = wg.shape
    tm = _row_tile(m)
    tf = _ff_tile(ff)
    use_comb = comb is not None
    in_specs = [pl.BlockSpec((tm, d), lambda i, e, j: (i, 0))]
    args = [x]
    if use_comb:
        in_specs.append(pl.BlockSpec((tm, comb.shape[1]), lambda i, e, j: (i, 0)))
        args.append(comb)
    in_specs += [pl.BlockSpec((1, d, tf), lambda i, e, j: (e, 0, j)),
                 pl.BlockSpec((1, d, tf), lambda i, e, j: (e, 0, j)),
                 pl.BlockSpec((1, tf, d), lambda i, e, j: (e, j, 0)),
                 pl.BlockSpec((1, d), lambda i, e, j: (0, 0)),
                 pl.BlockSpec((1, d), lambda i, e, j: (0, 0))]
    args += [wg, wu, wd, g.reshape(1, d), b.reshape(1, d)]
    return pl.pallas_call(
        functools.partial(_ffn_kernel, use_comb=use_comb),
        grid=(m // tm, n_e, ff // tf),
        in_specs=in_specs,
        out_specs=pl.BlockSpec((tm, d), lambda i, e, j: (i, 0)),
        out_shape=jax.ShapeDtypeStruct((m, d), F32),
        scratch_shapes=[pltpu.VMEM((tm, d), BF16), pltpu.VMEM((tm, d), F32)],
        compiler_params=_cparams(("parallel", "arbitrary", "arbitrary")),
        name="moe_ffn" if use_comb else "ffn",
    )(*args)


MOE_TOKENS = 1024
MOE_ROWS = 128


def _moe_kernel(x_ref, comb_ref, wg_ref, wu_ref, wd_ref, g_ref, b_ref, o_ref,
                xb_ref, acc_ref, xc_ref, yc_ref, rank_ref, rank_t_ref, comb_t_ref):
    e, j = pl.program_id(1), pl.program_id(2)
    tm, d = x_ref.shape
    r = MOE_ROWS
    lane = lax.broadcasted_iota(jnp.int32, (tm, LANES), 1)

    @pl.when(jnp.logical_and(e == 0, j == 0))
    def _():
        xb_ref[...] = x_ref[...].astype(BF16)
        acc_ref[...] = jnp.zeros_like(acc_ref)
        routed = jnp.where(comb_ref[...] > 0.0, 1.0, 0.0).astype(BF16)
        earlier = jnp.where(lax.broadcasted_iota(jnp.int32, (tm, tm), 1) < lax.broadcasted_iota(jnp.int32, (tm, tm), 0),
                            1.0, 0.0).astype(BF16)
        rank = _dot(earlier, routed)
        rank_ref[...] = rank
        rank_t_ref[...] = rank.T
        comb_t_ref[...] = comb_ref[...].T

    gate_row = comb_t_ref[pl.ds(e, 1), :]
    n_pass = (jnp.sum(jnp.where(gate_row > 0.0, 1, 0)) + (r - 1)) // r

    def passes(fn):
        def pair(k, carry):
            fn(k * (2 * r), 2 * r)
            return carry

        lax.fori_loop(0, n_pass // 2, pair, 0)

        @pl.when(n_pass % 2 == 1)
        def _():
            fn((n_pass - 1) * r, r)

    @pl.when(j == 0)
    def _():
        rank_row = rank_t_ref[pl.ds(e, 1), :]

        def compact(start, n_rows):
            slot = (start + lax.broadcasted_iota(jnp.int32, (n_rows, tm), 0)).astype(F32)
            pick = jnp.where(jnp.logical_and(rank_row == slot, gate_row > 0.0), 1.0, 0.0).astype(BF16)
            rows = pl.ds(pl.multiple_of(start, r), n_rows)
            xc_ref[rows, :] = _dot(pick, xb_ref[...]).astype(BF16)
            yc_ref[rows, :] = jnp.zeros((n_rows, d), F32)

        passes(compact)

    def expert(start, n_rows):
        rows = pl.ds(pl.multiple_of(start, r), n_rows)
        xk = xc_ref[rows, :]
        h = jax.nn.silu(_dot(xk, wg_ref[0])) * _dot(xk, wu_ref[0])
        yc_ref[rows, :] += _dot(h.astype(BF16), wd_ref[0])

    passes(expert)

    @pl.when(j == pl.num_programs(2) - 1)
    def _():
        rank_col = jnp.sum(jnp.where(lane == e, rank_ref[...], 0.0), axis=-1, keepdims=True)
        gate_col = jnp.sum(jnp.where(lane == e, comb_ref[...], 0.0), axis=-1, keepdims=True)

        def place(start, n_rows):
            slot = (start + lax.broadcasted_iota(jnp.int32, (tm, n_rows), 1)).astype(F32)
            put = jnp.where(jnp.logical_and(rank_col == slot, gate_col > 0.0), 1.0, 0.0).astype(BF16)
            rows = pl.ds(pl.multiple_of(start, r), n_rows)
            acc_ref[...] += gate_col * _dot(put, yc_ref[rows, :].astype(BF16))

        passes(place)

    @pl.when(jnp.logical_and(e == pl.num_programs(1) - 1, j == pl.num_programs(2) - 1))
    def _():
        o_ref[...] = _layernorm(DEEPNORM_ALPHA * x_ref[...] + acc_ref[...], g_ref[...], b_ref[...])


def _moe(x, comb, wg, wu, wd, g, b):
    m, d = x.shape
    n_e, _, ff = wg.shape
    tm = MOE_TOKENS
    if m % tm != 0:
        return _ffn(x, comb, wg, wu, wd, g, b)
    tf = _ff_tile(ff)
    assert tm % (2 * MOE_ROWS) == 0 and comb.shape[1] == LANES and n_e <= LANES
    return pl.pallas_call(
        _moe_kernel,
        grid=(m // tm, n_e, ff // tf),
        in_specs=[pl.BlockSpec((tm, d), lambda i, e, j: (i, 0)),
                  pl.BlockSpec((tm, LANES), lambda i, e, j: (i, 0)),
                  pl.BlockSpec((1, d, tf), lambda i, e, j: (e, 0, j)),
                  pl.BlockSpec((1, d, tf), lambda i, e, j: (e, 0, j)),
                  pl.BlockSpec((1, tf, d), lambda i, e, j: (e, j, 0)),
                  pl.BlockSpec((1, d), lambda i, e, j: (0, 0)),
                  pl.BlockSpec((1, d), lambda i, e, j: (0, 0))],
        out_specs=pl.BlockSpec((tm, d), lambda i, e, j: (i, 0)),
        out_shape=jax.ShapeDtypeStruct((m, d), F32),
        scratch_shapes=[pltpu.VMEM((tm, d), BF16), pltpu.VMEM((tm, d), F32), pltpu.VMEM((tm, d), BF16),
                        pltpu.VMEM((tm, d), F32), pltpu.VMEM((tm, LANES), F32), pltpu.VMEM((LANES, tm), F32),
                        pltpu.VMEM((LANES, tm), F32)],
        compiler_params=_cparams(("parallel", "arbitrary", "arbitrary")),
        name="moe_sparse",
    )(x, comb, wg, wu, wd, g.reshape(1, d), b.reshape(1, d))


def _router_kernel(x_ref, r_ref, o_ref):
    logits = _dot_hi(x_ref[...], r_ref[...])
    lane = lax.broadcasted_iota(jnp.int32, logits.shape, 1)
    logits = jnp.where(lane < N_EXPERTS, logits, -jnp.inf)
    m1 = jnp.max(logits, axis=-1, keepdims=True)
    i1 = jnp.min(jnp.where(logits == m1, lane, LANES), axis=-1, keepdims=True)
    rest = jnp.where(lane == i1, -jnp.inf, logits)
    m2 = jnp.max(rest, axis=-1, keepdims=True)
    i2 = jnp.min(jnp.where(rest == m2, lane, LANES), axis=-1, keepdims=True)
    e2 = jnp.exp(m2 - m1)
    den = 1.0 + e2
    o_ref[...] = jnp.where(lane == i1, 1.0 / den, 0.0) + jnp.where(lane == i2, e2 / den, 0.0)


def _router(x, router):
    m, d = x.shape
    tm = _row_tile(m)
    r = jnp.pad(router, ((0, 0), (0, LANES - router.shape[1])))
    return pl.pallas_call(
        _router_kernel,
        grid=(m // tm,),
        in_specs=[pl.BlockSpec((tm, d), lambda i: (i, 0)), pl.BlockSpec((d, LANES), lambda i: (0, 0))],
        out_specs=pl.BlockSpec((tm, LANES), lambda i: (i, 0)),
        out_shape=jax.ShapeDtypeStruct((m, LANES), F32),
        compiler_params=_cparams(("parallel",)),
        name="router",
    )(x, r)


def _topk_lanes(g, idx, axis=-1):
    sel = jnp.zeros(g.shape, F32)
    for _ in range(MOBA_TOPK):
        m = jnp.max(g, axis=axis, keepdims=True)
        first = jnp.min(jnp.where(g == m, idx, LANES), axis=axis, keepdims=True)
        hit = jnp.logical_and(idx == first, m > -jnp.inf)
        sel = jnp.where(hit, 1.0, sel)
        g = jnp.where(hit, -jnp.inf, g)
    return sel


def _moba_kernel(slope_ref, q_ref, k_ref, v_ref, o_ref, kb_ref, vt_ref, km_ref, m_ref, l_ref, acc_ref, sel_ref,
                 *, tq, nb, q_offset, hd):
    i = pl.program_id(2)
    blk = MOBA_BLOCK

    @pl.when(i == 0)
    def _():
        kb_ref[...] = k_ref[...].astype(BF16)
        km_ref[...] = jnp.zeros_like(km_ref)
        for j in range(nb):
            rows = slice(j * blk, (j + 1) * blk)
            vt_ref[j] = v_ref[rows, :].T.astype(BF16)
            km_ref[j:j + 1, :] = jnp.sum(k_ref[rows, :], axis=0, keepdims=True) * (1.0 / blk)

    q0 = q_offset + i * tq
    own = q0 // blk
    q = q_ref[...]
    lane = lax.broadcasted_iota(jnp.int32, (tq, LANES), 1)
    q2 = jnp.concatenate([jnp.where(lane < hd, q, 0.0), jnp.where(lane >= hd, q, 0.0)], axis=0)
    gate_t = _dot_mid(km_ref[...], q2, dot=_dot_nt)
    blk_id = lax.broadcasted_iota(jnp.int32, gate_t.shape, 0)
    sel_ref[...] = _topk_lanes(jnp.where(blk_id < own, gate_t, -jnp.inf), blk_id, axis=0)
    qb = (q2 * hd ** -0.5).astype(BF16)
    c2 = lax.broadcasted_iota(jnp.int32, (blk, 2 * tq), 1)
    d0 = (jnp.where(c2 >= tq, c2 - tq, c2) - lax.broadcasted_iota(jnp.int32, (blk, 2 * tq), 0)).astype(F32)
    slope_lane = slope_ref[...]
    c1 = lax.broadcasted_iota(jnp.int32, (1, 2 * tq), 1)
    slope = jnp.where(c1 < tq, slope_lane[:, 0:1], slope_lane[:, hd:hd + 1])
    slope_d0 = slope * d0
    first = lax.broadcasted_iota(jnp.int32, (2 * hd, tq), 0) < hd
    m_ref[...] = jnp.full(m_ref.shape, NEG, F32)
    l_ref[...] = jnp.zeros_like(l_ref)
    acc_ref[...] = jnp.zeros_like(acc_ref)

    def tiles(js, diagonal):
        scores = []
        for j in js:
            off = (q0 - j * blk).astype(F32)
            s = _dot_nt(kb_ref[pl.ds(pl.multiple_of(j * blk, blk), blk), :], qb) - (slope_d0 + slope * off)
            if diagonal:
                s = jnp.where(d0 + off >= 0.0, s, NEG)
            else:
                s = jnp.where(sel_ref[pl.ds(j, 1), :] > 0.0, s, NEG)
            scores.append(s)
        m_old = m_ref[...]
        m_new = functools.reduce(jnp.maximum, [m_old] + [jnp.max(s, axis=0, keepdims=True) for s in scores])
        alpha = jnp.exp(m_old - m_new)
        l_new = alpha * l_ref[...]
        pv = None
        for j, s in zip(js, scores):
            p = jnp.exp(s - m_new)
            l_new = l_new + jnp.sum(p, axis=0, keepdims=True)
            d = _dot(vt_ref[j], p.astype(BF16))
            pv = d if pv is None else pv + d
        m_ref[...] = m_new
        l_ref[...] = l_new
        acc_ref[...] = (jnp.where(first, alpha[:, :tq], alpha[:, tq:]) * acc_ref[...]
                        + jnp.where(first, pv[:, :tq], pv[:, tq:]))

    tiles([own], True)

    def body(t, carry):
        tiles([own - 1 - 2 * t, own - 2 - 2 * t], False)
        return carry

    lax.fori_loop(0, own // 2, body, 0)

    @pl.when(own % 2 == 1)
    def _():
        tiles([own - own], False)
    l = l_ref[...]
    o_ref[...] = (acc_ref[...] / jnp.where(first, l[:, :tq], l[:, tq:])).T


def _alibi_slopes_lanes(n_heads, hd):
    slopes = jnp.exp2(-8.0 * jnp.arange(1, n_heads + 1, dtype=F32) / n_heads)
    return jnp.repeat(slopes, hd).reshape(1, n_heads * hd)


def _attn_call(kernel, q, k, v, *, batch, q_offset, n_heads, extra_in=(), extra_specs=(), extra_scratch=(), name):
    lq, lk = q.shape[0] // batch, k.shape[0] // batch
    hd = q.shape[1] // n_heads
    assert 2 * hd == LANES and lk % MOBA_BLOCK == 0
    tq = min(MOBA_BLOCK, lq)
    assert lq % tq == 0 and MOBA_BLOCK % tq == 0 and q_offset % tq == 0
    nq = lq // tq
    return pl.pallas_call(
        functools.partial(kernel, tq=tq, q_offset=q_offset, hd=hd),
        grid=(batch, n_heads // 2, nq),
        in_specs=list(extra_specs) + [
            pl.BlockSpec((tq, LANES), lambda b, h, i: (b * nq + i, h)),
            pl.BlockSpec((lk, LANES), lambda b, h, i: (b, h)),
            pl.BlockSpec((lk, LANES), lambda b, h, i: (b, h))],
        out_specs=pl.BlockSpec((tq, LANES), lambda b, h, i: (b * nq + i, h)),
        out_shape=jax.ShapeDtypeStruct(q.shape, F32),
        scratch_shapes=list(extra_scratch),
        compiler_params=_cparams(("parallel", "parallel", "arbitrary")),
        name=name,
    )(*extra_in, q, k, v)


def _moba(q, k, v, *, batch, q_offset):
    nb = k.shape[0] // batch // MOBA_BLOCK
    assert nb <= LANES
    slopes = _alibi_slopes_lanes(H_A, q.shape[1] // H_A)
    tq = min(MOBA_BLOCK, q.shape[0] // batch)
    nb_rows = -(-nb // 8) * 8
    return _attn_call(
        functools.partial(_moba_kernel, nb=nb), q, k, v, batch=batch, q_offset=q_offset, n_heads=H_A,
        extra_in=(slopes,), extra_specs=(pl.BlockSpec((1, LANES), lambda b, h, i: (0, h)),),
        extra_scratch=(pltpu.VMEM((nb * MOBA_BLOCK, LANES), BF16), pltpu.VMEM((nb, LANES, MOBA_BLOCK), BF16),
                       pltpu.VMEM((nb_rows, LANES), F32), pltpu.VMEM((1, 2 * tq), F32), pltpu.VMEM((1, 2 * tq), F32),
                       pltpu.VMEM((LANES, tq), F32), pltpu.VMEM((nb_rows, 2 * tq), F32)), name="moba")


EXP_UNDERFLOW = -110.0


def _softplus(z):
    return jnp.maximum(z, 0.0) + jnp.log(1.0 + jnp.exp(-jnp.abs(z)))


def _stick_kernel(q_ref, k_ref, v_ref, o_ref, kb_ref, vb_ref, c_ref, acc_ref, *, tq, q_offset, hd):
    i = pl.program_id(2)
    blk = MOBA_BLOCK

    @pl.when(i == 0)
    def _():
        kb_ref[...] = k_ref[...].astype(BF16)
        vb_ref[...] = v_ref[...].astype(BF16)

    q0 = q_offset + i * tq
    own = q0 // blk
    q = q_ref[...] * hd ** -0.5
    lane = lax.broadcasted_iota(jnp.int32, (tq, LANES), 1)
    qb = jnp.concatenate([jnp.where(lane < hd, q, 0.0), jnp.where(lane >= hd, q, 0.0)], axis=0).astype(BF16)
    r2 = lax.broadcasted_iota(jnp.int32, (2 * tq, blk), 0)
    d0 = jnp.where(r2 >= tq, r2 - tq, r2) - lax.broadcasted_iota(jnp.int32, (2 * tq, blk), 1)
    u = jnp.where(lax.broadcasted_iota(jnp.int32, (blk, blk), 0) > lax.broadcasted_iota(jnp.int32, (blk, blk), 1),
                  1.0, 0.0).astype(BF16)
    c_ref[...] = jnp.zeros_like(c_ref)
    acc_ref[...] = jnp.zeros_like(acc_ref)

    def tile(j, diagonal):
        start = pl.multiple_of(j * blk, blk)
        z = _dot_nt(qb, kb_ref[pl.ds(start, blk), :])
        log_keep = -_softplus(z)
        log_beta = z + log_keep
        if diagonal:
            before = d0 + (q0 - j * blk) > 0
            log_keep = jnp.where(before, log_keep, 0.0)
        hi = log_keep.astype(BF16)
        lo = (log_keep - hi.astype(F32)).astype(BF16)
        later = _dot(jnp.concatenate([hi, lo], axis=0), u)
        later = later[:2 * tq] + later[2 * tq:]
        c = c_ref[...]
        w = jnp.exp(log_beta + later + c)
        if diagonal:
            w = jnp.where(before, w, 0.0)
        c_ref[...] = c + later[:, 0:1] + log_keep[:, 0:1]
        pv = _dot(w.astype(BF16), vb_ref[pl.ds(start, blk), :])
        acc_ref[...] += jnp.where(lane < hd, pv[:tq], pv[tq:])

    tile(own, True)

    def more(carry):
        t, c_max = carry
        return jnp.logical_and(t <= own, c_max > EXP_UNDERFLOW)

    def body(carry):
        t, _ = carry
        tile(own - t, False)
        return t + 1, jnp.max(c_ref[...])

    lax.while_loop(more, body, (jnp.int32(1), jnp.max(c_ref[...])))
    o_ref[...] = acc_ref[...]


def _stick(q, k, v, *, batch, q_offset):
    tq = min(MOBA_BLOCK, q.shape[0] // batch)
    lk = k.shape[0] // batch
    return _attn_call(_stick_kernel, q, k, v, batch=batch, q_offset=q_offset, n_heads=H_D,
                      extra_scratch=(pltpu.VMEM((lk, LANES), BF16), pltpu.VMEM((lk, LANES), BF16),
                                     pltpu.VMEM((2 * tq, 1), F32), pltpu.VMEM((tq, LANES), F32)), name="stick")


CHUNK = 64
SUB = 16
SEQ_BLOCK = 512
EXP_CLAMP = 80.0


def _dot_mid(a, b, dot=_dot):
    a0 = a.astype(BF16)
    a1 = (a - a0.astype(F32)).astype(BF16)
    b0 = b.astype(BF16)
    b1 = (b - b0.astype(F32)).astype(BF16)
    return dot(a0, b0) + (dot(a0, b1) + dot(a1, b0))


def _incl_lower(c):
    r = lax.broadcasted_iota(jnp.int32, (c, c), 0)
    s = lax.broadcasted_iota(jnp.int32, (c, c), 1)
    return r, s


def _hgrn_kernel(q_ref, f_ref, i_ref, g_ref, lb_ref, gn_ref, s0_ref, o_ref, s_ref, st_ref, *, c, n_chunks, n_heads,
                 layer):
    l = pl.program_id(1)
    dk = LANES

    w = n_heads * dk

    @pl.when(l == 0)
    def _():
        for h in range(n_heads):
            st_ref[h * dk:(h + 1) * dk, :] = s0_ref[0, h]

    r, s = _incl_lower(c)
    tri = jnp.where(s <= r, 1.0, 0.0).astype(BF16)
    sub = min(SUB, c)
    n = n_heads * c
    shift_c, shift_s = c.bit_length() - 1, sub.bit_length() - 1
    ri = lax.broadcasted_iota(jnp.int32, (n_heads * sub, n), 0)
    ci = lax.broadcasted_iota(jnp.int32, (n_heads * sub, n), 1)
    pair = jnp.right_shift(ri, shift_s) == jnp.right_shift(ci, shift_c)
    t_loc, s_loc = jnp.bitwise_and(ri, sub - 1), jnp.bitwise_and(ci, c - 1)
    own_cols = (jnp.right_shift(lax.broadcasted_iota(jnp.int32, (n, w), 0), shift_c)
                == lax.broadcasted_iota(jnp.int32, (n, w), 1) // dk)
    gn = gn_ref[...]
    e = jnp.exp(lb_ref[...] - jnp.max(lb_ref[...], axis=0, keepdims=True))
    lb = jnp.sum(e[:layer + 1], axis=0, keepdims=True) / jnp.sum(e, axis=0, keepdims=True)

    def stack(t):
        return jnp.concatenate([t[:, h * dk:(h + 1) * dk] for h in range(n_heads)], axis=0)

    def chunk(i, carry):
        rows = pl.ds(pl.multiple_of(i * c, c), c)
        z = f_ref[rows, :]
        f = lb + (1.0 - lb) * jax.nn.sigmoid(z)
        k = (1.0 - lb) * jax.nn.sigmoid(-z)
        q = jax.nn.silu(q_ref[rows, :])
        cg = _dot_exact_lhs(tri, jnp.log(f))
        v4 = stack(i_ref[rows, :]).astype(BF16)
        st = st_ref[...]
        intra = []
        for b in range(c // sub):
            lo, hi = b * sub, (b + 1) * sub
            ref_pt = cg[lo - 1:lo] if b else jnp.zeros((1, w), F32)
            qs = stack(q[lo:hi] * jnp.exp(cg[lo:hi] - ref_pt))
            ks = stack(k * jnp.exp(jnp.minimum(ref_pt - cg, EXP_CLAMP)))
            a = _dot_nt(qs.astype(BF16), ks.astype(BF16))
            a = jnp.where(jnp.logical_and(pair, s_loc <= t_loc + lo), a, 0.0)
            intra.append(_dot(a.astype(BF16), v4))
        from_state = _dot_nt(stack(q * jnp.exp(cg)).astype(BF16), st.astype(BF16))
        last = cg[c - 1:c]
        kt = stack(k * jnp.exp(last - cg))
        v_wide = jnp.where(own_cols, jnp.concatenate([v4] * n_heads, axis=1), 0.0)
        e_last = jnp.concatenate([jnp.broadcast_to(jnp.exp(last[:, h * dk:(h + 1) * dk]), (dk, dk))
                                  for h in range(n_heads)], axis=0)
        st_ref[...] = st * e_last + _dot_tn(v_wide, kt.astype(BF16))
        for h in range(n_heads):
            o = from_state[h * c:(h + 1) * c, h * dk:(h + 1) * dk] + jnp.concatenate(
                [part[h * sub:(h + 1) * sub] for part in intra], axis=0)
            o = o * lax.rsqrt(jnp.mean(o * o, axis=-1, keepdims=True) + RMS_EPS) * gn
            o_ref[rows, h * dk:(h + 1) * dk] = o * jax.nn.sigmoid(g_ref[rows, h * dk:(h + 1) * dk])
        return carry

    lax.fori_loop(0, n_chunks, chunk, 0)

    @pl.when(l == pl.num_programs(1) - 1)
    def _():
        for h in range(n_heads):
            s_ref[0, h] = st_ref[h * dk:(h + 1) * dk, :]


def _hgrn(hg, lb, gn, s0, *, batch, layer=0):
    t, width = hg.shape
    seq = t // batch
    w = width // 4
    n_heads = w // LANES
    c = min(CHUNK, seq)
    lblk = min(SEQ_BLOCK, seq)
    assert seq % lblk == 0 and lblk % c == 0 and c % min(SUB, c) == 0
    nl = seq // lblk
    part = lambda p: pl.BlockSpec((lblk, w), lambda b, l: (b * nl + l, p))
    state = pl.BlockSpec((1, n_heads, LANES, LANES), lambda b, l: (b, 0, 0, 0))
    o, st = pl.pallas_call(
        functools.partial(_hgrn_kernel, c=c, n_chunks=lblk // c, n_heads=n_heads, layer=layer),
        grid=(batch, nl),
        in_specs=[part(0), part(1), part(2), part(3),
                  pl.BlockSpec(lb.shape, lambda b, l: (0, 0)), pl.BlockSpec((1, LANES), lambda b, l: (0, 0)), state],
        out_specs=[pl.BlockSpec((lblk, w), lambda b, l: (b * nl + l, 0)), state],
        out_shape=[jax.ShapeDtypeStruct((t, w), F32), jax.ShapeDtypeStruct(s0.shape, F32)],
        scratch_shapes=[pltpu.VMEM((n_heads * LANES, LANES), F32)],
        compiler_params=_cparams(("parallel", "arbitrary")),
        name="hgrn2",
    )(hg, hg, hg, hg, lb, gn.reshape(1, LANES), jnp.swapaxes(s0, -1, -2))
    return o, jnp.swapaxes(st, -1, -2)


def _unit_lower_inverse(lm, n, period):
    bs = min(SUB, period)
    r, s = _incl_lower(n)
    eye = jnp.where(r == s, 1.0, 0.0)
    shift = bs.bit_length() - 1
    same = jnp.right_shift(r, shift) == jnp.right_shift(s, shift)

    def neumann(a, order):
        inv, pw, k = eye + a, a, 2
        while k < order:
            pw = _dot_mid(pw, pw)
            inv = inv + _dot_mid(inv, pw)
            k *= 2
        return inv

    inv_d = neumann(-jnp.where(same, lm, 0.0), bs)
    if period == bs:
        return inv_d
    m = _dot_mid(inv_d, jnp.where(same, 0.0, lm))
    return _dot_mid(neumann(-m, period // bs), inv_d)


def _gdn_kernel(x_ref, ab_ref, g_ref, cw_ref, al_ref, dtb_ref, gn_ref, s0_ref, cb_ref, o_ref, s_ref, nb_ref,
                xx_ref, y_ref, st_ref, *, c, n_chunks, n_heads, lblk):
    l = pl.program_id(1)
    dk = LANES
    halo = 8
    w = n_heads * dk

    @pl.when(l == 0)
    def _():
        for h in range(n_heads):
            st_ref[:, h * dk:(h + 1) * dk] = s0_ref[0, h]
        xx_ref[halo - (CONV_W - 1):halo, :] = cb_ref[0]

    xx_ref[halo:halo + lblk, :] = x_ref[...]
    y = xx_ref[halo:halo + lblk, :] * cw_ref[CONV_W - 1:CONV_W, :]
    for i in range(CONV_W - 1):
        off = halo - (CONV_W - 1) + i
        y = y + xx_ref[off:off + lblk, :] * cw_ref[i:i + 1, :]
    y_ref[...] = jax.nn.silu(y)
    tail = xx_ref[halo + lblk - (CONV_W - 1):halo + lblk, :]
    xx_ref[halo - (CONV_W - 1):halo, :] = tail

    r, s = _incl_lower(c)
    tri = jnp.where(s <= r, 1.0, 0.0).astype(BF16)
    n = n_heads * c
    rr, ss = _incl_lower(n)
    shift = c.bit_length() - 1
    same_head = jnp.right_shift(rr, shift) == jnp.right_shift(ss, shift)
    incl = jnp.logical_and(same_head, ss <= rr)
    strict = jnp.logical_and(same_head, ss < rr)
    own_cols = (jnp.right_shift(lax.broadcasted_iota(jnp.int32, (n, w), 0), shift)
                == lax.broadcasted_iota(jnp.int32, (n, w), 1) // dk)
    gn = gn_ref[...]

    def l2n(t):
        return t * lax.rsqrt(jnp.sum(t * t, axis=-1, keepdims=True) + RMS_EPS)

    def stack(f):
        return jnp.concatenate([f(h) for h in range(n_heads)], axis=0)

    def own_block(t):
        return stack(lambda h: t[h * c:(h + 1) * c, h * dk:(h + 1) * dk])

    def chunk(i, carry):
        rows = pl.ds(pl.multiple_of(i * c, c), c)
        ab = ab_ref[rows, :]
        pre = ab + dtb_ref[...]
        log_a = -jnp.exp(al_ref[...]) * (jnp.maximum(pre, 0.0) + jnp.log(1.0 + jnp.exp(-jnp.abs(pre))))
        beta_all = jax.nn.sigmoid(ab)
        cg_all = _dot_exact_lhs(tri, log_a)
        cg_t = cg_all.T
        q = stack(lambda h: l2n(y_ref[rows, h * dk:(h + 1) * dk])) * dk ** -0.5
        k = stack(lambda h: l2n(y_ref[rows, w + h * dk:w + (h + 1) * dk]))
        v = stack(lambda h: y_ref[rows, 2 * w + h * dk:2 * w + (h + 1) * dk])
        cg = stack(lambda h: cg_all[:, h:h + 1])
        beta = stack(lambda h: beta_all[:, n_heads + h:n_heads + h + 1])
        cg_row = jnp.concatenate([cg_t[h:h + 1, :c] for h in range(n_heads)], axis=1)
        last = stack(lambda h: jnp.broadcast_to(cg_all[c - 1:c, h:h + 1], (c, 1)))
        decay = jnp.exp(jnp.where(incl, cg - cg_row, NEG))
        kk = _dot_mid(k, k, dot=_dot_nt) * decay * beta
        t_inv = _unit_lower_inverse(jnp.where(strict, kk, 0.0), n, c)
        e_cg = jnp.exp(cg)
        sol = _dot_mid(t_inv, jnp.concatenate([v * beta, k * (beta * e_cg)], axis=1))
        st = st_ref[...]
        stb = st.astype(BF16)
        u = sol[:, :dk] - own_block(_dot(sol[:, dk:].astype(BF16), stb))
        qk = _dot_nt(q.astype(BF16), k.astype(BF16)) * decay
        o = own_block(_dot((q * e_cg).astype(BF16), stb)) + _dot(qk.astype(BF16), u.astype(BF16))
        kt = k * jnp.exp(last - cg)
        u_wide = jnp.where(own_cols, jnp.concatenate([u] * n_heads, axis=1), 0.0)
        e_last = jnp.concatenate([jnp.broadcast_to(jnp.exp(cg_all[c - 1:c, h:h + 1]), (1, dk)) for h in range(n_heads)],
                                 axis=1)
        st_ref[...] = st * e_last + _dot_tn(kt.astype(BF16), u_wide.astype(BF16))
        o = o * lax.rsqrt(jnp.mean(o * o, axis=-1, keepdims=True) + RMS_EPS) * gn
        for h in range(n_heads):
            o_ref[rows, h * dk:(h + 1) * dk] = o[h * c:(h + 1) * c] * jax.nn.silu(g_ref[rows, h * dk:(h + 1) * dk])
        return carry

    lax.fori_loop(0, n_chunks, chunk, 0)

    @pl.when(l == pl.num_programs(1) - 1)
    def _():
        for h in range(n_heads):
            s_ref[0, h] = st_ref[:, h * dk:(h + 1) * dk]
        nb_ref[0] = tail


def _gdn(qkv, ab, g, conv_w, a_log, dt_bias, gn, s0, conv_buf, *, batch):
    t, width = qkv.shape
    seq = t // batch
    w = width // 3
    n_heads = w // LANES
    c = min(CHUNK, seq)
    lblk = min(SEQ_BLOCK, seq)
    assert seq % lblk == 0 and lblk % c == 0 and lblk >= CONV_W - 1
    nl = seq // lblk
    pad_lane = lambda vec: jnp.pad(vec, (0, LANES - vec.shape[0])).reshape(1, LANES)
    rows = lambda width_: pl.BlockSpec((lblk, width_), lambda b, l: (b * nl + l, 0))
    const = lambda shape: pl.BlockSpec(shape, lambda b, l: (0,) * len(shape))
    state = pl.BlockSpec((1, n_heads, LANES, LANES), lambda b, l: (b, 0, 0, 0))
    buf = pl.BlockSpec((1, CONV_W - 1, width), lambda b, l: (b, 0, 0))
    return pl.pallas_call(
        functools.partial(_gdn_kernel, c=c, n_chunks=lblk // c, n_heads=n_heads, lblk=lblk),
        grid=(batch, nl),
        in_specs=[rows(width), rows(LANES), rows(w), const((CONV_W, width)), const((1, LANES)), const((1, LANES)),
                  const((1, LANES)), state, buf],
        out_specs=[rows(w), state, buf],
        out_shape=[jax.ShapeDtypeStruct((t, w), F32), jax.ShapeDtypeStruct(s0.shape, F32),
                   jax.ShapeDtypeStruct(conv_buf.shape, F32)],
        scratch_shapes=[pltpu.VMEM((lblk + 8, width), F32), pltpu.VMEM((lblk, width), F32),
                        pltpu.VMEM((LANES, w), F32)],
        compiler_params=_cparams(("parallel", "arbitrary")),
        name="gated_deltanet",
    )(qkv, ab, g, conv_w, pad_lane(a_log), pad_lane(dt_bias), gn.reshape(1, LANES), s0, conv_buf)


PAGES_PER_STEP = 8


def _head_fold(pv, n_heads, nq, hd):
    return jnp.concatenate([pv[h * nq:(h + 1) * nq, h * hd:(h + 1) * hd] for h in range(n_heads)], axis=0)


def _row_ids(rows, nq):
    r = lax.broadcasted_iota(jnp.int32, (rows, 1), 0)
    return r // nq, r % nq


def _page_group(refs, first, count):
    return jnp.concatenate([refs[first + r][0, 0].reshape(-1, refs[first + r].shape[-1]) for r in range(count)], axis=1)


def _stick_paged_kernel(pt_ref, qbd_ref, kn_ref, vn_ref, *rest, pps, n_heads, nq, hd):
    k_refs, v_refs = rest[:pps], rest[pps:2 * pps]
    o_ref, c_ref, acc_ref = rest[2 * pps:]
    s = pl.program_id(1)
    rows = n_heads * nq
    page = k_refs[0].shape[-1]
    ppg = MOBA_BLOCK // page
    _, row_q = _row_ids(rows, nq)
    qbd = (qbd_ref[0] * hd ** -0.5).astype(BF16)

    def strict_later(n):
        return jnp.where(lax.broadcasted_iota(jnp.int32, (n, n), 0) > lax.broadcasted_iota(jnp.int32, (n, n), 1),
                         1.0, 0.0).astype(BF16)

    def weights(z, mask, carry, u):
        log_keep = -_softplus(z)
        log_beta = z + log_keep
        if mask is not None:
            log_keep = jnp.where(mask, log_keep, 0.0)
        hi = log_keep.astype(BF16)
        lo = (log_keep - hi.astype(F32)).astype(BF16)
        later = _dot(hi, u) + _dot(lo, u)
        w = jnp.exp(log_beta + later + carry)
        if mask is not None:
            w = jnp.where(mask, w, 0.0)
        return w.astype(BF16), carry + jnp.sum(log_keep, axis=-1, keepdims=True)

    @pl.when(s == 0)
    def _():
        n = kn_ref.shape[1]
        z = _dot_nt(qbd, kn_ref[0].astype(BF16))
        col = lax.broadcasted_iota(jnp.int32, (rows, n), 1)
        w, c = weights(z, col < row_q, jnp.zeros((rows, 1), F32), strict_later(n))
        c_ref[...] = c
        acc_ref[...] = _dot(w, vn_ref[0].astype(BF16))

    @pl.when(jnp.logical_and(s > 0, jnp.max(c_ref[...]) > EXP_UNDERFLOW))
    def _():
        u = strict_later(ppg * page)
        c = c_ref[...]
        acc = acc_ref[...]
        for g in reversed(range(pps // ppg)):
            z = _dot(qbd, _page_group(k_refs, g * ppg, ppg).astype(BF16))
            w, c = weights(z, None, c, u)
            acc = acc + _dot_nt(w, _page_group(v_refs, g * ppg, ppg).astype(BF16))
        c_ref[...] = c
        acc_ref[...] = acc

    @pl.when(s == pl.num_programs(1) - 1)
    def _():
        o_ref[0] = _head_fold(acc_ref[...], n_heads, nq, hd)


def _moba_paged_kernel(pt_ref, qbd_ref, kn_ref, vn_ref, *rest, pps, n_heads, nq, hd, past_len):
    k_refs, v_refs = rest[:pps], rest[pps:2 * pps]
    o_ref, mo_ref, lo_ref, acco_ref, m_ref, l_ref, acc_ref, gate_ref = rest[2 * pps:]
    s = pl.program_id(1)
    rows = n_heads * nq
    page = k_refs[0].shape[-1]
    ppb = MOBA_BLOCK // page
    row_h, row_q = _row_ids(rows, nq)
    slope = jnp.exp2(-8.0 * (row_h + 1).astype(F32) / n_heads)
    lane = lax.broadcasted_iota(jnp.int32, (rows, LANES), 1)
    q_f32 = qbd_ref[0]
    qbd = (q_f32 * hd ** -0.5).astype(BF16)

    @pl.when(s == 0)
    def _():
        n = kn_ref.shape[1]
        col = lax.broadcasted_iota(jnp.int32, (rows, n), 1)
        sc = _dot_nt(qbd, kn_ref[0].astype(BF16)) - slope * (row_q - col).astype(F32)
        sc = jnp.where(col <= row_q, sc, NEG)
        m = jnp.max(sc, axis=-1, keepdims=True)
        p = jnp.exp(sc - m)
        mo_ref[...] = m
        lo_ref[...] = jnp.sum(p, axis=-1, keepdims=True)
        acco_ref[...] = _head_fold(_dot(p.astype(BF16), vn_ref[0].astype(BF16)), n_heads, nq, hd)
        m_ref[...] = jnp.full(m_ref.shape, NEG, F32)
        l_ref[...] = jnp.zeros_like(l_ref)
        gate_ref[...] = jnp.zeros_like(gate_ref)

    @pl.when(s > 0)
    def _():
        key = lax.broadcasted_iota(jnp.int32, (rows, MOBA_BLOCK), 1)
        m_all, l_all, gate_all = m_ref[...], l_ref[...], gate_ref[...]
        for bi in range(pps // ppb):
            blk = (s - 1) * (pps // ppb) + bi
            kf = _page_group(k_refs, bi * ppb, ppb)
            k_mean = jnp.broadcast_to(jnp.sum(kf, axis=-1, keepdims=True) * (1.0 / MOBA_BLOCK), (kf.shape[0], LANES))
            gate = _dot_mid(q_f32, k_mean)
            dist = past_len + row_q - (blk * MOBA_BLOCK + key)
            sc = _dot(qbd, kf.astype(BF16)) - slope * dist.astype(F32)
            m_b = jnp.max(sc, axis=-1, keepdims=True)
            p = jnp.exp(sc - m_b)
            l_b = jnp.sum(p, axis=-1, keepdims=True)
            pv = _dot_nt(p.astype(BF16), _page_group(v_refs, bi * ppb, ppb).astype(BF16))
            acc_ref[blk] = _head_fold(pv, n_heads, nq, hd)
            m_all = jnp.where(lane == blk, m_b, m_all)
            l_all = jnp.where(lane == blk, l_b, l_all)
            gate_all = jnp.where(lane == blk, gate, gate_all)
        m_ref[...] = m_all
        l_ref[...] = l_all
        gate_ref[...] = gate_all

    @pl.when(s == pl.num_programs(1) - 1)
    def _():
        n_blocks = past_len // MOBA_BLOCK
        sel = _topk_lanes(jnp.where(lane < n_blocks, gate_ref[...], -jnp.inf), lane) > 0.0
        m_all, l_all = m_ref[...], l_ref[...]
        m_o = mo_ref[...]
        m_tot = jnp.maximum(m_o, jnp.max(jnp.where(sel, m_all, NEG), axis=-1, keepdims=True))
        wgt = jnp.where(sel, jnp.exp(m_all - m_tot), 0.0)
        a_o = jnp.exp(m_o - m_tot)
        l_tot = lo_ref[...] * a_o + jnp.sum(wgt * l_all, axis=-1, keepdims=True)
        acc = acco_ref[...] * a_o
        for b in range(n_blocks):
            acc = acc + wgt[:, b:b + 1] * acc_ref[b]
        o_ref[0] = acc / l_tot


def _paged_attention(kind, q, k_new, v_new, k_pool, v_pool, page_table, *, n_heads):
    batch, n_pages = page_table.shape
    page, hd = k_pool.shape[2], k_pool.shape[4]
    nq = q.shape[0] // batch
    rows = n_heads * nq
    pps = PAGES_PER_STEP
    past_len = n_pages * page
    assert n_pages % pps == 0 and past_len % MOBA_BLOCK == 0 and MOBA_BLOCK % page == 0 and pps % (MOBA_BLOCK // page) == 0
    assert nq <= page and past_len // MOBA_BLOCK <= LANES and (2 * LANES) % n_heads == 0
    n_groups = n_pages // pps
    q4 = q.reshape(batch, nq, n_heads, hd).transpose(0, 2, 1, 3)
    q_bd = (q4[:, :, :, None, :] * jnp.eye(n_heads, dtype=F32)[None, :, None, :, None]).reshape(batch, rows, n_heads * hd)
    pad_new = lambda t: jnp.pad(t.reshape(batch, nq, n_heads * hd), ((0, 0), (0, page - nq), (0, 0)))
    k_t, v_t = (jnp.transpose(t, (0, 1, 3, 4, 2)) for t in (k_pool, v_pool))

    if kind == "stick":
        group = lambda s: n_groups - jnp.maximum(s, 1)
        body = functools.partial(_stick_paged_kernel, pps=pps, n_heads=n_heads, nq=nq, hd=hd)
        scratch = [pltpu.VMEM((rows, 1), F32), pltpu.VMEM((rows, n_heads * hd), F32)]
    else:
        group = lambda s: jnp.maximum(s, 1) - 1
        body = functools.partial(_moba_paged_kernel, pps=pps, n_heads=n_heads, nq=nq, hd=hd, past_len=past_len)
        scratch = [pltpu.VMEM((rows, 1), F32), pltpu.VMEM((rows, 1), F32), pltpu.VMEM((rows, hd), F32),
                   pltpu.VMEM((rows, LANES), F32), pltpu.VMEM((rows, LANES), F32),
                   pltpu.VMEM((past_len // MOBA_BLOCK, rows, hd), F32), pltpu.VMEM((rows, LANES), F32)]

    def page_spec(r):
        return pl.BlockSpec((1, 1, n_heads, hd, page), lambda b, s, pt: (0, pt[b, group(s) * pps + r], 0, 0, 0))

    per_seq = lambda shape: pl.BlockSpec((1,) + shape, lambda b, s, pt: (b, 0, 0))
    out = pl.pallas_call(
        body,
        grid_spec=pltpu.PrefetchScalarGridSpec(
            num_scalar_prefetch=1,
            grid=(batch, n_groups + 1),
            in_specs=[per_seq((rows, n_heads * hd)), per_seq((page, n_heads * hd)),
                      per_seq((page, n_heads * hd))] + [page_spec(r) for r in range(pps)] * 2,
            out_specs=per_seq((rows, hd)),
            scratch_shapes=scratch),
        out_shape=jax.ShapeDtypeStruct((batch, rows, hd), F32),
        compiler_params=_cparams(("parallel", "arbitrary")),
        name=kind + "_paged",
    )(page_table, q_bd, pad_new(k_new), pad_new(v_new), *([k_t] * pps), *([v_t] * pps))
    return out.reshape(batch, n_heads, nq, hd).transpose(0, 2, 1, 3).reshape(batch * nq, n_heads * hd)


def _layer_stack(x, batch, past, states, w):
    s_hgrn, s_gdn, conv_buf = states

    def attend(kind, q, k, v, pools, n_heads):
        if past is None:
            return (_moba if kind == "moba" else _stick)(q, k, v, batch=batch, q_offset=0)
        return _paged_attention(kind, q, k, v, *pools, past[4], n_heads=n_heads)

    q_a, k_a, v_a, hg = _project(x, w["in_a"] + [w["in_hgrn"]])
    o_a = attend("moba", q_a, k_a, v_a, past and past[0:2], H_A)
    o_b, s_hgrn_new = _hgrn(hg, w["hgrn_lb"], w["hgrn_norm"], s_hgrn, batch=batch, layer=0)
    x = _matmul([o_a, o_b], w["out_even"], ln_args=(x, w["ln1_g"][0], w["ln1_b"][0]))
    x = _ffn(x, None, w["ffn_wg"], w["ffn_wu"], w["ffn_wd"], w["ln2_g"][0], w["ln2_b"][0])

    qkv_c, ab, g_c, q_d, k_d, v_d = _project(x, [w["in_qkv_c"], w["in_ab"], w["in_g_c"]] + w["in_d"])
    o_c, s_gdn_new, conv_new = _gdn(qkv_c, ab, g_c, w["conv_w"], w["a_log"], w["dt_bias"], w["gdn_norm"],
                                    s_gdn, conv_buf, batch=batch)
    o_d = attend("stick", q_d, k_d, v_d, past and past[2:4], H_D)
    x = _matmul([o_c, o_d], w["out_odd"], ln_args=(x, w["ln1_g"][1], w["ln1_b"][1]))
    comb = _router(x, w["router"])
    x = _moe(x, comb, w["moe_wg"], w["moe_wu"], w["moe_wd"], w["ln2_g"][1], w["ln2_b"][1])
    return x, k_a, v_a, s_hgrn_new, s_gdn_new, conv_new, k_d, v_d


def kernel(x_prompt, x_sample, cache_k_moba, cache_v_moba, state_hgrn, state_gdn, state_gdn_conv, cache_k_sb,
           cache_v_sb, page_table, w_in_even, w_out_even, hgrn_lb, hgrn_norm, w_in_odd, w_out_odd, gdn_conv_w,
           gdn_a_log, gdn_dt_bias, gdn_norm, ln1_g, ln1_b, ln2_g, ln2_b, ffn_wg, ffn_wu, ffn_wd, router, moe_wg,
           moe_wu, moe_wd):
    assert w_in_even.shape[0] == 1 and w_in_odd.shape[0] == 1
    bp, lp, d = x_prompt.shape
    bs, ls, _ = x_sample.shape
    mix = d // 2
    hd_a, hd_d = mix // H_A, mix // H_D
    conv_dim = gdn_conv_w.shape[-1]
    bf = lambda t: t.astype(BF16)
    cols = lambda wt, lo, n: bf(wt[:, lo:lo + n])

    wie, wio = w_in_even[0], w_in_odd[0]
    g_lo = conv_dim + 2 * H_C
    d_lo = g_lo + mix
    w = {
        "in_a": [cols(wie, i * mix, mix) for i in range(3)],
        "in_hgrn": cols(wie, 3 * mix, 4 * mix),
        "out_even": [bf(w_out_even[0][:mix]), bf(w_out_even[0][mix:])],
        "hgrn_lb": hgrn_lb, "hgrn_norm": hgrn_norm[0],
        "in_qkv_c": cols(wio, 0, conv_dim),
        "in_ab": bf(jnp.pad(wio[:, conv_dim:g_lo], ((0, 0), (0, LANES - 2 * H_C)))),
        "in_g_c": cols(wio, g_lo, mix),
        "in_d": [cols(wio, d_lo + i * mix, mix) for i in range(3)],
        "out_odd": [bf(w_out_odd[0][:mix]), bf(w_out_odd[0][mix:])],
        "conv_w": gdn_conv_w[0], "a_log": gdn_a_log[0], "dt_bias": gdn_dt_bias[0], "gdn_norm": gdn_norm[0],
        "ln1_g": ln1_g, "ln1_b": ln1_b, "ln2_g": ln2_g, "ln2_b": ln2_b,
        "ffn_wg": bf(ffn_wg), "ffn_wu": bf(ffn_wu), "ffn_wd": bf(ffn_wd),
        "router": router[0], "moe_wg": bf(moe_wg[0]), "moe_wu": bf(moe_wu[0]), "moe_wd": bf(moe_wd[0]),
    }

    def run(x, batch, past, states):
        seq = x.shape[1]
        y, k_a, v_a, s_h, s_g, cv, k_d, v_d = _layer_stack(x.reshape(batch * seq, d), batch, past, states, w)
        return (y.reshape(batch, seq, d), k_a.reshape(1, batch, seq, H_A, hd_a), v_a.reshape(1, batch, seq, H_A, hd_a),
                s_h[None], s_g[None], cv[None], k_d.reshape(1, batch, seq, H_D, hd_d),
                v_d.reshape(1, batch, seq, H_D, hd_d))

    zeros_p = (jnp.zeros((bp,) + state_hgrn.shape[2:], F32), jnp.zeros((bp,) + state_gdn.shape[2:], F32),
               jnp.zeros((bp,) + state_gdn_conv.shape[2:], F32))
    out_p = run(x_prompt, bp, None, zeros_p)
    past = (cache_k_moba, cache_v_moba, cache_k_sb, cache_v_sb, page_table)
    out_s = run(x_sample, bs, past, (state_hgrn[0], state_gdn[0], state_gdn_conv[0]))
    return (out_p[0], out_s[0]) + out_p[1:] + out_s[1:]
```

```python
import functools

import jax
import jax.numpy as jnp
from jax import lax
from jax.experimental import pallas as pl
from jax.experimental.pallas import tpu as pltpu

F32 = jnp.float32
BF16 = jnp.bfloat16

H_A, H_B, H_C, H_D = 8, 4, 4, 8
MOBA_BLOCK = 256
MOBA_TOPK = 3
CONV_W = 4
N_EXPERTS = 8
DEPTH = 2
DEEPNORM_ALPHA = (2 * DEPTH) ** 0.25
LN_EPS = 1e-5
RMS_EPS = 1e-6
NEG = -1e30
LANES = 128
VMEM_LIMIT = 56 * 1024 * 1024


def _cparams(sem):
    return pltpu.CompilerParams(dimension_semantics=sem, vmem_limit_bytes=VMEM_LIMIT)


def _layernorm(y, g, b):
    mu = jnp.mean(y, axis=-1, keepdims=True)
    yc = y - mu
    var = jnp.mean(yc * yc, axis=-1, keepdims=True)
    return yc * lax.rsqrt(var + LN_EPS) * g + b


def _split3(x):
    hi = x.astype(BF16)
    r = x - hi.astype(F32)
    mid = r.astype(BF16)
    lo = (r - mid.astype(F32)).astype(BF16)
    return hi, mid, lo


def _dot(a, b):
    return jnp.dot(a, b, preferred_element_type=F32)


def _dot_nt(a, b):
    return lax.dot_general(a, b, (((1,), (1,)), ((), ())), preferred_element_type=F32)


def _dot_tn(a, b):
    return lax.dot_general(a, b, (((0,), (0,)), ((), ())), preferred_element_type=F32)


def _dot_hi(a, b, dot=_dot):
    a0, a1, a2 = _split3(a)
    b0, b1, b2 = _split3(b)
    return (dot(a0, b0) + (dot(a0, b1) + dot(a1, b0))
            + (dot(a1, b1) + dot(a0, b2) + dot(a2, b0)))


def _dot_exact_rhs(a, b_bf16):
    a0, a1, a2 = _split3(a)
    return _dot(a0, b_bf16) + _dot(a1, b_bf16) + _dot(a2, b_bf16)


def _dot_exact_lhs(a_bf16, b):
    b0, b1, b2 = _split3(b)
    return _dot(a_bf16, b0) + _dot(a_bf16, b1) + _dot(a_bf16, b2)


def _mm_kernel(*refs, n_in, ln):
    a_refs, w_refs, rest = refs[:n_in], refs[n_in:2 * n_in], refs[2 * n_in:]
    acc = None
    for a, w in zip(a_refs, w_refs):
        d = _dot(a[...].astype(BF16), w[...])
        acc = d if acc is None else acc + d
    if ln:
        res_ref, g_ref, b_ref, o_ref = rest
        o_ref[...] = _layernorm(DEEPNORM_ALPHA * res_ref[...] + acc, g_ref[...], b_ref[...])
    else:
        (o_ref,) = rest
        o_ref[...] = acc


def _row_tile(m, largest=512):
    for t in (1024, 512, 256, 128, 64, 32, 16, 8):
        if t <= largest and m % t == 0:
            return t
    raise ValueError(m)


def _proj_kernel(x_ref, *refs):
    n = len(refs) // 2
    xb = x_ref[...].astype(BF16)
    for w_ref, o_ref in zip(refs[:n], refs[n:]):
        o_ref[...] = _dot(xb, w_ref[...])


def _project(x, w_list):
    m, d = x.shape
    tm = _row_tile(m)
    return pl.pallas_call(
        _proj_kernel,
        grid=(m // tm,),
        in_specs=[pl.BlockSpec((tm, d), lambda i: (i, 0))] + [pl.BlockSpec(w.shape, lambda i: (0, 0)) for w in w_list],
        out_specs=[pl.BlockSpec((tm, w.shape[1]), lambda i: (i, 0)) for w in w_list],
        out_shape=[jax.ShapeDtypeStruct((m, w.shape[1]), F32) for w in w_list],
        compiler_params=_cparams(("parallel",)),
        name="in_proj",
    )(x, *w_list)


def _matmul(a_list, w_list, *, tn=None, ln_args=None):
    m = a_list[0].shape[0]
    n = w_list[0].shape[1]
    tm = _row_tile(m, largest=1024)
    ln = ln_args is not None
    tn = n if (ln or tn is None) else tn
    assert n % tn == 0
    in_specs = [pl.BlockSpec((tm, a.shape[1]), lambda i, j: (i, 0)) for a in a_list]
    in_specs += [pl.BlockSpec((w.shape[0], tn), lambda i, j: (0, j)) for w in w_list]
    args = list(a_list) + list(w_list)
    if ln:
        res, g, b = ln_args
        in_specs += [pl.BlockSpec((tm, n), lambda i, j: (i, 0)),
                     pl.BlockSpec((1, n), lambda i, j: (0, 0)),
                     pl.BlockSpec((1, n), lambda i, j: (0, 0))]
        args += [res, g.reshape(1, n), b.reshape(1, n)]
    return pl.pallas_call(
        functools.partial(_mm_kernel, n_in=len(a_list), ln=ln),
        grid=(m // tm, n // tn),
        in_specs=in_specs,
        out_specs=pl.BlockSpec((tm, tn), lambda i, j: (i, j)),
        out_shape=jax.ShapeDtypeStruct((m, n), F32),
        compiler_params=_cparams(("parallel", "arbitrary")),
        name="matmul_ln" if ln else "matmul",
    )(*args)


def _ffn_kernel(*refs, use_comb):
    if use_comb:
        x_ref, comb_ref, wg_ref, wu_ref, wd_ref, g_ref, b_ref, o_ref, xb_ref, acc_ref = refs
    else:
        x_ref, wg_ref, wu_ref, wd_ref, g_ref, b_ref, o_ref, xb_ref, acc_ref = refs
    e, j = pl.program_id(1), pl.program_id(2)
    first = jnp.logical_and(e == 0, j == 0)
    last = jnp.logical_and(e == pl.num_programs(1) - 1, j == pl.num_programs(2) - 1)

    @pl.when(first)
    def _():
        xb_ref[...] = x_ref[...].astype(BF16)
        acc_ref[...] = jnp.zeros_like(acc_ref)

    xb = xb_ref[...]
    h = jax.nn.silu(_dot(xb, wg_ref[0])) * _dot(xb, wu_ref[0])
    if use_comb:
        lane = lax.broadcasted_iota(jnp.int32, comb_ref.shape, 1)
        h = h * jnp.sum(jnp.where(lane == e, comb_ref[...], 0.0), axis=-1, keepdims=True)
    acc_ref[...] += _dot(h.astype(BF16), wd_ref[0])

    @pl.when(last)
    def _():
        o_ref[...] = _layernorm(DEEPNORM_ALPHA * x_ref[...] + acc_ref[...], g_ref[...], b_ref[...])


def _ff_tile(ff, largest=896):
    for t in (896, 512, 256, LANES):
        if t <= largest and ff % t == 0:
            return t
    raise ValueError(ff)


def _ffn(x, comb, wg, wu, wd, g, b):
    m, d = x.shape
    n_e, _, ff = wg.shape
    tm = _row_tile(m)
    tf = _ff_tile(ff)
    use_comb = comb is not None
    in_specs = [pl.BlockSpec((tm, d), lambda i, e, j: (i, 0))]
    args = [x]
    if use_comb:
        in_specs.append(pl.BlockSpec((tm, comb.shape[1]), lambda i, e, j: (i, 0)))
        args.append(comb)
    in_specs += [pl.BlockSpec((1, d, tf), lambda i, e, j: (e, 0, j)),
                 pl.BlockSpec((1, d, tf), lambda i, e, j: (e, 0, j)),
                 pl.BlockSpec((1, tf, d), lambda i, e, j: (e, j, 0)),
                 pl.BlockSpec((1, d), lambda i, e, j: (0, 0)),
                 pl.BlockSpec((1, d), lambda i, e, j: (0, 0))]
    args += [wg, wu, wd, g.reshape(1, d), b.reshape(1, d)]
    return pl.pallas_call(
        functools.partial(_ffn_kernel, use_comb=use_comb),
        grid=(m // tm, n_e, ff // tf),
        in_specs=in_specs,
        out_specs=pl.BlockSpec((tm, d), lambda i, e, j: (i, 0)),
        out_shape=jax.ShapeDtypeStruct((m, d), F32),
        scratch_shapes=[pltpu.VMEM((tm, d), BF16), pltpu.VMEM((tm, d), F32)],
        compiler_params=_cparams(("parallel", "arbitrary", "arbitrary")),
        name="moe_ffn" if use_comb else "ffn",
    )(*args)


MOE_TOKENS = 1024
MOE_SUBTILES = 2
MOE_ROWS = 128


def _moe_kernel(x_ref, comb_ref, wg_ref, wu_ref, wd_ref, g_ref, b_ref, o_ref,
                xb_ref, xc_ref, yc_ref, rank_ref, rank_t_ref, comb_t_ref):
    e, j = pl.program_id(1), pl.program_id(2)
    tm, d = x_ref.shape
    r, ts = MOE_ROWS, MOE_TOKENS
    subs = [slice(s * ts, (s + 1) * ts) for s in range(tm // ts)]
    lane = lax.broadcasted_iota(jnp.int32, (ts, LANES), 1)

    @pl.when(jnp.logical_and(e == 0, j == 0))
    def _():
        xb_ref[...] = x_ref[...].astype(BF16)
        o_ref[...] = jnp.zeros_like(o_ref)
        earlier = jnp.where(lax.broadcasted_iota(jnp.int32, (ts, ts), 1) < lax.broadcasted_iota(jnp.int32, (ts, ts), 0),
                            1.0, 0.0).astype(BF16)
        for sub in subs:
            routed = jnp.where(comb_ref[sub, :] > 0.0, 1.0, 0.0).astype(BF16)
            rank = _dot(earlier, routed)
            rank_ref[sub, :] = rank
            rank_t_ref[:, sub] = rank.T
            comb_t_ref[:, sub] = comb_ref[sub, :].T

    gate_rows = [comb_t_ref[pl.ds(e, 1), sub] for sub in subs]
    n_pass = [(jnp.sum(jnp.where(g_row > 0.0, 1, 0)) + (r - 1)) // r for g_row in gate_rows]
    base = [sum(n_pass[:s], jnp.int32(0)) * r for s in range(len(subs))]
    total_pass = sum(n_pass, jnp.int32(0))

    def passes(count, fn):
        def pair(k, carry):
            fn(k * (2 * r), 2 * r)
            return carry

        lax.fori_loop(0, count // 2, pair, 0)

        @pl.when(count % 2 == 1)
        def _():
            fn((count - 1) * r, r)

    @pl.when(j == 0)
    def _():
        for s, sub in enumerate(subs):
            rank_row = rank_t_ref[pl.ds(e, 1), sub]

            def compact(start, n_rows, s=s, sub=sub, rank_row=rank_row):
                slot = (start + lax.broadcasted_iota(jnp.int32, (n_rows, ts), 0)).astype(F32)
                pick = jnp.where(jnp.logical_and(rank_row == slot, gate_rows[s] > 0.0), 1.0, 0.0).astype(BF16)
                rows = pl.ds(pl.multiple_of(base[s] + start, r), n_rows)
                xc_ref[rows, :] = _dot(pick, xb_ref[sub, :]).astype(BF16)
                yc_ref[rows, :] = jnp.zeros((n_rows, d), F32)

            passes(n_pass[s], compact)

    def expert(start, n_rows):
        rows = pl.ds(pl.multiple_of(start, r), n_rows)
        xk = xc_ref[rows, :]
        h = jax.nn.silu(_dot(xk, wg_ref[0])) * _dot(xk, wu_ref[0])
        yc_ref[rows, :] += _dot(h.astype(BF16), wd_ref[0])

    passes(total_pass, expert)

    @pl.when(j == pl.num_programs(2) - 1)
    def _():
        for s, sub in enumerate(subs):
            rank_col = jnp.sum(jnp.where(lane == e, rank_ref[sub, :], 0.0), axis=-1, keepdims=True)
            gate_col = jnp.sum(jnp.where(lane == e, comb_ref[sub, :], 0.0), axis=-1, keepdims=True)

            def place(start, n_rows, s=s, sub=sub, rank_col=rank_col, gate_col=gate_col):
                slot = (start + lax.broadcasted_iota(jnp.int32, (ts, n_rows), 1)).astype(F32)
                put = jnp.where(jnp.logical_and(rank_col == slot, gate_col > 0.0), 1.0, 0.0).astype(BF16)
                rows = pl.ds(pl.multiple_of(base[s] + start, r), n_rows)
                o_ref[sub, :] += gate_col * _dot(put, yc_ref[rows, :].astype(BF16))

            passes(n_pass[s], place)

    @pl.when(jnp.logical_and(e == pl.num_programs(1) - 1, j == pl.num_programs(2) - 1))
    def _():
        o_ref[...] = _layernorm(DEEPNORM_ALPHA * x_ref[...] + o_ref[...], g_ref[...], b_ref[...])


def _moe(x, comb, wg, wu, wd, g, b):
    m, d = x.shape
    n_e, _, ff = wg.shape
    tm = MOE_TOKENS * MOE_SUBTILES
    if m % tm != 0:
        return _ffn(x, comb, wg, wu, wd, g, b)
    tf = _ff_tile(ff, largest=512)
    assert MOE_TOKENS % (2 * MOE_ROWS) == 0 and comb.shape[1] == LANES and n_e <= LANES
    cap = tm + MOE_SUBTILES * MOE_ROWS
    once = pl.Buffered(1)
    return pl.pallas_call(
        _moe_kernel,
        grid=(m // tm, n_e, ff // tf),
        in_specs=[pl.BlockSpec((tm, d), lambda i, e, j: (i, 0), pipeline_mode=once),
                  pl.BlockSpec((tm, LANES), lambda i, e, j: (i, 0), pipeline_mode=once),
                  pl.BlockSpec((1, d, tf), lambda i, e, j: (e, 0, j)),
                  pl.BlockSpec((1, d, tf), lambda i, e, j: (e, 0, j)),
                  pl.BlockSpec((1, tf, d), lambda i, e, j: (e, j, 0)),
                  pl.BlockSpec((1, d), lambda i, e, j: (0, 0)),
                  pl.BlockSpec((1, d), lambda i, e, j: (0, 0))],
        out_specs=pl.BlockSpec((tm, d), lambda i, e, j: (i, 0), pipeline_mode=once),
        out_shape=jax.ShapeDtypeStruct((m, d), F32),
        scratch_shapes=[pltpu.VMEM((tm, d), BF16), pltpu.VMEM((cap, d), BF16), pltpu.VMEM((cap, d), F32),
                        pltpu.VMEM((tm, LANES), F32), pltpu.VMEM((LANES, tm), F32), pltpu.VMEM((LANES, tm), F32)],
        compiler_params=_cparams(("parallel", "arbitrary", "arbitrary")),
        name="moe_sparse",
    )(x, comb, wg, wu, wd, g.reshape(1, d), b.reshape(1, d))


def _router_kernel(x_ref, r_ref, o_ref):
    logits = _dot_hi(x_ref[...], r_ref[...])
    lane = lax.broadcasted_iota(jnp.int32, logits.shape, 1)
    logits = jnp.where(lane < N_EXPERTS, logits, -jnp.inf)
    m1 = jnp.max(logits, axis=-1, keepdims=True)
    i1 = jnp.min(jnp.where(logits == m1, lane, LANES), axis=-1, keepdims=True)
    rest = jnp.where(lane == i1, -jnp.inf, logits)
    m2 = jnp.max(rest, axis=-1, keepdims=True)
    i2 = jnp.min(jnp.where(rest == m2, lane, LANES), axis=-1, keepdims=True)
    e2 = jnp.exp(m2 - m1)
    den = 1.0 + e2
    o_ref[...] = jnp.where(lane == i1, 1.0 / den, 0.0) + jnp.where(lane == i2, e2 / den, 0.0)


def _router(x, router):
    m, d = x.shape
    tm = _row_tile(m)
    r = jnp.pad(router, ((0, 0), (0, LANES - router.shape[1])))
    return pl.pallas_call(
        _router_kernel,
        grid=(m // tm,),
        in_specs=[pl.BlockSpec((tm, d), lambda i: (i, 0)), pl.BlockSpec((d, LANES), lambda i: (0, 0))],
        out_specs=pl.BlockSpec((tm, LANES), lambda i: (i, 0)),
        out_shape=jax.ShapeDtypeStruct((m, LANES), F32),
        compiler_params=_cparams(("parallel",)),
        name="router",
    )(x, r)


def _topk_lanes(g, idx, axis=-1):
    sel = jnp.zeros(g.shape, F32)
    for _ in range(MOBA_TOPK):
        m = jnp.max(g, axis=axis, keepdims=True)
        first = jnp.min(jnp.where(g == m, idx, LANES), axis=axis, keepdims=True)
        hit = jnp.logical_and(idx == first, m > -jnp.inf)
        sel = jnp.where(hit, 1.0, sel)
        g = jnp.where(hit, -jnp.inf, g)
    return sel


def _moba_kernel(slope_ref, q_ref, k_ref, v_ref, o_ref, kb_ref, vt_ref, km_ref, m_ref, l_ref, acc_ref, sel_ref,
                 *, tq, nb, q_offset, hd):
    i = pl.program_id(2)
    blk = MOBA_BLOCK

    @pl.when(i == 0)
    def _():
        kb_ref[...] = k_ref[...].astype(BF16)
        km_ref[...] = jnp.zeros_like(km_ref)
        for j in range(nb):
            rows = slice(j * blk, (j + 1) * blk)
            vt_ref[j] = v_ref[rows, :].T.astype(BF16)
            km_ref[j:j + 1, :] = jnp.sum(k_ref[rows, :], axis=0, keepdims=True) * (1.0 / blk)

    q0 = q_offset + i * tq
    own = q0 // blk
    q = q_ref[...]
    lane = lax.broadcasted_iota(jnp.int32, (tq, LANES), 1)
    q2 = jnp.concatenate([jnp.where(lane < hd, q, 0.0), jnp.where(lane >= hd, q, 0.0)], axis=0)
    gate_t = _dot_mid(km_ref[...], q2, dot=_dot_nt)
    blk_id = lax.broadcasted_iota(jnp.int32, gate_t.shape, 0)
    sel_ref[...] = _topk_lanes(jnp.where(blk_id < own, gate_t, -jnp.inf), blk_id, axis=0)
    qb = (q2 * hd ** -0.5).astype(BF16)
    c2 = lax.broadcasted_iota(jnp.int32, (blk, 2 * tq), 1)
    d0 = (jnp.where(c2 >= tq, c2 - tq, c2) - lax.broadcasted_iota(jnp.int32, (blk, 2 * tq), 0)).astype(F32)
    slope_lane = slope_ref[...]
    c1 = lax.broadcasted_iota(jnp.int32, (1, 2 * tq), 1)
    slope = jnp.where(c1 < tq, slope_lane[:, 0:1], slope_lane[:, hd:hd + 1])
    slope_d0 = slope * d0
    first = lax.broadcasted_iota(jnp.int32, (2 * hd, tq), 0) < hd
    m_ref[...] = jnp.full(m_ref.shape, NEG, F32)
    l_ref[...] = jnp.zeros_like(l_ref)
    acc_ref[...] = jnp.zeros_like(acc_ref)

    def tiles(js, diagonal):
        scores = []
        for j in js:
            off = (q0 - j * blk).astype(F32)
            s = _dot_nt(kb_ref[pl.ds(pl.multiple_of(j * blk, blk), blk), :], qb) - (slope_d0 + slope * off)
            if diagonal:
                s = jnp.where(d0 + off >= 0.0, s, NEG)
            else:
                s = jnp.where(sel_ref[pl.ds(j, 1), :] > 0.0, s, NEG)
            scores.append(s)
        m_old = m_ref[...]
        m_new = functools.reduce(jnp.maximum, [m_old] + [jnp.max(s, axis=0, keepdims=True) for s in scores])
        alpha = jnp.exp(m_old - m_new)
        l_new = alpha * l_ref[...]
        pv = None
        for j, s in zip(js, scores):
            p = jnp.exp(s - m_new)
            l_new = l_new + jnp.sum(p, axis=0, keepdims=True)
            d = _dot(vt_ref[j], p.astype(BF16))
            pv = d if pv is None else pv + d
        m_ref[...] = m_new
        l_ref[...] = l_new
        acc_ref[...] = (jnp.where(first, alpha[:, :tq], alpha[:, tq:]) * acc_ref[...]
                        + jnp.where(first, pv[:, :tq], pv[:, tq:]))

    tiles([own], True)

    def body(t, carry):
        tiles([own - 1 - 2 * t, own - 2 - 2 * t], False)
        return carry

    lax.fori_loop(0, own // 2, body, 0)

    @pl.when(own % 2 == 1)
    def _():
        tiles([own - own], False)
    l = l_ref[...]
    o_ref[...] = (acc_ref[...] / jnp.where(first, l[:, :tq], l[:, tq:])).T


def _alibi_slopes_lanes(n_heads, hd):
    slopes = jnp.exp2(-8.0 * jnp.arange(1, n_heads + 1, dtype=F32) / n_heads)
    return jnp.repeat(slopes, hd).reshape(1, n_heads * hd)


def _attn_call(kernel, q, k, v, *, batch, q_offset, n_heads, extra_in=(), extra_specs=(), extra_scratch=(), name):
    lq, lk = q.shape[0] // batch, k.shape[0] // batch
    hd = q.shape[1] // n_heads
    assert 2 * hd == LANES and lk % MOBA_BLOCK == 0
    tq = min(MOBA_BLOCK, lq)
    assert lq % tq == 0 and MOBA_BLOCK % tq == 0 and q_offset % tq == 0
    nq = lq // tq
    return pl.pallas_call(
        functools.partial(kernel, tq=tq, q_offset=q_offset, hd=hd),
        grid=(batch, n_heads // 2, nq),
        in_specs=list(extra_specs) + [
            pl.BlockSpec((tq, LANES), lambda b, h, i: (b * nq + i, h)),
            pl.BlockSpec((lk, LANES), lambda b, h, i: (b, h)),
            pl.BlockSpec((lk, LANES), lambda b, h, i: (b, h))],
        out_specs=pl.BlockSpec((tq, LANES), lambda b, h, i: (b * nq + i, h)),
        out_shape=jax.ShapeDtypeStruct(q.shape, F32),
        scratch_shapes=list(extra_scratch),
        compiler_params=_cparams(("parallel", "parallel", "arbitrary")),
        name=name,
    )(*extra_in, q, k, v)


def _moba(q, k, v, *, batch, q_offset):
    nb = k.shape[0] // batch // MOBA_BLOCK
    assert nb <= LANES
    slopes = _alibi_slopes_lanes(H_A, q.shape[1] // H_A)
    tq = min(MOBA_BLOCK, q.shape[0] // batch)
    nb_rows = -(-nb // 8) * 8
    return _attn_call(
        functools.partial(_moba_kernel, nb=nb), q, k, v, batch=batch, q_offset=q_offset, n_heads=H_A,
        extra_in=(slopes,), extra_specs=(pl.BlockSpec((1, LANES), lambda b, h, i: (0, h)),),
        extra_scratch=(pltpu.VMEM((nb * MOBA_BLOCK, LANES), BF16), pltpu.VMEM((nb, LANES, MOBA_BLOCK), BF16),
                       pltpu.VMEM((nb_rows, LANES), F32), pltpu.VMEM((1, 2 * tq), F32), pltpu.VMEM((1, 2 * tq), F32),
                       pltpu.VMEM((LANES, tq), F32), pltpu.VMEM((nb_rows, 2 * tq), F32)), name="moba")


EXP_UNDERFLOW = -110.0


def _softplus(z):
    return jnp.maximum(z, 0.0) + jnp.log(1.0 + jnp.exp(-jnp.abs(z)))


def _stick_kernel(q_ref, k_ref, v_ref, o_ref, kb_ref, vb_ref, c_ref, acc_ref, *, tq, q_offset, hd):
    i = pl.program_id(2)
    blk = MOBA_BLOCK

    @pl.when(i == 0)
    def _():
        kb_ref[...] = k_ref[...].astype(BF16)
        vb_ref[...] = v_ref[...].astype(BF16)

    q0 = q_offset + i * tq
    own = q0 // blk
    q = q_ref[...] * hd ** -0.5
    lane = lax.broadcasted_iota(jnp.int32, (tq, LANES), 1)
    qb = jnp.concatenate([jnp.where(lane < hd, q, 0.0), jnp.where(lane >= hd, q, 0.0)], axis=0).astype(BF16)
    r2 = lax.broadcasted_iota(jnp.int32, (2 * tq, blk), 0)
    d0 = jnp.where(r2 >= tq, r2 - tq, r2) - lax.broadcasted_iota(jnp.int32, (2 * tq, blk), 1)
    u = jnp.where(lax.broadcasted_iota(jnp.int32, (blk, blk), 0) > lax.broadcasted_iota(jnp.int32, (blk, blk), 1),
                  1.0, 0.0).astype(BF16)
    c_ref[...] = jnp.zeros_like(c_ref)
    acc_ref[...] = jnp.zeros_like(acc_ref)

    def tile(j, diagonal):
        start = pl.multiple_of(j * blk, blk)
        z = _dot_nt(qb, kb_ref[pl.ds(start, blk), :])
        log_keep = -_softplus(z)
        log_beta = z + log_keep
        if diagonal:
            before = d0 + (q0 - j * blk) > 0
            log_keep = jnp.where(before, log_keep, 0.0)
        hi = log_keep.astype(BF16)
        lo = (log_keep - hi.astype(F32)).astype(BF16)
        later = _dot(jnp.concatenate([hi, lo], axis=0), u)
        later = later[:2 * tq] + later[2 * tq:]
        c = c_ref[...]
        w = jnp.exp(log_beta + later + c)
        if diagonal:
            w = jnp.where(before, w, 0.0)
        c_ref[...] = c + later[:, 0:1] + log_keep[:, 0:1]
        pv = _dot(w.astype(BF16), vb_ref[pl.ds(start, blk), :])
        acc_ref[...] += jnp.where(lane < hd, pv[:tq], pv[tq:])

    tile(own, True)

    def more(carry):
        t, c_max = carry
        return jnp.logical_and(t <= own, c_max > EXP_UNDERFLOW)

    def body(carry):
        t, _ = carry
        tile(own - t, False)
        return t + 1, jnp.max(c_ref[...])

    lax.while_loop(more, body, (jnp.int32(1), jnp.max(c_ref[...])))
    o_ref[...] = acc_ref[...]


def _stick(q, k, v, *, batch, q_offset):
    tq = min(MOBA_BLOCK, q.shape[0] // batch)
    lk = k.shape[0] // batch
    return _attn_call(_stick_kernel, q, k, v, batch=batch, q_offset=q_offset, n_heads=H_D,
                      extra_scratch=(pltpu.VMEM((lk, LANES), BF16), pltpu.VMEM((lk, LANES), BF16),
                                     pltpu.VMEM((2 * tq, 1), F32), pltpu.VMEM((tq, LANES), F32)), name="stick")


CHUNK = 64
SUB = 16
SEQ_BLOCK = 512
EXP_CLAMP = 80.0


def _dot_mid(a, b, dot=_dot):
    a0 = a.astype(BF16)
    a1 = (a - a0.astype(F32)).astype(BF16)
    b0 = b.astype(BF16)
    b1 = (b - b0.astype(F32)).astype(BF16)
    return dot(a0, b0) + (dot(a0, b1) + dot(a1, b0))


def _incl_lower(c):
    r = lax.broadcasted_iota(jnp.int32, (c, c), 0)
    s = lax.broadcasted_iota(jnp.int32, (c, c), 1)
    return r, s


def _hgrn_kernel(q_ref, f_ref, i_ref, g_ref, lb_ref, gn_ref, s0_ref, o_ref, s_ref, st_ref, *, c, n_chunks, n_heads,
                 layer):
    l = pl.program_id(1)
    dk = LANES

    w = n_heads * dk

    @pl.when(l == 0)
    def _():
        for h in range(n_heads):
            st_ref[h * dk:(h + 1) * dk, :] = s0_ref[0, h]

    r, s = _incl_lower(c)
    tri = jnp.where(s <= r, 1.0, 0.0).astype(BF16)
    sub = min(SUB, c)
    n = n_heads * c
    shift_c, shift_s = c.bit_length() - 1, sub.bit_length() - 1
    ri = lax.broadcasted_iota(jnp.int32, (n_heads * sub, n), 0)
    ci = lax.broadcasted_iota(jnp.int32, (n_heads * sub, n), 1)
    pair = jnp.right_shift(ri, shift_s) == jnp.right_shift(ci, shift_c)
    t_loc, s_loc = jnp.bitwise_and(ri, sub - 1), jnp.bitwise_and(ci, c - 1)
    own_cols = (jnp.right_shift(lax.broadcasted_iota(jnp.int32, (n, w), 0), shift_c)
                == lax.broadcasted_iota(jnp.int32, (n, w), 1) // dk)
    gn = gn_ref[...]
    e = jnp.exp(lb_ref[...] - jnp.max(lb_ref[...], axis=0, keepdims=True))
    lb = jnp.sum(e[:layer + 1], axis=0, keepdims=True) / jnp.sum(e, axis=0, keepdims=True)

    def stack(t):
        return jnp.concatenate([t[:, h * dk:(h + 1) * dk] for h in range(n_heads)], axis=0)

    def chunk(i, carry):
        rows = pl.ds(pl.multiple_of(i * c, c), c)
        z = f_ref[rows, :]
        f = lb + (1.0 - lb) * jax.nn.sigmoid(z)
        k = (1.0 - lb) * jax.nn.sigmoid(-z)
        q = jax.nn.silu(q_ref[rows, :])
        cg = _dot_exact_lhs(tri, jnp.log(f))
        v4 = stack(i_ref[rows, :]).astype(BF16)
        st = st_ref[...]
        intra = []
        for b in range(c // sub):
            lo, hi = b * sub, (b + 1) * sub
            ref_pt = cg[lo - 1:lo] if b else jnp.zeros((1, w), F32)
            qs = stack(q[lo:hi] * jnp.exp(cg[lo:hi] - ref_pt))
            ks = stack(k * jnp.exp(jnp.minimum(ref_pt - cg, EXP_CLAMP)))
            a = _dot_nt(qs.astype(BF16), ks.astype(BF16))
            a = jnp.where(jnp.logical_and(pair, s_loc <= t_loc + lo), a, 0.0)
            intra.append(_dot(a.astype(BF16), v4))
        from_state = _dot_nt(stack(q * jnp.exp(cg)).astype(BF16), st.astype(BF16))
        last = cg[c - 1:c]
        kt = stack(k * jnp.exp(last - cg))
        v_wide = jnp.where(own_cols, jnp.concatenate([v4] * n_heads, axis=1), 0.0)
        e_last = jnp.concatenate([jnp.broadcast_to(jnp.exp(last[:, h * dk:(h + 1) * dk]), (dk, dk))
                                  for h in range(n_heads)], axis=0)
        st_ref[...] = st * e_last + _dot_tn(v_wide, kt.astype(BF16))
        for h in range(n_heads):
            o = from_state[h * c:(h + 1) * c, h * dk:(h + 1) * dk] + jnp.concatenate(
                [part[h * sub:(h + 1) * sub] for part in intra], axis=0)
            o = o * lax.rsqrt(jnp.mean(o * o, axis=-1, keepdims=True) + RMS_EPS) * gn
            o_ref[rows, h * dk:(h + 1) * dk] = o * jax.nn.sigmoid(g_ref[rows, h * dk:(h + 1) * dk])
        return carry

    lax.fori_loop(0, n_chunks, chunk, 0)

    @pl.when(l == pl.num_programs(1) - 1)
    def _():
        for h in range(n_heads):
            s_ref[0, h] = st_ref[h * dk:(h + 1) * dk, :]


def _hgrn(hg, lb, gn, s0, *, batch, layer=0):
    t, width = hg.shape
    seq = t // batch
    w = width // 4
    n_heads = w // LANES
    c = min(CHUNK, seq)
    lblk = min(SEQ_BLOCK, seq)
    assert seq % lblk == 0 and lblk % c == 0 and c % min(SUB, c) == 0
    nl = seq // lblk
    part = lambda p: pl.BlockSpec((lblk, w), lambda b, l: (b * nl + l, p))
    state = pl.BlockSpec((1, n_heads, LANES, LANES), lambda b, l: (b, 0, 0, 0))
    o, st = pl.pallas_call(
        functools.partial(_hgrn_kernel, c=c, n_chunks=lblk // c, n_heads=n_heads, layer=layer),
        grid=(batch, nl),
        in_specs=[part(0), part(1), part(2), part(3),
                  pl.BlockSpec(lb.shape, lambda b, l: (0, 0)), pl.BlockSpec((1, LANES), lambda b, l: (0, 0)), state],
        out_specs=[pl.BlockSpec((lblk, w), lambda b, l: (b * nl + l, 0)), state],
        out_shape=[jax.ShapeDtypeStruct((t, w), F32), jax.ShapeDtypeStruct(s0.shape, F32)],
        scratch_shapes=[pltpu.VMEM((n_heads * LANES, LANES), F32)],
        compiler_params=_cparams(("parallel", "arbitrary")),
        name="hgrn2",
    )(hg, hg, hg, hg, lb, gn.reshape(1, LANES), jnp.swapaxes(s0, -1, -2))
    return o, jnp.swapaxes(st, -1, -2)


def _unit_lower_inverse(lm, n, period):
    bs = min(SUB, period)
    r, s = _incl_lower(n)
    eye = jnp.where(r == s, 1.0, 0.0)
    shift = bs.bit_length() - 1
    same = jnp.right_shift(r, shift) == jnp.right_shift(s, shift)

    def mm(a, b):
        return _dot(a.astype(BF16), b.astype(BF16))

    def neumann(a, order):
        inv, pw, k = eye + a, a, 2
        while k < order:
            pw = mm(pw, pw)
            inv = inv + mm(inv, pw)
            k *= 2
        return inv

    inv_d = neumann(-jnp.where(same, lm, 0.0), bs)
    if period == bs:
        return inv_d
    m = mm(inv_d, jnp.where(same, 0.0, lm))
    return mm(neumann(-m, period // bs), inv_d)


def _gdn_kernel(x_ref, ab_ref, g_ref, cw_ref, al_ref, dtb_ref, gn_ref, s0_ref, cb_ref, o_ref, s_ref, nb_ref,
                xx_ref, y_ref, st_ref, *, c, n_chunks, n_heads, lblk):
    l = pl.program_id(1)
    dk = LANES
    halo = 8
    w = n_heads * dk

    @pl.when(l == 0)
    def _():
        for h in range(n_heads):
            st_ref[:, h * dk:(h + 1) * dk] = s0_ref[0, h]
        xx_ref[halo - (CONV_W - 1):halo, :] = cb_ref[0]

    xx_ref[halo:halo + lblk, :] = x_ref[...]
    y = xx_ref[halo:halo + lblk, :] * cw_ref[CONV_W - 1:CONV_W, :]
    for i in range(CONV_W - 1):
        off = halo - (CONV_W - 1) + i
        y = y + xx_ref[off:off + lblk, :] * cw_ref[i:i + 1, :]
    y_ref[...] = jax.nn.silu(y)
    tail = xx_ref[halo + lblk - (CONV_W - 1):halo + lblk, :]
    xx_ref[halo - (CONV_W - 1):halo, :] = tail

    r, s = _incl_lower(c)
    tri = jnp.where(s <= r, 1.0, 0.0).astype(BF16)
    n = n_heads * c
    rr, ss = _incl_lower(n)
    shift = c.bit_length() - 1
    same_head = jnp.right_shift(rr, shift) == jnp.right_shift(ss, shift)
    incl = jnp.logical_and(same_head, ss <= rr)
    strict = jnp.logical_and(same_head, ss < rr)
    own_cols = (jnp.right_shift(lax.broadcasted_iota(jnp.int32, (n, w), 0), shift)
                == lax.broadcasted_iota(jnp.int32, (n, w), 1) // dk)
    gn = gn_ref[...]

    def l2n(t):
        return t * lax.rsqrt(jnp.sum(t * t, axis=-1, keepdims=True) + RMS_EPS)

    def stack(f):
        return jnp.concatenate([f(h) for h in range(n_heads)], axis=0)

    def own_block(t):
        return stack(lambda h: t[h * c:(h + 1) * c, h * dk:(h + 1) * dk])

    def prepare(i):
        rows = pl.ds(i * c, c) if isinstance(i, int) else pl.ds(pl.multiple_of(i * c, c), c)
        ab = ab_ref[rows, :]
        pre = ab + dtb_ref[...]
        log_a = -jnp.exp(al_ref[...]) * (jnp.maximum(pre, 0.0) + jnp.log(1.0 + jnp.exp(-jnp.abs(pre))))
        beta_all = jax.nn.sigmoid(ab)
        cg_all = _dot_exact_lhs(tri, log_a)
        cg_t = cg_all.T
        q = stack(lambda h: l2n(y_ref[rows, h * dk:(h + 1) * dk])) * dk ** -0.5
        k = stack(lambda h: l2n(y_ref[rows, w + h * dk:w + (h + 1) * dk]))
        v = stack(lambda h: y_ref[rows, 2 * w + h * dk:2 * w + (h + 1) * dk])
        cg = stack(lambda h: cg_all[:, h:h + 1])
        beta = stack(lambda h: beta_all[:, n_heads + h:n_heads + h + 1])
        cg_row = jnp.concatenate([cg_t[h:h + 1, :c] for h in range(n_heads)], axis=1)
        last = stack(lambda h: jnp.broadcast_to(cg_all[c - 1:c, h:h + 1], (c, 1)))
        decay = jnp.exp(jnp.where(incl, cg - cg_row, NEG))
        kk = _dot_mid(k, k, dot=_dot_nt) * decay * beta
        t_inv = _unit_lower_inverse(jnp.where(strict, kk, 0.0), n, c)
        e_cg = jnp.exp(cg)
        sol = _dot_mid(t_inv, jnp.concatenate([v * beta, k * (beta * e_cg)], axis=1))
        qk = (_dot_nt(q.astype(BF16), k.astype(BF16)) * decay).astype(BF16)
        kt = (k * jnp.exp(last - cg)).astype(BF16)
        e_last = jnp.concatenate([jnp.broadcast_to(jnp.exp(cg_all[c - 1:c, h:h + 1]), (1, dk)) for h in range(n_heads)],
                                 axis=1)
        return rows, sol[:, :dk], sol[:, dk:].astype(BF16), (q * e_cg).astype(BF16), qk, kt, e_last

    def advance(prepared):
        rows, sol_v, sol_k, q_decayed, qk, kt, e_last = prepared
        st = st_ref[...]
        stb = st.astype(BF16)
        u = sol_v - own_block(_dot(sol_k, stb))
        o = own_block(_dot(q_decayed, stb)) + _dot(qk, u.astype(BF16))
        u_wide = jnp.where(own_cols, jnp.concatenate([u] * n_heads, axis=1), 0.0)
        st_ref[...] = st * e_last + _dot_tn(kt, u_wide.astype(BF16))
        o = o * lax.rsqrt(jnp.mean(o * o, axis=-1, keepdims=True) + RMS_EPS) * gn
        for h in range(n_heads):
            o_ref[rows, h * dk:(h + 1) * dk] = o[h * c:(h + 1) * c] * jax.nn.silu(g_ref[rows, h * dk:(h + 1) * dk])

    def chunk_pair(i, carry):
        first, second = prepare(2 * i), prepare(2 * i + 1)
        advance(first)
        advance(second)
        return carry

    lax.fori_loop(0, n_chunks // 2, chunk_pair, 0)
    if n_chunks % 2:
        advance(prepare(n_chunks - 1))

    @pl.when(l == pl.num_programs(1) - 1)
    def _():
        for h in range(n_heads):
            s_ref[0, h] = st_ref[:, h * dk:(h + 1) * dk]
        nb_ref[0] = tail


def _gdn(qkv, ab, g, conv_w, a_log, dt_bias, gn, s0, conv_buf, *, batch):
    t, width = qkv.shape
    seq = t // batch
    w = width // 3
    n_heads = w // LANES
    c = min(CHUNK, seq)
    lblk = min(SEQ_BLOCK, seq)
    assert seq % lblk == 0 and lblk % c == 0 and lblk >= CONV_W - 1
    nl = seq // lblk
    pad_lane = lambda vec: jnp.pad(vec, (0, LANES - vec.shape[0])).reshape(1, LANES)
    rows = lambda width_: pl.BlockSpec((lblk, width_), lambda b, l: (b * nl + l, 0))
    const = lambda shape: pl.BlockSpec(shape, lambda b, l: (0,) * len(shape))
    state = pl.BlockSpec((1, n_heads, LANES, LANES), lambda b, l: (b, 0, 0, 0))
    buf = pl.BlockSpec((1, CONV_W - 1, width), lambda b, l: (b, 0, 0))
    return pl.pallas_call(
        functools.partial(_gdn_kernel, c=c, n_chunks=lblk // c, n_heads=n_heads, lblk=lblk),
        grid=(batch, nl),
        in_specs=[rows(width), rows(LANES), rows(w), const((CONV_W, width)), const((1, LANES)), const((1, LANES)),
                  const((1, LANES)), state, buf],
        out_specs=[rows(w), state, buf],
        out_shape=[jax.ShapeDtypeStruct((t, w), F32), jax.ShapeDtypeStruct(s0.shape, F32),
                   jax.ShapeDtypeStruct(conv_buf.shape, F32)],
        scratch_shapes=[pltpu.VMEM((lblk + 8, width), F32), pltpu.VMEM((lblk, width), F32),
                        pltpu.VMEM((LANES, w), F32)],
        compiler_params=_cparams(("parallel", "arbitrary")),
        name="gated_deltanet",
    )(qkv, ab, g, conv_w, pad_lane(a_log), pad_lane(dt_bias), gn.reshape(1, LANES), s0, conv_buf)


PAGES_PER_STEP = 8


def _head_fold(pv, n_heads, nq, hd):
    return jnp.concatenate([pv[h * nq:(h + 1) * nq, h * hd:(h + 1) * hd] for h in range(n_heads)], axis=0)


def _row_ids(rows, nq):
    r = lax.broadcasted_iota(jnp.int32, (rows, 1), 0)
    return r // nq, r % nq


def _page_group(refs, first, count):
    return jnp.concatenate([refs[first + r][0, 0].reshape(-1, refs[first + r].shape[-1]) for r in range(count)], axis=1)


def _stick_paged_kernel(pt_ref, qbd_ref, kn_ref, vn_ref, k_hbm, v_hbm, o_ref, kbuf, vbuf, sem, *, n_heads, nq, hd,
                        n_pages):
    b = pl.program_id(0)
    rows = n_heads * nq
    page = kbuf.shape[-1]
    ppg = kbuf.shape[1]
    n_groups = n_pages // ppg
    _, row_q = _row_ids(rows, nq)
    qbd = (qbd_ref[0] * hd ** -0.5).astype(BF16)

    def copies(g, slot):
        out = []
        for r in range(ppg):
            pg = pt_ref[b, g * ppg + r]
            out.append(pltpu.make_async_copy(k_hbm.at[0, pg], kbuf.at[slot, r], sem.at[0, slot, r]))
            out.append(pltpu.make_async_copy(v_hbm.at[0, pg], vbuf.at[slot, r], sem.at[1, slot, r]))
        return out

    def group(buf, slot):
        return jnp.concatenate([buf[slot, r].reshape(n_heads * hd, page) for r in range(ppg)], axis=1).astype(BF16)

    def strict_later(n):
        return jnp.where(lax.broadcasted_iota(jnp.int32, (n, n), 0) > lax.broadcasted_iota(jnp.int32, (n, n), 1),
                         1.0, 0.0).astype(BF16)

    def weights(z, mask, carry, u):
        log_keep = -_softplus(z)
        log_beta = z + log_keep
        if mask is not None:
            log_keep = jnp.where(mask, log_keep, 0.0)
        hi = log_keep.astype(BF16)
        lo = (log_keep - hi.astype(F32)).astype(BF16)
        later = _dot(hi, u) + _dot(lo, u)
        w = jnp.exp(log_beta + later + carry)
        if mask is not None:
            w = jnp.where(mask, w, 0.0)
        return w.astype(BF16), carry + jnp.sum(log_keep, axis=-1, keepdims=True)

    for cp in copies(n_groups - 1, 0):
        cp.start()

    n = kn_ref.shape[1]
    z = _dot_nt(qbd, kn_ref[0].astype(BF16))
    col = lax.broadcasted_iota(jnp.int32, (rows, n), 1)
    w, c = weights(z, col < row_q, jnp.zeros((rows, 1), F32), strict_later(n))
    acc = _dot(w, vn_ref[0].astype(BF16))
    u = strict_later(ppg * page)

    def more(carry):
        g, c, _ = carry
        return jnp.logical_and(g >= 0, jnp.max(c) > EXP_UNDERFLOW)

    def body(carry):
        g, c, acc = carry
        slot = (n_groups - 1 - g) % 2

        @pl.when(g > 0)
        def _():
            for cp in copies(g - 1, 1 - slot):
                cp.start()

        for cp in copies(g, slot):
            cp.wait()
        w, c = weights(_dot(qbd, group(kbuf, slot)), None, c, u)
        return g - 1, c, acc + _dot_nt(w, group(vbuf, slot))

    g, _, acc = lax.while_loop(more, body, (jnp.int32(n_groups - 1), c, acc))

    @pl.when(g >= 0)
    def _():
        for cp in copies(g, (n_groups - 1 - g) % 2):
            cp.wait()

    o_ref[0] = _head_fold(acc, n_heads, nq, hd)


def _moba_paged_kernel(pt_ref, qbd_ref, kn_ref, vn_ref, *rest, pps, n_heads, nq, hd, past_len):
    k_refs, v_refs = rest[:pps], rest[pps:2 * pps]
    o_ref, mo_ref, lo_ref, acco_ref, m_ref, l_ref, acc_ref, gate_ref = rest[2 * pps:]
    s = pl.program_id(1)
    rows = n_heads * nq
    page = k_refs[0].shape[-1]
    ppb = MOBA_BLOCK // page
    row_h, row_q = _row_ids(rows, nq)
    slope = jnp.exp2(-8.0 * (row_h + 1).astype(F32) / n_heads)
    lane = lax.broadcasted_iota(jnp.int32, (rows, LANES), 1)
    q_f32 = qbd_ref[0]
    qbd = (q_f32 * hd ** -0.5).astype(BF16)

    @pl.when(s == 0)
    def _():
        n = kn_ref.shape[1]
        col = lax.broadcasted_iota(jnp.int32, (rows, n), 1)
        sc = _dot_nt(qbd, kn_ref[0].astype(BF16)) - slope * (row_q - col).astype(F32)
        sc = jnp.where(col <= row_q, sc, NEG)
        m = jnp.max(sc, axis=-1, keepdims=True)
        p = jnp.exp(sc - m)
        mo_ref[...] = m
        lo_ref[...] = jnp.sum(p, axis=-1, keepdims=True)
        acco_ref[...] = _head_fold(_dot(p.astype(BF16), vn_ref[0].astype(BF16)), n_heads, nq, hd)
        m_ref[...] = jnp.full(m_ref.shape, NEG, F32)
        l_ref[...] = jnp.zeros_like(l_ref)
        gate_ref[...] = jnp.zeros_like(gate_ref)

    @pl.when(s > 0)
    def _():
        key = lax.broadcasted_iota(jnp.int32, (rows, MOBA_BLOCK), 1)
        m_all, l_all, gate_all = m_ref[...], l_ref[...], gate_ref[...]
        for bi in range(pps // ppb):
            blk = (s - 1) * (pps // ppb) + bi
            kf = _page_group(k_refs, bi * ppb, ppb)
            k_mean = jnp.broadcast_to(jnp.sum(kf, axis=-1, keepdims=True) * (1.0 / MOBA_BLOCK), (kf.shape[0], LANES))
            gate = _dot_mid(q_f32, k_mean)
            dist = past_len + row_q - (blk * MOBA_BLOCK + key)
            sc = _dot(qbd, kf.astype(BF16)) - slope * dist.astype(F32)
            m_b = jnp.max(sc, axis=-1, keepdims=True)
            p = jnp.exp(sc - m_b)
            l_b = jnp.sum(p, axis=-1, keepdims=True)
            pv = _dot_nt(p.astype(BF16), _page_group(v_refs, bi * ppb, ppb).astype(BF16))
            acc_ref[blk] = _head_fold(pv, n_heads, nq, hd)
            m_all = jnp.where(lane == blk, m_b, m_all)
            l_all = jnp.where(lane == blk, l_b, l_all)
            gate_all = jnp.where(lane == blk, gate, gate_all)
        m_ref[...] = m_all
        l_ref[...] = l_all
        gate_ref[...] = gate_all

    @pl.when(s == pl.num_programs(1) - 1)
    def _():
        n_blocks = past_len // MOBA_BLOCK
        sel = _topk_lanes(jnp.where(lane < n_blocks, gate_ref[...], -jnp.inf), lane) > 0.0
        m_all, l_all = m_ref[...], l_ref[...]
        m_o = mo_ref[...]
        m_tot = jnp.maximum(m_o, jnp.max(jnp.where(sel, m_all, NEG), axis=-1, keepdims=True))
        wgt = jnp.where(sel, jnp.exp(m_all - m_tot), 0.0)
        a_o = jnp.exp(m_o - m_tot)
        l_tot = lo_ref[...] * a_o + jnp.sum(wgt * l_all, axis=-1, keepdims=True)
        acc = acco_ref[...] * a_o
        for b in range(n_blocks):
            acc = acc + wgt[:, b:b + 1] * acc_ref[b]
        o_ref[0] = acc / l_tot


def _paged_attention(kind, q, k_new, v_new, k_pool, v_pool, page_table, *, n_heads):
    batch, n_pages = page_table.shape
    page, hd = k_pool.shape[2], k_pool.shape[4]
    nq = q.shape[0] // batch
    rows = n_heads * nq
    pps = PAGES_PER_STEP
    past_len = n_pages * page
    assert n_pages % pps == 0 and past_len % MOBA_BLOCK == 0 and MOBA_BLOCK % page == 0 and pps % (MOBA_BLOCK // page) == 0
    assert nq <= page and past_len // MOBA_BLOCK <= LANES and (2 * LANES) % n_heads == 0
    n_groups = n_pages // pps
    q4 = q.reshape(batch, nq, n_heads, hd).transpose(0, 2, 1, 3)
    q_bd = (q4[:, :, :, None, :] * jnp.eye(n_heads, dtype=F32)[None, :, None, :, None]).reshape(batch, rows, n_heads * hd)
    pad_new = lambda t: jnp.pad(t.reshape(batch, nq, n_heads * hd), ((0, 0), (0, page - nq), (0, 0)))
    k_t, v_t = (jnp.transpose(t, (0, 1, 3, 4, 2)) for t in (k_pool, v_pool))

    if kind == "stick":
        ppg = MOBA_BLOCK // page
        seq = lambda shape: pl.BlockSpec((1,) + shape, lambda b, pt: (b, 0, 0))
        buf = pltpu.VMEM((2, ppg, n_heads, hd, page), F32)
        out = pl.pallas_call(
            functools.partial(_stick_paged_kernel, n_heads=n_heads, nq=nq, hd=hd, n_pages=n_pages),
            grid_spec=pltpu.PrefetchScalarGridSpec(
                num_scalar_prefetch=1,
                grid=(batch,),
                in_specs=[seq((rows, n_heads * hd)), seq((page, n_heads * hd)), seq((page, n_heads * hd)),
                          pl.BlockSpec(memory_space=pl.ANY), pl.BlockSpec(memory_space=pl.ANY)],
                out_specs=seq((rows, hd)),
                scratch_shapes=[buf, buf, pltpu.SemaphoreType.DMA((2, 2, ppg))]),
            out_shape=jax.ShapeDtypeStruct((batch, rows, hd), F32),
            compiler_params=_cparams(("arbitrary",)),
            name="stick_paged",
        )(page_table, q_bd, pad_new(k_new), pad_new(v_new), k_t, v_t)
    else:
        group = lambda s: jnp.maximum(s, 1) - 1
        scratch = [pltpu.VMEM((rows, 1), F32), pltpu.VMEM((rows, 1), F32), pltpu.VMEM((rows, hd), F32),
                   pltpu.VMEM((rows, LANES), F32), pltpu.VMEM((rows, LANES), F32),
                   pltpu.VMEM((past_len // MOBA_BLOCK, rows, hd), F32), pltpu.VMEM((rows, LANES), F32)]

        def page_spec(r):
            return pl.BlockSpec((1, 1, n_heads, hd, page), lambda b, s, pt: (0, pt[b, group(s) * pps + r], 0, 0, 0))

        per_seq = lambda shape: pl.BlockSpec((1,) + shape, lambda b, s, pt: (b, 0, 0))
        out = pl.pallas_call(
            functools.partial(_moba_paged_kernel, pps=pps, n_heads=n_heads, nq=nq, hd=hd, past_len=past_len),
            grid_spec=pltpu.PrefetchScalarGridSpec(
                num_scalar_prefetch=1,
                grid=(batch, n_groups + 1),
                in_specs=[per_seq((rows, n_heads * hd)), per_seq((page, n_heads * hd)),
                          per_seq((page, n_heads * hd))] + [page_spec(r) for r in range(pps)] * 2,
                out_specs=per_seq((rows, hd)),
                scratch_shapes=scratch),
            out_shape=jax.ShapeDtypeStruct((batch, rows, hd), F32),
            compiler_params=_cparams(("parallel", "arbitrary")),
            name="moba_paged",
        )(page_table, q_bd, pad_new(k_new), pad_new(v_new), *([k_t] * pps), *([v_t] * pps))
    return out.reshape(batch, n_heads, nq, hd).transpose(0, 2, 1, 3).reshape(batch * nq, n_heads * hd)


def _layer_stack(x, batch, past, states, w):
    s_hgrn, s_gdn, conv_buf = states

    def attend(kind, q, k, v, pools, n_heads):
        if past is None:
            return (_moba if kind == "moba" else _stick)(q, k, v, batch=batch, q_offset=0)
        return _paged_attention(kind, q, k, v, *pools, past[4], n_heads=n_heads)

    q_a, k_a, v_a, hg = _project(x, w["in_a"] + [w["in_hgrn"]])
    o_a = attend("moba", q_a, k_a, v_a, past and past[0:2], H_A)
    o_b, s_hgrn_new = _hgrn(hg, w["hgrn_lb"], w["hgrn_norm"], s_hgrn, batch=batch, layer=0)
    x = _matmul([o_a, o_b], w["out_even"], ln_args=(x, w["ln1_g"][0], w["ln1_b"][0]))
    x = _ffn(x, None, w["ffn_wg"], w["ffn_wu"], w["ffn_wd"], w["ln2_g"][0], w["ln2_b"][0])

    qkv_c, ab, g_c, q_d, k_d, v_d = _project(x, [w["in_qkv_c"], w["in_ab"], w["in_g_c"]] + w["in_d"])
    o_c, s_gdn_new, conv_new = _gdn(qkv_c, ab, g_c, w["conv_w"], w["a_log"], w["dt_bias"], w["gdn_norm"],
                                    s_gdn, conv_buf, batch=batch)
    o_d = attend("stick", q_d, k_d, v_d, past and past[2:4], H_D)
    x = _matmul([o_c, o_d], w["out_odd"], ln_args=(x, w["ln1_g"][1], w["ln1_b"][1]))
    comb = _router(x, w["router"])
    x = _moe(x, comb, w["moe_wg"], w["moe_wu"], w["moe_wd"], w["ln2_g"][1], w["ln2_b"][1])
    return x, k_a, v_a, s_hgrn_new, s_gdn_new, conv_new, k_d, v_d


def kernel(x_prompt, x_sample, cache_k_moba, cache_v_moba, state_hgrn, state_gdn, state_gdn_conv, cache_k_sb,
           cache_v_sb, page_table, w_in_even, w_out_even, hgrn_lb, hgrn_norm, w_in_odd, w_out_odd, gdn_conv_w,
           gdn_a_log, gdn_dt_bias, gdn_norm, ln1_g, ln1_b, ln2_g, ln2_b, ffn_wg, ffn_wu, ffn_wd, router, moe_wg,
           moe_wu, moe_wd):
    assert w_in_even.shape[0] == 1 and w_in_odd.shape[0] == 1
    bp, lp, d = x_prompt.shape
    bs, ls, _ = x_sample.shape
    mix = d // 2
    hd_a, hd_d = mix // H_A, mix // H_D
    conv_dim = gdn_conv_w.shape[-1]
    bf = lambda t: t.astype(BF16)
    cols = lambda wt, lo, n: bf(wt[:, lo:lo + n])

    wie, wio = w_in_even[0], w_in_odd[0]
    g_lo = conv_dim + 2 * H_C
    d_lo = g_lo + mix
    w = {
        "in_a": [cols(wie, i * mix, mix) for i in range(3)],
        "in_hgrn": cols(wie, 3 * mix, 4 * mix),
        "out_even": [bf(w_out_even[0][:mix]), bf(w_out_even[0][mix:])],
        "hgrn_lb": hgrn_lb, "hgrn_norm": hgrn_norm[0],
        "in_qkv_c": cols(wio, 0, conv_dim),
        "in_ab": bf(jnp.pad(wio[:, conv_dim:g_lo], ((0, 0), (0, LANES - 2 * H_C)))),
        "in_g_c": cols(wio, g_lo, mix),
        "in_d": [cols(wio, d_lo + i * mix, mix) for i in range(3)],
        "out_odd": [bf(w_out_odd[0][:mix]), bf(w_out_odd[0][mix:])],
        "conv_w": gdn_conv_w[0], "a_log": gdn_a_log[0], "dt_bias": gdn_dt_bias[0], "gdn_norm": gdn_norm[0],
        "ln1_g": ln1_g, "ln1_b": ln1_b, "ln2_g": ln2_g, "ln2_b": ln2_b,
        "ffn_wg": bf(ffn_wg), "ffn_wu": bf(ffn_wu), "ffn_wd": bf(ffn_wd),
        "router": router[0], "moe_wg": bf(moe_wg[0]), "moe_wu": bf(moe_wu[0]), "moe_wd": bf(moe_wd[0]),
    }

    def run(x, batch, past, states):
        seq = x.shape[1]
        y, k_a, v_a, s_h, s_g, cv, k_d, v_d = _layer_stack(x.reshape(batch * seq, d), batch, past, states, w)
        return (y.reshape(batch, seq, d), k_a.reshape(1, batch, seq, H_A, hd_a), v_a.reshape(1, batch, seq, H_A, hd_a),
                s_h[None], s_g[None], cv[None], k_d.reshape(1, batch, seq, H_D, hd_d),
                v_d.reshape(1, batch, seq, H_D, hd_d))

    past = (cache_k_moba, cache_v_moba, cache_k_sb, cache_v_sb, page_table)
    out_s = run(x_sample, bs, past, (state_hgrn[0], state_gdn[0], state_gdn_conv[0]))
    zeros_p = (jnp.zeros((bp,) + state_hgrn.shape[2:], F32), jnp.zeros((bp,) + state_gdn.shape[2:], F32),
               jnp.zeros((bp,) + state_gdn_conv.shape[2:], F32))
    out_p = run(x_prompt, bp, None, zeros_p)
    return (out_p[0], out_s[0]) + out_p[1:] + out_s[1:]
```

```python
import functools

import jax
import jax.numpy as jnp
from jax import lax
from jax.experimental import pallas as pl
from jax.experimental.pallas import tpu as pltpu

F32 = jnp.float32
BF16 = jnp.bfloat16

H_A, H_B, H_C, H_D = 8, 4, 4, 8
MOBA_BLOCK = 256
MOBA_TOPK = 3
CONV_W = 4
N_EXPERTS = 8
DEPTH = 2
DEEPNORM_ALPHA = (2 * DEPTH) ** 0.25
LN_EPS = 1e-5
RMS_EPS = 1e-6
NEG = -1e30
LANES = 128
VMEM_LIMIT = 56 * 1024 * 1024


def _cparams(sem):
    return pltpu.CompilerParams(dimension_semantics=sem, vmem_limit_bytes=VMEM_LIMIT)


def _layernorm(y, g, b):
    mu = jnp.mean(y, axis=-1, keepdims=True)
    yc = y - mu
    var = jnp.mean(yc * yc, axis=-1, keepdims=True)
    return yc * lax.rsqrt(var + LN_EPS) * g + b


def _split3(x):
    hi = x.astype(BF16)
    r = x - hi.astype(F32)
    mid = r.astype(BF16)
    lo = (r - mid.astype(F32)).astype(BF16)
    return hi, mid, lo


def _dot(a, b):
    return jnp.dot(a, b, preferred_element_type=F32)


def _dot_nt(a, b):
    return lax.dot_general(a, b, (((1,), (1,)), ((), ())), preferred_element_type=F32)


def _dot_tn(a, b):
    return lax.dot_general(a, b, (((0,), (0,)), ((), ())), preferred_element_type=F32)


def _dot_hi(a, b, dot=_dot):
    a0, a1, a2 = _split3(a)
    b0, b1, b2 = _split3(b)
    return (dot(a0, b0) + (dot(a0, b1) + dot(a1, b0))
            + (dot(a1, b1) + dot(a0, b2) + dot(a2, b0)))


def _dot_exact_rhs(a, b_bf16):
    a0, a1, a2 = _split3(a)
    return _dot(a0, b_bf16) + _dot(a1, b_bf16) + _dot(a2, b_bf16)


def _dot_exact_lhs(a_bf16, b):
    b0, b1, b2 = _split3(b)
    return _dot(a_bf16, b0) + _dot(a_bf16, b1) + _dot(a_bf16, b2)


def _mm_kernel(*refs, n_in, ln):
    a_refs, w_refs, rest = refs[:n_in], refs[n_in:2 * n_in], refs[2 * n_in:]
    acc = None
    for a, w in zip(a_refs, w_refs):
        d = _dot(a[...].astype(BF16), w[...])
        acc = d if acc is None else acc + d
    if ln:
        res_ref, g_ref, b_ref, o_ref = rest
        o_ref[...] = _layernorm(DEEPNORM_ALPHA * res_ref[...] + acc, g_ref[...], b_ref[...])
    else:
        (o_ref,) = rest
        o_ref[...] = acc


def _row_tile(m, largest=512):
    for t in (1024, 512, 256, 128, 64, 32, 16, 8):
        if t <= largest and m % t == 0:
            return t
    raise ValueError(m)


def _proj_kernel(x_ref, *refs, n_w, n_t):
    w_refs, wt_refs = refs[:n_w], refs[n_w:n_w + n_t]
    o_refs, ot_refs = refs[n_w + n_t:2 * n_w + n_t], refs[2 * n_w + n_t:]
    xb = x_ref[...].astype(BF16)
    for w_ref, o_ref in zip(w_refs, o_refs):
        o_ref[...] = _dot(xb, w_ref[...])
    for wt_ref, ot_ref in zip(wt_refs, ot_refs):
        ot_ref[0] = _dot_nt(wt_ref[...], xb)


def _project(x, w_list, batch, transposed=()):
    m, d = x.shape
    tm = _row_tile(m)
    seq = m // batch
    wt_list = [w_list[t].T for t in transposed] if seq % tm == 0 else []
    nl = max(seq // tm, 1)
    const = lambda w: pl.BlockSpec(w.shape, lambda i: (0, 0))
    outs = pl.pallas_call(
        functools.partial(_proj_kernel, n_w=len(w_list), n_t=len(wt_list)),
        grid=(m // tm,),
        in_specs=[pl.BlockSpec((tm, d), lambda i: (i, 0))] + [const(w) for w in w_list] + [const(w) for w in wt_list],
        out_specs=[pl.BlockSpec((tm, w.shape[1]), lambda i: (i, 0)) for w in w_list]
        + [pl.BlockSpec((1, w.shape[0], tm), lambda i: (i // nl, 0, i % nl)) for w in wt_list],
        out_shape=[jax.ShapeDtypeStruct((m, w.shape[1]), F32) for w in w_list]
        + [jax.ShapeDtypeStruct((batch, w.shape[0], seq), F32) for w in wt_list],
        compiler_params=_cparams(("parallel",)),
        name="in_proj",
    )(x, *w_list, *wt_list)
    return outs[:len(w_list)], outs[len(w_list):]


def _matmul(a_list, w_list, *, tn=None, ln_args=None):
    m = a_list[0].shape[0]
    n = w_list[0].shape[1]
    tm = _row_tile(m, largest=1024)
    ln = ln_args is not None
    tn = n if (ln or tn is None) else tn
    assert n % tn == 0
    in_specs = [pl.BlockSpec((tm, a.shape[1]), lambda i, j: (i, 0)) for a in a_list]
    in_specs += [pl.BlockSpec((w.shape[0], tn), lambda i, j: (0, j)) for w in w_list]
    args = list(a_list) + list(w_list)
    if ln:
        res, g, b = ln_args
        in_specs += [pl.BlockSpec((tm, n), lambda i, j: (i, 0)),
                     pl.BlockSpec((1, n), lambda i, j: (0, 0)),
                     pl.BlockSpec((1, n), lambda i, j: (0, 0))]
        args += [res, g.reshape(1, n), b.reshape(1, n)]
    return pl.pallas_call(
        functools.partial(_mm_kernel, n_in=len(a_list), ln=ln),
        grid=(m // tm, n // tn),
        in_specs=in_specs,
        out_specs=pl.BlockSpec((tm, tn), lambda i, j: (i, j)),
        out_shape=jax.ShapeDtypeStruct((m, n), F32),
        compiler_params=_cparams(("parallel", "arbitrary")),
        name="matmul_ln" if ln else "matmul",
    )(*args)


def _ffn_kernel(*refs, use_comb):
    if use_comb:
        x_ref, comb_ref, wg_ref, wu_ref, wd_ref, g_ref, b_ref, o_ref, xb_ref, acc_ref = refs
    else:
        x_ref, wg_ref, wu_ref, wd_ref, g_ref, b_ref, o_ref, xb_ref, acc_ref = refs
    e, j = pl.program_id(1), pl.program_id(2)
    first = jnp.logical_and(e == 0, j == 0)
    last = jnp.logical_and(e == pl.num_programs(1) - 1, j == pl.num_programs(2) - 1)

    @pl.when(first)
    def _():
        xb_ref[...] = x_ref[...].astype(BF16)
        acc_ref[...] = jnp.zeros_like(acc_ref)

    xb = xb_ref[...]
    h = jax.nn.silu(_dot(xb, wg_ref[0])) * _dot(xb, wu_ref[0])
    if use_comb:
        lane = lax.broadcasted_iota(jnp.int32, comb_ref.shape, 1)
        h = h * jnp.sum(jnp.where(lane == e, comb_ref[...], 0.0), axis=-1, keepdims=True)
    acc_ref[...] += _dot(h.astype(BF16), wd_ref[0])

    @pl.when(last)
    def _():
        o_ref[...] = _layernorm(DEEPNORM_ALPHA * x_ref[...] + acc_ref[...], g_ref[...], b_ref[...])


def _ff_tile(ff, largest=896):
    for t in (896, 512, 256, LANES):
        if t <= largest and ff % t == 0:
            return t
    raise ValueError(ff)


def _ffn(x, comb, wg, wu, wd, g, b):
    m, d = x.shape
    n_e, _, ff = wg.shape
    tm = _row_tile(m, largest=1024)
    tf = _ff_tile(ff, largest=512)
    use_comb = comb is not None
    in_specs = [pl.BlockSpec((tm, d), lambda i, e, j: (i, 0))]
    args = [x]
    if use_comb:
        in_specs.append(pl.BlockSpec((tm, comb.shape[1]), lambda i, e, j: (i, 0)))
        args.append(comb)
    in_specs += [pl.BlockSpec((1, d, tf), lambda i, e, j: (e, 0, j)),
                 pl.BlockSpec((1, d, tf), lambda i, e, j: (e, 0, j)),
                 pl.BlockSpec((1, tf, d), lambda i, e, j: (e, j, 0)),
                 pl.BlockSpec((1, d), lambda i, e, j: (0, 0)),
                 pl.BlockSpec((1, d), lambda i, e, j: (0, 0))]
    args += [wg, wu, wd, g.reshape(1, d), b.reshape(1, d)]
    return pl.pallas_call(
        functools.partial(_ffn_kernel, use_comb=use_comb),
        grid=(m // tm, n_e, ff // tf),
        in_specs=in_specs,
        out_specs=pl.BlockSpec((tm, d), lambda i, e, j: (i, 0)),
        out_shape=jax.ShapeDtypeStruct((m, d), F32),
        scratch_shapes=[pltpu.VMEM((tm, d), BF16), pltpu.VMEM((tm, d), F32)],
        compiler_params=_cparams(("parallel", "arbitrary", "arbitrary")),
        name="moe_ffn" if use_comb else "ffn",
    )(*args)


MOE_TOKENS = 1024
MOE_SUBTILES = 1
MOE_ROWS = 128


def _moe_kernel(x_ref, comb_ref, wg_ref, wu_ref, wd_ref, g_ref, b_ref, o_ref,
                xb_ref, xc_ref, yc_ref, rank_ref, rank_t_ref, comb_t_ref):
    e, j = pl.program_id(1), pl.program_id(2)
    tm, d = x_ref.shape
    r, ts = MOE_ROWS, MOE_TOKENS
    subs = [slice(s * ts, (s + 1) * ts) for s in range(tm // ts)]
    lane = lax.broadcasted_iota(jnp.int32, (ts, LANES), 1)

    @pl.when(jnp.logical_and(e == 0, j == 0))
    def _():
        xb_ref[...] = x_ref[...].astype(BF16)
        o_ref[...] = jnp.zeros_like(o_ref)
        earlier = jnp.where(lax.broadcasted_iota(jnp.int32, (ts, ts), 1) < lax.broadcasted_iota(jnp.int32, (ts, ts), 0),
                            1.0, 0.0).astype(BF16)
        for sub in subs:
            routed = jnp.where(comb_ref[sub, :] > 0.0, 1.0, 0.0).astype(BF16)
            rank = _dot(earlier, routed)
            rank_ref[sub, :] = rank
            rank_t_ref[:, sub] = rank.T
            comb_t_ref[:, sub] = comb_ref[sub, :].T

    gate_rows = [comb_t_ref[pl.ds(e, 1), sub] for sub in subs]
    n_pass = [(jnp.sum(jnp.where(g_row > 0.0, 1, 0)) + (r - 1)) // r for g_row in gate_rows]
    base = [sum(n_pass[:s], jnp.int32(0)) * r for s in range(len(subs))]
    total_pass = sum(n_pass, jnp.int32(0))

    def passes(count, fn):
        def pair(k, carry):
            fn(k * (2 * r), 2 * r)
            return carry

        lax.fori_loop(0, count // 2, pair, 0)

        @pl.when(count % 2 == 1)
        def _():
            fn((count - 1) * r, r)

    @pl.when(j == 0)
    def _():
        for s, sub in enumerate(subs):
            rank_row = rank_t_ref[pl.ds(e, 1), sub]

            def compact(start, n_rows, s=s, sub=sub, rank_row=rank_row):
                slot = (start + lax.broadcasted_iota(jnp.int32, (n_rows, ts), 0)).astype(F32)
                pick = jnp.where(jnp.logical_and(rank_row == slot, gate_rows[s] > 0.0), 1.0, 0.0).astype(BF16)
                rows = pl.ds(pl.multiple_of(base[s] + start, r), n_rows)
                xc_ref[rows, :] = _dot(pick, xb_ref[sub, :]).astype(BF16)
                yc_ref[rows, :] = jnp.zeros((n_rows, d), F32)

            passes(n_pass[s], compact)

    def expert(start, n_rows):
        rows = pl.ds(pl.multiple_of(start, r), n_rows)
        xk = xc_ref[rows, :]
        h = jax.nn.silu(_dot(xk, wg_ref[0])) * _dot(xk, wu_ref[0])
        yc_ref[rows, :] += _dot(h.astype(BF16), wd_ref[0])

    passes(total_pass, expert)

    @pl.when(j == pl.num_programs(2) - 1)
    def _():
        for s, sub in enumerate(subs):
            rank_col = jnp.sum(jnp.where(lane == e, rank_ref[sub, :], 0.0), axis=-1, keepdims=True)
            gate_col = jnp.sum(jnp.where(lane == e, comb_ref[sub, :], 0.0), axis=-1, keepdims=True)

            def place(start, n_rows, s=s, sub=sub, rank_col=rank_col, gate_col=gate_col):
                slot = (start + lax.broadcasted_iota(jnp.int32, (ts, n_rows), 1)).astype(F32)
                put = jnp.where(jnp.logical_and(rank_col == slot, gate_col > 0.0), 1.0, 0.0).astype(BF16)
                rows = pl.ds(pl.multiple_of(base[s] + start, r), n_rows)
                o_ref[sub, :] += gate_col * _dot(put, yc_ref[rows, :].astype(BF16))

            passes(n_pass[s], place)

    @pl.when(jnp.logical_and(e == pl.num_programs(1) - 1, j == pl.num_programs(2) - 1))
    def _():
        o_ref[...] = _layernorm(DEEPNORM_ALPHA * x_ref[...] + o_ref[...], g_ref[...], b_ref[...])


def _moe(x, comb, wg, wu, wd, g, b):
    m, d = x.shape
    n_e, _, ff = wg.shape
    tm = MOE_TOKENS * MOE_SUBTILES
    if m % tm != 0:
        return _ffn(x, comb, wg, wu, wd, g, b)
    tf = _ff_tile(ff)
    assert MOE_TOKENS % (2 * MOE_ROWS) == 0 and comb.shape[1] == LANES and n_e <= LANES
    cap = tm + MOE_SUBTILES * MOE_ROWS
    once = pl.Buffered(1)
    return pl.pallas_call(
        _moe_kernel,
        grid=(m // tm, n_e, ff // tf),
        in_specs=[pl.BlockSpec((tm, d), lambda i, e, j: (i, 0), pipeline_mode=once),
                  pl.BlockSpec((tm, LANES), lambda i, e, j: (i, 0), pipeline_mode=once),
                  pl.BlockSpec((1, d, tf), lambda i, e, j: (e, 0, j)),
                  pl.BlockSpec((1, d, tf), lambda i, e, j: (e, 0, j)),
                  pl.BlockSpec((1, tf, d), lambda i, e, j: (e, j, 0)),
                  pl.BlockSpec((1, d), lambda i, e, j: (0, 0)),
                  pl.BlockSpec((1, d), lambda i, e, j: (0, 0))],
        out_specs=pl.BlockSpec((tm, d), lambda i, e, j: (i, 0), pipeline_mode=once),
        out_shape=jax.ShapeDtypeStruct((m, d), F32),
        scratch_shapes=[pltpu.VMEM((tm, d), BF16), pltpu.VMEM((cap, d), BF16), pltpu.VMEM((cap, d), F32),
                        pltpu.VMEM((tm, LANES), F32), pltpu.VMEM((LANES, tm), F32), pltpu.VMEM((LANES, tm), F32)],
        compiler_params=_cparams(("parallel", "arbitrary", "arbitrary")),
        name="moe_sparse",
    )(x, comb, wg, wu, wd, g.reshape(1, d), b.reshape(1, d))


def _router_kernel(x_ref, r_ref, o_ref):
    logits = _dot_mid(x_ref[...], r_ref[...])
    lane = lax.broadcasted_iota(jnp.int32, logits.shape, 1)
    logits = jnp.where(lane < N_EXPERTS, logits, -jnp.inf)
    m1 = jnp.max(logits, axis=-1, keepdims=True)
    i1 = jnp.min(jnp.where(logits == m1, lane, LANES), axis=-1, keepdims=True)
    rest = jnp.where(lane == i1, -jnp.inf, logits)
    m2 = jnp.max(rest, axis=-1, keepdims=True)
    i2 = jnp.min(jnp.where(rest == m2, lane, LANES), axis=-1, keepdims=True)
    e2 = jnp.exp(m2 - m1)
    den = 1.0 + e2
    o_ref[...] = jnp.where(lane == i1, 1.0 / den, 0.0) + jnp.where(lane == i2, e2 / den, 0.0)


def _router(x, router):
    m, d = x.shape
    tm = _row_tile(m)
    r = jnp.pad(router, ((0, 0), (0, LANES - router.shape[1])))
    return pl.pallas_call(
        _router_kernel,
        grid=(m // tm,),
        in_specs=[pl.BlockSpec((tm, d), lambda i: (i, 0)), pl.BlockSpec((d, LANES), lambda i: (0, 0))],
        out_specs=pl.BlockSpec((tm, LANES), lambda i: (i, 0)),
        out_shape=jax.ShapeDtypeStruct((m, LANES), F32),
        compiler_params=_cparams(("parallel",)),
        name="router",
    )(x, r)


def _topk_lanes(g, idx, axis=-1):
    sel = jnp.zeros(g.shape, F32)
    for _ in range(MOBA_TOPK):
        m = jnp.max(g, axis=axis, keepdims=True)
        first = jnp.min(jnp.where(g == m, idx, LANES), axis=axis, keepdims=True)
        hit = jnp.logical_and(idx == first, m > -jnp.inf)
        sel = jnp.where(hit, 1.0, sel)
        g = jnp.where(hit, -jnp.inf, g)
    return sel


def _moba_kernel(slope_ref, q_ref, k_ref, v_ref, o_ref, kb_ref, vt_ref, km_ref, m_ref, l_ref, acc_ref, sel_ref,
                 *, tq, nb, q_offset, hd):
    i = pl.program_id(2)
    blk = MOBA_BLOCK

    @pl.when(i == 0)
    def _():
        kb_ref[...] = k_ref[...].astype(BF16)
        km_ref[...] = jnp.zeros_like(km_ref)
        for j in range(nb):
            rows = slice(j * blk, (j + 1) * blk)
            vt_ref[j] = v_ref[rows, :].T.astype(BF16)
            km_ref[j:j + 1, :] = jnp.sum(k_ref[rows, :], axis=0, keepdims=True) * (1.0 / blk)

    q0 = q_offset + i * tq
    own = q0 // blk
    q = q_ref[...]
    lane = lax.broadcasted_iota(jnp.int32, (tq, LANES), 1)
    q2 = jnp.concatenate([jnp.where(lane < hd, q, 0.0), jnp.where(lane >= hd, q, 0.0)], axis=0)
    gate_t = _dot_mid(km_ref[...], q2, dot=_dot_nt)
    blk_id = lax.broadcasted_iota(jnp.int32, gate_t.shape, 0)
    sel_ref[...] = _topk_lanes(jnp.where(blk_id < own, gate_t, -jnp.inf), blk_id, axis=0)
    qb = (q2 * hd ** -0.5).astype(BF16)
    c2 = lax.broadcasted_iota(jnp.int32, (blk, 2 * tq), 1)
    d0 = (jnp.where(c2 >= tq, c2 - tq, c2) - lax.broadcasted_iota(jnp.int32, (blk, 2 * tq), 0)).astype(F32)
    slope_lane = slope_ref[...]
    c1 = lax.broadcasted_iota(jnp.int32, (1, 2 * tq), 1)
    slope = jnp.where(c1 < tq, slope_lane[:, 0:1], slope_lane[:, hd:hd + 1])
    slope_d0 = slope * d0
    first = lax.broadcasted_iota(jnp.int32, (2 * hd, tq), 0) < hd
    m_ref[...] = jnp.full(m_ref.shape, NEG, F32)
    l_ref[...] = jnp.zeros_like(l_ref)
    acc_ref[...] = jnp.zeros_like(acc_ref)

    def tiles(js, diagonal):
        scores = []
        for j in js:
            off = (q0 - j * blk).astype(F32)
            s = _dot_nt(kb_ref[pl.ds(pl.multiple_of(j * blk, blk), blk), :], qb) - (slope_d0 + slope * off)
            if diagonal:
                s = jnp.where(d0 + off >= 0.0, s, NEG)
            else:
                s = jnp.where(sel_ref[pl.ds(j, 1), :] > 0.0, s, NEG)
            scores.append(s)
        m_old = m_ref[...]
        m_new = functools.reduce(jnp.maximum, [m_old] + [jnp.max(s, axis=0, keepdims=True) for s in scores])
        alpha = jnp.exp(m_old - m_new)
        l_new = alpha * l_ref[...]
        pv = None
        for j, s in zip(js, scores):
            p = jnp.exp(s - m_new)
            l_new = l_new + jnp.sum(p, axis=0, keepdims=True)
            d = _dot(vt_ref[j], p.astype(BF16))
            pv = d if pv is None else pv + d
        m_ref[...] = m_new
        l_ref[...] = l_new
        acc_ref[...] = (jnp.where(first, alpha[:, :tq], alpha[:, tq:]) * acc_ref[...]
                        + jnp.where(first, pv[:, :tq], pv[:, tq:]))

    tiles([own], True)

    def body(t, carry):
        tiles([own - 1 - 2 * t, own - 2 - 2 * t], False)
        return carry

    lax.fori_loop(0, own // 2, body, 0)

    @pl.when(own % 2 == 1)
    def _():
        tiles([own - own], False)
    l = l_ref[...]
    o_ref[...] = (acc_ref[...] / jnp.where(first, l[:, :tq], l[:, tq:])).T


def _alibi_slopes_lanes(n_heads, hd):
    slopes = jnp.exp2(-8.0 * jnp.arange(1, n_heads + 1, dtype=F32) / n_heads)
    return jnp.repeat(slopes, hd).reshape(1, n_heads * hd)


def _attn_call(kernel, q, k, v, *, batch, q_offset, n_heads, extra_in=(), extra_specs=(), extra_scratch=(), name):
    lq, lk = q.shape[0] // batch, k.shape[0] // batch
    hd = q.shape[1] // n_heads
    assert 2 * hd == LANES and lk % MOBA_BLOCK == 0
    tq = min(MOBA_BLOCK, lq)
    assert lq % tq == 0 and MOBA_BLOCK % tq == 0 and q_offset % tq == 0
    nq = lq // tq
    return pl.pallas_call(
        functools.partial(kernel, tq=tq, q_offset=q_offset, hd=hd),
        grid=(batch, n_heads // 2, nq),
        in_specs=list(extra_specs) + [
            pl.BlockSpec((tq, LANES), lambda b, h, i: (b * nq + i, h)),
            pl.BlockSpec((lk, LANES), lambda b, h, i: (b, h)),
            pl.BlockSpec((lk, LANES), lambda b, h, i: (b, h))],
        out_specs=pl.BlockSpec((tq, LANES), lambda b, h, i: (b * nq + i, h)),
        out_shape=jax.ShapeDtypeStruct(q.shape, F32),
        scratch_shapes=list(extra_scratch),
        compiler_params=_cparams(("parallel", "parallel", "arbitrary")),
        name=name,
    )(*extra_in, q, k, v)


def _moba(q, k, v, *, batch, q_offset):
    nb = k.shape[0] // batch // MOBA_BLOCK
    assert nb <= LANES
    slopes = _alibi_slopes_lanes(H_A, q.shape[1] // H_A)
    tq = min(MOBA_BLOCK, q.shape[0] // batch)
    nb_rows = -(-nb // 8) * 8
    return _attn_call(
        functools.partial(_moba_kernel, nb=nb), q, k, v, batch=batch, q_offset=q_offset, n_heads=H_A,
        extra_in=(slopes,), extra_specs=(pl.BlockSpec((1, LANES), lambda b, h, i: (0, h)),),
        extra_scratch=(pltpu.VMEM((nb * MOBA_BLOCK, LANES), BF16), pltpu.VMEM((nb, LANES, MOBA_BLOCK), BF16),
                       pltpu.VMEM((nb_rows, LANES), F32), pltpu.VMEM((1, 2 * tq), F32), pltpu.VMEM((1, 2 * tq), F32),
                       pltpu.VMEM((LANES, tq), F32), pltpu.VMEM((nb_rows, 2 * tq), F32)), name="moba")


EXP_UNDERFLOW = -110.0


def _softplus(z):
    return jnp.maximum(z, 0.0) + jnp.log(1.0 + jnp.exp(-jnp.abs(z)))


def _stick_kernel(q_ref, k_ref, v_ref, o_ref, kb_ref, vb_ref, c_ref, acc_ref, *, tq, q_offset, hd):
    i = pl.program_id(2)
    blk = MOBA_BLOCK

    @pl.when(i == 0)
    def _():
        kb_ref[...] = k_ref[...].astype(BF16)
        vb_ref[...] = v_ref[...].astype(BF16)

    q0 = q_offset + i * tq
    own = q0 // blk
    q = q_ref[...] * hd ** -0.5
    lane = lax.broadcasted_iota(jnp.int32, (tq, LANES), 1)
    qb = jnp.concatenate([jnp.where(lane < hd, q, 0.0), jnp.where(lane >= hd, q, 0.0)], axis=0).astype(BF16)
    r2 = lax.broadcasted_iota(jnp.int32, (2 * tq, blk), 0)
    d0 = jnp.where(r2 >= tq, r2 - tq, r2) - lax.broadcasted_iota(jnp.int32, (2 * tq, blk), 1)
    u = jnp.where(lax.broadcasted_iota(jnp.int32, (blk, blk), 0) > lax.broadcasted_iota(jnp.int32, (blk, blk), 1),
                  1.0, 0.0).astype(BF16)
    c_ref[...] = jnp.zeros_like(c_ref)
    acc_ref[...] = jnp.zeros_like(acc_ref)

    def tile(j, diagonal):
        start = pl.multiple_of(j * blk, blk)
        z = _dot_nt(qb, kb_ref[pl.ds(start, blk), :])
        log_keep = -_softplus(z)
        log_beta = z + log_keep
        if diagonal:
            before = d0 + (q0 - j * blk) > 0
            log_keep = jnp.where(before, log_keep, 0.0)
        hi = log_keep.astype(BF16)
        lo = (log_keep - hi.astype(F32)).astype(BF16)
        later = _dot(jnp.concatenate([hi, lo], axis=0), u)
        later = later[:2 * tq] + later[2 * tq:]
        c = c_ref[...]
        w = jnp.exp(log_beta + later + c)
        if diagonal:
            w = jnp.where(before, w, 0.0)
        c_ref[...] = c + later[:, 0:1] + log_keep[:, 0:1]
        pv = _dot(w.astype(BF16), vb_ref[pl.ds(start, blk), :])
        acc_ref[...] += jnp.where(lane < hd, pv[:tq], pv[tq:])

    tile(own, True)

    def more(carry):
        t, c_max = carry
        return jnp.logical_and(t <= own, c_max > EXP_UNDERFLOW)

    def body(carry):
        t, _ = carry
        tile(own - t, False)
        return t + 1, jnp.max(c_ref[...])

    lax.while_loop(more, body, (jnp.int32(1), jnp.max(c_ref[...])))
    o_ref[...] = acc_ref[...]


def _stick(q, k, v, *, batch, q_offset):
    tq = min(MOBA_BLOCK, q.shape[0] // batch)
    lk = k.shape[0] // batch
    return _attn_call(_stick_kernel, q, k, v, batch=batch, q_offset=q_offset, n_heads=H_D,
                      extra_scratch=(pltpu.VMEM((lk, LANES), BF16), pltpu.VMEM((lk, LANES), BF16),
                                     pltpu.VMEM((2 * tq, 1), F32), pltpu.VMEM((tq, LANES), F32)), name="stick")


CHUNK = 64
SUB = 16
SEQ_BLOCK = 512
EXP_CLAMP = 80.0


def _dot_mid(a, b, dot=_dot):
    a0 = a.astype(BF16)
    a1 = (a - a0.astype(F32)).astype(BF16)
    b0 = b.astype(BF16)
    b1 = (b - b0.astype(F32)).astype(BF16)
    return dot(a0, b0) + (dot(a0, b1) + dot(a1, b0))


def _incl_lower(c):
    r = lax.broadcasted_iota(jnp.int32, (c, c), 0)
    s = lax.broadcasted_iota(jnp.int32, (c, c), 1)
    return r, s


def _hgrn_kernel(q_ref, f_ref, i_ref, g_ref, lb_ref, gn_ref, s0_ref, o_ref, s_ref, st_ref, *, c, n_chunks, n_heads,
                 layer):
    l = pl.program_id(1)
    dk = LANES

    w = n_heads * dk

    @pl.when(l == 0)
    def _():
        for h in range(n_heads):
            st_ref[h * dk:(h + 1) * dk, :] = s0_ref[0, h]

    r, s = _incl_lower(c)
    tri = jnp.where(s <= r, 1.0, 0.0).astype(BF16)
    sub = min(SUB, c)
    n = n_heads * c
    shift_c, shift_s = c.bit_length() - 1, sub.bit_length() - 1
    ri = lax.broadcasted_iota(jnp.int32, (n_heads * sub, n), 0)
    ci = lax.broadcasted_iota(jnp.int32, (n_heads * sub, n), 1)
    pair = jnp.right_shift(ri, shift_s) == jnp.right_shift(ci, shift_c)
    t_loc, s_loc = jnp.bitwise_and(ri, sub - 1), jnp.bitwise_and(ci, c - 1)
    own_cols = (jnp.right_shift(lax.broadcasted_iota(jnp.int32, (n, w), 0), shift_c)
                == lax.broadcasted_iota(jnp.int32, (n, w), 1) // dk)
    gn = gn_ref[...]
    e = jnp.exp(lb_ref[...] - jnp.max(lb_ref[...], axis=0, keepdims=True))
    lb = jnp.sum(e[:layer + 1], axis=0, keepdims=True) / jnp.sum(e, axis=0, keepdims=True)

    def stack(t):
        return jnp.concatenate([t[:, h * dk:(h + 1) * dk] for h in range(n_heads)], axis=0)

    def chunk(i, carry):
        rows = pl.ds(pl.multiple_of(i * c, c), c)
        z = f_ref[rows, :]
        f = lb + (1.0 - lb) * jax.nn.sigmoid(z)
        k = (1.0 - lb) * jax.nn.sigmoid(-z)
        q = jax.nn.silu(q_ref[rows, :])
        cg = _dot_exact_lhs(tri, jnp.log(f))
        v4 = stack(i_ref[rows, :]).astype(BF16)
        st = st_ref[...]
        intra = []
        for b in range(c // sub):
            lo, hi = b * sub, (b + 1) * sub
            ref_pt = cg[lo - 1:lo] if b else jnp.zeros((1, w), F32)
            qs = stack(q[lo:hi] * jnp.exp(cg[lo:hi] - ref_pt))
            ks = stack(k * jnp.exp(jnp.minimum(ref_pt - cg, EXP_CLAMP)))
            a = _dot_nt(qs.astype(BF16), ks.astype(BF16))
            a = jnp.where(jnp.logical_and(pair, s_loc <= t_loc + lo), a, 0.0)
            intra.append(_dot(a.astype(BF16), v4))
        from_state = _dot_nt(stack(q * jnp.exp(cg)).astype(BF16), st.astype(BF16))
        last = cg[c - 1:c]
        kt = stack(k * jnp.exp(last - cg))
        v_wide = jnp.where(own_cols, jnp.concatenate([v4] * n_heads, axis=1), 0.0)
        e_last = jnp.concatenate([jnp.broadcast_to(jnp.exp(last[:, h * dk:(h + 1) * dk]), (dk, dk))
                                  for h in range(n_heads)], axis=0)
        st_ref[...] = st * e_last + _dot_tn(v_wide, kt.astype(BF16))
        for h in range(n_heads):
            o = from_state[h * c:(h + 1) * c, h * dk:(h + 1) * dk] + jnp.concatenate(
                [part[h * sub:(h + 1) * sub] for part in intra], axis=0)
            o = o * lax.rsqrt(jnp.mean(o * o, axis=-1, keepdims=True) + RMS_EPS) * gn
            o_ref[rows, h * dk:(h + 1) * dk] = o * jax.nn.sigmoid(g_ref[rows, h * dk:(h + 1) * dk])
        return carry

    lax.fori_loop(0, n_chunks, chunk, 0)

    @pl.when(l == pl.num_programs(1) - 1)
    def _():
        for h in range(n_heads):
            s_ref[0, h] = st_ref[h * dk:(h + 1) * dk, :]


def _hgrn(hg, lb, gn, s0, *, batch, layer=0):
    t, width = hg.shape
    seq = t // batch
    w = width // 4
    n_heads = w // LANES
    c = min(CHUNK, seq)
    lblk = min(SEQ_BLOCK, seq)
    assert seq % lblk == 0 and lblk % c == 0 and c % min(SUB, c) == 0
    nl = seq // lblk
    part = lambda p: pl.BlockSpec((lblk, w), lambda b, l: (b * nl + l, p))
    state = pl.BlockSpec((1, n_heads, LANES, LANES), lambda b, l: (b, 0, 0, 0))
    o, st = pl.pallas_call(
        functools.partial(_hgrn_kernel, c=c, n_chunks=lblk // c, n_heads=n_heads, layer=layer),
        grid=(batch, nl),
        in_specs=[part(0), part(1), part(2), part(3),
                  pl.BlockSpec(lb.shape, lambda b, l: (0, 0)), pl.BlockSpec((1, LANES), lambda b, l: (0, 0)), state],
        out_specs=[pl.BlockSpec((lblk, w), lambda b, l: (b * nl + l, 0)), state],
        out_shape=[jax.ShapeDtypeStruct((t, w), F32), jax.ShapeDtypeStruct(s0.shape, F32)],
        scratch_shapes=[pltpu.VMEM((n_heads * LANES, LANES), F32)],
        compiler_params=_cparams(("parallel", "arbitrary")),
        name="hgrn2",
    )(hg, hg, hg, hg, lb, gn.reshape(1, LANES), jnp.swapaxes(s0, -1, -2))
    return o, jnp.swapaxes(st, -1, -2)


def _unit_lower_inverse(lm, n, period):
    bs = min(SUB, period)
    r, s = _incl_lower(n)
    eye = jnp.where(r == s, 1.0, 0.0)
    shift = bs.bit_length() - 1
    same = jnp.right_shift(r, shift) == jnp.right_shift(s, shift)

    def mm(a, b):
        return _dot(a.astype(BF16), b.astype(BF16))

    def neumann(a, order):
        inv, pw, k = eye + a, a, 2
        while k < order:
            pw = mm(pw, pw)
            inv = inv + mm(inv, pw)
            k *= 2
        return inv

    inv_d = neumann(-jnp.where(same, lm, 0.0), bs)
    if period == bs:
        return inv_d
    m = mm(inv_d, jnp.where(same, 0.0, lm))
    return mm(neumann(-m, period // bs), inv_d)


def _gdn_kernel(x_ref, ab_ref, g_ref, cw_ref, al_ref, dtb_ref, gn_ref, s0_ref, cb_ref, o_ref, s_ref, nb_ref,
                xx_ref, y_ref, st_ref, *, c, n_chunks, n_heads, lblk):
    l = pl.program_id(1)
    dk = LANES
    halo = 8
    w = n_heads * dk

    @pl.when(l == 0)
    def _():
        for h in range(n_heads):
            st_ref[:, h * dk:(h + 1) * dk] = s0_ref[0, h]
        xx_ref[halo - (CONV_W - 1):halo, :] = cb_ref[0]

    xx_ref[halo:halo + lblk, :] = x_ref[...]
    y = xx_ref[halo:halo + lblk, :] * cw_ref[CONV_W - 1:CONV_W, :]
    for i in range(CONV_W - 1):
        off = halo - (CONV_W - 1) + i
        y = y + xx_ref[off:off + lblk, :] * cw_ref[i:i + 1, :]
    y_ref[...] = jax.nn.silu(y)
    tail = xx_ref[halo + lblk - (CONV_W - 1):halo + lblk, :]
    xx_ref[halo - (CONV_W - 1):halo, :] = tail

    r, s = _incl_lower(c)
    tri = jnp.where(s <= r, 1.0, 0.0).astype(BF16)
    n = n_heads * c
    rr, ss = _incl_lower(n)
    shift = c.bit_length() - 1
    same_head = jnp.right_shift(rr, shift) == jnp.right_shift(ss, shift)
    incl = jnp.logical_and(same_head, ss <= rr)
    strict = jnp.logical_and(same_head, ss < rr)
    own_cols = (jnp.right_shift(lax.broadcasted_iota(jnp.int32, (n, w), 0), shift)
                == lax.broadcasted_iota(jnp.int32, (n, w), 1) // dk)
    gn = gn_ref[...]

    def l2n(t):
        return t * lax.rsqrt(jnp.sum(t * t, axis=-1, keepdims=True) + RMS_EPS)

    def stack(f):
        return jnp.concatenate([f(h) for h in range(n_heads)], axis=0)

    def own_block(t):
        return stack(lambda h: t[h * c:(h + 1) * c, h * dk:(h + 1) * dk])

    def prepare(i):
        rows = pl.ds(i * c, c) if isinstance(i, int) else pl.ds(pl.multiple_of(i * c, c), c)
        ab = ab_ref[rows, :]
        pre = ab + dtb_ref[...]
        log_a = -jnp.exp(al_ref[...]) * (jnp.maximum(pre, 0.0) + jnp.log(1.0 + jnp.exp(-jnp.abs(pre))))
        beta_all = jax.nn.sigmoid(ab)
        cg_all = _dot_exact_lhs(tri, log_a)
        cg_t = cg_all.T
        q = stack(lambda h: l2n(y_ref[rows, h * dk:(h + 1) * dk])) * dk ** -0.5
        k = stack(lambda h: l2n(y_ref[rows, w + h * dk:w + (h + 1) * dk]))
        v = stack(lambda h: y_ref[rows, 2 * w + h * dk:2 * w + (h + 1) * dk])
        cg = stack(lambda h: cg_all[:, h:h + 1])
        beta = stack(lambda h: beta_all[:, n_heads + h:n_heads + h + 1])
        cg_row = jnp.concatenate([cg_t[h:h + 1, :c] for h in range(n_heads)], axis=1)
        last = stack(lambda h: jnp.broadcast_to(cg_all[c - 1:c, h:h + 1], (c, 1)))
        decay = jnp.exp(jnp.where(incl, cg - cg_row, NEG))
        kk = _dot_mid(k, k, dot=_dot_nt) * decay * beta
        t_inv = _unit_lower_inverse(jnp.where(strict, kk, 0.0), n, c)
        e_cg = jnp.exp(cg)
        sol = _dot_mid(t_inv, jnp.concatenate([v * beta, k * (beta * e_cg)], axis=1))
        qk = (_dot_nt(q.astype(BF16), k.astype(BF16)) * decay).astype(BF16)
        kt = (k * jnp.exp(last - cg)).astype(BF16)
        e_last = jnp.concatenate([jnp.broadcast_to(jnp.exp(cg_all[c - 1:c, h:h + 1]), (1, dk)) for h in range(n_heads)],
                                 axis=1)
        return rows, sol[:, :dk], sol[:, dk:].astype(BF16), (q * e_cg).astype(BF16), qk, kt, e_last

    def advance(prepared):
        rows, sol_v, sol_k, q_decayed, qk, kt, e_last = prepared
        st = st_ref[...]
        stb = st.astype(BF16)
        u = sol_v - own_block(_dot(sol_k, stb))
        o = own_block(_dot(q_decayed, stb)) + _dot(qk, u.astype(BF16))
        u_wide = jnp.where(own_cols, jnp.concatenate([u] * n_heads, axis=1), 0.0)
        st_ref[...] = st * e_last + _dot_tn(kt, u_wide.astype(BF16))
        o = o * lax.rsqrt(jnp.mean(o * o, axis=-1, keepdims=True) + RMS_EPS) * gn
        for h in range(n_heads):
            o_ref[rows, h * dk:(h + 1) * dk] = o[h * c:(h + 1) * c] * jax.nn.silu(g_ref[rows, h * dk:(h + 1) * dk])

    def chunk_pair(i, carry):
        first, second = prepare(2 * i), prepare(2 * i + 1)
        advance(first)
        advance(second)
        return carry

    lax.fori_loop(0, n_chunks // 2, chunk_pair, 0)
    if n_chunks % 2:
        advance(prepare(n_chunks - 1))

    @pl.when(l == pl.num_programs(1) - 1)
    def _():
        for h in range(n_heads):
            s_ref[0, h] = st_ref[:, h * dk:(h + 1) * dk]
        nb_ref[0] = tail


def _gdn(qkv, ab, g, conv_w, a_log, dt_bias, gn, s0, conv_buf, *, batch):
    t, width = qkv.shape
    seq = t // batch
    w = width // 3
    n_heads = w // LANES
    c = min(CHUNK, seq)
    lblk = min(SEQ_BLOCK, seq)
    assert seq % lblk == 0 and lblk % c == 0 and lblk >= CONV_W - 1
    nl = seq // lblk
    pad_lane = lambda vec: jnp.pad(vec, (0, LANES - vec.shape[0])).reshape(1, LANES)
    rows = lambda width_: pl.BlockSpec((lblk, width_), lambda b, l: (b * nl + l, 0))
    const = lambda shape: pl.BlockSpec(shape, lambda b, l: (0,) * len(shape))
    state = pl.BlockSpec((1, n_heads, LANES, LANES), lambda b, l: (b, 0, 0, 0))
    buf = pl.BlockSpec((1, CONV_W - 1, width), lambda b, l: (b, 0, 0))
    return pl.pallas_call(
        functools.partial(_gdn_kernel, c=c, n_chunks=lblk // c, n_heads=n_heads, lblk=lblk),
        grid=(batch, nl),
        in_specs=[rows(width), rows(LANES), rows(w), const((CONV_W, width)), const((1, LANES)), const((1, LANES)),
                  const((1, LANES)), state, buf],
        out_specs=[rows(w), state, buf],
        out_shape=[jax.ShapeDtypeStruct((t, w), F32), jax.ShapeDtypeStruct(s0.shape, F32),
                   jax.ShapeDtypeStruct(conv_buf.shape, F32)],
        scratch_shapes=[pltpu.VMEM((lblk + 8, width), F32), pltpu.VMEM((lblk, width), F32),
                        pltpu.VMEM((LANES, w), F32)],
        compiler_params=_cparams(("parallel", "arbitrary")),
        name="gated_deltanet",
    )(qkv, ab, g, conv_w, pad_lane(a_log), pad_lane(dt_bias), gn.reshape(1, LANES), s0, conv_buf)


PAGES_PER_STEP = 8


def _head_fold(pv, n_heads, nq, hd):
    return jnp.concatenate([pv[h * nq:(h + 1) * nq, h * hd:(h + 1) * hd] for h in range(n_heads)], axis=0)


def _row_ids(rows, nq):
    r = lax.broadcasted_iota(jnp.int32, (rows, 1), 0)
    return r // nq, r % nq


def _page_group(refs, first, count):
    return jnp.concatenate([refs[first + r][0, 0].reshape(-1, refs[first + r].shape[-1]) for r in range(count)], axis=1)


def _stick_paged_kernel(pt_ref, qbd_ref, kn_ref, vn_ref, k_hbm, v_hbm, o_ref, kbuf, vbuf, sem, *, n_heads, nq, hd,
                        n_pages):
    b = pl.program_id(0)
    rows = n_heads * nq
    page = kbuf.shape[-1]
    ppg = kbuf.shape[1]
    n_groups = n_pages // ppg
    _, row_q = _row_ids(rows, nq)
    qbd = (qbd_ref[0] * hd ** -0.5).astype(BF16)

    def copies(g, slot):
        out = []
        for r in range(ppg):
            pg = pt_ref[b, g * ppg + r]
            out.append(pltpu.make_async_copy(k_hbm.at[0, pg], kbuf.at[slot, r], sem.at[0, slot, r]))
            out.append(pltpu.make_async_copy(v_hbm.at[0, pg], vbuf.at[slot, r], sem.at[1, slot, r]))
        return out

    def group(buf, slot):
        return jnp.concatenate([buf[slot, r].reshape(n_heads * hd, page) for r in range(ppg)], axis=1).astype(BF16)

    def strict_later(n):
        return jnp.where(lax.broadcasted_iota(jnp.int32, (n, n), 0) > lax.broadcasted_iota(jnp.int32, (n, n), 1),
                         1.0, 0.0).astype(BF16)

    def weights(z, mask, carry, u):
        log_keep = -_softplus(z)
        log_beta = z + log_keep
        if mask is not None:
            log_keep = jnp.where(mask, log_keep, 0.0)
        hi = log_keep.astype(BF16)
        lo = (log_keep - hi.astype(F32)).astype(BF16)
        later = _dot(hi, u) + _dot(lo, u)
        w = jnp.exp(log_beta + later + carry)
        if mask is not None:
            w = jnp.where(mask, w, 0.0)
        return w.astype(BF16), carry + jnp.sum(log_keep, axis=-1, keepdims=True)

    for cp in copies(n_groups - 1, 0):
        cp.start()

    n = kn_ref.shape[1]
    z = _dot_nt(qbd, kn_ref[0].astype(BF16))
    col = lax.broadcasted_iota(jnp.int32, (rows, n), 1)
    w, c = weights(z, col < row_q, jnp.zeros((rows, 1), F32), strict_later(n))
    acc = _dot(w, vn_ref[0].astype(BF16))
    u = strict_later(ppg * page)

    def more(carry):
        g, c, _ = carry
        return jnp.logical_and(g >= 0, jnp.max(c) > EXP_UNDERFLOW)

    def body(carry):
        g, c, acc = carry
        slot = (n_groups - 1 - g) % 2

        @pl.when(g > 0)
        def _():
            for cp in copies(g - 1, 1 - slot):
                cp.start()

        for cp in copies(g, slot):
            cp.wait()
        w, c = weights(_dot(qbd, group(kbuf, slot)), None, c, u)
        return g - 1, c, acc + _dot_nt(w, group(vbuf, slot))

    g, _, acc = lax.while_loop(more, body, (jnp.int32(n_groups - 1), c, acc))

    @pl.when(g >= 0)
    def _():
        for cp in copies(g, (n_groups - 1 - g) % 2):
            cp.wait()

    o_ref[0] = _head_fold(acc, n_heads, nq, hd)


def _moba_paged_kernel(pt_ref, qbd_ref, kn_ref, vn_ref, *rest, pps, n_heads, nq, hd, past_len):
    k_refs, v_refs = rest[:pps], rest[pps:2 * pps]
    o_ref, mo_ref, lo_ref, acco_ref, m_ref, l_ref, acc_ref, gate_ref = rest[2 * pps:]
    s = pl.program_id(1)
    rows = n_heads * nq
    page = k_refs[0].shape[-1]
    ppb = MOBA_BLOCK // page
    row_h, row_q = _row_ids(rows, nq)
    slope = jnp.exp2(-8.0 * (row_h + 1).astype(F32) / n_heads)
    lane = lax.broadcasted_iota(jnp.int32, (rows, LANES), 1)
    q_f32 = qbd_ref[0]
    qbd = (q_f32 * hd ** -0.5).astype(BF16)

    @pl.when(s == 0)
    def _():
        n = kn_ref.shape[1]
        col = lax.broadcasted_iota(jnp.int32, (rows, n), 1)
        sc = _dot_nt(qbd, kn_ref[0].astype(BF16)) - slope * (row_q - col).astype(F32)
        sc = jnp.where(col <= row_q, sc, NEG)
        m = jnp.max(sc, axis=-1, keepdims=True)
        p = jnp.exp(sc - m)
        mo_ref[...] = m
        lo_ref[...] = jnp.sum(p, axis=-1, keepdims=True)
        acco_ref[...] = _head_fold(_dot(p.astype(BF16), vn_ref[0].astype(BF16)), n_heads, nq, hd)
        m_ref[...] = jnp.full(m_ref.shape, NEG, F32)
        l_ref[...] = jnp.zeros_like(l_ref)
        gate_ref[...] = jnp.zeros_like(gate_ref)

    @pl.when(s > 0)
    def _():
        key = lax.broadcasted_iota(jnp.int32, (rows, MOBA_BLOCK), 1)
        m_all, l_all, gate_all = m_ref[...], l_ref[...], gate_ref[...]
        for bi in range(pps // ppb):
            blk = (s - 1) * (pps // ppb) + bi
            kf = _page_group(k_refs, bi * ppb, ppb)
            k_mean = jnp.broadcast_to(jnp.sum(kf, axis=-1, keepdims=True) * (1.0 / MOBA_BLOCK), (kf.shape[0], LANES))
            gate = _dot_mid(q_f32, k_mean)
            dist = past_len + row_q - (blk * MOBA_BLOCK + key)
            sc = _dot(qbd, kf.astype(BF16)) - slope * dist.astype(F32)
            m_b = jnp.max(sc, axis=-1, keepdims=True)
            p = jnp.exp(sc - m_b)
            l_b = jnp.sum(p, axis=-1, keepdims=True)
            pv = _dot_nt(p.astype(BF16), _page_group(v_refs, bi * ppb, ppb).astype(BF16))
            acc_ref[blk] = _head_fold(pv, n_heads, nq, hd)
            m_all = jnp.where(lane == blk, m_b, m_all)
            l_all = jnp.where(lane == blk, l_b, l_all)
            gate_all = jnp.where(lane == blk, gate, gate_all)
        m_ref[...] = m_all
        l_ref[...] = l_all
        gate_ref[...] = gate_all

    @pl.when(s == pl.num_programs(1) - 1)
    def _():
        n_blocks = past_len // MOBA_BLOCK
        sel = _topk_lanes(jnp.where(lane < n_blocks, gate_ref[...], -jnp.inf), lane) > 0.0
        m_all, l_all = m_ref[...], l_ref[...]
        m_o = mo_ref[...]
        m_tot = jnp.maximum(m_o, jnp.max(jnp.where(sel, m_all, NEG), axis=-1, keepdims=True))
        wgt = jnp.where(sel, jnp.exp(m_all - m_tot), 0.0)
        a_o = jnp.exp(m_o - m_tot)
        l_tot = lo_ref[...] * a_o + jnp.sum(wgt * l_all, axis=-1, keepdims=True)
        acc = acco_ref[...] * a_o
        for b in range(n_blocks):
            acc = acc + wgt[:, b:b + 1] * acc_ref[b]
        o_ref[0] = acc / l_tot


def _paged_attention(kind, q, k_new, v_new, k_pool, v_pool, page_table, *, n_heads):
    batch, n_pages = page_table.shape
    page, hd = k_pool.shape[2], k_pool.shape[4]
    nq = q.shape[0] // batch
    rows = n_heads * nq
    pps = PAGES_PER_STEP
    past_len = n_pages * page
    assert n_pages % pps == 0 and past_len % MOBA_BLOCK == 0 and MOBA_BLOCK % page == 0 and pps % (MOBA_BLOCK // page) == 0
    assert nq <= page and past_len // MOBA_BLOCK <= LANES and (2 * LANES) % n_heads == 0
    n_groups = n_pages // pps
    q4 = q.reshape(batch, nq, n_heads, hd).transpose(0, 2, 1, 3)
    q_bd = (q4[:, :, :, None, :] * jnp.eye(n_heads, dtype=F32)[None, :, None, :, None]).reshape(batch, rows, n_heads * hd)
    pad_new = lambda t: jnp.pad(t.reshape(batch, nq, n_heads * hd), ((0, 0), (0, page - nq), (0, 0)))
    k_t, v_t = (jnp.transpose(t, (0, 1, 3, 4, 2)) for t in (k_pool, v_pool))

    if kind == "stick":
        ppg = MOBA_BLOCK // page
        seq = lambda shape: pl.BlockSpec((1,) + shape, lambda b, pt: (b, 0, 0))
        buf = pltpu.VMEM((2, ppg, n_heads, hd, page), F32)
        out = pl.pallas_call(
            functools.partial(_stick_paged_kernel, n_heads=n_heads, nq=nq, hd=hd, n_pages=n_pages),
            grid_spec=pltpu.PrefetchScalarGridSpec(
                num_scalar_prefetch=1,
                grid=(batch,),
                in_specs=[seq((rows, n_heads * hd)), seq((page, n_heads * hd)), seq((page, n_heads * hd)),
                          pl.BlockSpec(memory_space=pl.ANY), pl.BlockSpec(memory_space=pl.ANY)],
                out_specs=seq((rows, hd)),
                scratch_shapes=[buf, buf, pltpu.SemaphoreType.DMA((2, 2, ppg))]),
            out_shape=jax.ShapeDtypeStruct((batch, rows, hd), F32),
            compiler_params=_cparams(("arbitrary",)),
            name="stick_paged",
        )(page_table, q_bd, pad_new(k_new), pad_new(v_new), k_t, v_t)
    else:
        group = lambda s: jnp.maximum(s, 1) - 1
        scratch = [pltpu.VMEM((rows, 1), F32), pltpu.VMEM((rows, 1), F32), pltpu.VMEM((rows, hd), F32),
                   pltpu.VMEM((rows, LANES), F32), pltpu.VMEM((rows, LANES), F32),
                   pltpu.VMEM((past_len // MOBA_BLOCK, rows, hd), F32), pltpu.VMEM((rows, LANES), F32)]

        def page_spec(r):
            return pl.BlockSpec((1, 1, n_heads, hd, page), lambda b, s, pt: (0, pt[b, group(s) * pps + r], 0, 0, 0))

        per_seq = lambda shape: pl.BlockSpec((1,) + shape, lambda b, s, pt: (b, 0, 0))
        out = pl.pallas_call(
            functools.partial(_moba_paged_kernel, pps=pps, n_heads=n_heads, nq=nq, hd=hd, past_len=past_len),
            grid_spec=pltpu.PrefetchScalarGridSpec(
                num_scalar_prefetch=1,
                grid=(batch, n_groups + 1),
                in_specs=[per_seq((rows, n_heads * hd)), per_seq((page, n_heads * hd)),
                          per_seq((page, n_heads * hd))] + [page_spec(r) for r in range(pps)] * 2,
                out_specs=per_seq((rows, hd)),
                scratch_shapes=scratch),
            out_shape=jax.ShapeDtypeStruct((batch, rows, hd), F32),
            compiler_params=_cparams(("parallel", "arbitrary")),
            name="moba_paged",
        )(page_table, q_bd, pad_new(k_new), pad_new(v_new), *([k_t] * pps), *([v_t] * pps))
    return out.reshape(batch, n_heads, nq, hd).transpose(0, 2, 1, 3).reshape(batch * nq, n_heads * hd)


def _layer_stack(x, batch, past, states, w):
    s_hgrn, s_gdn, conv_buf = states

    def attend(kind, q, k, v, pools, n_heads):
        if past is None:
            return (_moba if kind == "moba" else _stick)(q, k, v, batch=batch, q_offset=0)
        return _paged_attention(kind, q, k, v, *pools, past[4], n_heads=n_heads)

    (q_a, k_a, v_a, hg), kv_a_t = _project(x, w["in_a"] + [w["in_hgrn"]], batch, transposed=(1, 2))
    o_a = attend("moba", q_a, k_a, v_a, past and past[0:2], H_A)
    o_b, s_hgrn_new = _hgrn(hg, w["hgrn_lb"], w["hgrn_norm"], s_hgrn, batch=batch, layer=0)
    x = _matmul([o_a, o_b], w["out_even"], ln_args=(x, w["ln1_g"][0], w["ln1_b"][0]))
    x = _ffn(x, None, w["ffn_wg"], w["ffn_wu"], w["ffn_wd"], w["ln2_g"][0], w["ln2_b"][0])

    (qkv_c, ab, g_c, q_d, k_d, v_d), kv_d_t = _project(x, [w["in_qkv_c"], w["in_ab"], w["in_g_c"]] + w["in_d"], batch,
                                                       transposed=(4, 5))
    o_c, s_gdn_new, conv_new = _gdn(qkv_c, ab, g_c, w["conv_w"], w["a_log"], w["dt_bias"], w["gdn_norm"],
                                    s_gdn, conv_buf, batch=batch)
    o_d = attend("stick", q_d, k_d, v_d, past and past[2:4], H_D)
    x = _matmul([o_c, o_d], w["out_odd"], ln_args=(x, w["ln1_g"][1], w["ln1_b"][1]))
    comb = _router(x, w["router"])
    x = _moe(x, comb, w["moe_wg"], w["moe_wu"], w["moe_wd"], w["ln2_g"][1], w["ln2_b"][1])
    k_a, v_a = kv_a_t or (k_a, v_a)
    k_d, v_d = kv_d_t or (k_d, v_d)
    return x, k_a, v_a, s_hgrn_new, s_gdn_new, conv_new, k_d, v_d


def kernel(x_prompt, x_sample, cache_k_moba, cache_v_moba, state_hgrn, state_gdn, state_gdn_conv, cache_k_sb,
           cache_v_sb, page_table, w_in_even, w_out_even, hgrn_lb, hgrn_norm, w_in_odd, w_out_odd, gdn_conv_w,
           gdn_a_log, gdn_dt_bias, gdn_norm, ln1_g, ln1_b, ln2_g, ln2_b, ffn_wg, ffn_wu, ffn_wd, router, moe_wg,
           moe_wu, moe_wd):
    assert w_in_even.shape[0] == 1 and w_in_odd.shape[0] == 1
    bp, lp, d = x_prompt.shape
    bs, ls, _ = x_sample.shape
    mix = d // 2
    hd_a, hd_d = mix // H_A, mix // H_D
    conv_dim = gdn_conv_w.shape[-1]
    bf = lambda t: t.astype(BF16)
    cols = lambda wt, lo, n: bf(wt[:, lo:lo + n])

    wie, wio = w_in_even[0], w_in_odd[0]
    g_lo = conv_dim + 2 * H_C
    d_lo = g_lo + mix
    w = {
        "in_a": [cols(wie, i * mix, mix) for i in range(3)],
        "in_hgrn": cols(wie, 3 * mix, 4 * mix),
        "out_even": [bf(w_out_even[0][:mix]), bf(w_out_even[0][mix:])],
        "hgrn_lb": hgrn_lb, "hgrn_norm": hgrn_norm[0],
        "in_qkv_c": cols(wio, 0, conv_dim),
        "in_ab": bf(jnp.pad(wio[:, conv_dim:g_lo], ((0, 0), (0, LANES - 2 * H_C)))),
        "in_g_c": cols(wio, g_lo, mix),
        "in_d": [cols(wio, d_lo + i * mix, mix) for i in range(3)],
        "out_odd": [bf(w_out_odd[0][:mix]), bf(w_out_odd[0][mix:])],
        "conv_w": gdn_conv_w[0], "a_log": gdn_a_log[0], "dt_bias": gdn_dt_bias[0], "gdn_norm": gdn_norm[0],
        "ln1_g": ln1_g, "ln1_b": ln1_b, "ln2_g": ln2_g, "ln2_b": ln2_b,
        "ffn_wg": bf(ffn_wg), "ffn_wu": bf(ffn_wu), "ffn_wd": bf(ffn_wd),
        "router": router[0], "moe_wg": bf(moe_wg[0]), "moe_wu": bf(moe_wu[0]), "moe_wd": bf(moe_wd[0]),
    }

    def run(x, batch, past, states):
        seq = x.shape[1]
        y, k_a, v_a, s_h, s_g, cv, k_d, v_d = _layer_stack(x.reshape(batch * seq, d), batch, past, states, w)
        def cache(t, n_heads, hd):
            if t.ndim == 3:
                return t.reshape(batch, n_heads, hd, seq).transpose(0, 3, 1, 2)[None]
            return t.reshape(1, batch, seq, n_heads, hd)

        return (y.reshape(batch, seq, d), cache(k_a, H_A, hd_a), cache(v_a, H_A, hd_a), s_h[None], s_g[None], cv[None],
                cache(k_d, H_D, hd_d), cache(v_d, H_D, hd_d))

    past = (cache_k_moba, cache_v_moba, cache_k_sb, cache_v_sb, page_table)
    out_s = run(x_sample, bs, past, (state_hgrn[0], state_gdn[0], state_gdn_conv[0]))
    zeros_p = (jnp.zeros((bp,) + state_hgrn.shape[2:], F32), jnp.zeros((bp,) + state_gdn.shape[2:], F32),
               jnp.zeros((bp,) + state_gdn_conv.shape[2:], F32))
    out_p = run(x_prompt, bp, None, zeros_p)
    return (out_p[0], out_s[0]) + out_p[1:] + out_s[1:]
```

```python
import functools

import jax
import jax.numpy as jnp
from jax import lax
from jax.experimental import pallas as pl
from jax.experimental.pallas import tpu as pltpu

F32 = jnp.float32
BF16 = jnp.bfloat16

H_A, H_B, H_C, H_D = 8, 4, 4, 8
MOBA_BLOCK = 256
MOBA_TOPK = 3
CONV_W = 4
N_EXPERTS = 8
DEPTH = 2
DEEPNORM_ALPHA = (2 * DEPTH) ** 0.25
LN_EPS = 1e-5
RMS_EPS = 1e-6
NEG = -1e30
LANES = 128
VMEM_LIMIT = 56 * 1024 * 1024


def _cparams(sem):
    return pltpu.CompilerParams(dimension_semantics=sem, vmem_limit_bytes=VMEM_LIMIT)


def _layernorm(y, g, b):
    mu = jnp.mean(y, axis=-1, keepdims=True)
    yc = y - mu
    var = jnp.mean(yc * yc, axis=-1, keepdims=True)
    return yc * lax.rsqrt(var + LN_EPS) * g + b


def _split3(x):
    hi = x.astype(BF16)
    r = x - hi.astype(F32)
    mid = r.astype(BF16)
    lo = (r - mid.astype(F32)).astype(BF16)
    return hi, mid, lo


def _dot(a, b):
    return jnp.dot(a, b, preferred_element_type=F32)


def _dot_nt(a, b):
    return lax.dot_general(a, b, (((1,), (1,)), ((), ())), preferred_element_type=F32)


def _dot_tn(a, b):
    return lax.dot_general(a, b, (((0,), (0,)), ((), ())), preferred_element_type=F32)


def _dot_hi(a, b, dot=_dot):
    a0, a1, a2 = _split3(a)
    b0, b1, b2 = _split3(b)
    return (dot(a0, b0) + (dot(a0, b1) + dot(a1, b0))
            + (dot(a1, b1) + dot(a0, b2) + dot(a2, b0)))


def _dot_exact_rhs(a, b_bf16):
    a0, a1, a2 = _split3(a)
    return _dot(a0, b_bf16) + _dot(a1, b_bf16) + _dot(a2, b_bf16)


def _dot_exact_lhs(a_bf16, b):
    b0, b1, b2 = _split3(b)
    return _dot(a_bf16, b0) + _dot(a_bf16, b1) + _dot(a_bf16, b2)


def _mm_kernel(*refs, n_in, ln):
    a_refs, w_refs, rest = refs[:n_in], refs[n_in:2 * n_in], refs[2 * n_in:]
    acc = None
    for a, w in zip(a_refs, w_refs):
        d = _dot(a[...].astype(BF16), w[...])
        acc = d if acc is None else acc + d
    if ln:
        res_ref, g_ref, b_ref, o_ref = rest
        o_ref[...] = _layernorm(DEEPNORM_ALPHA * res_ref[...] + acc, g_ref[...], b_ref[...])
    else:
        (o_ref,) = rest
        o_ref[...] = acc


def _row_tile(m, largest=512):
    for t in (1024, 512, 256, 128, 64, 32, 16, 8):
        if t <= largest and m % t == 0:
            return t
    raise ValueError(m)


def _proj_kernel(x_ref, *refs, n_w, n_t):
    w_refs, wt_refs = refs[:n_w], refs[n_w:n_w + n_t]
    o_refs, ot_refs = refs[n_w + n_t:2 * n_w + n_t], refs[2 * n_w + n_t:]
    xb = x_ref[...].astype(BF16)
    for w_ref, o_ref in zip(w_refs, o_refs):
        o_ref[...] = _dot(xb, w_ref[...])
    for wt_ref, ot_ref in zip(wt_refs, ot_refs):
        ot_ref[0] = _dot_nt(wt_ref[...], xb)


def _project(x, w_list, batch, transposed=()):
    m, d = x.shape
    tm = _row_tile(m)
    seq = m // batch
    wt_list = [w_list[t].T for t in transposed] if seq % tm == 0 else []
    nl = max(seq // tm, 1)
    const = lambda w: pl.BlockSpec(w.shape, lambda i: (0, 0))
    outs = pl.pallas_call(
        functools.partial(_proj_kernel, n_w=len(w_list), n_t=len(wt_list)),
        grid=(m // tm,),
        in_specs=[pl.BlockSpec((tm, d), lambda i: (i, 0))] + [const(w) for w in w_list] + [const(w) for w in wt_list],
        out_specs=[pl.BlockSpec((tm, w.shape[1]), lambda i: (i, 0)) for w in w_list]
        + [pl.BlockSpec((1, w.shape[0], tm), lambda i: (i // nl, 0, i % nl)) for w in wt_list],
        out_shape=[jax.ShapeDtypeStruct((m, w.shape[1]), F32) for w in w_list]
        + [jax.ShapeDtypeStruct((batch, w.shape[0], seq), F32) for w in wt_list],
        compiler_params=_cparams(("parallel",)),
        name="in_proj",
    )(x, *w_list, *wt_list)
    return outs[:len(w_list)], outs[len(w_list):]


def _matmul(a_list, w_list, *, tn=None, ln_args=None):
    m = a_list[0].shape[0]
    n = w_list[0].shape[1]
    tm = _row_tile(m, largest=1024)
    ln = ln_args is not None
    tn = n if (ln or tn is None) else tn
    assert n % tn == 0
    in_specs = [pl.BlockSpec((tm, a.shape[1]), lambda i, j: (i, 0)) for a in a_list]
    in_specs += [pl.BlockSpec((w.shape[0], tn), lambda i, j: (0, j)) for w in w_list]
    args = list(a_list) + list(w_list)
    if ln:
        res, g, b = ln_args
        in_specs += [pl.BlockSpec((tm, n), lambda i, j: (i, 0)),
                     pl.BlockSpec((1, n), lambda i, j: (0, 0)),
                     pl.BlockSpec((1, n), lambda i, j: (0, 0))]
        args += [res, g.reshape(1, n), b.reshape(1, n)]
    return pl.pallas_call(
        functools.partial(_mm_kernel, n_in=len(a_list), ln=ln),
        grid=(m // tm, n // tn),
        in_specs=in_specs,
        out_specs=pl.BlockSpec((tm, tn), lambda i, j: (i, j)),
        out_shape=jax.ShapeDtypeStruct((m, n), F32),
        compiler_params=_cparams(("parallel", "arbitrary")),
        name="matmul_ln" if ln else "matmul",
    )(*args)


def _ffn_kernel(*refs, use_comb):
    if use_comb:
        x_ref, comb_ref, wg_ref, wu_ref, wd_ref, g_ref, b_ref, o_ref, xb_ref, acc_ref = refs
    else:
        x_ref, wg_ref, wu_ref, wd_ref, g_ref, b_ref, o_ref, xb_ref, acc_ref = refs
    e, j = pl.program_id(1), pl.program_id(2)
    first = jnp.logical_and(e == 0, j == 0)
    last = jnp.logical_and(e == pl.num_programs(1) - 1, j == pl.num_programs(2) - 1)

    @pl.when(first)
    def _():
        xb_ref[...] = x_ref[...].astype(BF16)
        acc_ref[...] = jnp.zeros_like(acc_ref)

    xb = xb_ref[...]
    h = jax.nn.silu(_dot(xb, wg_ref[0])) * _dot(xb, wu_ref[0])
    if use_comb:
        lane = lax.broadcasted_iota(jnp.int32, comb_ref.shape, 1)
        h = h * jnp.sum(jnp.where(lane == e, comb_ref[...], 0.0), axis=-1, keepdims=True)
    acc_ref[...] += _dot(h.astype(BF16), wd_ref[0])

    @pl.when(last)
    def _():
        o_ref[...] = _layernorm(DEEPNORM_ALPHA * x_ref[...] + acc_ref[...], g_ref[...], b_ref[...])


def _ff_tile(ff, largest=896):
    for t in (896, 512, 256, LANES):
        if t <= largest and ff % t == 0:
            return t
    raise ValueError(ff)


def _ffn(x, comb, wg, wu, wd, g, b):
    m, d = x.shape
    n_e, _, ff = wg.shape
    tm = _row_tile(m, largest=1024)
    tf = _ff_tile(ff, largest=512)
    use_comb = comb is not None
    in_specs = [pl.BlockSpec((tm, d), lambda i, e, j: (i, 0))]
    args = [x]
    if use_comb:
        in_specs.append(pl.BlockSpec((tm, comb.shape[1]), lambda i, e, j: (i, 0)))
        args.append(comb)
    in_specs += [pl.BlockSpec((1, d, tf), lambda i, e, j: (e, 0, j)),
                 pl.BlockSpec((1, d, tf), lambda i, e, j: (e, 0, j)),
                 pl.BlockSpec((1, tf, d), lambda i, e, j: (e, j, 0)),
                 pl.BlockSpec((1, d), lambda i, e, j: (0, 0)),
                 pl.BlockSpec((1, d), lambda i, e, j: (0, 0))]
    args += [wg, wu, wd, g.reshape(1, d), b.reshape(1, d)]
    return pl.pallas_call(
        functools.partial(_ffn_kernel, use_comb=use_comb),
        grid=(m // tm, n_e, ff // tf),
        in_specs=in_specs,
        out_specs=pl.BlockSpec((tm, d), lambda i, e, j: (i, 0)),
        out_shape=jax.ShapeDtypeStruct((m, d), F32),
        scratch_shapes=[pltpu.VMEM((tm, d), BF16), pltpu.VMEM((tm, d), F32)],
        compiler_params=_cparams(("parallel", "arbitrary", "arbitrary")),
        name="moe_ffn" if use_comb else "ffn",
    )(*args)


MOE_TOKENS = 1024
MOE_SUBTILES = 1
MOE_ROWS = 128


def _moe_kernel(x_ref, comb_ref, wg_ref, wu_ref, wd_ref, g_ref, b_ref, o_ref,
                xb_ref, xc_ref, yc_ref, rank_ref, rank_t_ref, comb_t_ref):
    e, j = pl.program_id(1), pl.program_id(2)
    tm, d = x_ref.shape
    r, ts = MOE_ROWS, MOE_TOKENS
    subs = [slice(s * ts, (s + 1) * ts) for s in range(tm // ts)]
    lane = lax.broadcasted_iota(jnp.int32, (ts, LANES), 1)

    @pl.when(jnp.logical_and(e == 0, j == 0))
    def _():
        xb_ref[...] = x_ref[...].astype(BF16)
        o_ref[...] = jnp.zeros_like(o_ref)
        earlier = jnp.where(lax.broadcasted_iota(jnp.int32, (ts, ts), 1) < lax.broadcasted_iota(jnp.int32, (ts, ts), 0),
                            1.0, 0.0).astype(BF16)
        for sub in subs:
            routed = jnp.where(comb_ref[sub, :] > 0.0, 1.0, 0.0).astype(BF16)
            rank = _dot(earlier, routed)
            rank_ref[sub, :] = rank
            rank_t_ref[:, sub] = rank.T
            comb_t_ref[:, sub] = comb_ref[sub, :].T

    gate_rows = [comb_t_ref[pl.ds(e, 1), sub] for sub in subs]
    n_pass = [(jnp.sum(jnp.where(g_row > 0.0, 1, 0)) + (r - 1)) // r for g_row in gate_rows]
    base = [sum(n_pass[:s], jnp.int32(0)) * r for s in range(len(subs))]
    total_pass = sum(n_pass, jnp.int32(0))

    def passes(count, fn):
        def pair(k, carry):
            fn(k * (2 * r), 2 * r)
            return carry

        lax.fori_loop(0, count // 2, pair, 0)

        @pl.when(count % 2 == 1)
        def _():
            fn((count - 1) * r, r)

    @pl.when(j == 0)
    def _():
        for s, sub in enumerate(subs):
            rank_row = rank_t_ref[pl.ds(e, 1), sub]

            def compact(start, n_rows, s=s, sub=sub, rank_row=rank_row):
                slot = (start + lax.broadcasted_iota(jnp.int32, (n_rows, ts), 0)).astype(F32)
                pick = jnp.where(jnp.logical_and(rank_row == slot, gate_rows[s] > 0.0), 1.0, 0.0).astype(BF16)
                rows = pl.ds(pl.multiple_of(base[s] + start, r), n_rows)
                xc_ref[rows, :] = _dot(pick, xb_ref[sub, :]).astype(BF16)
                yc_ref[rows, :] = jnp.zeros((n_rows, d), F32)

            passes(n_pass[s], compact)

    def expert(start, n_rows):
        rows = pl.ds(pl.multiple_of(start, r), n_rows)
        xk = xc_ref[rows, :]
        h = jax.nn.silu(_dot(xk, wg_ref[0])) * _dot(xk, wu_ref[0])
        yc_ref[rows, :] += _dot(h.astype(BF16), wd_ref[0])

    passes(total_pass, expert)

    @pl.when(j == pl.num_programs(2) - 1)
    def _():
        for s, sub in enumerate(subs):
            rank_col = jnp.sum(jnp.where(lane == e, rank_ref[sub, :], 0.0), axis=-1, keepdims=True)
            gate_col = jnp.sum(jnp.where(lane == e, comb_ref[sub, :], 0.0), axis=-1, keepdims=True)

            def place(start, n_rows, s=s, sub=sub, rank_col=rank_col, gate_col=gate_col):
                slot = (start + lax.broadcasted_iota(jnp.int32, (ts, n_rows), 1)).astype(F32)
                put = jnp.where(jnp.logical_and(rank_col == slot, gate_col > 0.0), 1.0, 0.0).astype(BF16)
                rows = pl.ds(pl.multiple_of(base[s] + start, r), n_rows)
                o_ref[sub, :] += gate_col * _dot(put, yc_ref[rows, :].astype(BF16))

            passes(n_pass[s], place)

    @pl.when(jnp.logical_and(e == pl.num_programs(1) - 1, j == pl.num_programs(2) - 1))
    def _():
        o_ref[...] = _layernorm(DEEPNORM_ALPHA * x_ref[...] + o_ref[...], g_ref[...], b_ref[...])


def _moe(x, comb, wg, wu, wd, g, b):
    m, d = x.shape
    n_e, _, ff = wg.shape
    tm = MOE_TOKENS * MOE_SUBTILES
    if m % tm != 0:
        return _ffn(x, comb, wg, wu, wd, g, b)
    tf = _ff_tile(ff)
    assert MOE_TOKENS % (2 * MOE_ROWS) == 0 and comb.shape[1] == LANES and n_e <= LANES
    cap = tm + MOE_SUBTILES * MOE_ROWS
    once = pl.Buffered(1) if MOE_SUBTILES > 1 else None
    return pl.pallas_call(
        _moe_kernel,
        grid=(m // tm, n_e, ff // tf),
        in_specs=[pl.BlockSpec((tm, d), lambda i, e, j: (i, 0), pipeline_mode=once),
                  pl.BlockSpec((tm, LANES), lambda i, e, j: (i, 0), pipeline_mode=once),
                  pl.BlockSpec((1, d, tf), lambda i, e, j: (e, 0, j)),
                  pl.BlockSpec((1, d, tf), lambda i, e, j: (e, 0, j)),
                  pl.BlockSpec((1, tf, d), lambda i, e, j: (e, j, 0)),
                  pl.BlockSpec((1, d), lambda i, e, j: (0, 0)),
                  pl.BlockSpec((1, d), lambda i, e, j: (0, 0))],
        out_specs=pl.BlockSpec((tm, d), lambda i, e, j: (i, 0), pipeline_mode=once),
        out_shape=jax.ShapeDtypeStruct((m, d), F32),
        scratch_shapes=[pltpu.VMEM((tm, d), BF16), pltpu.VMEM((cap, d), BF16), pltpu.VMEM((cap, d), F32),
                        pltpu.VMEM((tm, LANES), F32), pltpu.VMEM((LANES, tm), F32), pltpu.VMEM((LANES, tm), F32)],
        compiler_params=_cparams(("parallel", "arbitrary", "arbitrary")),
        name="moe_sparse",
    )(x, comb, wg, wu, wd, g.reshape(1, d), b.reshape(1, d))


def _router_kernel(x_ref, r_ref, o_ref):
    logits = _dot_mid(x_ref[...], r_ref[...])
    lane = lax.broadcasted_iota(jnp.int32, logits.shape, 1)
    logits = jnp.where(lane < N_EXPERTS, logits, -jnp.inf)
    m1 = jnp.max(logits, axis=-1, keepdims=True)
    i1 = jnp.min(jnp.where(logits == m1, lane, LANES), axis=-1, keepdims=True)
    rest = jnp.where(lane == i1, -jnp.inf, logits)
    m2 = jnp.max(rest, axis=-1, keepdims=True)
    i2 = jnp.min(jnp.where(rest == m2, lane, LANES), axis=-1, keepdims=True)
    e2 = jnp.exp(m2 - m1)
    den = 1.0 + e2
    o_ref[...] = jnp.where(lane == i1, 1.0 / den, 0.0) + jnp.where(lane == i2, e2 / den, 0.0)


def _router(x, router):
    m, d = x.shape
    tm = _row_tile(m)
    r = jnp.pad(router, ((0, 0), (0, LANES - router.shape[1])))
    return pl.pallas_call(
        _router_kernel,
        grid=(m // tm,),
        in_specs=[pl.BlockSpec((tm, d), lambda i: (i, 0)), pl.BlockSpec((d, LANES), lambda i: (0, 0))],
        out_specs=pl.BlockSpec((tm, LANES), lambda i: (i, 0)),
        out_shape=jax.ShapeDtypeStruct((m, LANES), F32),
        compiler_params=_cparams(("parallel",)),
        name="router",
    )(x, r)


def _topk_lanes(g, idx, axis=-1):
    sel = jnp.zeros(g.shape, F32)
    for _ in range(MOBA_TOPK):
        m = jnp.max(g, axis=axis, keepdims=True)
        first = jnp.min(jnp.where(g == m, idx, LANES), axis=axis, keepdims=True)
        hit = jnp.logical_and(idx == first, m > -jnp.inf)
        sel = jnp.where(hit, 1.0, sel)
        g = jnp.where(hit, -jnp.inf, g)
    return sel


def _moba_kernel(slope_ref, q_ref, k_ref, v_ref, o_ref, kb_ref, vt_ref, km_ref, m_ref, l_ref, acc_ref, sel_ref,
                 *, tq, nb, q_offset, hd):
    i = pl.program_id(2)
    blk = MOBA_BLOCK

    @pl.when(i == 0)
    def _():
        kb_ref[...] = k_ref[...].astype(BF16)
        km_ref[...] = jnp.zeros_like(km_ref)
        for j in range(nb):
            rows = slice(j * blk, (j + 1) * blk)
            vt_ref[j] = v_ref[rows, :].T.astype(BF16)
            km_ref[j:j + 1, :] = jnp.sum(k_ref[rows, :], axis=0, keepdims=True) * (1.0 / blk)

    q0 = q_offset + i * tq
    own = q0 // blk
    q = q_ref[...]
    lane = lax.broadcasted_iota(jnp.int32, (tq, LANES), 1)
    q2 = jnp.concatenate([jnp.where(lane < hd, q, 0.0), jnp.where(lane >= hd, q, 0.0)], axis=0)
    gate_t = _dot_mid(km_ref[...], q2, dot=_dot_nt)
    blk_id = lax.broadcasted_iota(jnp.int32, gate_t.shape, 0)
    sel_ref[...] = _topk_lanes(jnp.where(blk_id < own, gate_t, -jnp.inf), blk_id, axis=0)
    qb = (q2 * hd ** -0.5).astype(BF16)
    c2 = lax.broadcasted_iota(jnp.int32, (blk, 2 * tq), 1)
    d0 = (jnp.where(c2 >= tq, c2 - tq, c2) - lax.broadcasted_iota(jnp.int32, (blk, 2 * tq), 0)).astype(F32)
    slope_lane = slope_ref[...]
    c1 = lax.broadcasted_iota(jnp.int32, (1, 2 * tq), 1)
    slope = jnp.where(c1 < tq, slope_lane[:, 0:1], slope_lane[:, hd:hd + 1])
    slope_d0 = slope * d0
    first = lax.broadcasted_iota(jnp.int32, (2 * hd, tq), 0) < hd
    m_ref[...] = jnp.full(m_ref.shape, NEG, F32)
    l_ref[...] = jnp.zeros_like(l_ref)
    acc_ref[...] = jnp.zeros_like(acc_ref)

    def tiles(js, diagonal):
        scores = []
        for j in js:
            off = (q0 - j * blk).astype(F32)
            s = _dot_nt(kb_ref[pl.ds(pl.multiple_of(j * blk, blk), blk), :], qb) - (slope_d0 + slope * off)
            if diagonal:
                s = jnp.where(d0 + off >= 0.0, s, NEG)
            else:
                s = jnp.where(sel_ref[pl.ds(j, 1), :] > 0.0, s, NEG)
            scores.append(s)
        m_old = m_ref[...]
        m_new = functools.reduce(jnp.maximum, [m_old] + [jnp.max(s, axis=0, keepdims=True) for s in scores])
        alpha = jnp.exp(m_old - m_new)
        l_new = alpha * l_ref[...]
        pv = None
        for j, s in zip(js, scores):
            p = jnp.exp(s - m_new)
            l_new = l_new + jnp.sum(p, axis=0, keepdims=True)
            d = _dot(vt_ref[j], p.astype(BF16))
            pv = d if pv is None else pv + d
        m_ref[...] = m_new
        l_ref[...] = l_new
        acc_ref[...] = (jnp.where(first, alpha[:, :tq], alpha[:, tq:]) * acc_ref[...]
                        + jnp.where(first, pv[:, :tq], pv[:, tq:]))

    tiles([own], True)

    def body(t, carry):
        tiles([own - 1 - 4 * t - n for n in range(4)], False)
        return carry

    lax.fori_loop(0, own // 4, body, 0)
    rest = own % 4

    @pl.when(rest >= 2)
    def _():
        tiles([rest - 1, rest - 2], False)

    @pl.when(rest % 2 == 1)
    def _():
        tiles([own - own], False)
    l = l_ref[...]
    o_ref[...] = (acc_ref[...] / jnp.where(first, l[:, :tq], l[:, tq:])).T


def _alibi_slopes_lanes(n_heads, hd):
    slopes = jnp.exp2(-8.0 * jnp.arange(1, n_heads + 1, dtype=F32) / n_heads)
    return jnp.repeat(slopes, hd).reshape(1, n_heads * hd)


def _attn_call(kernel, q, k, v, *, batch, q_offset, n_heads, extra_in=(), extra_specs=(), extra_scratch=(), name):
    lq, lk = q.shape[0] // batch, k.shape[0] // batch
    hd = q.shape[1] // n_heads
    assert 2 * hd == LANES and lk % MOBA_BLOCK == 0
    tq = min(MOBA_BLOCK, lq)
    assert lq % tq == 0 and MOBA_BLOCK % tq == 0 and q_offset % tq == 0
    nq = lq // tq
    return pl.pallas_call(
        functools.partial(kernel, tq=tq, q_offset=q_offset, hd=hd),
        grid=(batch, n_heads // 2, nq),
        in_specs=list(extra_specs) + [
            pl.BlockSpec((tq, LANES), lambda b, h, i: (b * nq + i, h)),
            pl.BlockSpec((lk, LANES), lambda b, h, i: (b, h)),
            pl.BlockSpec((lk, LANES), lambda b, h, i: (b, h))],
        out_specs=pl.BlockSpec((tq, LANES), lambda b, h, i: (b * nq + i, h)),
        out_shape=jax.ShapeDtypeStruct(q.shape, F32),
        scratch_shapes=list(extra_scratch),
        compiler_params=_cparams(("parallel", "parallel", "arbitrary")),
        name=name,
    )(*extra_in, q, k, v)


def _moba(q, k, v, *, batch, q_offset):
    nb = k.shape[0] // batch // MOBA_BLOCK
    assert nb <= LANES
    slopes = _alibi_slopes_lanes(H_A, q.shape[1] // H_A)
    tq = min(MOBA_BLOCK, q.shape[0] // batch)
    nb_rows = -(-nb // 8) * 8
    return _attn_call(
        functools.partial(_moba_kernel, nb=nb), q, k, v, batch=batch, q_offset=q_offset, n_heads=H_A,
        extra_in=(slopes,), extra_specs=(pl.BlockSpec((1, LANES), lambda b, h, i: (0, h)),),
        extra_scratch=(pltpu.VMEM((nb * MOBA_BLOCK, LANES), BF16), pltpu.VMEM((nb, LANES, MOBA_BLOCK), BF16),
                       pltpu.VMEM((nb_rows, LANES), F32), pltpu.VMEM((1, 2 * tq), F32), pltpu.VMEM((1, 2 * tq), F32),
                       pltpu.VMEM((LANES, tq), F32), pltpu.VMEM((nb_rows, 2 * tq), F32)), name="moba")


EXP_UNDERFLOW = -110.0


def _softplus(z):
    return jnp.maximum(z, 0.0) + jnp.log(1.0 + jnp.exp(-jnp.abs(z)))


def _stick_kernel(q_ref, k_ref, v_ref, o_ref, kb_ref, vb_ref, c_ref, acc_ref, *, tq, q_offset, hd):
    i = pl.program_id(2)
    blk = MOBA_BLOCK

    @pl.when(i == 0)
    def _():
        kb_ref[...] = k_ref[...].astype(BF16)
        vb_ref[...] = v_ref[...].astype(BF16)

    q0 = q_offset + i * tq
    own = q0 // blk
    q = q_ref[...] * hd ** -0.5
    lane = lax.broadcasted_iota(jnp.int32, (tq, LANES), 1)
    qb = jnp.concatenate([jnp.where(lane < hd, q, 0.0), jnp.where(lane >= hd, q, 0.0)], axis=0).astype(BF16)
    r2 = lax.broadcasted_iota(jnp.int32, (2 * tq, blk), 0)
    d0 = jnp.where(r2 >= tq, r2 - tq, r2) - lax.broadcasted_iota(jnp.int32, (2 * tq, blk), 1)
    u = jnp.where(lax.broadcasted_iota(jnp.int32, (blk, blk), 0) > lax.broadcasted_iota(jnp.int32, (blk, blk), 1),
                  1.0, 0.0).astype(BF16)
    c_ref[...] = jnp.zeros_like(c_ref)
    acc_ref[...] = jnp.zeros_like(acc_ref)

    def tile(j, diagonal):
        start = pl.multiple_of(j * blk, blk)
        z = _dot_nt(qb, kb_ref[pl.ds(start, blk), :])
        log_keep = -_softplus(z)
        log_beta = z + log_keep
        if diagonal:
            before = d0 + (q0 - j * blk) > 0
            log_keep = jnp.where(before, log_keep, 0.0)
        hi = log_keep.astype(BF16)
        lo = (log_keep - hi.astype(F32)).astype(BF16)
        later = _dot(jnp.concatenate([hi, lo], axis=0), u)
        later = later[:2 * tq] + later[2 * tq:]
        c = c_ref[...]
        w = jnp.exp(log_beta + later + c)
        if diagonal:
            w = jnp.where(before, w, 0.0)
        c_ref[...] = c + later[:, 0:1] + log_keep[:, 0:1]
        pv = _dot(w.astype(BF16), vb_ref[pl.ds(start, blk), :])
        acc_ref[...] += jnp.where(lane < hd, pv[:tq], pv[tq:])

    tile(own, True)

    def more(carry):
        t, c_max = carry
        return jnp.logical_and(t <= own, c_max > EXP_UNDERFLOW)

    def body(carry):
        t, _ = carry
        tile(own - t, False)
        return t + 1, jnp.max(c_ref[...])

    lax.while_loop(more, body, (jnp.int32(1), jnp.max(c_ref[...])))
    o_ref[...] = acc_ref[...]


def _stick(q, k, v, *, batch, q_offset):
    tq = min(MOBA_BLOCK, q.shape[0] // batch)
    lk = k.shape[0] // batch
    return _attn_call(_stick_kernel, q, k, v, batch=batch, q_offset=q_offset, n_heads=H_D,
                      extra_scratch=(pltpu.VMEM((lk, LANES), BF16), pltpu.VMEM((lk, LANES), BF16),
                                     pltpu.VMEM((2 * tq, 1), F32), pltpu.VMEM((tq, LANES), F32)), name="stick")


CHUNK = 64
SUB = 16
SEQ_BLOCK = 512
EXP_CLAMP = 80.0


def _dot_mid(a, b, dot=_dot):
    a0 = a.astype(BF16)
    a1 = (a - a0.astype(F32)).astype(BF16)
    b0 = b.astype(BF16)
    b1 = (b - b0.astype(F32)).astype(BF16)
    return dot(a0, b0) + (dot(a0, b1) + dot(a1, b0))


def _incl_lower(c):
    r = lax.broadcasted_iota(jnp.int32, (c, c), 0)
    s = lax.broadcasted_iota(jnp.int32, (c, c), 1)
    return r, s


def _hgrn_kernel(q_ref, f_ref, i_ref, g_ref, lb_ref, gn_ref, s0_ref, o_ref, s_ref, st_ref, *, c, n_chunks, n_heads,
                 layer):
    l = pl.program_id(1)
    dk = LANES

    w = n_heads * dk

    @pl.when(l == 0)
    def _():
        for h in range(n_heads):
            st_ref[h * dk:(h + 1) * dk, :] = s0_ref[0, h]

    r, s = _incl_lower(c)
    tri = jnp.where(s <= r, 1.0, 0.0).astype(BF16)
    sub = min(SUB, c)
    n = n_heads * c
    shift_c, shift_s = c.bit_length() - 1, sub.bit_length() - 1
    ri = lax.broadcasted_iota(jnp.int32, (n_heads * sub, n), 0)
    ci = lax.broadcasted_iota(jnp.int32, (n_heads * sub, n), 1)
    pair = jnp.right_shift(ri, shift_s) == jnp.right_shift(ci, shift_c)
    t_loc, s_loc = jnp.bitwise_and(ri, sub - 1), jnp.bitwise_and(ci, c - 1)
    own_cols = (jnp.right_shift(lax.broadcasted_iota(jnp.int32, (n, w), 0), shift_c)
                == lax.broadcasted_iota(jnp.int32, (n, w), 1) // dk)
    gn = gn_ref[...]
    e = jnp.exp(lb_ref[...] - jnp.max(lb_ref[...], axis=0, keepdims=True))
    lb = jnp.sum(e[:layer + 1], axis=0, keepdims=True) / jnp.sum(e, axis=0, keepdims=True)

    def stack(t):
        return jnp.concatenate([t[:, h * dk:(h + 1) * dk] for h in range(n_heads)], axis=0)

    def run(chunks):
        m = len(chunks)
        rows = [pl.ds(i * c, c) if isinstance(i, int) else pl.ds(pl.multiple_of(i * c, c), c) for i in chunks]
        zs = [f_ref[r_, :] for r_ in rows]
        ks = [(1.0 - lb) * jax.nn.sigmoid(-z) for z in zs]
        qs = [jax.nn.silu(q_ref[r_, :]) for r_ in rows]
        cgs = [_dot_exact_lhs(tri, jnp.log(lb + (1.0 - lb) * jax.nn.sigmoid(z))) for z in zs]
        v4s = [stack(i_ref[r_, :]).astype(BF16) for r_ in rows]
        intra = [[] for _ in range(m)]
        for b in range(c // sub):
            lo, hi = b * sub, (b + 1) * sub
            refs = [cg[lo - 1:lo] if b else jnp.zeros((1, w), F32) for cg in cgs]
            q_sub = [stack(q[lo:hi] * jnp.exp(cg[lo:hi] - ref_pt)).astype(BF16) for q, cg, ref_pt in zip(qs, cgs, refs)]
            k_all = [stack(k * jnp.exp(jnp.minimum(ref_pt - cg, EXP_CLAMP))).astype(BF16)
                     for k, cg, ref_pt in zip(ks, cgs, refs)]
            a = [_dot_nt(q_, k_) for q_, k_ in zip(q_sub, k_all)]
            a = [jnp.where(jnp.logical_and(pair, s_loc <= t_loc + lo), a_, 0.0).astype(BF16) for a_ in a]
            for x_, (a_, v4) in enumerate(zip(a, v4s)):
                intra[x_].append(_dot(a_, v4))
        q_dec = [stack(q * jnp.exp(cg)).astype(BF16) for q, cg in zip(qs, cgs)]
        lasts = [cg[c - 1:c] for cg in cgs]
        kts = [stack(k * jnp.exp(last - cg)).astype(BF16) for k, last, cg in zip(ks, lasts, cgs)]
        for x_ in range(m):
            st = st_ref[...]
            from_state = _dot_nt(q_dec[x_], st.astype(BF16))
            v_wide = jnp.where(own_cols, jnp.concatenate([v4s[x_]] * n_heads, axis=1), 0.0)
            e_last = jnp.concatenate([jnp.broadcast_to(jnp.exp(lasts[x_][:, h * dk:(h + 1) * dk]), (dk, dk))
                                      for h in range(n_heads)], axis=0)
            st_ref[...] = st * e_last + _dot_tn(v_wide, kts[x_])
            for h in range(n_heads):
                o = from_state[h * c:(h + 1) * c, h * dk:(h + 1) * dk] + jnp.concatenate(
                    [part[h * sub:(h + 1) * sub] for part in intra[x_]], axis=0)
                o = o * lax.rsqrt(jnp.mean(o * o, axis=-1, keepdims=True) + RMS_EPS) * gn
                o_ref[rows[x_], h * dk:(h + 1) * dk] = o * jax.nn.sigmoid(g_ref[rows[x_], h * dk:(h + 1) * dk])

    per_trip = 2 if n_chunks % 2 == 0 else 1

    def trip(i, carry):
        run([per_trip * i + x_ for x_ in range(per_trip)])
        return carry

    lax.fori_loop(0, n_chunks // per_trip, trip, 0)

    @pl.when(l == pl.num_programs(1) - 1)
    def _():
        for h in range(n_heads):
            s_ref[0, h] = st_ref[h * dk:(h + 1) * dk, :]


def _hgrn(hg, lb, gn, s0, *, batch, layer=0):
    t, width = hg.shape
    seq = t // batch
    w = width // 4
    n_heads = w // LANES
    c = min(CHUNK, seq)
    lblk = min(SEQ_BLOCK, seq)
    assert seq % lblk == 0 and lblk % c == 0 and c % min(SUB, c) == 0
    nl = seq // lblk
    part = lambda p: pl.BlockSpec((lblk, w), lambda b, l: (b * nl + l, p))
    state = pl.BlockSpec((1, n_heads, LANES, LANES), lambda b, l: (b, 0, 0, 0))
    o, st = pl.pallas_call(
        functools.partial(_hgrn_kernel, c=c, n_chunks=lblk // c, n_heads=n_heads, layer=layer),
        grid=(batch, nl),
        in_specs=[part(0), part(1), part(2), part(3),
                  pl.BlockSpec(lb.shape, lambda b, l: (0, 0)), pl.BlockSpec((1, LANES), lambda b, l: (0, 0)), state],
        out_specs=[pl.BlockSpec((lblk, w), lambda b, l: (b * nl + l, 0)), state],
        out_shape=[jax.ShapeDtypeStruct((t, w), F32), jax.ShapeDtypeStruct(s0.shape, F32)],
        scratch_shapes=[pltpu.VMEM((n_heads * LANES, LANES), F32)],
        compiler_params=_cparams(("parallel", "arbitrary")),
        name="hgrn2",
    )(hg, hg, hg, hg, lb, gn.reshape(1, LANES), jnp.swapaxes(s0, -1, -2))
    return o, jnp.swapaxes(st, -1, -2)


def _unit_lower_inverses(lms, n, period):
    bs = min(SUB, period)
    r, s = _incl_lower(n)
    eye = jnp.where(r == s, 1.0, 0.0)
    shift = bs.bit_length() - 1
    same = jnp.right_shift(r, shift) == jnp.right_shift(s, shift)

    def mm(a, b):
        return _dot(a.astype(BF16), b.astype(BF16))

    def neumann(mats, order):
        invs, pws, k = [eye + a for a in mats], list(mats), 2
        while k < order:
            pws = [mm(pw, pw) for pw in pws]
            invs = [inv + mm(inv, pw) for inv, pw in zip(invs, pws)]
            k *= 2
        return invs

    inv_ds = neumann([-jnp.where(same, lm, 0.0) for lm in lms], bs)
    if period == bs:
        return inv_ds
    ms = [mm(inv_d, jnp.where(same, 0.0, lm)) for inv_d, lm in zip(inv_ds, lms)]
    return [mm(a, inv_d) for a, inv_d in zip(neumann([-m for m in ms], period // bs), inv_ds)]


def _gdn_kernel(x_ref, ab_ref, g_ref, cw_ref, al_ref, dtb_ref, gn_ref, s0_ref, cb_ref, o_ref, s_ref, nb_ref,
                xx_ref, y_ref, st_ref, *, c, n_chunks, n_heads, lblk):
    l = pl.program_id(1)
    dk = LANES
    halo = 8
    w = n_heads * dk

    @pl.when(l == 0)
    def _():
        for h in range(n_heads):
            st_ref[:, h * dk:(h + 1) * dk] = s0_ref[0, h]
        xx_ref[halo - (CONV_W - 1):halo, :] = cb_ref[0]

    xx_ref[halo:halo + lblk, :] = x_ref[...]
    y = xx_ref[halo:halo + lblk, :] * cw_ref[CONV_W - 1:CONV_W, :]
    for i in range(CONV_W - 1):
        off = halo - (CONV_W - 1) + i
        y = y + xx_ref[off:off + lblk, :] * cw_ref[i:i + 1, :]
    y_ref[...] = jax.nn.silu(y)
    tail = xx_ref[halo + lblk - (CONV_W - 1):halo + lblk, :]
    xx_ref[halo - (CONV_W - 1):halo, :] = tail

    r, s = _incl_lower(c)
    tri = jnp.where(s <= r, 1.0, 0.0).astype(BF16)
    n = n_heads * c
    rr, ss = _incl_lower(n)
    shift = c.bit_length() - 1
    same_head = jnp.right_shift(rr, shift) == jnp.right_shift(ss, shift)
    incl = jnp.logical_and(same_head, ss <= rr)
    strict = jnp.logical_and(same_head, ss < rr)
    own_cols = (jnp.right_shift(lax.broadcasted_iota(jnp.int32, (n, w), 0), shift)
                == lax.broadcasted_iota(jnp.int32, (n, w), 1) // dk)
    gn = gn_ref[...]

    def l2n(t):
        return t * lax.rsqrt(jnp.sum(t * t, axis=-1, keepdims=True) + RMS_EPS)

    def stack(f):
        return jnp.concatenate([f(h) for h in range(n_heads)], axis=0)

    def own_block(t):
        return stack(lambda h: t[h * c:(h + 1) * c, h * dk:(h + 1) * dk])

    def front(i):
        rows = pl.ds(i * c, c) if isinstance(i, int) else pl.ds(pl.multiple_of(i * c, c), c)
        ab = ab_ref[rows, :]
        pre = ab + dtb_ref[...]
        log_a = -jnp.exp(al_ref[...]) * (jnp.maximum(pre, 0.0) + jnp.log(1.0 + jnp.exp(-jnp.abs(pre))))
        beta_all = jax.nn.sigmoid(ab)
        cg_all = _dot_exact_lhs(tri, log_a)
        cg_t = cg_all.T
        q = stack(lambda h: l2n(y_ref[rows, h * dk:(h + 1) * dk])) * dk ** -0.5
        k = stack(lambda h: l2n(y_ref[rows, w + h * dk:w + (h + 1) * dk]))
        v = stack(lambda h: y_ref[rows, 2 * w + h * dk:2 * w + (h + 1) * dk])
        cg = stack(lambda h: cg_all[:, h:h + 1])
        beta = stack(lambda h: beta_all[:, n_heads + h:n_heads + h + 1])
        cg_row = jnp.concatenate([cg_t[h:h + 1, :c] for h in range(n_heads)], axis=1)
        last = stack(lambda h: jnp.broadcast_to(cg_all[c - 1:c, h:h + 1], (c, 1)))
        decay = jnp.exp(jnp.where(incl, cg - cg_row, NEG))
        kk = _dot_mid(k, k, dot=_dot_nt) * decay * beta
        return jnp.where(strict, kk, 0.0), (rows, q, k, v, cg, beta, last, decay, cg_all)

    def back(side, t_inv):
        rows, q, k, v, cg, beta, last, decay, cg_all = side
        e_cg = jnp.exp(cg)
        sol = _dot_mid(t_inv, jnp.concatenate([v * beta, k * (beta * e_cg)], axis=1))
        qk = (_dot_nt(q.astype(BF16), k.astype(BF16)) * decay).astype(BF16)
        kt = (k * jnp.exp(last - cg)).astype(BF16)
        e_last = jnp.concatenate([jnp.broadcast_to(jnp.exp(cg_all[c - 1:c, h:h + 1]), (1, dk)) for h in range(n_heads)],
                                 axis=1)
        return rows, sol[:, :dk], sol[:, dk:].astype(BF16), (q * e_cg).astype(BF16), qk, kt, e_last

    def advance(prepared):
        rows, sol_v, sol_k, q_decayed, qk, kt, e_last = prepared
        st = st_ref[...]
        stb = st.astype(BF16)
        u = sol_v - own_block(_dot(sol_k, stb))
        o = own_block(_dot(q_decayed, stb)) + _dot(qk, u.astype(BF16))
        u_wide = jnp.where(own_cols, jnp.concatenate([u] * n_heads, axis=1), 0.0)
        st_ref[...] = st * e_last + _dot_tn(kt, u_wide.astype(BF16))
        o = o * lax.rsqrt(jnp.mean(o * o, axis=-1, keepdims=True) + RMS_EPS) * gn
        for h in range(n_heads):
            o_ref[rows, h * dk:(h + 1) * dk] = o[h * c:(h + 1) * c] * jax.nn.silu(g_ref[rows, h * dk:(h + 1) * dk])

    def run(chunks):
        fronts = [front(i) for i in chunks]
        inverses = _unit_lower_inverses([lm for lm, _ in fronts], n, c)
        for prepared in [back(side, t_inv) for (_, side), t_inv in zip(fronts, inverses)]:
            advance(prepared)

    per_trip = 4 if n_chunks % 4 == 0 else 2

    def trip(i, carry):
        run([per_trip * i + m for m in range(per_trip)])
        return carry

    lax.fori_loop(0, n_chunks // per_trip, trip, 0)
    if n_chunks % per_trip:
        run([n_chunks - 1])

    @pl.when(l == pl.num_programs(1) - 1)
    def _():
        for h in range(n_heads):
            s_ref[0, h] = st_ref[:, h * dk:(h + 1) * dk]
        nb_ref[0] = tail


def _gdn(qkv, ab, g, conv_w, a_log, dt_bias, gn, s0, conv_buf, *, batch):
    t, width = qkv.shape
    seq = t // batch
    w = width // 3
    n_heads = w // LANES
    c = min(CHUNK, seq)
    lblk = min(SEQ_BLOCK, seq)
    assert seq % lblk == 0 and lblk % c == 0 and lblk >= CONV_W - 1
    nl = seq // lblk
    pad_lane = lambda vec: jnp.pad(vec, (0, LANES - vec.shape[0])).reshape(1, LANES)
    rows = lambda width_: pl.BlockSpec((lblk, width_), lambda b, l: (b * nl + l, 0))
    const = lambda shape: pl.BlockSpec(shape, lambda b, l: (0,) * len(shape))
    state = pl.BlockSpec((1, n_heads, LANES, LANES), lambda b, l: (b, 0, 0, 0))
    buf = pl.BlockSpec((1, CONV_W - 1, width), lambda b, l: (b, 0, 0))
    return pl.pallas_call(
        functools.partial(_gdn_kernel, c=c, n_chunks=lblk // c, n_heads=n_heads, lblk=lblk),
        grid=(batch, nl),
        in_specs=[rows(width), rows(LANES), rows(w), const((CONV_W, width)), const((1, LANES)), const((1, LANES)),
                  const((1, LANES)), state, buf],
        out_specs=[rows(w), state, buf],
        out_shape=[jax.ShapeDtypeStruct((t, w), F32), jax.ShapeDtypeStruct(s0.shape, F32),
                   jax.ShapeDtypeStruct(conv_buf.shape, F32)],
        scratch_shapes=[pltpu.VMEM((lblk + 8, width), F32), pltpu.VMEM((lblk, width), F32),
                        pltpu.VMEM((LANES, w), F32)],
        compiler_params=_cparams(("parallel", "arbitrary")),
        name="gated_deltanet",
    )(qkv, ab, g, conv_w, pad_lane(a_log), pad_lane(dt_bias), gn.reshape(1, LANES), s0, conv_buf)


PAGES_PER_STEP = 8


def _head_fold(pv, n_heads, nq, hd):
    return jnp.concatenate([pv[h * nq:(h + 1) * nq, h * hd:(h + 1) * hd] for h in range(n_heads)], axis=0)


def _row_ids(rows, nq):
    r = lax.broadcasted_iota(jnp.int32, (rows, 1), 0)
    return r // nq, r % nq


def _page_group(refs, first, count):
    return jnp.concatenate([refs[first + r][0, 0].reshape(-1, refs[first + r].shape[-1]) for r in range(count)], axis=1)


def _stick_paged_kernel(pt_ref, qbd_ref, kn_ref, vn_ref, k_hbm, v_hbm, o_ref, kbuf, vbuf, sem, *, n_heads, nq, hd,
                        n_pages):
    b = pl.program_id(0)
    rows = n_heads * nq
    page = kbuf.shape[-1]
    ppg = kbuf.shape[1]
    n_groups = n_pages // ppg
    _, row_q = _row_ids(rows, nq)
    qbd = (qbd_ref[0] * hd ** -0.5).astype(BF16)

    def copies(g, slot):
        out = []
        for r in range(ppg):
            pg = pt_ref[b, g * ppg + r]
            out.append(pltpu.make_async_copy(k_hbm.at[0, pg], kbuf.at[slot, r], sem.at[0, slot, r]))
            out.append(pltpu.make_async_copy(v_hbm.at[0, pg], vbuf.at[slot, r], sem.at[1, slot, r]))
        return out

    def group(buf, slot):
        return jnp.concatenate([buf[slot, r].reshape(n_heads * hd, page) for r in range(ppg)], axis=1).astype(BF16)

    def strict_later(n):
        return jnp.where(lax.broadcasted_iota(jnp.int32, (n, n), 0) > lax.broadcasted_iota(jnp.int32, (n, n), 1),
                         1.0, 0.0).astype(BF16)

    def weights(z, mask, carry, u):
        log_keep = -_softplus(z)
        log_beta = z + log_keep
        if mask is not None:
            log_keep = jnp.where(mask, log_keep, 0.0)
        hi = log_keep.astype(BF16)
        lo = (log_keep - hi.astype(F32)).astype(BF16)
        later = _dot(hi, u) + _dot(lo, u)
        w = jnp.exp(log_beta + later + carry)
        if mask is not None:
            w = jnp.where(mask, w, 0.0)
        return w.astype(BF16), carry + jnp.sum(log_keep, axis=-1, keepdims=True)

    for cp in copies(n_groups - 1, 0):
        cp.start()

    n = kn_ref.shape[1]
    z = _dot_nt(qbd, kn_ref[0].astype(BF16))
    col = lax.broadcasted_iota(jnp.int32, (rows, n), 1)
    w, c = weights(z, col < row_q, jnp.zeros((rows, 1), F32), strict_later(n))
    acc = _dot(w, vn_ref[0].astype(BF16))
    u = strict_later(ppg * page)

    def more(carry):
        g, c, _ = carry
        return jnp.logical_and(g >= 0, jnp.max(c) > EXP_UNDERFLOW)

    def body(carry):
        g, c, acc = carry
        slot = (n_groups - 1 - g) % 2

        @pl.when(g > 0)
        def _():
            for cp in copies(g - 1, 1 - slot):
                cp.start()

        for cp in copies(g, slot):
            cp.wait()
        w, c = weights(_dot(qbd, group(kbuf, slot)), None, c, u)
        return g - 1, c, acc + _dot_nt(w, group(vbuf, slot))

    g, _, acc = lax.while_loop(more, body, (jnp.int32(n_groups - 1), c, acc))

    @pl.when(g >= 0)
    def _():
        for cp in copies(g, (n_groups - 1 - g) % 2):
            cp.wait()

    o_ref[0] = _head_fold(acc, n_heads, nq, hd)


def _moba_paged_kernel(pt_ref, qbd_ref, kn_ref, vn_ref, *rest, pps, n_heads, nq, hd, past_len):
    k_refs, v_refs = rest[:pps], rest[pps:2 * pps]
    o_ref, mo_ref, lo_ref, acco_ref, m_ref, l_ref, acc_ref, km_ref = rest[2 * pps:]
    s = pl.program_id(1)
    rows = n_heads * nq
    page = k_refs[0].shape[-1]
    ppb = MOBA_BLOCK // page
    row_h, row_q = _row_ids(rows, nq)
    slope = jnp.exp2(-8.0 * (row_h + 1).astype(F32) / n_heads)
    lane = lax.broadcasted_iota(jnp.int32, (rows, LANES), 1)
    q_f32 = qbd_ref[0]
    qbd = (q_f32 * hd ** -0.5).astype(BF16)

    @pl.when(s == 0)
    def _():
        n = kn_ref.shape[1]
        col = lax.broadcasted_iota(jnp.int32, (rows, n), 1)
        sc = _dot_nt(qbd, kn_ref[0].astype(BF16)) - slope * (row_q - col).astype(F32)
        sc = jnp.where(col <= row_q, sc, NEG)
        m = jnp.max(sc, axis=-1, keepdims=True)
        p = jnp.exp(sc - m)
        mo_ref[...] = m
        lo_ref[...] = jnp.sum(p, axis=-1, keepdims=True)
        acco_ref[...] = _head_fold(_dot(p.astype(BF16), vn_ref[0].astype(BF16)), n_heads, nq, hd)
        m_ref[...] = jnp.full(m_ref.shape, NEG, F32)
        l_ref[...] = jnp.zeros_like(l_ref)
        km_ref[...] = jnp.zeros_like(km_ref)

    @pl.when(s > 0)
    def _():
        key = lax.broadcasted_iota(jnp.int32, (rows, MOBA_BLOCK), 1)
        m_all, l_all, km = m_ref[...], l_ref[...], km_ref[...]
        lane_k = lax.broadcasted_iota(jnp.int32, km.shape, 1)
        blocks = [(s - 1) * (pps // ppb) + bi for bi in range(pps // ppb)]
        kfs = [_page_group(k_refs, bi * ppb, ppb) for bi in range(len(blocks))]
        raw = [_dot(qbd, kf.astype(BF16)) for kf in kfs]
        means = [jnp.sum(kf, axis=-1, keepdims=True) * (1.0 / MOBA_BLOCK) for kf in kfs]
        scs = [sc - slope * (past_len + row_q - (blk * MOBA_BLOCK + key)).astype(F32) for sc, blk in zip(raw, blocks)]
        m_bs = [jnp.max(sc, axis=-1, keepdims=True) for sc in scs]
        ps = [jnp.exp(sc - m_b) for sc, m_b in zip(scs, m_bs)]
        pvs = [_dot_nt(p.astype(BF16), _page_group(v_refs, bi * ppb, ppb).astype(BF16)) for bi, p in enumerate(ps)]
        for blk, mean, m_b, p, pv in zip(blocks, means, m_bs, ps, pvs):
            km = jnp.where(lane_k == blk, mean, km)
            acc_ref[blk] = _head_fold(pv, n_heads, nq, hd)
            m_all = jnp.where(lane == blk, m_b, m_all)
            l_all = jnp.where(lane == blk, jnp.sum(p, axis=-1, keepdims=True), l_all)
        m_ref[...] = m_all
        l_ref[...] = l_all
        km_ref[...] = km

    @pl.when(s == pl.num_programs(1) - 1)
    def _():
        n_blocks = past_len // MOBA_BLOCK
        gate = _dot_mid(q_f32, km_ref[...])
        sel = _topk_lanes(jnp.where(lane < n_blocks, gate, -jnp.inf), lane) > 0.0
        m_all, l_all = m_ref[...], l_ref[...]
        m_o = mo_ref[...]
        m_tot = jnp.maximum(m_o, jnp.max(jnp.where(sel, m_all, NEG), axis=-1, keepdims=True))
        wgt = jnp.where(sel, jnp.exp(m_all - m_tot), 0.0)
        a_o = jnp.exp(m_o - m_tot)
        l_tot = lo_ref[...] * a_o + jnp.sum(wgt * l_all, axis=-1, keepdims=True)
        acc = acco_ref[...] * a_o
        for b in range(n_blocks):
            acc = acc + wgt[:, b:b + 1] * acc_ref[b]
        o_ref[0] = acc / l_tot


def _paged_attention(kind, q, k_new, v_new, k_pool, v_pool, page_table, *, n_heads):
    batch, n_pages = page_table.shape
    page, hd = k_pool.shape[2], k_pool.shape[4]
    nq = q.shape[0] // batch
    rows = n_heads * nq
    pps = PAGES_PER_STEP
    past_len = n_pages * page
    assert n_pages % pps == 0 and past_len % MOBA_BLOCK == 0 and MOBA_BLOCK % page == 0 and pps % (MOBA_BLOCK // page) == 0
    assert nq <= page and past_len // MOBA_BLOCK <= LANES and (2 * LANES) % n_heads == 0
    n_groups = n_pages // pps
    q4 = q.reshape(batch, nq, n_heads, hd).transpose(0, 2, 1, 3)
    q_bd = (q4[:, :, :, None, :] * jnp.eye(n_heads, dtype=F32)[None, :, None, :, None]).reshape(batch, rows, n_heads * hd)
    pad_new = lambda t: jnp.pad(t.reshape(batch, nq, n_heads * hd), ((0, 0), (0, page - nq), (0, 0)))
    k_t, v_t = (jnp.transpose(t, (0, 1, 3, 4, 2)) for t in (k_pool, v_pool))

    if kind == "stick":
        ppg = MOBA_BLOCK // page
        seq = lambda shape: pl.BlockSpec((1,) + shape, lambda b, pt: (b, 0, 0))
        buf = pltpu.VMEM((2, ppg, n_heads, hd, page), F32)
        out = pl.pallas_call(
            functools.partial(_stick_paged_kernel, n_heads=n_heads, nq=nq, hd=hd, n_pages=n_pages),
            grid_spec=pltpu.PrefetchScalarGridSpec(
                num_scalar_prefetch=1,
                grid=(batch,),
                in_specs=[seq((rows, n_heads * hd)), seq((page, n_heads * hd)), seq((page, n_heads * hd)),
                          pl.BlockSpec(memory_space=pl.ANY), pl.BlockSpec(memory_space=pl.ANY)],
                out_specs=seq((rows, hd)),
                scratch_shapes=[buf, buf, pltpu.SemaphoreType.DMA((2, 2, ppg))]),
            out_shape=jax.ShapeDtypeStruct((batch, rows, hd), F32),
            compiler_params=_cparams(("arbitrary",)),
            name="stick_paged",
        )(page_table, q_bd, pad_new(k_new), pad_new(v_new), k_t, v_t)
    else:
        group = lambda s: jnp.maximum(s, 1) - 1
        scratch = [pltpu.VMEM((rows, 1), F32), pltpu.VMEM((rows, 1), F32), pltpu.VMEM((rows, hd), F32),
                   pltpu.VMEM((rows, LANES), F32), pltpu.VMEM((rows, LANES), F32),
                   pltpu.VMEM((past_len // MOBA_BLOCK, rows, hd), F32), pltpu.VMEM((n_heads * hd, LANES), F32)]

        def page_spec(r):
            return pl.BlockSpec((1, 1, n_heads, hd, page), lambda b, s, pt: (0, pt[b, group(s) * pps + r], 0, 0, 0))

        per_seq = lambda shape: pl.BlockSpec((1,) + shape, lambda b, s, pt: (b, 0, 0))
        out = pl.pallas_call(
            functools.partial(_moba_paged_kernel, pps=pps, n_heads=n_heads, nq=nq, hd=hd, past_len=past_len),
            grid_spec=pltpu.PrefetchScalarGridSpec(
                num_scalar_prefetch=1,
                grid=(batch, n_groups + 1),
                in_specs=[per_seq((rows, n_heads * hd)), per_seq((page, n_heads * hd)),
                          per_seq((page, n_heads * hd))] + [page_spec(r) for r in range(pps)] * 2,
                out_specs=per_seq((rows, hd)),
                scratch_shapes=scratch),
            out_shape=jax.ShapeDtypeStruct((batch, rows, hd), F32),
            compiler_params=_cparams(("parallel", "arbitrary")),
            name="moba_paged",
        )(page_table, q_bd, pad_new(k_new), pad_new(v_new), *([k_t] * pps), *([v_t] * pps))
    return out.reshape(batch, n_heads, nq, hd).transpose(0, 2, 1, 3).reshape(batch * nq, n_heads * hd)


def _layer_stack(x, batch, past, states, w):
    s_hgrn, s_gdn, conv_buf = states

    def attend(kind, q, k, v, pools, n_heads):
        if past is None:
            return (_moba if kind == "moba" else _stick)(q, k, v, batch=batch, q_offset=0)
        return _paged_attention(kind, q, k, v, *pools, past[4], n_heads=n_heads)

    (q_a, k_a, v_a, hg), kv_a_t = _project(x, w["in_a"] + [w["in_hgrn"]], batch, transposed=(1, 2))
    o_a = attend("moba", q_a, k_a, v_a, past and past[0:2], H_A)
    o_b, s_hgrn_new = _hgrn(hg, w["hgrn_lb"], w["hgrn_norm"], s_hgrn, batch=batch, layer=0)
    x = _matmul([o_a, o_b], w["out_even"], ln_args=(x, w["ln1_g"][0], w["ln1_b"][0]))
    x = _ffn(x, None, w["ffn_wg"], w["ffn_wu"], w["ffn_wd"], w["ln2_g"][0], w["ln2_b"][0])

    (qkv_c, ab, g_c, q_d, k_d, v_d), kv_d_t = _project(x, [w["in_qkv_c"], w["in_ab"], w["in_g_c"]] + w["in_d"], batch,
                                                       transposed=(4, 5))
    o_c, s_gdn_new, conv_new = _gdn(qkv_c, ab, g_c, w["conv_w"], w["a_log"], w["dt_bias"], w["gdn_norm"],
                                    s_gdn, conv_buf, batch=batch)
    o_d = attend("stick", q_d, k_d, v_d, past and past[2:4], H_D)
    x = _matmul([o_c, o_d], w["out_odd"], ln_args=(x, w["ln1_g"][1], w["ln1_b"][1]))
    comb = _router(x, w["router"])
    x = _moe(x, comb, w["moe_wg"], w["moe_wu"], w["moe_wd"], w["ln2_g"][1], w["ln2_b"][1])
    k_a, v_a = kv_a_t or (k_a, v_a)
    k_d, v_d = kv_d_t or (k_d, v_d)
    return x, k_a, v_a, s_hgrn_new, s_gdn_new, conv_new, k_d, v_d


def kernel(x_prompt, x_sample, cache_k_moba, cache_v_moba, state_hgrn, state_gdn, state_gdn_conv, cache_k_sb,
           cache_v_sb, page_table, w_in_even, w_out_even, hgrn_lb, hgrn_norm, w_in_odd, w_out_odd, gdn_conv_w,
           gdn_a_log, gdn_dt_bias, gdn_norm, ln1_g, ln1_b, ln2_g, ln2_b, ffn_wg, ffn_wu, ffn_wd, router, moe_wg,
           moe_wu, moe_wd):
    assert w_in_even.shape[0] == 1 and w_in_odd.shape[0] == 1
    bp, lp, d = x_prompt.shape
    bs, ls, _ = x_sample.shape
    mix = d // 2
    hd_a, hd_d = mix // H_A, mix // H_D
    conv_dim = gdn_conv_w.shape[-1]
    bf = lambda t: t.astype(BF16)
    cols = lambda wt, lo, n: bf(wt[:, lo:lo + n])

    wie, wio = w_in_even[0], w_in_odd[0]
    g_lo = conv_dim + 2 * H_C
    d_lo = g_lo + mix
    w = {
        "in_a": [cols(wie, i * mix, mix) for i in range(3)],
        "in_hgrn": cols(wie, 3 * mix, 4 * mix),
        "out_even": [bf(w_out_even[0][:mix]), bf(w_out_even[0][mix:])],
        "hgrn_lb": hgrn_lb, "hgrn_norm": hgrn_norm[0],
        "in_qkv_c": cols(wio, 0, conv_dim),
        "in_ab": bf(jnp.pad(wio[:, conv_dim:g_lo], ((0, 0), (0, LANES - 2 * H_C)))),
        "in_g_c": cols(wio, g_lo, mix),
        "in_d": [cols(wio, d_lo + i * mix, mix) for i in range(3)],
        "out_odd": [bf(w_out_odd[0][:mix]), bf(w_out_odd[0][mix:])],
        "conv_w": gdn_conv_w[0], "a_log": gdn_a_log[0], "dt_bias": gdn_dt_bias[0], "gdn_norm": gdn_norm[0],
        "ln1_g": ln1_g, "ln1_b": ln1_b, "ln2_g": ln2_g, "ln2_b": ln2_b,
        "ffn_wg": bf(ffn_wg), "ffn_wu": bf(ffn_wu), "ffn_wd": bf(ffn_wd),
        "router": router[0], "moe_wg": bf(moe_wg[0]), "moe_wu": bf(moe_wu[0]), "moe_wd": bf(moe_wd[0]),
    }

    def run(x, batch, past, states):
        seq = x.shape[1]
        y, k_a, v_a, s_h, s_g, cv, k_d, v_d = _layer_stack(x.reshape(batch * seq, d), batch, past, states, w)
        def cache(t, n_heads, hd):
            if t.ndim == 3:
                return t.reshape(batch, n_heads, hd, seq).transpose(0, 3, 1, 2)[None]
            return t.reshape(1, batch, seq, n_heads, hd)

        return (y.reshape(batch, seq, d), cache(k_a, H_A, hd_a), cache(v_a, H_A, hd_a), s_h[None], s_g[None], cv[None],
                cache(k_d, H_D, hd_d), cache(v_d, H_D, hd_d))

    past = (cache_k_moba, cache_v_moba, cache_k_sb, cache_v_sb, page_table)
    out_s = run(x_sample, bs, past, (state_hgrn[0], state_gdn[0], state_gdn_conv[0]))
    zeros_p = (jnp.zeros((bp,) + state_hgrn.shape[2:], F32), jnp.zeros((bp,) + state_gdn.shape[2:], F32),
               jnp.zeros((bp,) + state_gdn_conv.shape[2:], F32))
    out_p = run(x_prompt, bp, None, zeros_p)
    return (out_p[0], out_s[0]) + out_p[1:] + out_s[1:]
```

```python
import functools

import jax
import jax.numpy as jnp
from jax import lax
from jax.experimental import pallas as pl
from jax.experimental.pallas import tpu as pltpu

F32 = jnp.float32
BF16 = jnp.bfloat16

H_A, H_B, H_C, H_D = 8, 4, 4, 8
MOBA_BLOCK = 256
MOBA_TOPK = 3
CONV_W = 4
N_EXPERTS = 8
DEPTH = 2
DEEPNORM_ALPHA = (2 * DEPTH) ** 0.25
LN_EPS = 1e-5
RMS_EPS = 1e-6
NEG = -1e30
LANES = 128
VMEM_LIMIT = 56 * 1024 * 1024


def _cparams(sem):
    return pltpu.CompilerParams(dimension_semantics=sem, vmem_limit_bytes=VMEM_LIMIT)


def _layernorm(y, g, b):
    mu = jnp.mean(y, axis=-1, keepdims=True)
    yc = y - mu
    var = jnp.mean(yc * yc, axis=-1, keepdims=True)
    return yc * lax.rsqrt(var + LN_EPS) * g + b


def _split3(x):
    hi = x.astype(BF16)
    r = x - hi.astype(F32)
    mid = r.astype(BF16)
    lo = (r - mid.astype(F32)).astype(BF16)
    return hi, mid, lo


def _dot(a, b):
    return jnp.dot(a, b, preferred_element_type=F32)


def _dot_nt(a, b):
    return lax.dot_general(a, b, (((1,), (1,)), ((), ())), preferred_element_type=F32)


def _dot_tn(a, b):
    return lax.dot_general(a, b, (((0,), (0,)), ((), ())), preferred_element_type=F32)


def _dot_exact_lhs(a_bf16, b):
    b0, b1, b2 = _split3(b)
    return _dot(a_bf16, b0) + _dot(a_bf16, b1) + _dot(a_bf16, b2)


def _mm_kernel(*refs, n_in, ln):
    a_refs, w_refs, rest = refs[:n_in], refs[n_in:2 * n_in], refs[2 * n_in:]
    acc = None
    for a, w in zip(a_refs, w_refs):
        d = _dot(a[...].astype(BF16), w[...])
        acc = d if acc is None else acc + d
    if ln:
        res_ref, g_ref, b_ref, o_ref = rest
        o_ref[...] = _layernorm(DEEPNORM_ALPHA * res_ref[...] + acc, g_ref[...], b_ref[...])
    else:
        (o_ref,) = rest
        o_ref[...] = acc


def _row_tile(m, largest=512):
    for t in (1024, 512, 256, 128, 64, 32, 16, 8):
        if t <= largest and m % t == 0:
            return t
    raise ValueError(m)


def _proj_kernel(x_ref, *refs, n_w, n_t):
    w_refs, wt_refs = refs[:n_w], refs[n_w:n_w + n_t]
    o_refs, ot_refs = refs[n_w + n_t:2 * n_w + n_t], refs[2 * n_w + n_t:]
    xb = x_ref[...].astype(BF16)
    for w_ref, o_ref in zip(w_refs, o_refs):
        o_ref[...] = _dot(xb, w_ref[...])
    for wt_ref, ot_ref in zip(wt_refs, ot_refs):
        ot_ref[0] = _dot_nt(wt_ref[...], xb)


def _project(x, w_list, batch, transposed=()):
    m, d = x.shape
    tm = _row_tile(m)
    seq = m // batch
    wt_list = [w_list[t].T for t in transposed] if seq % tm == 0 else []
    nl = max(seq // tm, 1)
    const = lambda w: pl.BlockSpec(w.shape, lambda i: (0, 0))
    outs = pl.pallas_call(
        functools.partial(_proj_kernel, n_w=len(w_list), n_t=len(wt_list)),
        grid=(m // tm,),
        in_specs=[pl.BlockSpec((tm, d), lambda i: (i, 0))] + [const(w) for w in w_list] + [const(w) for w in wt_list],
        out_specs=[pl.BlockSpec((tm, w.shape[1]), lambda i: (i, 0)) for w in w_list]
        + [pl.BlockSpec((1, w.shape[0], tm), lambda i: (i // nl, 0, i % nl)) for w in wt_list],
        out_shape=[jax.ShapeDtypeStruct((m, w.shape[1]), F32) for w in w_list]
        + [jax.ShapeDtypeStruct((batch, w.shape[0], seq), F32) for w in wt_list],
        compiler_params=_cparams(("parallel",)),
        name="in_proj",
    )(x, *w_list, *wt_list)
    return outs[:len(w_list)], outs[len(w_list):]


def _matmul(a_list, w_list, *, tn=None, ln_args=None):
    m = a_list[0].shape[0]
    n = w_list[0].shape[1]
    tm = _row_tile(m, largest=1024)
    ln = ln_args is not None
    tn = n if (ln or tn is None) else tn
    assert n % tn == 0
    in_specs = [pl.BlockSpec((tm, a.shape[1]), lambda i, j: (i, 0)) for a in a_list]
    in_specs += [pl.BlockSpec((w.shape[0], tn), lambda i, j: (0, j)) for w in w_list]
    args = list(a_list) + list(w_list)
    if ln:
        res, g, b = ln_args
        in_specs += [pl.BlockSpec((tm, n), lambda i, j: (i, 0)),
                     pl.BlockSpec((1, n), lambda i, j: (0, 0)),
                     pl.BlockSpec((1, n), lambda i, j: (0, 0))]
        args += [res, g.reshape(1, n), b.reshape(1, n)]
    return pl.pallas_call(
        functools.partial(_mm_kernel, n_in=len(a_list), ln=ln),
        grid=(m // tm, n // tn),
        in_specs=in_specs,
        out_specs=pl.BlockSpec((tm, tn), lambda i, j: (i, j)),
        out_shape=jax.ShapeDtypeStruct((m, n), F32),
        compiler_params=_cparams(("parallel", "arbitrary")),
        name="matmul_ln" if ln else "matmul",
    )(*args)


def _ffn_kernel(*refs, use_comb):
    if use_comb:
        x_ref, comb_ref, wg_ref, wu_ref, wd_ref, g_ref, b_ref, o_ref, xb_ref, acc_ref = refs
    else:
        x_ref, wg_ref, wu_ref, wd_ref, g_ref, b_ref, o_ref, xb_ref, acc_ref = refs
    e, j = pl.program_id(1), pl.program_id(2)
    first = jnp.logical_and(e == 0, j == 0)
    last = jnp.logical_and(e == pl.num_programs(1) - 1, j == pl.num_programs(2) - 1)

    @pl.when(first)
    def _():
        xb_ref[...] = x_ref[...].astype(BF16)
        acc_ref[...] = jnp.zeros_like(acc_ref)

    xb = xb_ref[...]
    h = jax.nn.silu(_dot(xb, wg_ref[0, 0])) * _dot(xb, wu_ref[0, 0])
    if use_comb:
        lane = lax.broadcasted_iota(jnp.int32, comb_ref.shape, 1)
        h = h * jnp.sum(jnp.where(lane == e, comb_ref[...], 0.0), axis=-1, keepdims=True)
    acc_ref[...] += _dot(h.astype(BF16), wd_ref[0])

    @pl.when(last)
    def _():
        o_ref[...] = _layernorm(DEEPNORM_ALPHA * x_ref[...] + acc_ref[...], g_ref[...], b_ref[...])


def _ff_tile(ff, largest=896):
    for t in (896, 512, 256, LANES):
        if t <= largest and ff % t == 0:
            return t
    raise ValueError(ff)


def _tile_columns(w, tf):
    n_e, d, ff = w.shape
    return w.reshape(n_e, d, ff // tf, tf).transpose(0, 2, 1, 3)


def _ffn(x, comb, wg, wu, wd, g, b):
    m, d = x.shape
    n_e, n_j, _, tf = wg.shape
    ff = n_j * tf
    tm = _row_tile(m, largest=1024)
    use_comb = comb is not None
    in_specs = [pl.BlockSpec((tm, d), lambda i, e, j: (i, 0))]
    args = [x]
    if use_comb:
        in_specs.append(pl.BlockSpec((tm, comb.shape[1]), lambda i, e, j: (i, 0)))
        args.append(comb)
    in_specs += [pl.BlockSpec((1, 1, d, tf), lambda i, e, j: (e, j, 0, 0)),
                 pl.BlockSpec((1, 1, d, tf), lambda i, e, j: (e, j, 0, 0)),
                 pl.BlockSpec((1, tf, d), lambda i, e, j: (e, j, 0)),
                 pl.BlockSpec((1, d), lambda i, e, j: (0, 0)),
                 pl.BlockSpec((1, d), lambda i, e, j: (0, 0))]
    args += [wg, wu, wd, g.reshape(1, d), b.reshape(1, d)]
    return pl.pallas_call(
        functools.partial(_ffn_kernel, use_comb=use_comb),
        grid=(m // tm, n_e, ff // tf),
        in_specs=in_specs,
        out_specs=pl.BlockSpec((tm, d), lambda i, e, j: (i, 0)),
        out_shape=jax.ShapeDtypeStruct((m, d), F32),
        scratch_shapes=[pltpu.VMEM((tm, d), BF16), pltpu.VMEM((tm, d), F32)],
        compiler_params=_cparams(("parallel", "arbitrary", "arbitrary")),
        name="moe_ffn" if use_comb else "ffn",
    )(*args)


MOE_TOKENS = 1024
MOE_SUBTILES = 1
MOE_ROWS = 64


def _moe_kernel(x_ref, comb_ref, wg_ref, wu_ref, wd_ref, g_ref, b_ref, o_ref,
                xb_ref, xc_ref, yc_ref, rank_ref, rank_t_ref, comb_t_ref):
    e, j = pl.program_id(1), pl.program_id(2)
    tm, d = x_ref.shape
    r, ts = MOE_ROWS, MOE_TOKENS
    subs = [slice(s * ts, (s + 1) * ts) for s in range(tm // ts)]
    lane = lax.broadcasted_iota(jnp.int32, (ts, LANES), 1)

    @pl.when(jnp.logical_and(e == 0, j == 0))
    def _():
        xb_ref[...] = x_ref[...].astype(BF16)
        o_ref[...] = jnp.zeros_like(o_ref)
        earlier = jnp.where(lax.broadcasted_iota(jnp.int32, (ts, ts), 1) < lax.broadcasted_iota(jnp.int32, (ts, ts), 0),
                            1.0, 0.0).astype(BF16)
        for sub in subs:
            routed = jnp.where(comb_ref[sub, :] > 0.0, 1.0, 0.0).astype(BF16)
            rank = _dot(earlier, routed)
            rank_ref[sub, :] = rank
            rank_t_ref[:, sub] = rank.T
            comb_t_ref[:, sub] = comb_ref[sub, :].T

    gate_rows = [comb_t_ref[pl.ds(e, 1), sub] for sub in subs]
    n_pass = [(jnp.sum(jnp.where(g_row > 0.0, 1, 0)) + (r - 1)) // r for g_row in gate_rows]
    base = [sum(n_pass[:s], jnp.int32(0)) * r for s in range(len(subs))]
    total_pass = sum(n_pass, jnp.int32(0))

    def passes(count, fn):
        def quad(k, carry):
            fn(k * (4 * r), 4 * r)
            return carry

        lax.fori_loop(0, count // 4, quad, 0)

        @pl.when(count % 4 >= 2)
        def _():
            fn((count // 4) * (4 * r), 2 * r)

        @pl.when(count % 2 == 1)
        def _():
            fn((count - 1) * r, r)

    @pl.when(j == 0)
    def _():
        for s, sub in enumerate(subs):
            rank_row = rank_t_ref[pl.ds(e, 1), sub]

            def compact(start, n_rows, s=s, sub=sub, rank_row=rank_row):
                slot = (start + lax.broadcasted_iota(jnp.int32, (n_rows, ts), 0)).astype(F32)
                pick = jnp.where(jnp.logical_and(rank_row == slot, gate_rows[s] > 0.0), 1.0, 0.0).astype(BF16)
                rows = pl.ds(pl.multiple_of(base[s] + start, r), n_rows)
                xc_ref[rows, :] = _dot(pick, xb_ref[sub, :]).astype(BF16)
                yc_ref[rows, :] = jnp.zeros((n_rows, d), F32)

            passes(n_pass[s], compact)

    def expert(start, n_rows):
        rows = pl.ds(pl.multiple_of(start, r), n_rows)
        xk = xc_ref[rows, :]
        h = jax.nn.silu(_dot(xk, wg_ref[0, 0])) * _dot(xk, wu_ref[0, 0])
        yc_ref[rows, :] += _dot(h.astype(BF16), wd_ref[0])

    passes(total_pass, expert)

    @pl.when(j == pl.num_programs(2) - 1)
    def _():
        for s, sub in enumerate(subs):
            rank_col = jnp.sum(jnp.where(lane == e, rank_ref[sub, :], 0.0), axis=-1, keepdims=True)
            gate_col = jnp.sum(jnp.where(lane == e, comb_ref[sub, :], 0.0), axis=-1, keepdims=True)

            def place(start, n_rows, s=s, sub=sub, rank_col=rank_col, gate_col=gate_col):
                slot = (start + lax.broadcasted_iota(jnp.int32, (ts, n_rows), 1)).astype(F32)
                put = jnp.where(jnp.logical_and(rank_col == slot, gate_col > 0.0), 1.0, 0.0).astype(BF16)
                rows = pl.ds(pl.multiple_of(base[s] + start, r), n_rows)
                o_ref[sub, :] += gate_col * _dot(put, yc_ref[rows, :].astype(BF16))

            passes(n_pass[s], place)

    @pl.when(jnp.logical_and(e == pl.num_programs(1) - 1, j == pl.num_programs(2) - 1))
    def _():
        o_ref[...] = _layernorm(DEEPNORM_ALPHA * x_ref[...] + o_ref[...], g_ref[...], b_ref[...])


def _moe(x, comb, wg, wu, wd, g, b):
    m, d = x.shape
    n_e, n_j, _, tf = wg.shape
    tm = MOE_TOKENS * MOE_SUBTILES
    if m % tm != 0:
        return _ffn(x, comb, wg, wu, wd, g, b)
    assert MOE_TOKENS % (4 * MOE_ROWS) == 0 and comb.shape[1] == LANES and n_e <= LANES
    cap = tm + MOE_SUBTILES * MOE_ROWS
    once = pl.Buffered(1) if MOE_SUBTILES > 1 else None
    return pl.pallas_call(
        _moe_kernel,
        grid=(m // tm, n_e, n_j),
        in_specs=[pl.BlockSpec((tm, d), lambda i, e, j: (i, 0), pipeline_mode=once),
                  pl.BlockSpec((tm, LANES), lambda i, e, j: (i, 0), pipeline_mode=once),
                  pl.BlockSpec((1, 1, d, tf), lambda i, e, j: (e, j, 0, 0)),
                  pl.BlockSpec((1, 1, d, tf), lambda i, e, j: (e, j, 0, 0)),
                  pl.BlockSpec((1, tf, d), lambda i, e, j: (e, j, 0)),
                  pl.BlockSpec((1, d), lambda i, e, j: (0, 0)),
                  pl.BlockSpec((1, d), lambda i, e, j: (0, 0))],
        out_specs=pl.BlockSpec((tm, d), lambda i, e, j: (i, 0), pipeline_mode=once),
        out_shape=jax.ShapeDtypeStruct((m, d), F32),
        scratch_shapes=[pltpu.VMEM((tm, d), BF16), pltpu.VMEM((cap, d), BF16), pltpu.VMEM((cap, d), F32),
                        pltpu.VMEM((tm, LANES), F32), pltpu.VMEM((LANES, tm), F32), pltpu.VMEM((LANES, tm), F32)],
        compiler_params=_cparams(("parallel", "arbitrary", "arbitrary")),
        name="moe_sparse",
    )(x, comb, wg, wu, wd, g.reshape(1, d), b.reshape(1, d))


def _router_kernel(x_ref, r_ref, o_ref):
    logits = _dot_mid(x_ref[...], r_ref[...])
    lane = lax.broadcasted_iota(jnp.int32, logits.shape, 1)
    logits = jnp.where(lane < N_EXPERTS, logits, -jnp.inf)
    m1 = jnp.max(logits, axis=-1, keepdims=True)
    i1 = jnp.min(jnp.where(logits == m1, lane, LANES), axis=-1, keepdims=True)
    rest = jnp.where(lane == i1, -jnp.inf, logits)
    m2 = jnp.max(rest, axis=-1, keepdims=True)
    i2 = jnp.min(jnp.where(rest == m2, lane, LANES), axis=-1, keepdims=True)
    e2 = jnp.exp(m2 - m1)
    den = 1.0 + e2
    o_ref[...] = jnp.where(lane == i1, 1.0 / den, 0.0) + jnp.where(lane == i2, e2 / den, 0.0)


def _router(x, router):
    m, d = x.shape
    tm = _row_tile(m)
    r = jnp.pad(router, ((0, 0), (0, LANES - router.shape[1])))
    return pl.pallas_call(
        _router_kernel,
        grid=(m // tm,),
        in_specs=[pl.BlockSpec((tm, d), lambda i: (i, 0)), pl.BlockSpec((d, LANES), lambda i: (0, 0))],
        out_specs=pl.BlockSpec((tm, LANES), lambda i: (i, 0)),
        out_shape=jax.ShapeDtypeStruct((m, LANES), F32),
        compiler_params=_cparams(("parallel",)),
        name="router",
    )(x, r)


def _topk_lanes(g, idx, axis=-1):
    sel = jnp.zeros(g.shape, F32)
    for _ in range(MOBA_TOPK):
        m = jnp.max(g, axis=axis, keepdims=True)
        first = jnp.min(jnp.where(g == m, idx, LANES), axis=axis, keepdims=True)
        hit = jnp.logical_and(idx == first, m > -jnp.inf)
        sel = jnp.where(hit, 1.0, sel)
        g = jnp.where(hit, -jnp.inf, g)
    return sel


def _moba_kernel(slope_ref, q_ref, k_ref, v_ref, o_ref, kb_ref, vt_ref, km_ref, m_ref, l_ref, acc_ref, sel_ref,
                 *, tq, nb, q_offset, hd):
    i = pl.program_id(2)
    blk = MOBA_BLOCK

    @pl.when(i == 0)
    def _():
        kb_ref[...] = k_ref[...].astype(BF16)
        km_ref[...] = jnp.zeros_like(km_ref)
        for j in range(nb):
            rows = slice(j * blk, (j + 1) * blk)
            vt_ref[j] = v_ref[rows, :].T.astype(BF16)
            km_ref[j:j + 1, :] = jnp.sum(k_ref[rows, :], axis=0, keepdims=True) * (1.0 / blk)

    q0 = q_offset + i * tq
    own = q0 // blk
    q = q_ref[...]
    lane = lax.broadcasted_iota(jnp.int32, (tq, LANES), 1)
    q2 = jnp.concatenate([jnp.where(lane < hd, q, 0.0), jnp.where(lane >= hd, q, 0.0)], axis=0)
    gate_t = _dot_mid(km_ref[...], q2, dot=_dot_nt)
    blk_id = lax.broadcasted_iota(jnp.int32, gate_t.shape, 0)
    sel_ref[...] = _topk_lanes(jnp.where(blk_id < own, gate_t, -jnp.inf), blk_id, axis=0)
    qb = (q2 * hd ** -0.5).astype(BF16)
    c2 = lax.broadcasted_iota(jnp.int32, (blk, 2 * tq), 1)
    d0 = (jnp.where(c2 >= tq, c2 - tq, c2) - lax.broadcasted_iota(jnp.int32, (blk, 2 * tq), 0)).astype(F32)
    slope_lane = slope_ref[...]
    c1 = lax.broadcasted_iota(jnp.int32, (1, 2 * tq), 1)
    slope = jnp.where(c1 < tq, slope_lane[:, 0:1], slope_lane[:, hd:hd + 1])
    slope_d0 = slope * d0
    first = lax.broadcasted_iota(jnp.int32, (2 * hd, tq), 0) < hd
    m_ref[...] = jnp.full(m_ref.shape, NEG, F32)
    l_ref[...] = jnp.zeros_like(l_ref)
    acc_ref[...] = jnp.zeros_like(acc_ref)

    def tiles(js, diagonal):
        scores = []
        for j in js:
            off = (q0 - j * blk).astype(F32)
            s = _dot_nt(kb_ref[pl.ds(pl.multiple_of(j * blk, blk), blk), :], qb) - (slope_d0 + slope * off)
            if diagonal:
                s = jnp.where(d0 + off >= 0.0, s, NEG)
            else:
                s = jnp.where(sel_ref[pl.ds(j, 1), :] > 0.0, s, NEG)
            scores.append(s)
        m_old = m_ref[...]
        m_new = functools.reduce(jnp.maximum, [m_old] + [jnp.max(s, axis=0, keepdims=True) for s in scores])
        alpha = jnp.exp(m_old - m_new)
        l_new = alpha * l_ref[...]
        pv = None
        for j, s in zip(js, scores):
            p = jnp.exp(s - m_new)
            l_new = l_new + jnp.sum(p, axis=0, keepdims=True)
            d = _dot(vt_ref[j], p.astype(BF16))
            pv = d if pv is None else pv + d
        m_ref[...] = m_new
        l_ref[...] = l_new
        acc_ref[...] = (jnp.where(first, alpha[:, :tq], alpha[:, tq:]) * acc_ref[...]
                        + jnp.where(first, pv[:, :tq], pv[:, tq:]))

    tiles([own], True)

    def body(t, carry):
        tiles([own - 1 - 4 * t - n for n in range(4)], False)
        return carry

    lax.fori_loop(0, own // 4, body, 0)
    rest = own % 4

    @pl.when(rest >= 2)
    def _():
        tiles([rest - 1, rest - 2], False)

    @pl.when(rest % 2 == 1)
    def _():
        tiles([own - own], False)
    l = l_ref[...]
    o_ref[...] = (acc_ref[...] / jnp.where(first, l[:, :tq], l[:, tq:])).T


def _alibi_slopes_lanes(n_heads, hd):
    slopes = jnp.exp2(-8.0 * jnp.arange(1, n_heads + 1, dtype=F32) / n_heads)
    return jnp.repeat(slopes, hd).reshape(1, n_heads * hd)


def _attn_call(kernel, q, k, v, *, batch, q_offset, n_heads, extra_in=(), extra_specs=(), extra_scratch=(), name):
    lq, lk = q.shape[0] // batch, k.shape[0] // batch
    hd = q.shape[1] // n_heads
    assert 2 * hd == LANES and lk % MOBA_BLOCK == 0
    tq = min(MOBA_BLOCK, lq)
    assert lq % tq == 0 and MOBA_BLOCK % tq == 0 and q_offset % tq == 0
    nq = lq // tq
    return pl.pallas_call(
        functools.partial(kernel, tq=tq, q_offset=q_offset, hd=hd),
        grid=(batch, n_heads // 2, nq),
        in_specs=list(extra_specs) + [
            pl.BlockSpec((tq, LANES), lambda b, h, i: (b * nq + i, h)),
            pl.BlockSpec((lk, LANES), lambda b, h, i: (b, h)),
            pl.BlockSpec((lk, LANES), lambda b, h, i: (b, h))],
        out_specs=pl.BlockSpec((tq, LANES), lambda b, h, i: (b * nq + i, h)),
        out_shape=jax.ShapeDtypeStruct(q.shape, F32),
        scratch_shapes=list(extra_scratch),
        compiler_params=_cparams(("parallel", "parallel", "arbitrary")),
        name=name,
    )(*extra_in, q, k, v)


def _moba(q, k, v, *, batch, q_offset):
    nb = k.shape[0] // batch // MOBA_BLOCK
    assert nb <= LANES
    slopes = _alibi_slopes_lanes(H_A, q.shape[1] // H_A)
    tq = min(MOBA_BLOCK, q.shape[0] // batch)
    nb_rows = -(-nb // 8) * 8
    return _attn_call(
        functools.partial(_moba_kernel, nb=nb), q, k, v, batch=batch, q_offset=q_offset, n_heads=H_A,
        extra_in=(slopes,), extra_specs=(pl.BlockSpec((1, LANES), lambda b, h, i: (0, h)),),
        extra_scratch=(pltpu.VMEM((nb * MOBA_BLOCK, LANES), BF16), pltpu.VMEM((nb, LANES, MOBA_BLOCK), BF16),
                       pltpu.VMEM((nb_rows, LANES), F32), pltpu.VMEM((1, 2 * tq), F32), pltpu.VMEM((1, 2 * tq), F32),
                       pltpu.VMEM((LANES, tq), F32), pltpu.VMEM((nb_rows, 2 * tq), F32)), name="moba")


EXP_UNDERFLOW = -110.0


def _softplus(z):
    return jnp.maximum(z, 0.0) + jnp.log(1.0 + jnp.exp(-jnp.abs(z)))


def _stick_kernel(q_ref, k_ref, v_ref, o_ref, kb_ref, vb_ref, c_ref, acc_ref, *, tq, q_offset, hd):
    i = pl.program_id(2)
    blk = MOBA_BLOCK

    @pl.when(i == 0)
    def _():
        kb_ref[...] = k_ref[...].astype(BF16)
        vb_ref[...] = v_ref[...].astype(BF16)

    q0 = q_offset + i * tq
    own = q0 // blk
    q = q_ref[...] * hd ** -0.5
    lane = lax.broadcasted_iota(jnp.int32, (tq, LANES), 1)
    qb = jnp.concatenate([jnp.where(lane < hd, q, 0.0), jnp.where(lane >= hd, q, 0.0)], axis=0).astype(BF16)
    r2 = lax.broadcasted_iota(jnp.int32, (2 * tq, blk), 0)
    d0 = jnp.where(r2 >= tq, r2 - tq, r2) - lax.broadcasted_iota(jnp.int32, (2 * tq, blk), 1)
    u = jnp.where(lax.broadcasted_iota(jnp.int32, (blk, blk), 0) > lax.broadcasted_iota(jnp.int32, (blk, blk), 1),
                  1.0, 0.0).astype(BF16)
    c_ref[...] = jnp.zeros_like(c_ref)
    acc_ref[...] = jnp.zeros_like(acc_ref)

    halves = [slice(0, tq), slice(tq, 2 * tq)]

    def tile(j, diagonal):
        start = pl.multiple_of(j * blk, blk)
        kj = kb_ref[pl.ds(start, blk), :]
        zs = [_dot_nt(qb[h], kj) for h in halves]
        log_keep = [-_softplus(z) for z in zs]
        log_beta = [z + lk for z, lk in zip(zs, log_keep)]
        if diagonal:
            before = d0[halves[0]] + (q0 - j * blk) > 0
            log_keep = [jnp.where(before, lk, 0.0) for lk in log_keep]
        his = [lk.astype(BF16) for lk in log_keep]
        los = [(lk - hi.astype(F32)).astype(BF16) for lk, hi in zip(log_keep, his)]
        later = [_dot(jnp.concatenate([hi, lo], axis=0), u) for hi, lo in zip(his, los)]
        later = [t[:tq] + t[tq:] for t in later]
        cs = [c_ref[h, :] for h in halves]
        ws = [jnp.exp(lb + lt + c) for lb, lt, c in zip(log_beta, later, cs)]
        if diagonal:
            ws = [jnp.where(before, w, 0.0) for w in ws]
        for h, c, lt, lk in zip(halves, cs, later, log_keep):
            c_ref[h, :] = c + lt[:, 0:1] + lk[:, 0:1]
        vj = vb_ref[pl.ds(start, blk), :]
        pv = [_dot(w.astype(BF16), vj) for w in ws]
        acc_ref[...] += jnp.where(lane < hd, pv[0], pv[1])

    tile(own, True)

    def more(carry):
        t, c_max = carry
        return jnp.logical_and(t <= own, c_max > EXP_UNDERFLOW)

    def body(carry):
        t, _ = carry
        tile(own - t, False)
        return t + 1, jnp.max(c_ref[...])

    lax.while_loop(more, body, (jnp.int32(1), jnp.max(c_ref[...])))
    o_ref[...] = acc_ref[...]


def _stick(q, k, v, *, batch, q_offset):
    tq = min(MOBA_BLOCK, q.shape[0] // batch)
    lk = k.shape[0] // batch
    return _attn_call(_stick_kernel, q, k, v, batch=batch, q_offset=q_offset, n_heads=H_D,
                      extra_scratch=(pltpu.VMEM((lk, LANES), BF16), pltpu.VMEM((lk, LANES), BF16),
                                     pltpu.VMEM((2 * tq, 1), F32), pltpu.VMEM((tq, LANES), F32)), name="stick")


CHUNK = 64
SUB = 16
SEQ_BLOCK = 512
EXP_CLAMP = 80.0


def _dot_mid(a, b, dot=_dot):
    a0 = a.astype(BF16)
    a1 = (a - a0.astype(F32)).astype(BF16)
    b0 = b.astype(BF16)
    b1 = (b - b0.astype(F32)).astype(BF16)
    return dot(a0, b0) + (dot(a0, b1) + dot(a1, b0))


def _incl_lower(c):
    r = lax.broadcasted_iota(jnp.int32, (c, c), 0)
    s = lax.broadcasted_iota(jnp.int32, (c, c), 1)
    return r, s


def _hgrn_kernel(q_ref, f_ref, i_ref, g_ref, lb_ref, gn_ref, s0_ref, o_ref, s_ref, st_ref, *, c, n_chunks, n_heads,
                 layer):
    l = pl.program_id(1)
    dk = LANES

    w = n_heads * dk

    @pl.when(l == 0)
    def _():
        for h in range(n_heads):
            st_ref[h * dk:(h + 1) * dk, :] = s0_ref[0, h]

    r, s = _incl_lower(c)
    tri = jnp.where(s <= r, 1.0, 0.0).astype(BF16)
    sub = min(SUB, c)
    n = n_heads * c
    shift_c, shift_s = c.bit_length() - 1, sub.bit_length() - 1
    ri = lax.broadcasted_iota(jnp.int32, (n_heads * sub, n), 0)
    ci = lax.broadcasted_iota(jnp.int32, (n_heads * sub, n), 1)
    pair = jnp.right_shift(ri, shift_s) == jnp.right_shift(ci, shift_c)
    t_loc, s_loc = jnp.bitwise_and(ri, sub - 1), jnp.bitwise_and(ci, c - 1)
    own_cols = (jnp.right_shift(lax.broadcasted_iota(jnp.int32, (n, w), 0), shift_c)
                == lax.broadcasted_iota(jnp.int32, (n, w), 1) // dk)
    gn = gn_ref[...]
    e = jnp.exp(lb_ref[...] - jnp.max(lb_ref[...], axis=0, keepdims=True))
    lb = jnp.sum(e[:layer + 1], axis=0, keepdims=True) / jnp.sum(e, axis=0, keepdims=True)

    def stack(t):
        return jnp.concatenate([t[:, h * dk:(h + 1) * dk] for h in range(n_heads)], axis=0)

    def run(chunks):
        m = len(chunks)
        rows = [pl.ds(i * c, c) if isinstance(i, int) else pl.ds(pl.multiple_of(i * c, c), c) for i in chunks]
        zs = [f_ref[r_, :] for r_ in rows]
        ks = [(1.0 - lb) * jax.nn.sigmoid(-z) for z in zs]
        qs = [jax.nn.silu(q_ref[r_, :]) for r_ in rows]
        cgs = [_dot_exact_lhs(tri, jnp.log(lb + (1.0 - lb) * jax.nn.sigmoid(z))) for z in zs]
        v4s = [stack(i_ref[r_, :]).astype(BF16) for r_ in rows]
        intra = [[] for _ in range(m)]
        for b in range(c // sub):
            lo, hi = b * sub, (b + 1) * sub
            refs = [cg[lo - 1:lo] if b else jnp.zeros((1, w), F32) for cg in cgs]
            q_sub = [stack(q[lo:hi] * jnp.exp(cg[lo:hi] - ref_pt)).astype(BF16) for q, cg, ref_pt in zip(qs, cgs, refs)]
            k_all = [stack(k * jnp.exp(jnp.minimum(ref_pt - cg, EXP_CLAMP))).astype(BF16)
                     for k, cg, ref_pt in zip(ks, cgs, refs)]
            a = [_dot_nt(q_, k_) for q_, k_ in zip(q_sub, k_all)]
            a = [jnp.where(jnp.logical_and(pair, s_loc <= t_loc + lo), a_, 0.0).astype(BF16) for a_ in a]
            for x_, (a_, v4) in enumerate(zip(a, v4s)):
                intra[x_].append(_dot(a_, v4))
        q_dec = [stack(q * jnp.exp(cg)).astype(BF16) for q, cg in zip(qs, cgs)]
        lasts = [cg[c - 1:c] for cg in cgs]
        kts = [stack(k * jnp.exp(last - cg)).astype(BF16) for k, last, cg in zip(ks, lasts, cgs)]
        for x_ in range(m):
            st = st_ref[...]
            from_state = _dot_nt(q_dec[x_], st.astype(BF16))
            v_wide = jnp.where(own_cols, jnp.concatenate([v4s[x_]] * n_heads, axis=1), 0.0)
            e_last = jnp.concatenate([jnp.broadcast_to(jnp.exp(lasts[x_][:, h * dk:(h + 1) * dk]), (dk, dk))
                                      for h in range(n_heads)], axis=0)
            st_ref[...] = st * e_last + _dot_tn(v_wide, kts[x_])
            for h in range(n_heads):
                o = from_state[h * c:(h + 1) * c, h * dk:(h + 1) * dk] + jnp.concatenate(
                    [part[h * sub:(h + 1) * sub] for part in intra[x_]], axis=0)
                o = o * lax.rsqrt(jnp.mean(o * o, axis=-1, keepdims=True) + RMS_EPS) * gn
                o_ref[rows[x_], h * dk:(h + 1) * dk] = o * jax.nn.sigmoid(g_ref[rows[x_], h * dk:(h + 1) * dk])

    per_trip = 2 if n_chunks % 2 == 0 else 1

    def trip(i, carry):
        run([per_trip * i + x_ for x_ in range(per_trip)])
        return carry

    lax.fori_loop(0, n_chunks // per_trip, trip, 0)

    @pl.when(l == pl.num_programs(1) - 1)
    def _():
        for h in range(n_heads):
            s_ref[0, h] = st_ref[h * dk:(h + 1) * dk, :]


def _hgrn(hg, lb, gn, s0, *, batch, layer=0):
    t, width = hg.shape
    seq = t // batch
    w = width // 4
    n_heads = w // LANES
    c = min(CHUNK, seq)
    lblk = min(SEQ_BLOCK, seq)
    assert seq % lblk == 0 and lblk % c == 0 and c % min(SUB, c) == 0
    nl = seq // lblk
    part = lambda p: pl.BlockSpec((lblk, w), lambda b, l: (b * nl + l, p))
    state = pl.BlockSpec((1, n_heads, LANES, LANES), lambda b, l: (b, 0, 0, 0))
    o, st = pl.pallas_call(
        functools.partial(_hgrn_kernel, c=c, n_chunks=lblk // c, n_heads=n_heads, layer=layer),
        grid=(batch, nl),
        in_specs=[part(0), part(1), part(2), part(3),
                  pl.BlockSpec(lb.shape, lambda b, l: (0, 0)), pl.BlockSpec((1, LANES), lambda b, l: (0, 0)), state],
        out_specs=[pl.BlockSpec((lblk, w), lambda b, l: (b * nl + l, 0)), state],
        out_shape=[jax.ShapeDtypeStruct((t, w), F32), jax.ShapeDtypeStruct(s0.shape, F32)],
        scratch_shapes=[pltpu.VMEM((n_heads * LANES, LANES), F32)],
        compiler_params=_cparams(("parallel", "arbitrary")),
        name="hgrn2",
    )(hg, hg, hg, hg, lb, gn.reshape(1, LANES), jnp.swapaxes(s0, -1, -2))
    return o, jnp.swapaxes(st, -1, -2)


def _unit_lower_inverses(lms, n, period):
    bs = min(SUB, period)
    r, s = _incl_lower(n)
    eye = jnp.where(r == s, 1.0, 0.0)
    shift = bs.bit_length() - 1
    same = jnp.right_shift(r, shift) == jnp.right_shift(s, shift)

    def mm(a, b):
        return _dot(a.astype(BF16), b.astype(BF16))

    def neumann(mats, order):
        invs, pws, k = [eye + a for a in mats], list(mats), 2
        while k < order:
            pws = [mm(pw, pw) for pw in pws]
            invs = [inv + mm(inv, pw) for inv, pw in zip(invs, pws)]
            k *= 2
        return invs

    inv_ds = neumann([-jnp.where(same, lm, 0.0) for lm in lms], bs)
    if period == bs:
        return inv_ds
    ms = [mm(inv_d, jnp.where(same, 0.0, lm)) for inv_d, lm in zip(inv_ds, lms)]
    return [mm(a, inv_d) for a, inv_d in zip(neumann([-m for m in ms], period // bs), inv_ds)]


def _gdn_kernel(x_ref, ab_ref, g_ref, cw_ref, al_ref, dtb_ref, gn_ref, s0_ref, cb_ref, o_ref, s_ref, nb_ref,
                xx_ref, y_ref, st_ref, *, c, n_chunks, n_heads, lblk):
    l = pl.program_id(1)
    dk = LANES
    halo = 8
    w = n_heads * dk

    @pl.when(l == 0)
    def _():
        for h in range(n_heads):
            st_ref[:, h * dk:(h + 1) * dk] = s0_ref[0, h]
        xx_ref[halo - (CONV_W - 1):halo, :] = cb_ref[0]

    xx_ref[halo:halo + lblk, :] = x_ref[...]
    y = xx_ref[halo:halo + lblk, :] * cw_ref[CONV_W - 1:CONV_W, :]
    for i in range(CONV_W - 1):
        off = halo - (CONV_W - 1) + i
        y = y + xx_ref[off:off + lblk, :] * cw_ref[i:i + 1, :]
    y_ref[...] = jax.nn.silu(y)
    tail = xx_ref[halo + lblk - (CONV_W - 1):halo + lblk, :]
    xx_ref[halo - (CONV_W - 1):halo, :] = tail

    r, s = _incl_lower(c)
    tri = jnp.where(s <= r, 1.0, 0.0).astype(BF16)
    n = n_heads * c
    rr, ss = _incl_lower(n)
    shift = c.bit_length() - 1
    same_head = jnp.right_shift(rr, shift) == jnp.right_shift(ss, shift)
    incl = jnp.logical_and(same_head, ss <= rr)
    strict = jnp.logical_and(same_head, ss < rr)
    own_cols = (jnp.right_shift(lax.broadcasted_iota(jnp.int32, (n, w), 0), shift)
                == lax.broadcasted_iota(jnp.int32, (n, w), 1) // dk)
    gn = gn_ref[...]

    def l2n(t):
        return t * lax.rsqrt(jnp.sum(t * t, axis=-1, keepdims=True) + RMS_EPS)

    def stack(f):
        return jnp.concatenate([f(h) for h in range(n_heads)], axis=0)

    def own_block(t):
        return stack(lambda h: t[h * c:(h + 1) * c, h * dk:(h + 1) * dk])

    def front(i):
        rows = pl.ds(i * c, c) if isinstance(i, int) else pl.ds(pl.multiple_of(i * c, c), c)
        ab = ab_ref[rows, :]
        pre = ab + dtb_ref[...]
        log_a = -jnp.exp(al_ref[...]) * (jnp.maximum(pre, 0.0) + jnp.log(1.0 + jnp.exp(-jnp.abs(pre))))
        beta_all = jax.nn.sigmoid(ab)
        cg_all = _dot_exact_lhs(tri, log_a)
        cg_t = cg_all.T
        q = stack(lambda h: l2n(y_ref[rows, h * dk:(h + 1) * dk])) * dk ** -0.5
        k = stack(lambda h: l2n(y_ref[rows, w + h * dk:w + (h + 1) * dk]))
        v = stack(lambda h: y_ref[rows, 2 * w + h * dk:2 * w + (h + 1) * dk])
        cg = stack(lambda h: cg_all[:, h:h + 1])
        beta = stack(lambda h: beta_all[:, n_heads + h:n_heads + h + 1])
        cg_row = jnp.concatenate([cg_t[h:h + 1, :c] for h in range(n_heads)], axis=1)
        last = stack(lambda h: jnp.broadcast_to(cg_all[c - 1:c, h:h + 1], (c, 1)))
        decay = jnp.exp(jnp.where(incl, cg - cg_row, NEG))
        kk = _dot_mid(k, k, dot=_dot_nt) * decay * beta
        return jnp.where(strict, kk, 0.0), (rows, q, k, v, cg, beta, last, decay, cg_all)

    def back(side, t_inv):
        rows, q, k, v, cg, beta, last, decay, cg_all = side
        e_cg = jnp.exp(cg)
        sol = _dot_mid(t_inv, jnp.concatenate([v * beta, k * (beta * e_cg)], axis=1))
        qk = (_dot_nt(q.astype(BF16), k.astype(BF16)) * decay).astype(BF16)
        kt = (k * jnp.exp(last - cg)).astype(BF16)
        e_last = jnp.concatenate([jnp.broadcast_to(jnp.exp(cg_all[c - 1:c, h:h + 1]), (1, dk)) for h in range(n_heads)],
                                 axis=1)
        return rows, sol[:, :dk], sol[:, dk:].astype(BF16), (q * e_cg).astype(BF16), qk, kt, e_last

    def advance(prepared):
        rows, sol_v, sol_k, q_decayed, qk, kt, e_last = prepared
        st = st_ref[...]
        stb = st.astype(BF16)
        u = sol_v - own_block(_dot(sol_k, stb))
        o = own_block(_dot(q_decayed, stb)) + _dot(qk, u.astype(BF16))
        u_wide = jnp.where(own_cols, jnp.concatenate([u] * n_heads, axis=1), 0.0)
        st_ref[...] = st * e_last + _dot_tn(kt, u_wide.astype(BF16))
        o = o * lax.rsqrt(jnp.mean(o * o, axis=-1, keepdims=True) + RMS_EPS) * gn
        for h in range(n_heads):
            o_ref[rows, h * dk:(h + 1) * dk] = o[h * c:(h + 1) * c] * jax.nn.silu(g_ref[rows, h * dk:(h + 1) * dk])

    def run(chunks):
        fronts = [front(i) for i in chunks]
        inverses = _unit_lower_inverses([lm for lm, _ in fronts], n, c)
        for prepared in [back(side, t_inv) for (_, side), t_inv in zip(fronts, inverses)]:
            advance(prepared)

    per_trip = 4 if n_chunks % 4 == 0 else 2

    def trip(i, carry):
        run([per_trip * i + m for m in range(per_trip)])
        return carry

    lax.fori_loop(0, n_chunks // per_trip, trip, 0)
    if n_chunks % per_trip:
        run([n_chunks - 1])

    @pl.when(l == pl.num_programs(1) - 1)
    def _():
        for h in range(n_heads):
            s_ref[0, h] = st_ref[:, h * dk:(h + 1) * dk]
        nb_ref[0] = tail


def _gdn(qkv, ab, g, conv_w, a_log, dt_bias, gn, s0, conv_buf, *, batch):
    t, width = qkv.shape
    seq = t // batch
    w = width // 3
    n_heads = w // LANES
    c = min(CHUNK, seq)
    lblk = min(SEQ_BLOCK, seq)
    assert seq % lblk == 0 and lblk % c == 0 and lblk >= CONV_W - 1
    nl = seq // lblk
    pad_lane = lambda vec: jnp.pad(vec, (0, LANES - vec.shape[0])).reshape(1, LANES)
    rows = lambda width_: pl.BlockSpec((lblk, width_), lambda b, l: (b * nl + l, 0))
    const = lambda shape: pl.BlockSpec(shape, lambda b, l: (0,) * len(shape))
    state = pl.BlockSpec((1, n_heads, LANES, LANES), lambda b, l: (b, 0, 0, 0))
    buf = pl.BlockSpec((1, CONV_W - 1, width), lambda b, l: (b, 0, 0))
    return pl.pallas_call(
        functools.partial(_gdn_kernel, c=c, n_chunks=lblk // c, n_heads=n_heads, lblk=lblk),
        grid=(batch, nl),
        in_specs=[rows(width), rows(LANES), rows(w), const((CONV_W, width)), const((1, LANES)), const((1, LANES)),
                  const((1, LANES)), state, buf],
        out_specs=[rows(w), state, buf],
        out_shape=[jax.ShapeDtypeStruct((t, w), F32), jax.ShapeDtypeStruct(s0.shape, F32),
                   jax.ShapeDtypeStruct(conv_buf.shape, F32)],
        scratch_shapes=[pltpu.VMEM((lblk + 8, width), F32), pltpu.VMEM((lblk, width), F32),
                        pltpu.VMEM((LANES, w), F32)],
        compiler_params=_cparams(("parallel", "arbitrary")),
        name="gated_deltanet",
    )(qkv, ab, g, conv_w, pad_lane(a_log), pad_lane(dt_bias), gn.reshape(1, LANES), s0, conv_buf)


PAGES_PER_STEP = 8


def _head_fold(pv, n_heads, nq, hd):
    return jnp.concatenate([pv[h * nq:(h + 1) * nq, h * hd:(h + 1) * hd] for h in range(n_heads)], axis=0)


def _row_ids(rows, nq):
    r = lax.broadcasted_iota(jnp.int32, (rows, 1), 0)
    return r // nq, r % nq


def _page_group(refs, first, count):
    return jnp.concatenate([refs[first + r][0, 0].reshape(-1, refs[first + r].shape[-1]) for r in range(count)], axis=1)


def _stick_paged_kernel(pt_ref, qbd_ref, kn_ref, vn_ref, k_hbm, v_hbm, o_ref, kbuf, vbuf, sem, *, n_heads, nq, hd,
                        n_pages):
    b = pl.program_id(0)
    rows = n_heads * nq
    page = kbuf.shape[-1]
    ppg = kbuf.shape[1]
    n_groups = n_pages // ppg
    _, row_q = _row_ids(rows, nq)
    qbd = (qbd_ref[0] * hd ** -0.5).astype(BF16)

    def copies(g, slot):
        out = []
        for r in range(ppg):
            pg = pt_ref[b, g * ppg + r]
            out.append(pltpu.make_async_copy(k_hbm.at[0, pg], kbuf.at[slot, r], sem.at[0, slot, r]))
            out.append(pltpu.make_async_copy(v_hbm.at[0, pg], vbuf.at[slot, r], sem.at[1, slot, r]))
        return out

    def group(buf, slot):
        return jnp.concatenate([buf[slot, r].reshape(n_heads * hd, page) for r in range(ppg)], axis=1).astype(BF16)

    def strict_later(n):
        return jnp.where(lax.broadcasted_iota(jnp.int32, (n, n), 0) > lax.broadcasted_iota(jnp.int32, (n, n), 1),
                         1.0, 0.0).astype(BF16)

    def weights(z, mask, carry, u):
        log_keep = -_softplus(z)
        log_beta = z + log_keep
        if mask is not None:
            log_keep = jnp.where(mask, log_keep, 0.0)
        hi = log_keep.astype(BF16)
        lo = (log_keep - hi.astype(F32)).astype(BF16)
        later = _dot(hi, u) + _dot(lo, u)
        w = jnp.exp(log_beta + later + carry)
        if mask is not None:
            w = jnp.where(mask, w, 0.0)
        return w.astype(BF16), carry + jnp.sum(log_keep, axis=-1, keepdims=True)

    for cp in copies(n_groups - 1, 0):
        cp.start()

    n = kn_ref.shape[1]
    z = _dot_nt(qbd, kn_ref[0].astype(BF16))
    col = lax.broadcasted_iota(jnp.int32, (rows, n), 1)
    w, c = weights(z, col < row_q, jnp.zeros((rows, 1), F32), strict_later(n))
    acc = _dot(w, vn_ref[0].astype(BF16))
    u = strict_later(ppg * page)

    def more(carry):
        g, c, _ = carry
        return jnp.logical_and(g >= 0, jnp.max(c) > EXP_UNDERFLOW)

    def body(carry):
        g, c, acc = carry
        slot = (n_groups - 1 - g) % 2

        @pl.when(g > 0)
        def _():
            for cp in copies(g - 1, 1 - slot):
                cp.start()

        for cp in copies(g, slot):
            cp.wait()
        w, c = weights(_dot(qbd, group(kbuf, slot)), None, c, u)
        return g - 1, c, acc + _dot_nt(w, group(vbuf, slot))

    g, _, acc = lax.while_loop(more, body, (jnp.int32(n_groups - 1), c, acc))

    @pl.when(g >= 0)
    def _():
        for cp in copies(g, (n_groups - 1 - g) % 2):
            cp.wait()

    o_ref[0] = _head_fold(acc, n_heads, nq, hd)


def _moba_paged_kernel(pt_ref, qbd_ref, kn_ref, vn_ref, *rest, pps, n_heads, nq, hd, past_len):
    k_refs, v_refs = rest[:pps], rest[pps:2 * pps]
    o_ref, mo_ref, lo_ref, acco_ref, m_ref, l_ref, acc_ref, km_ref = rest[2 * pps:]
    s = pl.program_id(1)
    rows = n_heads * nq
    page = k_refs[0].shape[-1]
    ppb = MOBA_BLOCK // page
    row_h, row_q = _row_ids(rows, nq)
    slope = jnp.exp2(-8.0 * (row_h + 1).astype(F32) / n_heads)
    lane = lax.broadcasted_iota(jnp.int32, (rows, LANES), 1)
    q_f32 = qbd_ref[0]
    qbd = (q_f32 * hd ** -0.5).astype(BF16)

    @pl.when(s == 0)
    def _():
        n = kn_ref.shape[1]
        col = lax.broadcasted_iota(jnp.int32, (rows, n), 1)
        sc = _dot_nt(qbd, kn_ref[0].astype(BF16)) - slope * (row_q - col).astype(F32)
        sc = jnp.where(col <= row_q, sc, NEG)
        m = jnp.max(sc, axis=-1, keepdims=True)
        p = jnp.exp(sc - m)
        mo_ref[...] = m
        lo_ref[...] = jnp.sum(p, axis=-1, keepdims=True)
        acco_ref[...] = _head_fold(_dot(p.astype(BF16), vn_ref[0].astype(BF16)), n_heads, nq, hd)
        m_ref[...] = jnp.full(m_ref.shape, NEG, F32)
        l_ref[...] = jnp.zeros_like(l_ref)
        km_ref[...] = jnp.zeros_like(km_ref)

    @pl.when(s > 0)
    def _():
        key = lax.broadcasted_iota(jnp.int32, (rows, MOBA_BLOCK), 1)
        m_all, l_all, km = m_ref[...], l_ref[...], km_ref[...]
        lane_k = lax.broadcasted_iota(jnp.int32, km.shape, 1)
        blocks = [(s - 1) * (pps // ppb) + bi for bi in range(pps // ppb)]
        kfs = [_page_group(k_refs, bi * ppb, ppb) for bi in range(len(blocks))]
        raw = [_dot(qbd, kf.astype(BF16)) for kf in kfs]
        means = [jnp.sum(kf, axis=-1, keepdims=True) * (1.0 / MOBA_BLOCK) for kf in kfs]
        scs = [sc - slope * (past_len + row_q - (blk * MOBA_BLOCK + key)).astype(F32) for sc, blk in zip(raw, blocks)]
        m_bs = [jnp.max(sc, axis=-1, keepdims=True) for sc in scs]
        ps = [jnp.exp(sc - m_b) for sc, m_b in zip(scs, m_bs)]
        pvs = [_dot_nt(p.astype(BF16), _page_group(v_refs, bi * ppb, ppb).astype(BF16)) for bi, p in enumerate(ps)]
        for blk, mean, m_b, p, pv in zip(blocks, means, m_bs, ps, pvs):
            km = jnp.where(lane_k == blk, mean, km)
            acc_ref[blk] = _head_fold(pv, n_heads, nq, hd)
            m_all = jnp.where(lane == blk, m_b, m_all)
            l_all = jnp.where(lane == blk, jnp.sum(p, axis=-1, keepdims=True), l_all)
        m_ref[...] = m_all
        l_ref[...] = l_all
        km_ref[...] = km

    @pl.when(s == pl.num_programs(1) - 1)
    def _():
        n_blocks = past_len // MOBA_BLOCK
        gate = _dot_mid(q_f32, km_ref[...])
        sel = _topk_lanes(jnp.where(lane < n_blocks, gate, -jnp.inf), lane) > 0.0
        m_all, l_all = m_ref[...], l_ref[...]
        m_o = mo_ref[...]
        m_tot = jnp.maximum(m_o, jnp.max(jnp.where(sel, m_all, NEG), axis=-1, keepdims=True))
        wgt = jnp.where(sel, jnp.exp(m_all - m_tot), 0.0)
        a_o = jnp.exp(m_o - m_tot)
        l_tot = lo_ref[...] * a_o + jnp.sum(wgt * l_all, axis=-1, keepdims=True)
        acc = acco_ref[...] * a_o
        for b in range(n_blocks):
            acc = acc + wgt[:, b:b + 1] * acc_ref[b]
        o_ref[0] = acc / l_tot


def _paged_attention(kind, q, k_new, v_new, k_pool, v_pool, page_table, *, n_heads):
    batch, n_pages = page_table.shape
    page, hd = k_pool.shape[2], k_pool.shape[4]
    nq = q.shape[0] // batch
    rows = n_heads * nq
    pps = PAGES_PER_STEP
    past_len = n_pages * page
    assert n_pages % pps == 0 and past_len % MOBA_BLOCK == 0 and MOBA_BLOCK % page == 0 and pps % (MOBA_BLOCK // page) == 0
    assert nq <= page and past_len // MOBA_BLOCK <= LANES and (2 * LANES) % n_heads == 0
    n_groups = n_pages // pps
    q4 = q.reshape(batch, nq, n_heads, hd).transpose(0, 2, 1, 3)
    q_bd = (q4[:, :, :, None, :] * jnp.eye(n_heads, dtype=F32)[None, :, None, :, None]).reshape(batch, rows, n_heads * hd)
    pad_new = lambda t: jnp.pad(t.reshape(batch, nq, n_heads * hd), ((0, 0), (0, page - nq), (0, 0)))
    k_t, v_t = (jnp.transpose(t, (0, 1, 3, 4, 2)) for t in (k_pool, v_pool))

    if kind == "stick":
        ppg = MOBA_BLOCK // page
        seq = lambda shape: pl.BlockSpec((1,) + shape, lambda b, pt: (b, 0, 0))
        buf = pltpu.VMEM((2, ppg, n_heads, hd, page), F32)
        out = pl.pallas_call(
            functools.partial(_stick_paged_kernel, n_heads=n_heads, nq=nq, hd=hd, n_pages=n_pages),
            grid_spec=pltpu.PrefetchScalarGridSpec(
                num_scalar_prefetch=1,
                grid=(batch,),
                in_specs=[seq((rows, n_heads * hd)), seq((page, n_heads * hd)), seq((page, n_heads * hd)),
                          pl.BlockSpec(memory_space=pl.ANY), pl.BlockSpec(memory_space=pl.ANY)],
                out_specs=seq((rows, hd)),
                scratch_shapes=[buf, buf, pltpu.SemaphoreType.DMA((2, 2, ppg))]),
            out_shape=jax.ShapeDtypeStruct((batch, rows, hd), F32),
            compiler_params=_cparams(("arbitrary",)),
            name="stick_paged",
        )(page_table, q_bd, pad_new(k_new), pad_new(v_new), k_t, v_t)
    else:
        group = lambda s: jnp.maximum(s, 1) - 1
        scratch = [pltpu.VMEM((rows, 1), F32), pltpu.VMEM((rows, 1), F32), pltpu.VMEM((rows, hd), F32),
                   pltpu.VMEM((rows, LANES), F32), pltpu.VMEM((rows, LANES), F32),
                   pltpu.VMEM((past_len // MOBA_BLOCK, rows, hd), F32), pltpu.VMEM((n_heads * hd, LANES), F32)]

        def page_spec(r):
            return pl.BlockSpec((1, 1, n_heads, hd, page), lambda b, s, pt: (0, pt[b, group(s) * pps + r], 0, 0, 0))

        per_seq = lambda shape: pl.BlockSpec((1,) + shape, lambda b, s, pt: (b, 0, 0))
        out = pl.pallas_call(
            functools.partial(_moba_paged_kernel, pps=pps, n_heads=n_heads, nq=nq, hd=hd, past_len=past_len),
            grid_spec=pltpu.PrefetchScalarGridSpec(
                num_scalar_prefetch=1,
                grid=(batch, n_groups + 1),
                in_specs=[per_seq((rows, n_heads * hd)), per_seq((page, n_heads * hd)),
                          per_seq((page, n_heads * hd))] + [page_spec(r) for r in range(pps)] * 2,
                out_specs=per_seq((rows, hd)),
                scratch_shapes=scratch),
            out_shape=jax.ShapeDtypeStruct((batch, rows, hd), F32),
            compiler_params=_cparams(("parallel", "arbitrary")),
            name="moba_paged",
        )(page_table, q_bd, pad_new(k_new), pad_new(v_new), *([k_t] * pps), *([v_t] * pps))
    return out.reshape(batch, n_heads, nq, hd).transpose(0, 2, 1, 3).reshape(batch * nq, n_heads * hd)


def _layer_stack(x, batch, past, states, w):
    s_hgrn, s_gdn, conv_buf = states

    def attend(kind, q, k, v, pools, n_heads):
        if past is None:
            return (_moba if kind == "moba" else _stick)(q, k, v, batch=batch, q_offset=0)
        return _paged_attention(kind, q, k, v, *pools, past[4], n_heads=n_heads)

    (q_a, k_a, v_a, hg), kv_a_t = _project(x, w["in_a"] + [w["in_hgrn"]], batch, transposed=(1, 2))
    o_a = attend("moba", q_a, k_a, v_a, past and past[0:2], H_A)
    o_b, s_hgrn_new = _hgrn(hg, w["hgrn_lb"], w["hgrn_norm"], s_hgrn, batch=batch, layer=0)
    x = _matmul([o_a, o_b], w["out_even"], ln_args=(x, w["ln1_g"][0], w["ln1_b"][0]))
    x = _ffn(x, None, w["ffn_wg"], w["ffn_wu"], w["ffn_wd"], w["ln2_g"][0], w["ln2_b"][0])

    (qkv_c, ab, g_c, q_d, k_d, v_d), kv_d_t = _project(x, [w["in_qkv_c"], w["in_ab"], w["in_g_c"]] + w["in_d"], batch,
                                                       transposed=(4, 5))
    o_c, s_gdn_new, conv_new = _gdn(qkv_c, ab, g_c, w["conv_w"], w["a_log"], w["dt_bias"], w["gdn_norm"],
                                    s_gdn, conv_buf, batch=batch)
    o_d = attend("stick", q_d, k_d, v_d, past and past[2:4], H_D)
    x = _matmul([o_c, o_d], w["out_odd"], ln_args=(x, w["ln1_g"][1], w["ln1_b"][1]))
    comb = _router(x, w["router"])
    x = _moe(x, comb, w["moe_wg"], w["moe_wu"], w["moe_wd"], w["ln2_g"][1], w["ln2_b"][1])
    k_a, v_a = kv_a_t or (k_a, v_a)
    k_d, v_d = kv_d_t or (k_d, v_d)
    return x, k_a, v_a, s_hgrn_new, s_gdn_new, conv_new, k_d, v_d


def kernel(x_prompt, x_sample, cache_k_moba, cache_v_moba, state_hgrn, state_gdn, state_gdn_conv, cache_k_sb,
           cache_v_sb, page_table, w_in_even, w_out_even, hgrn_lb, hgrn_norm, w_in_odd, w_out_odd, gdn_conv_w,
           gdn_a_log, gdn_dt_bias, gdn_norm, ln1_g, ln1_b, ln2_g, ln2_b, ffn_wg, ffn_wu, ffn_wd, router, moe_wg,
           moe_wu, moe_wd):
    assert w_in_even.shape[0] == 1 and w_in_odd.shape[0] == 1
    bp, lp, d = x_prompt.shape
    bs, ls, _ = x_sample.shape
    mix = d // 2
    hd_a, hd_d = mix // H_A, mix // H_D
    conv_dim = gdn_conv_w.shape[-1]
    bf = lambda t: t.astype(BF16)
    tiled = lambda t, largest: _tile_columns(bf(t), _ff_tile(t.shape[-1], largest))
    cols = lambda wt, lo, n: bf(wt[:, lo:lo + n])

    wie, wio = w_in_even[0], w_in_odd[0]
    g_lo = conv_dim + 2 * H_C
    d_lo = g_lo + mix
    w = {
        "in_a": [cols(wie, i * mix, mix) for i in range(3)],
        "in_hgrn": cols(wie, 3 * mix, 4 * mix),
        "out_even": [bf(w_out_even[0][:mix]), bf(w_out_even[0][mix:])],
        "hgrn_lb": hgrn_lb, "hgrn_norm": hgrn_norm[0],
        "in_qkv_c": cols(wio, 0, conv_dim),
        "in_ab": bf(jnp.pad(wio[:, conv_dim:g_lo], ((0, 0), (0, LANES - 2 * H_C)))),
        "in_g_c": cols(wio, g_lo, mix),
        "in_d": [cols(wio, d_lo + i * mix, mix) for i in range(3)],
        "out_odd": [bf(w_out_odd[0][:mix]), bf(w_out_odd[0][mix:])],
        "conv_w": gdn_conv_w[0], "a_log": gdn_a_log[0], "dt_bias": gdn_dt_bias[0], "gdn_norm": gdn_norm[0],
        "ln1_g": ln1_g, "ln1_b": ln1_b, "ln2_g": ln2_g, "ln2_b": ln2_b,
        "ffn_wg": tiled(ffn_wg, 512), "ffn_wu": tiled(ffn_wu, 512), "ffn_wd": bf(ffn_wd),
        "router": router[0], "moe_wg": tiled(moe_wg[0], 896), "moe_wu": tiled(moe_wu[0], 896), "moe_wd": bf(moe_wd[0]),
    }

    def run(x, batch, past, states):
        seq = x.shape[1]
        y, k_a, v_a, s_h, s_g, cv, k_d, v_d = _layer_stack(x.reshape(batch * seq, d), batch, past, states, w)
        def cache(t, n_heads, hd):
            if t.ndim == 3:
                return t.reshape(batch, n_heads, hd, seq).transpose(0, 3, 1, 2)[None]
            return t.reshape(1, batch, seq, n_heads, hd)

        return (y.reshape(batch, seq, d), cache(k_a, H_A, hd_a), cache(v_a, H_A, hd_a), s_h[None], s_g[None], cv[None],
                cache(k_d, H_D, hd_d), cache(v_d, H_D, hd_d))

    past = (cache_k_moba, cache_v_moba, cache_k_sb, cache_v_sb, page_table)
    out_s = run(x_sample, bs, past, (state_hgrn[0], state_gdn[0], state_gdn_conv[0]))
    zeros_p = (jnp.zeros((bp,) + state_hgrn.shape[2:], F32), jnp.zeros((bp,) + state_gdn.shape[2:], F32),
               jnp.zeros((bp,) + state_gdn_conv.shape[2:], F32))
    out_p = run(x_prompt, bp, None, zeros_p)
    return (out_p[0], out_s[0]) + out_p[1:] + out_s[1:]
```

```python
import functools

import jax
import jax.numpy as jnp
from jax import lax
from jax.experimental import pallas as pl
from jax.experimental.pallas import tpu as pltpu

F32 = jnp.float32
BF16 = jnp.bfloat16

H_A, H_B, H_C, H_D = 8, 4, 4, 8
MOBA_BLOCK = 256
MOBA_TOPK = 3
CONV_W = 4
N_EXPERTS = 8
DEPTH = 2
DEEPNORM_ALPHA = (2 * DEPTH) ** 0.25
LN_EPS = 1e-5
RMS_EPS = 1e-6
NEG = -1e30
LANES = 128
VMEM_LIMIT = 56 * 1024 * 1024


def _cparams(sem):
    return pltpu.CompilerParams(dimension_semantics=sem, vmem_limit_bytes=VMEM_LIMIT)


def _layernorm(y, g, b):
    mu = jnp.mean(y, axis=-1, keepdims=True)
    yc = y - mu
    var = jnp.mean(yc * yc, axis=-1, keepdims=True)
    return yc * lax.rsqrt(var + LN_EPS) * g + b


def _split3(x):
    hi = x.astype(BF16)
    r = x - hi.astype(F32)
    mid = r.astype(BF16)
    lo = (r - mid.astype(F32)).astype(BF16)
    return hi, mid, lo


def _dot(a, b):
    return jnp.dot(a, b, preferred_element_type=F32)


def _dot_nt(a, b):
    return lax.dot_general(a, b, (((1,), (1,)), ((), ())), preferred_element_type=F32)


def _dot_tn(a, b):
    return lax.dot_general(a, b, (((0,), (0,)), ((), ())), preferred_element_type=F32)


def _dot_exact_lhs(a_bf16, b):
    b0, b1, b2 = _split3(b)
    return _dot(a_bf16, b0) + _dot(a_bf16, b1) + _dot(a_bf16, b2)


def _mm_kernel(*refs, n_in, ln):
    a_refs, w_refs, rest = refs[:n_in], refs[n_in:2 * n_in], refs[2 * n_in:]
    acc = None
    for a, w in zip(a_refs, w_refs):
        d = _dot(a[...].astype(BF16), w[...])
        acc = d if acc is None else acc + d
    if ln:
        res_ref, g_ref, b_ref, o_ref = rest
        o_ref[...] = _layernorm(DEEPNORM_ALPHA * res_ref[...] + acc, g_ref[...], b_ref[...])
    else:
        (o_ref,) = rest
        o_ref[...] = acc


def _row_tile(m, largest=512):
    for t in (1024, 512, 256, 128, 64, 32, 16, 8):
        if t <= largest and m % t == 0:
            return t
    raise ValueError(m)


def _proj_kernel(x_ref, *refs, n_w, n_t):
    w_refs, wt_refs = refs[:n_w], refs[n_w:n_w + n_t]
    o_refs, ot_refs = refs[n_w + n_t:2 * n_w + n_t], refs[2 * n_w + n_t:]
    xb = x_ref[...].astype(BF16)
    for w_ref, o_ref in zip(w_refs, o_refs):
        o_ref[...] = _dot(xb, w_ref[...])
    for wt_ref, ot_ref in zip(wt_refs, ot_refs):
        ot_ref[0] = _dot_nt(wt_ref[...], xb)


def _project(x, w_list, batch, transposed=()):
    m, d = x.shape
    tm = _row_tile(m)
    seq = m // batch
    wt_list = [w_list[t].T for t in transposed] if seq % tm == 0 else []
    nl = max(seq // tm, 1)
    const = lambda w: pl.BlockSpec(w.shape, lambda i: (0, 0))
    outs = pl.pallas_call(
        functools.partial(_proj_kernel, n_w=len(w_list), n_t=len(wt_list)),
        grid=(m // tm,),
        in_specs=[pl.BlockSpec((tm, d), lambda i: (i, 0))] + [const(w) for w in w_list] + [const(w) for w in wt_list],
        out_specs=[pl.BlockSpec((tm, w.shape[1]), lambda i: (i, 0)) for w in w_list]
        + [pl.BlockSpec((1, w.shape[0], tm), lambda i: (i // nl, 0, i % nl)) for w in wt_list],
        out_shape=[jax.ShapeDtypeStruct((m, w.shape[1]), F32) for w in w_list]
        + [jax.ShapeDtypeStruct((batch, w.shape[0], seq), F32) for w in wt_list],
        compiler_params=_cparams(("parallel",)),
        name="in_proj",
    )(x, *w_list, *wt_list)
    return outs[:len(w_list)], outs[len(w_list):]


def _matmul(a_list, w_list, *, tn=None, ln_args=None):
    m = a_list[0].shape[0]
    n = w_list[0].shape[1]
    tm = _row_tile(m, largest=1024)
    ln = ln_args is not None
    tn = n if (ln or tn is None) else tn
    assert n % tn == 0
    in_specs = [pl.BlockSpec((tm, a.shape[1]), lambda i, j: (i, 0)) for a in a_list]
    in_specs += [pl.BlockSpec((w.shape[0], tn), lambda i, j: (0, j)) for w in w_list]
    args = list(a_list) + list(w_list)
    if ln:
        res, g, b = ln_args
        in_specs += [pl.BlockSpec((tm, n), lambda i, j: (i, 0)),
                     pl.BlockSpec((1, n), lambda i, j: (0, 0)),
                     pl.BlockSpec((1, n), lambda i, j: (0, 0))]
        args += [res, g.reshape(1, n), b.reshape(1, n)]
    return pl.pallas_call(
        functools.partial(_mm_kernel, n_in=len(a_list), ln=ln),
        grid=(m // tm, n // tn),
        in_specs=in_specs,
        out_specs=pl.BlockSpec((tm, tn), lambda i, j: (i, j)),
        out_shape=jax.ShapeDtypeStruct((m, n), F32),
        compiler_params=_cparams(("parallel", "arbitrary")),
        name="matmul_ln" if ln else "matmul",
    )(*args)


def _ffn_kernel(*refs, use_comb):
    if use_comb:
        x_ref, comb_ref, wg_ref, wu_ref, wd_ref, g_ref, b_ref, o_ref, xb_ref, acc_ref = refs
    else:
        x_ref, wg_ref, wu_ref, wd_ref, g_ref, b_ref, o_ref, xb_ref, acc_ref = refs
    e, j = pl.program_id(1), pl.program_id(2)
    first = jnp.logical_and(e == 0, j == 0)
    last = jnp.logical_and(e == pl.num_programs(1) - 1, j == pl.num_programs(2) - 1)

    @pl.when(first)
    def _():
        xb_ref[...] = x_ref[...].astype(BF16)
        acc_ref[...] = jnp.zeros_like(acc_ref)

    xb = xb_ref[...]
    h = jax.nn.silu(_dot(xb, wg_ref[0])) * _dot(xb, wu_ref[0])
    if use_comb:
        lane = lax.broadcasted_iota(jnp.int32, comb_ref.shape, 1)
        h = h * jnp.sum(jnp.where(lane == e, comb_ref[...], 0.0), axis=-1, keepdims=True)
    acc_ref[...] += _dot(h.astype(BF16), wd_ref[0])

    @pl.when(last)
    def _():
        o_ref[...] = _layernorm(DEEPNORM_ALPHA * x_ref[...] + acc_ref[...], g_ref[...], b_ref[...])


def _ff_tile(ff, largest=896):
    for t in (1792, 896, 512, 256, LANES):
        if t <= largest and ff % t == 0:
            return t
    raise ValueError(ff)


def _ffn(x, comb, wg, wu, wd, g, b):
    m, d = x.shape
    n_e, _, ff = wg.shape
    tm = _row_tile(m, largest=1024)
    tf = _ff_tile(ff, largest=512)
    use_comb = comb is not None
    in_specs = [pl.BlockSpec((tm, d), lambda i, e, j: (i, 0))]
    args = [x]
    if use_comb:
        in_specs.append(pl.BlockSpec((tm, comb.shape[1]), lambda i, e, j: (i, 0)))
        args.append(comb)
    in_specs += [pl.BlockSpec((1, d, tf), lambda i, e, j: (e, 0, j)),
                 pl.BlockSpec((1, d, tf), lambda i, e, j: (e, 0, j)),
                 pl.BlockSpec((1, tf, d), lambda i, e, j: (e, j, 0)),
                 pl.BlockSpec((1, d), lambda i, e, j: (0, 0)),
                 pl.BlockSpec((1, d), lambda i, e, j: (0, 0))]
    args += [wg, wu, wd, g.reshape(1, d), b.reshape(1, d)]
    return pl.pallas_call(
        functools.partial(_ffn_kernel, use_comb=use_comb),
        grid=(m // tm, n_e, ff // tf),
        in_specs=in_specs,
        out_specs=pl.BlockSpec((tm, d), lambda i, e, j: (i, 0)),
        out_shape=jax.ShapeDtypeStruct((m, d), F32),
        scratch_shapes=[pltpu.VMEM((tm, d), BF16), pltpu.VMEM((tm, d), F32)],
        compiler_params=_cparams(("parallel", "arbitrary", "arbitrary")),
        name="moe_ffn" if use_comb else "ffn",
    )(*args)


MOE_TOKENS = 1024
MOE_SUBTILES = 1
MOE_ROWS = 128


def _moe_kernel(x_ref, comb_ref, wg_ref, wu_ref, wd_ref, g_ref, b_ref, o_ref,
                xb_ref, xc_ref, yc_ref, rank_ref, rank_t_ref, comb_t_ref):
    e, j = pl.program_id(1), pl.program_id(2)
    tm, d = x_ref.shape
    r, ts = MOE_ROWS, MOE_TOKENS
    subs = [slice(s * ts, (s + 1) * ts) for s in range(tm // ts)]
    lane = lax.broadcasted_iota(jnp.int32, (ts, LANES), 1)

    @pl.when(jnp.logical_and(e == 0, j == 0))
    def _():
        xb_ref[...] = x_ref[...].astype(BF16)
        o_ref[...] = jnp.zeros_like(o_ref)
        earlier = jnp.where(lax.broadcasted_iota(jnp.int32, (ts, ts), 1) < lax.broadcasted_iota(jnp.int32, (ts, ts), 0),
                            1.0, 0.0).astype(BF16)
        for sub in subs:
            routed = jnp.where(comb_ref[sub, :] > 0.0, 1.0, 0.0).astype(BF16)
            rank = _dot(earlier, routed)
            rank_ref[sub, :] = rank
            rank_t_ref[:, sub] = rank.T
            comb_t_ref[:, sub] = comb_ref[sub, :].T

    gate_rows = [comb_t_ref[pl.ds(e, 1), sub] for sub in subs]
    n_pass = [(jnp.sum(jnp.where(g_row > 0.0, 1, 0)) + (r - 1)) // r for g_row in gate_rows]
    base = [sum(n_pass[:s], jnp.int32(0)) * r for s in range(len(subs))]
    total_pass = sum(n_pass, jnp.int32(0))

    def passes(count, fn):
        def pair(k, carry):
            fn(k * (2 * r), 2 * r)
            return carry

        lax.fori_loop(0, count // 2, pair, 0)

        @pl.when(count % 2 == 1)
        def _():
            fn((count - 1) * r, r)

    @pl.when(j == 0)
    def _():
        for s, sub in enumerate(subs):
            rank_row = rank_t_ref[pl.ds(e, 1), sub]

            def compact(start, n_rows, s=s, sub=sub, rank_row=rank_row):
                slot = (start + lax.broadcasted_iota(jnp.int32, (n_rows, ts), 0)).astype(F32)
                pick = jnp.where(jnp.logical_and(rank_row == slot, gate_rows[s] > 0.0), 1.0, 0.0).astype(BF16)
                rows = pl.ds(pl.multiple_of(base[s] + start, r), n_rows)
                xc_ref[rows, :] = _dot(pick, xb_ref[sub, :]).astype(BF16)
                yc_ref[rows, :] = jnp.zeros((n_rows, d), F32)

            passes(n_pass[s], compact)

    def expert(start, n_rows):
        rows = pl.ds(pl.multiple_of(start, r), n_rows)
        xk = xc_ref[rows, :]
        h = jax.nn.silu(_dot(xk, wg_ref[0])) * _dot(xk, wu_ref[0])
        yc_ref[rows, :] += _dot(h.astype(BF16), wd_ref[0])

    passes(total_pass, expert)

    @pl.when(j == pl.num_programs(2) - 1)
    def _():
        for s, sub in enumerate(subs):
            rank_col = jnp.sum(jnp.where(lane == e, rank_ref[sub, :], 0.0), axis=-1, keepdims=True)
            gate_col = jnp.sum(jnp.where(lane == e, comb_ref[sub, :], 0.0), axis=-1, keepdims=True)

            def place(start, n_rows, s=s, sub=sub, rank_col=rank_col, gate_col=gate_col):
                slot = (start + lax.broadcasted_iota(jnp.int32, (ts, n_rows), 1)).astype(F32)
                put = jnp.where(jnp.logical_and(rank_col == slot, gate_col > 0.0), 1.0, 0.0).astype(BF16)
                rows = pl.ds(pl.multiple_of(base[s] + start, r), n_rows)
                o_ref[sub, :] += gate_col * _dot(put, yc_ref[rows, :].astype(BF16))

            passes(n_pass[s], place)

    @pl.when(jnp.logical_and(e == pl.num_programs(1) - 1, j == pl.num_programs(2) - 1))
    def _():
        o_ref[...] = _layernorm(DEEPNORM_ALPHA * x_ref[...] + o_ref[...], g_ref[...], b_ref[...])


def _moe(x, comb, wg, wu, wd, g, b):
    m, d = x.shape
    n_e, _, ff = wg.shape
    tm = MOE_TOKENS * MOE_SUBTILES
    if m % tm != 0:
        return _ffn(x, comb, wg, wu, wd, g, b)
    tf = _ff_tile(ff, largest=1792)
    assert MOE_TOKENS % (2 * MOE_ROWS) == 0 and comb.shape[1] == LANES and n_e <= LANES
    cap = tm + MOE_SUBTILES * MOE_ROWS
    once = pl.Buffered(1)
    return pl.pallas_call(
        _moe_kernel,
        grid=(m // tm, n_e, ff // tf),
        in_specs=[pl.BlockSpec((tm, d), lambda i, e, j: (i, 0), pipeline_mode=once),
                  pl.BlockSpec((tm, LANES), lambda i, e, j: (i, 0), pipeline_mode=once),
                  pl.BlockSpec((1, d, tf), lambda i, e, j: (e, 0, j)),
                  pl.BlockSpec((1, d, tf), lambda i, e, j: (e, 0, j)),
                  pl.BlockSpec((1, tf, d), lambda i, e, j: (e, j, 0)),
                  pl.BlockSpec((1, d), lambda i, e, j: (0, 0)),
                  pl.BlockSpec((1, d), lambda i, e, j: (0, 0))],
        out_specs=pl.BlockSpec((tm, d), lambda i, e, j: (i, 0), pipeline_mode=once),
        out_shape=jax.ShapeDtypeStruct((m, d), F32),
        scratch_shapes=[pltpu.VMEM((tm, d), BF16), pltpu.VMEM((cap, d), BF16), pltpu.VMEM((cap, d), F32),
                        pltpu.VMEM((tm, LANES), F32), pltpu.VMEM((LANES, tm), F32), pltpu.VMEM((LANES, tm), F32)],
        compiler_params=_cparams(("parallel", "arbitrary", "arbitrary")),
        name="moe_sparse",
    )(x, comb, wg, wu, wd, g.reshape(1, d), b.reshape(1, d))


def _router_kernel(x_ref, r_ref, o_ref):
    logits = _dot_mid(x_ref[...], r_ref[...])
    lane = lax.broadcasted_iota(jnp.int32, logits.shape, 1)
    logits = jnp.where(lane < N_EXPERTS, logits, -jnp.inf)
    m1 = jnp.max(logits, axis=-1, keepdims=True)
    i1 = jnp.min(jnp.where(logits == m1, lane, LANES), axis=-1, keepdims=True)
    rest = jnp.where(lane == i1, -jnp.inf, logits)
    m2 = jnp.max(rest, axis=-1, keepdims=True)
    i2 = jnp.min(jnp.where(rest == m2, lane, LANES), axis=-1, keepdims=True)
    e2 = jnp.exp(m2 - m1)
    den = 1.0 + e2
    o_ref[...] = jnp.where(lane == i1, 1.0 / den, 0.0) + jnp.where(lane == i2, e2 / den, 0.0)


def _router(x, router):
    m, d = x.shape
    tm = _row_tile(m)
    r = jnp.pad(router, ((0, 0), (0, LANES - router.shape[1])))
    return pl.pallas_call(
        _router_kernel,
        grid=(m // tm,),
        in_specs=[pl.BlockSpec((tm, d), lambda i: (i, 0)), pl.BlockSpec((d, LANES), lambda i: (0, 0))],
        out_specs=pl.BlockSpec((tm, LANES), lambda i: (i, 0)),
        out_shape=jax.ShapeDtypeStruct((m, LANES), F32),
        compiler_params=_cparams(("parallel",)),
        name="router",
    )(x, r)


def _topk_lanes(g, idx, axis=-1):
    sel = jnp.zeros(g.shape, F32)
    for _ in range(MOBA_TOPK):
        m = jnp.max(g, axis=axis, keepdims=True)
        first = jnp.min(jnp.where(g == m, idx, LANES), axis=axis, keepdims=True)
        hit = jnp.logical_and(idx == first, m > -jnp.inf)
        sel = jnp.where(hit, 1.0, sel)
        g = jnp.where(hit, -jnp.inf, g)
    return sel


def _moba_kernel(slope_ref, q_ref, k_ref, v_ref, o_ref, kb_ref, vt_ref, km_ref, m_ref, l_ref, acc_ref, sel_ref,
                 *, tq, nb, q_offset, hd):
    i = pl.program_id(2)
    blk = MOBA_BLOCK

    @pl.when(i == 0)
    def _():
        kb_ref[...] = k_ref[...].astype(BF16)
        km_ref[...] = jnp.zeros_like(km_ref)
        for j in range(nb):
            rows = slice(j * blk, (j + 1) * blk)
            vt_ref[j] = v_ref[rows, :].T.astype(BF16)
            km_ref[j:j + 1, :] = jnp.sum(k_ref[rows, :], axis=0, keepdims=True) * (1.0 / blk)

    q0 = q_offset + i * tq
    own = q0 // blk
    q = q_ref[...]
    lane = lax.broadcasted_iota(jnp.int32, (tq, LANES), 1)
    q2 = jnp.concatenate([jnp.where(lane < hd, q, 0.0), jnp.where(lane >= hd, q, 0.0)], axis=0)
    gate_t = _dot_mid(km_ref[...], q2, dot=_dot_nt)
    blk_id = lax.broadcasted_iota(jnp.int32, gate_t.shape, 0)
    sel_ref[...] = _topk_lanes(jnp.where(blk_id < own, gate_t, -jnp.inf), blk_id, axis=0)
    qb = (q2 * hd ** -0.5).astype(BF16)
    c2 = lax.broadcasted_iota(jnp.int32, (blk, 2 * tq), 1)
    d0 = (jnp.where(c2 >= tq, c2 - tq, c2) - lax.broadcasted_iota(jnp.int32, (blk, 2 * tq), 0)).astype(F32)
    slope_lane = slope_ref[...]
    c1 = lax.broadcasted_iota(jnp.int32, (1, 2 * tq), 1)
    slope = jnp.where(c1 < tq, slope_lane[:, 0:1], slope_lane[:, hd:hd + 1])
    slope_d0 = slope * d0
    first = lax.broadcasted_iota(jnp.int32, (2 * hd, tq), 0) < hd
    m_ref[...] = jnp.full(m_ref.shape, NEG, F32)
    l_ref[...] = jnp.zeros_like(l_ref)
    acc_ref[...] = jnp.zeros_like(acc_ref)

    def tiles(js, diagonal):
        scores = []
        for j in js:
            off = (q0 - j * blk).astype(F32)
            s = _dot_nt(kb_ref[pl.ds(pl.multiple_of(j * blk, blk), blk), :], qb) - (slope_d0 + slope * off)
            if diagonal:
                s = jnp.where(d0 + off >= 0.0, s, NEG)
            else:
                s = jnp.where(sel_ref[pl.ds(j, 1), :] > 0.0, s, NEG)
            scores.append(s)
        m_old = m_ref[...]
        m_new = functools.reduce(jnp.maximum, [m_old] + [jnp.max(s, axis=0, keepdims=True) for s in scores])
        alpha = jnp.exp(m_old - m_new)
        l_new = alpha * l_ref[...]
        pv = None
        for j, s in zip(js, scores):
            p = jnp.exp(s - m_new)
            l_new = l_new + jnp.sum(p, axis=0, keepdims=True)
            d = _dot(vt_ref[j], p.astype(BF16))
            pv = d if pv is None else pv + d
        m_ref[...] = m_new
        l_ref[...] = l_new
        acc_ref[...] = (jnp.where(first, alpha[:, :tq], alpha[:, tq:]) * acc_ref[...]
                        + jnp.where(first, pv[:, :tq], pv[:, tq:]))

    tiles([own], True)

    def body(t, carry):
        tiles([own - 1 - 4 * t - n for n in range(4)], False)
        return carry

    lax.fori_loop(0, own // 4, body, 0)
    rest = own % 4

    @pl.when(rest >= 2)
    def _():
        tiles([rest - 1, rest - 2], False)

    @pl.when(rest % 2 == 1)
    def _():
        tiles([own - own], False)
    l = l_ref[...]
    o_ref[...] = (acc_ref[...] / jnp.where(first, l[:, :tq], l[:, tq:])).T


def _alibi_slopes_lanes(n_heads, hd):
    slopes = jnp.exp2(-8.0 * jnp.arange(1, n_heads + 1, dtype=F32) / n_heads)
    return jnp.repeat(slopes, hd).reshape(1, n_heads * hd)


def _attn_call(kernel, q, k, v, *, batch, q_offset, n_heads, extra_in=(), extra_specs=(), extra_scratch=(), name):
    lq, lk = q.shape[0] // batch, k.shape[0] // batch
    hd = q.shape[1] // n_heads
    assert 2 * hd == LANES and lk % MOBA_BLOCK == 0
    tq = min(MOBA_BLOCK, lq)
    assert lq % tq == 0 and MOBA_BLOCK % tq == 0 and q_offset % tq == 0
    nq = lq // tq
    return pl.pallas_call(
        functools.partial(kernel, tq=tq, q_offset=q_offset, hd=hd),
        grid=(batch, n_heads // 2, nq),
        in_specs=list(extra_specs) + [
            pl.BlockSpec((tq, LANES), lambda b, h, i: (b * nq + i, h)),
            pl.BlockSpec((lk, LANES), lambda b, h, i: (b, h)),
            pl.BlockSpec((lk, LANES), lambda b, h, i: (b, h))],
        out_specs=pl.BlockSpec((tq, LANES), lambda b, h, i: (b * nq + i, h)),
        out_shape=jax.ShapeDtypeStruct(q.shape, F32),
        scratch_shapes=list(extra_scratch),
        compiler_params=_cparams(("parallel", "parallel", "arbitrary")),
        name=name,
    )(*extra_in, q, k, v)


def _moba(q, k, v, *, batch, q_offset):
    nb = k.shape[0] // batch // MOBA_BLOCK
    assert nb <= LANES
    slopes = _alibi_slopes_lanes(H_A, q.shape[1] // H_A)
    tq = min(MOBA_BLOCK, q.shape[0] // batch)
    nb_rows = -(-nb // 8) * 8
    return _attn_call(
        functools.partial(_moba_kernel, nb=nb), q, k, v, batch=batch, q_offset=q_offset, n_heads=H_A,
        extra_in=(slopes,), extra_specs=(pl.BlockSpec((1, LANES), lambda b, h, i: (0, h)),),
        extra_scratch=(pltpu.VMEM((nb * MOBA_BLOCK, LANES), BF16), pltpu.VMEM((nb, LANES, MOBA_BLOCK), BF16),
                       pltpu.VMEM((nb_rows, LANES), F32), pltpu.VMEM((1, 2 * tq), F32), pltpu.VMEM((1, 2 * tq), F32),
                       pltpu.VMEM((LANES, tq), F32), pltpu.VMEM((nb_rows, 2 * tq), F32)), name="moba")


EXP_UNDERFLOW = -110.0


def _softplus(z):
    return jnp.maximum(z, 0.0) + jnp.log(1.0 + jnp.exp(-jnp.abs(z)))


def _stick_kernel(q_ref, k_ref, v_ref, o_ref, kb_ref, vb_ref, c_ref, acc_ref, *, tq, q_offset, hd):
    i = pl.program_id(2)
    blk = MOBA_BLOCK

    @pl.when(i == 0)
    def _():
        kb_ref[...] = k_ref[...].astype(BF16)
        vb_ref[...] = v_ref[...].astype(BF16)

    q0 = q_offset + i * tq
    own = q0 // blk
    q = q_ref[...] * hd ** -0.5
    lane = lax.broadcasted_iota(jnp.int32, (tq, LANES), 1)
    qb = jnp.concatenate([jnp.where(lane < hd, q, 0.0), jnp.where(lane >= hd, q, 0.0)], axis=0).astype(BF16)
    r2 = lax.broadcasted_iota(jnp.int32, (2 * tq, blk), 0)
    d0 = jnp.where(r2 >= tq, r2 - tq, r2) - lax.broadcasted_iota(jnp.int32, (2 * tq, blk), 1)
    u = jnp.where(lax.broadcasted_iota(jnp.int32, (blk, blk), 0) > lax.broadcasted_iota(jnp.int32, (blk, blk), 1),
                  1.0, 0.0).astype(BF16)
    c_ref[...] = jnp.zeros_like(c_ref)
    acc_ref[...] = jnp.zeros_like(acc_ref)

    halves = [slice(0, tq), slice(tq, 2 * tq)]

    def tile(j, diagonal):
        start = pl.multiple_of(j * blk, blk)
        kj = kb_ref[pl.ds(start, blk), :]
        zs = [_dot_nt(qb[h], kj) for h in halves]
        log_keep = [-_softplus(z) for z in zs]
        log_beta = [z + lk for z, lk in zip(zs, log_keep)]
        if diagonal:
            before = d0[halves[0]] + (q0 - j * blk) > 0
            log_keep = [jnp.where(before, lk, 0.0) for lk in log_keep]
        his = [lk.astype(BF16) for lk in log_keep]
        los = [(lk - hi.astype(F32)).astype(BF16) for lk, hi in zip(log_keep, his)]
        later = [_dot(jnp.concatenate([hi, lo], axis=0), u) for hi, lo in zip(his, los)]
        later = [t[:tq] + t[tq:] for t in later]
        cs = [c_ref[h, :] for h in halves]
        ws = [jnp.exp(lb + lt + c) for lb, lt, c in zip(log_beta, later, cs)]
        if diagonal:
            ws = [jnp.where(before, w, 0.0) for w in ws]
        for h, c, lt, lk in zip(halves, cs, later, log_keep):
            c_ref[h, :] = c + lt[:, 0:1] + lk[:, 0:1]
        vj = vb_ref[pl.ds(start, blk), :]
        pv = [_dot(w.astype(BF16), vj) for w in ws]
        acc_ref[...] += jnp.where(lane < hd, pv[0], pv[1])

    tile(own, True)

    def more(carry):
        t, c_max = carry
        return jnp.logical_and(t <= own, c_max > EXP_UNDERFLOW)

    def body(carry):
        t, _ = carry
        tile(own - t, False)
        return t + 1, jnp.max(c_ref[...])

    lax.while_loop(more, body, (jnp.int32(1), jnp.max(c_ref[...])))
    o_ref[...] = acc_ref[...]


def _stick(q, k, v, *, batch, q_offset):
    tq = min(MOBA_BLOCK, q.shape[0] // batch)
    lk = k.shape[0] // batch
    return _attn_call(_stick_kernel, q, k, v, batch=batch, q_offset=q_offset, n_heads=H_D,
                      extra_scratch=(pltpu.VMEM((lk, LANES), BF16), pltpu.VMEM((lk, LANES), BF16),
                                     pltpu.VMEM((2 * tq, 1), F32), pltpu.VMEM((tq, LANES), F32)), name="stick")


CHUNK = 64
SUB = 16
SEQ_BLOCK = 512
EXP_CLAMP = 80.0


def _dot_mid(a, b, dot=_dot):
    a0 = a.astype(BF16)
    a1 = (a - a0.astype(F32)).astype(BF16)
    b0 = b.astype(BF16)
    b1 = (b - b0.astype(F32)).astype(BF16)
    return dot(a0, b0) + (dot(a0, b1) + dot(a1, b0))


def _incl_lower(c):
    r = lax.broadcasted_iota(jnp.int32, (c, c), 0)
    s = lax.broadcasted_iota(jnp.int32, (c, c), 1)
    return r, s


def _hgrn_kernel(q_ref, f_ref, i_ref, g_ref, lb_ref, gn_ref, s0_ref, o_ref, s_ref, st_ref, *, c, n_chunks, n_heads,
                 layer):
    l = pl.program_id(1)
    dk = LANES

    w = n_heads * dk

    @pl.when(l == 0)
    def _():
        for h in range(n_heads):
            st_ref[h * dk:(h + 1) * dk, :] = s0_ref[0, h]

    r, s = _incl_lower(c)
    tri = jnp.where(s <= r, 1.0, 0.0).astype(BF16)
    sub = min(SUB, c)
    n = n_heads * c
    shift_c, shift_s = c.bit_length() - 1, sub.bit_length() - 1
    ri = lax.broadcasted_iota(jnp.int32, (n_heads * sub, n), 0)
    ci = lax.broadcasted_iota(jnp.int32, (n_heads * sub, n), 1)
    pair = jnp.right_shift(ri, shift_s) == jnp.right_shift(ci, shift_c)
    t_loc, s_loc = jnp.bitwise_and(ri, sub - 1), jnp.bitwise_and(ci, c - 1)
    own_cols = (jnp.right_shift(lax.broadcasted_iota(jnp.int32, (n, w), 0), shift_c)
                == lax.broadcasted_iota(jnp.int32, (n, w), 1) // dk)
    gn = gn_ref[...]
    e = jnp.exp(lb_ref[...] - jnp.max(lb_ref[...], axis=0, keepdims=True))
    lb = jnp.sum(e[:layer + 1], axis=0, keepdims=True) / jnp.sum(e, axis=0, keepdims=True)

    def stack(t):
        return jnp.concatenate([t[:, h * dk:(h + 1) * dk] for h in range(n_heads)], axis=0)

    def run(chunks):
        m = len(chunks)
        rows = [pl.ds(i * c, c) if isinstance(i, int) else pl.ds(pl.multiple_of(i * c, c), c) for i in chunks]
        zs = [f_ref[r_, :] for r_ in rows]
        ks = [(1.0 - lb) * jax.nn.sigmoid(-z) for z in zs]
        qs = [jax.nn.silu(q_ref[r_, :]) for r_ in rows]
        cgs = [_dot_exact_lhs(tri, jnp.log(lb + (1.0 - lb) * jax.nn.sigmoid(z))) for z in zs]
        v4s = [stack(i_ref[r_, :]).astype(BF16) for r_ in rows]
        intra = [[] for _ in range(m)]
        for b in range(c // sub):
            lo, hi = b * sub, (b + 1) * sub
            refs = [cg[lo - 1:lo] if b else jnp.zeros((1, w), F32) for cg in cgs]
            q_sub = [stack(q[lo:hi] * jnp.exp(cg[lo:hi] - ref_pt)).astype(BF16) for q, cg, ref_pt in zip(qs, cgs, refs)]
            k_all = [stack(k * jnp.exp(jnp.minimum(ref_pt - cg, EXP_CLAMP))).astype(BF16)
                     for k, cg, ref_pt in zip(ks, cgs, refs)]
            a = [_dot_nt(q_, k_) for q_, k_ in zip(q_sub, k_all)]
            a = [jnp.where(jnp.logical_and(pair, s_loc <= t_loc + lo), a_, 0.0).astype(BF16) for a_ in a]
            for x_, (a_, v4) in enumerate(zip(a, v4s)):
                intra[x_].append(_dot(a_, v4))
        q_dec = [stack(q * jnp.exp(cg)).astype(BF16) for q, cg in zip(qs, cgs)]
        lasts = [cg[c - 1:c] for cg in cgs]
        kts = [stack(k * jnp.exp(last - cg)).astype(BF16) for k, last, cg in zip(ks, lasts, cgs)]
        for x_ in range(m):
            st = st_ref[...]
            from_state = _dot_nt(q_dec[x_], st.astype(BF16))
            v_wide = jnp.where(own_cols, jnp.concatenate([v4s[x_]] * n_heads, axis=1), 0.0)
            e_last = jnp.concatenate([jnp.broadcast_to(jnp.exp(lasts[x_][:, h * dk:(h + 1) * dk]), (dk, dk))
                                      for h in range(n_heads)], axis=0)
            st_ref[...] = st * e_last + _dot_tn(v_wide, kts[x_])
            for h in range(n_heads):
                o = from_state[h * c:(h + 1) * c, h * dk:(h + 1) * dk] + jnp.concatenate(
                    [part[h * sub:(h + 1) * sub] for part in intra[x_]], axis=0)
                o = o * lax.rsqrt(jnp.mean(o * o, axis=-1, keepdims=True) + RMS_EPS) * gn
                o_ref[rows[x_], h * dk:(h + 1) * dk] = o * jax.nn.sigmoid(g_ref[rows[x_], h * dk:(h + 1) * dk])

    per_trip = 2 if n_chunks % 2 == 0 else 1

    def trip(i, carry):
        run([per_trip * i + x_ for x_ in range(per_trip)])
        return carry

    lax.fori_loop(0, n_chunks // per_trip, trip, 0)

    @pl.when(l == pl.num_programs(1) - 1)
    def _():
        for h in range(n_heads):
            s_ref[0, h] = st_ref[h * dk:(h + 1) * dk, :]


def _hgrn(hg, lb, gn, s0, *, batch, layer=0):
    t, width = hg.shape
    seq = t // batch
    w = width // 4
    n_heads = w // LANES
    c = min(CHUNK, seq)
    lblk = min(SEQ_BLOCK, seq)
    assert seq % lblk == 0 and lblk % c == 0 and c % min(SUB, c) == 0
    nl = seq // lblk
    part = lambda p: pl.BlockSpec((lblk, w), lambda b, l: (b * nl + l, p))
    state = pl.BlockSpec((1, n_heads, LANES, LANES), lambda b, l: (b, 0, 0, 0))
    o, st = pl.pallas_call(
        functools.partial(_hgrn_kernel, c=c, n_chunks=lblk // c, n_heads=n_heads, layer=layer),
        grid=(batch, nl),
        in_specs=[part(0), part(1), part(2), part(3),
                  pl.BlockSpec(lb.shape, lambda b, l: (0, 0)), pl.BlockSpec((1, LANES), lambda b, l: (0, 0)), state],
        out_specs=[pl.BlockSpec((lblk, w), lambda b, l: (b * nl + l, 0)), state],
        out_shape=[jax.ShapeDtypeStruct((t, w), F32), jax.ShapeDtypeStruct(s0.shape, F32)],
        scratch_shapes=[pltpu.VMEM((n_heads * LANES, LANES), F32)],
        compiler_params=_cparams(("parallel", "arbitrary")),
        name="hgrn2",
    )(hg, hg, hg, hg, lb, gn.reshape(1, LANES), jnp.swapaxes(s0, -1, -2))
    return o, jnp.swapaxes(st, -1, -2)


def _unit_lower_inverses(lms, n, period):
    bs = min(SUB, period)
    r, s = _incl_lower(n)
    eye = jnp.where(r == s, 1.0, 0.0)
    shift = bs.bit_length() - 1
    same = jnp.right_shift(r, shift) == jnp.right_shift(s, shift)

    def mm(a, b):
        return _dot(a.astype(BF16), b.astype(BF16))

    def neumann(mats, order):
        invs, pws, k = [eye + a for a in mats], list(mats), 2
        while k < order:
            pws = [mm(pw, pw) for pw in pws]
            invs = [inv + mm(inv, pw) for inv, pw in zip(invs, pws)]
            k *= 2
        return invs

    inv_ds = neumann([-jnp.where(same, lm, 0.0) for lm in lms], bs)
    if period == bs:
        return inv_ds
    ms = [mm(inv_d, jnp.where(same, 0.0, lm)) for inv_d, lm in zip(inv_ds, lms)]
    return [mm(a, inv_d) for a, inv_d in zip(neumann([-m for m in ms], period // bs), inv_ds)]


def _gdn_kernel(x_ref, ab_ref, g_ref, cw_ref, al_ref, dtb_ref, gn_ref, s0_ref, cb_ref, o_ref, s_ref, nb_ref,
                xx_ref, y_ref, st_ref, *, c, n_chunks, n_heads, lblk):
    l = pl.program_id(1)
    dk = LANES
    halo = 8
    w = n_heads * dk

    @pl.when(l == 0)
    def _():
        for h in range(n_heads):
            st_ref[:, h * dk:(h + 1) * dk] = s0_ref[0, h]
        xx_ref[halo - (CONV_W - 1):halo, :] = cb_ref[0]

    xx_ref[halo:halo + lblk, :] = x_ref[...]
    y = xx_ref[halo:halo + lblk, :] * cw_ref[CONV_W - 1:CONV_W, :]
    for i in range(CONV_W - 1):
        off = halo - (CONV_W - 1) + i
        y = y + xx_ref[off:off + lblk, :] * cw_ref[i:i + 1, :]
    y_ref[...] = jax.nn.silu(y)
    tail = xx_ref[halo + lblk - (CONV_W - 1):halo + lblk, :]
    xx_ref[halo - (CONV_W - 1):halo, :] = tail

    r, s = _incl_lower(c)
    tri = jnp.where(s <= r, 1.0, 0.0).astype(BF16)
    n = n_heads * c
    rr, ss = _incl_lower(n)
    shift = c.bit_length() - 1
    same_head = jnp.right_shift(rr, shift) == jnp.right_shift(ss, shift)
    incl = jnp.logical_and(same_head, ss <= rr)
    strict = jnp.logical_and(same_head, ss < rr)
    own_cols = (jnp.right_shift(lax.broadcasted_iota(jnp.int32, (n, w), 0), shift)
                == lax.broadcasted_iota(jnp.int32, (n, w), 1) // dk)
    gn = gn_ref[...]

    def l2n(t):
        return t * lax.rsqrt(jnp.sum(t * t, axis=-1, keepdims=True) + RMS_EPS)

    def stack(f):
        return jnp.concatenate([f(h) for h in range(n_heads)], axis=0)

    def own_block(t):
        return stack(lambda h: t[h * c:(h + 1) * c, h * dk:(h + 1) * dk])

    def front(i):
        rows = pl.ds(i * c, c) if isinstance(i, int) else pl.ds(pl.multiple_of(i * c, c), c)
        ab = ab_ref[rows, :]
        pre = ab + dtb_ref[...]
        log_a = -jnp.exp(al_ref[...]) * (jnp.maximum(pre, 0.0) + jnp.log(1.0 + jnp.exp(-jnp.abs(pre))))
        beta_all = jax.nn.sigmoid(ab)
        cg_all = _dot_exact_lhs(tri, log_a)
        cg_t = cg_all.T
        q = stack(lambda h: l2n(y_ref[rows, h * dk:(h + 1) * dk])) * dk ** -0.5
        k = stack(lambda h: l2n(y_ref[rows, w + h * dk:w + (h + 1) * dk]))
        v = stack(lambda h: y_ref[rows, 2 * w + h * dk:2 * w + (h + 1) * dk])
        cg = stack(lambda h: cg_all[:, h:h + 1])
        beta = stack(lambda h: beta_all[:, n_heads + h:n_heads + h + 1])
        cg_row = jnp.concatenate([cg_t[h:h + 1, :c] for h in range(n_heads)], axis=1)
        last = stack(lambda h: jnp.broadcast_to(cg_all[c - 1:c, h:h + 1], (c, 1)))
        decay = jnp.exp(jnp.where(incl, cg - cg_row, NEG))
        kk = _dot_mid(k, k, dot=_dot_nt) * decay * beta
        return jnp.where(strict, kk, 0.0), (rows, q, k, v, cg, beta, last, decay, cg_all)

    def back(side, t_inv):
        rows, q, k, v, cg, beta, last, decay, cg_all = side
        e_cg = jnp.exp(cg)
        sol = _dot_mid(t_inv, jnp.concatenate([v * beta, k * (beta * e_cg)], axis=1))
        qk = (_dot_nt(q.astype(BF16), k.astype(BF16)) * decay).astype(BF16)
        kt = (k * jnp.exp(last - cg)).astype(BF16)
        e_last = jnp.concatenate([jnp.broadcast_to(jnp.exp(cg_all[c - 1:c, h:h + 1]), (1, dk)) for h in range(n_heads)],
                                 axis=1)
        return rows, sol[:, :dk], sol[:, dk:].astype(BF16), (q * e_cg).astype(BF16), qk, kt, e_last

    def advance(prepared):
        rows, sol_v, sol_k, q_decayed, qk, kt, e_last = prepared
        st = st_ref[...]
        stb = st.astype(BF16)
        u = sol_v - own_block(_dot(sol_k, stb))
        o = own_block(_dot(q_decayed, stb)) + _dot(qk, u.astype(BF16))
        u_wide = jnp.where(own_cols, jnp.concatenate([u] * n_heads, axis=1), 0.0)
        st_ref[...] = st * e_last + _dot_tn(kt, u_wide.astype(BF16))
        o = o * lax.rsqrt(jnp.mean(o * o, axis=-1, keepdims=True) + RMS_EPS) * gn
        for h in range(n_heads):
            o_ref[rows, h * dk:(h + 1) * dk] = o[h * c:(h + 1) * c] * jax.nn.silu(g_ref[rows, h * dk:(h + 1) * dk])

    def run(chunks):
        fronts = [front(i) for i in chunks]
        inverses = _unit_lower_inverses([lm for lm, _ in fronts], n, c)
        for prepared in [back(side, t_inv) for (_, side), t_inv in zip(fronts, inverses)]:
            advance(prepared)

    per_trip = 4 if n_chunks % 4 == 0 else 2

    def trip(i, carry):
        run([per_trip * i + m for m in range(per_trip)])
        return carry

    lax.fori_loop(0, n_chunks // per_trip, trip, 0)
    if n_chunks % per_trip:
        run([n_chunks - 1])

    @pl.when(l == pl.num_programs(1) - 1)
    def _():
        for h in range(n_heads):
            s_ref[0, h] = st_ref[:, h * dk:(h + 1) * dk]
        nb_ref[0] = tail


def _gdn(qkv, ab, g, conv_w, a_log, dt_bias, gn, s0, conv_buf, *, batch):
    t, width = qkv.shape
    seq = t // batch
    w = width // 3
    n_heads = w // LANES
    c = min(CHUNK, seq)
    lblk = min(SEQ_BLOCK, seq)
    assert seq % lblk == 0 and lblk % c == 0 and lblk >= CONV_W - 1
    nl = seq // lblk
    pad_lane = lambda vec: jnp.pad(vec, (0, LANES - vec.shape[0])).reshape(1, LANES)
    rows = lambda width_: pl.BlockSpec((lblk, width_), lambda b, l: (b * nl + l, 0))
    const = lambda shape: pl.BlockSpec(shape, lambda b, l: (0,) * len(shape))
    state = pl.BlockSpec((1, n_heads, LANES, LANES), lambda b, l: (b, 0, 0, 0))
    buf = pl.BlockSpec((1, CONV_W - 1, width), lambda b, l: (b, 0, 0))
    return pl.pallas_call(
        functools.partial(_gdn_kernel, c=c, n_chunks=lblk // c, n_heads=n_heads, lblk=lblk),
        grid=(batch, nl),
        in_specs=[rows(width), rows(LANES), rows(w), const((CONV_W, width)), const((1, LANES)), const((1, LANES)),
                  const((1, LANES)), state, buf],
        out_specs=[rows(w), state, buf],
        out_shape=[jax.ShapeDtypeStruct((t, w), F32), jax.ShapeDtypeStruct(s0.shape, F32),
                   jax.ShapeDtypeStruct(conv_buf.shape, F32)],
        scratch_shapes=[pltpu.VMEM((lblk + 8, width), F32), pltpu.VMEM((lblk, width), F32),
                        pltpu.VMEM((LANES, w), F32)],
        compiler_params=_cparams(("parallel", "arbitrary")),
        name="gated_deltanet",
    )(qkv, ab, g, conv_w, pad_lane(a_log), pad_lane(dt_bias), gn.reshape(1, LANES), s0, conv_buf)


PAGES_PER_STEP = 16


def _head_fold(pv, n_heads, nq, hd):
    return jnp.concatenate([pv[h * nq:(h + 1) * nq, h * hd:(h + 1) * hd] for h in range(n_heads)], axis=0)


def _row_ids(rows, nq):
    r = lax.broadcasted_iota(jnp.int32, (rows, 1), 0)
    return r // nq, r % nq


def _page_group(refs, first, count):
    return jnp.concatenate([refs[first + r][0, 0].reshape(-1, refs[first + r].shape[-1]) for r in range(count)], axis=1)


def _stick_paged_kernel(pt_ref, qbd_ref, kn_ref, vn_ref, k_hbm, v_hbm, o_ref, kbuf, vbuf, sem, *, n_heads, nq, hd,
                        n_pages):
    b = pl.program_id(0)
    rows = n_heads * nq
    page = kbuf.shape[-1]
    ppg = kbuf.shape[1]
    n_groups = n_pages // ppg
    _, row_q = _row_ids(rows, nq)
    qbd = (qbd_ref[0] * hd ** -0.5).astype(BF16)

    def copies(g, slot):
        out = []
        for r in range(ppg):
            pg = pt_ref[b, g * ppg + r]
            out.append(pltpu.make_async_copy(k_hbm.at[0, pg], kbuf.at[slot, r], sem.at[0, slot, r]))
            out.append(pltpu.make_async_copy(v_hbm.at[0, pg], vbuf.at[slot, r], sem.at[1, slot, r]))
        return out

    def group(buf, slot):
        return jnp.concatenate([buf[slot, r].reshape(n_heads * hd, page) for r in range(ppg)], axis=1).astype(BF16)

    def strict_later(n):
        return jnp.where(lax.broadcasted_iota(jnp.int32, (n, n), 0) > lax.broadcasted_iota(jnp.int32, (n, n), 1),
                         1.0, 0.0).astype(BF16)

    def weights(z, mask, carry, u):
        log_keep = -_softplus(z)
        log_beta = z + log_keep
        if mask is not None:
            log_keep = jnp.where(mask, log_keep, 0.0)
        hi = log_keep.astype(BF16)
        lo = (log_keep - hi.astype(F32)).astype(BF16)
        later = _dot(hi, u) + _dot(lo, u)
        w = jnp.exp(log_beta + later + carry)
        if mask is not None:
            w = jnp.where(mask, w, 0.0)
        return w.astype(BF16), carry + jnp.sum(log_keep, axis=-1, keepdims=True)

    for cp in copies(n_groups - 1, 0):
        cp.start()

    n = kn_ref.shape[1]
    z = _dot_nt(qbd, kn_ref[0].astype(BF16))
    col = lax.broadcasted_iota(jnp.int32, (rows, n), 1)
    w, c = weights(z, col < row_q, jnp.zeros((rows, 1), F32), strict_later(n))
    acc = _dot(w, vn_ref[0].astype(BF16))
    u = strict_later(ppg * page)

    def more(carry):
        g, c, _ = carry
        return jnp.logical_and(g >= 0, jnp.max(c) > EXP_UNDERFLOW)

    def body(carry):
        g, c, acc = carry
        slot = (n_groups - 1 - g) % 2

        @pl.when(g > 0)
        def _():
            for cp in copies(g - 1, 1 - slot):
                cp.start()

        for cp in copies(g, slot):
            cp.wait()
        w, c = weights(_dot(qbd, group(kbuf, slot)), None, c, u)
        return g - 1, c, acc + _dot_nt(w, group(vbuf, slot))

    g, _, acc = lax.while_loop(more, body, (jnp.int32(n_groups - 1), c, acc))

    @pl.when(g >= 0)
    def _():
        for cp in copies(g, (n_groups - 1 - g) % 2):
            cp.wait()

    o_ref[0] = _head_fold(acc, n_heads, nq, hd)


def _moba_paged_kernel(pt_ref, qbd_ref, kn_ref, vn_ref, *rest, pps, n_heads, nq, hd, past_len):
    k_refs, v_refs = rest[:pps], rest[pps:2 * pps]
    o_ref, mo_ref, lo_ref, acco_ref, m_ref, l_ref, acc_ref, km_ref = rest[2 * pps:]
    s = pl.program_id(1)
    rows = n_heads * nq
    page = k_refs[0].shape[-1]
    ppb = MOBA_BLOCK // page
    row_h, row_q = _row_ids(rows, nq)
    slope = jnp.exp2(-8.0 * (row_h + 1).astype(F32) / n_heads)
    lane = lax.broadcasted_iota(jnp.int32, (rows, LANES), 1)
    q_f32 = qbd_ref[0]
    qbd = (q_f32 * hd ** -0.5).astype(BF16)

    @pl.when(s == 0)
    def _():
        n = kn_ref.shape[1]
        col = lax.broadcasted_iota(jnp.int32, (rows, n), 1)
        sc = _dot_nt(qbd, kn_ref[0].astype(BF16)) - slope * (row_q - col).astype(F32)
        sc = jnp.where(col <= row_q, sc, NEG)
        m = jnp.max(sc, axis=-1, keepdims=True)
        p = jnp.exp(sc - m)
        mo_ref[...] = m
        lo_ref[...] = jnp.sum(p, axis=-1, keepdims=True)
        acco_ref[...] = _head_fold(_dot(p.astype(BF16), vn_ref[0].astype(BF16)), n_heads, nq, hd)
        m_ref[...] = jnp.full(m_ref.shape, NEG, F32)
        l_ref[...] = jnp.zeros_like(l_ref)
        km_ref[...] = jnp.zeros_like(km_ref)

    @pl.when(s > 0)
    def _():
        key = lax.broadcasted_iota(jnp.int32, (rows, MOBA_BLOCK), 1)
        m_all, l_all, km = m_ref[...], l_ref[...], km_ref[...]
        lane_k = lax.broadcasted_iota(jnp.int32, km.shape, 1)
        blocks = [(s - 1) * (pps // ppb) + bi for bi in range(pps // ppb)]
        kfs = [_page_group(k_refs, bi * ppb, ppb) for bi in range(len(blocks))]
        raw = [_dot(qbd, kf.astype(BF16)) for kf in kfs]
        means = [jnp.sum(kf, axis=-1, keepdims=True) * (1.0 / MOBA_BLOCK) for kf in kfs]
        scs = [sc - slope * (past_len + row_q - (blk * MOBA_BLOCK + key)).astype(F32) for sc, blk in zip(raw, blocks)]
        m_bs = [jnp.max(sc, axis=-1, keepdims=True) for sc in scs]
        ps = [jnp.exp(sc - m_b) for sc, m_b in zip(scs, m_bs)]
        pvs = [_dot_nt(p.astype(BF16), _page_group(v_refs, bi * ppb, ppb).astype(BF16)) for bi, p in enumerate(ps)]
        for blk, mean, m_b, p, pv in zip(blocks, means, m_bs, ps, pvs):
            km = jnp.where(lane_k == blk, mean, km)
            acc_ref[blk] = _head_fold(pv, n_heads, nq, hd)
            m_all = jnp.where(lane == blk, m_b, m_all)
            l_all = jnp.where(lane == blk, jnp.sum(p, axis=-1, keepdims=True), l_all)
        m_ref[...] = m_all
        l_ref[...] = l_all
        km_ref[...] = km

    @pl.when(s == pl.num_programs(1) - 1)
    def _():
        n_blocks = past_len // MOBA_BLOCK
        gate = _dot_mid(q_f32, km_ref[...])
        sel = _topk_lanes(jnp.where(lane < n_blocks, gate, -jnp.inf), lane) > 0.0
        m_all, l_all = m_ref[...], l_ref[...]
        m_o = mo_ref[...]
        m_tot = jnp.maximum(m_o, jnp.max(jnp.where(sel, m_all, NEG), axis=-1, keepdims=True))
        wgt = jnp.where(sel, jnp.exp(m_all - m_tot), 0.0)
        a_o = jnp.exp(m_o - m_tot)
        l_tot = lo_ref[...] * a_o + jnp.sum(wgt * l_all, axis=-1, keepdims=True)
        acc = acco_ref[...] * a_o
        for b in range(n_blocks):
            acc = acc + wgt[:, b:b + 1] * acc_ref[b]
        o_ref[0] = acc / l_tot


def _paged_attention(kind, q, k_new, v_new, k_pool, v_pool, page_table, *, n_heads):
    batch, n_pages = page_table.shape
    page, hd = k_pool.shape[2], k_pool.shape[4]
    nq = q.shape[0] // batch
    rows = n_heads * nq
    pps = PAGES_PER_STEP
    past_len = n_pages * page
    assert n_pages % pps == 0 and past_len % MOBA_BLOCK == 0 and MOBA_BLOCK % page == 0 and pps % (MOBA_BLOCK // page) == 0
    assert nq <= page and past_len // MOBA_BLOCK <= LANES and (2 * LANES) % n_heads == 0
    n_groups = n_pages // pps
    q4 = q.reshape(batch, nq, n_heads, hd).transpose(0, 2, 1, 3)
    q_bd = (q4[:, :, :, None, :] * jnp.eye(n_heads, dtype=F32)[None, :, None, :, None]).reshape(batch, rows, n_heads * hd)
    pad_new = lambda t: jnp.pad(t.reshape(batch, nq, n_heads * hd), ((0, 0), (0, page - nq), (0, 0)))
    k_t, v_t = (jnp.transpose(t, (0, 1, 3, 4, 2)) for t in (k_pool, v_pool))

    if kind == "stick":
        ppg = MOBA_BLOCK // page
        seq = lambda shape: pl.BlockSpec((1,) + shape, lambda b, pt: (b, 0, 0))
        buf = pltpu.VMEM((2, ppg, n_heads, hd, page), F32)
        out = pl.pallas_call(
            functools.partial(_stick_paged_kernel, n_heads=n_heads, nq=nq, hd=hd, n_pages=n_pages),
            grid_spec=pltpu.PrefetchScalarGridSpec(
                num_scalar_prefetch=1,
                grid=(batch,),
                in_specs=[seq((rows, n_heads * hd)), seq((page, n_heads * hd)), seq((page, n_heads * hd)),
                          pl.BlockSpec(memory_space=pl.ANY), pl.BlockSpec(memory_space=pl.ANY)],
                out_specs=seq((rows, hd)),
                scratch_shapes=[buf, buf, pltpu.SemaphoreType.DMA((2, 2, ppg))]),
            out_shape=jax.ShapeDtypeStruct((batch, rows, hd), F32),
            compiler_params=_cparams(("arbitrary",)),
            name="stick_paged",
        )(page_table, q_bd, pad_new(k_new), pad_new(v_new), k_t, v_t)
    else:
        group = lambda s: jnp.maximum(s, 1) - 1
        scratch = [pltpu.VMEM((rows, 1), F32), pltpu.VMEM((rows, 1), F32), pltpu.VMEM((rows, hd), F32),
                   pltpu.VMEM((rows, LANES), F32), pltpu.VMEM((rows, LANES), F32),
                   pltpu.VMEM((past_len // MOBA_BLOCK, rows, hd), F32), pltpu.VMEM((n_heads * hd, LANES), F32)]

        def page_spec(r):
            return pl.BlockSpec((1, 1, n_heads, hd, page), lambda b, s, pt: (0, pt[b, group(s) * pps + r], 0, 0, 0))

        per_seq = lambda shape: pl.BlockSpec((1,) + shape, lambda b, s, pt: (b, 0, 0))
        out = pl.pallas_call(
            functools.partial(_moba_paged_kernel, pps=pps, n_heads=n_heads, nq=nq, hd=hd, past_len=past_len),
            grid_spec=pltpu.PrefetchScalarGridSpec(
                num_scalar_prefetch=1,
                grid=(batch, n_groups + 1),
                in_specs=[per_seq((rows, n_heads * hd)), per_seq((page, n_heads * hd)),
                          per_seq((page, n_heads * hd))] + [page_spec(r) for r in range(pps)] * 2,
                out_specs=per_seq((rows, hd)),
                scratch_shapes=scratch),
            out_shape=jax.ShapeDtypeStruct((batch, rows, hd), F32),
            compiler_params=_cparams(("parallel", "arbitrary")),
            name="moba_paged",
        )(page_table, q_bd, pad_new(k_new), pad_new(v_new), *([k_t] * pps), *([v_t] * pps))
    return out.reshape(batch, n_heads, nq, hd).transpose(0, 2, 1, 3).reshape(batch * nq, n_heads * hd)


def _layer_stack(x, batch, past, states, w):
    s_hgrn, s_gdn, conv_buf = states

    def attend(kind, q, k, v, pools, n_heads):
        if past is None:
            return (_moba if kind == "moba" else _stick)(q, k, v, batch=batch, q_offset=0)
        return _paged_attention(kind, q, k, v, *pools, past[4], n_heads=n_heads)

    (q_a, k_a, v_a, hg), kv_a_t = _project(x, w["in_a"] + [w["in_hgrn"]], batch, transposed=(1, 2))
    o_a = attend("moba", q_a, k_a, v_a, past and past[0:2], H_A)
    o_b, s_hgrn_new = _hgrn(hg, w["hgrn_lb"], w["hgrn_norm"], s_hgrn, batch=batch, layer=0)
    x = _matmul([o_a, o_b], w["out_even"], ln_args=(x, w["ln1_g"][0], w["ln1_b"][0]))
    x = _ffn(x, None, w["ffn_wg"], w["ffn_wu"], w["ffn_wd"], w["ln2_g"][0], w["ln2_b"][0])

    (qkv_c, ab, g_c, q_d, k_d, v_d), kv_d_t = _project(x, [w["in_qkv_c"], w["in_ab"], w["in_g_c"]] + w["in_d"], batch,
                                                       transposed=(4, 5))
    o_c, s_gdn_new, conv_new = _gdn(qkv_c, ab, g_c, w["conv_w"], w["a_log"], w["dt_bias"], w["gdn_norm"],
                                    s_gdn, conv_buf, batch=batch)
    o_d = attend("stick", q_d, k_d, v_d, past and past[2:4], H_D)
    x = _matmul([o_c, o_d], w["out_odd"], ln_args=(x, w["ln1_g"][1], w["ln1_b"][1]))
    comb = _router(x, w["router"])
    x = _moe(x, comb, w["moe_wg"], w["moe_wu"], w["moe_wd"], w["ln2_g"][1], w["ln2_b"][1])
    k_a, v_a = kv_a_t or (k_a, v_a)
    k_d, v_d = kv_d_t or (k_d, v_d)
    return x, k_a, v_a, s_hgrn_new, s_gdn_new, conv_new, k_d, v_d


def kernel(x_prompt, x_sample, cache_k_moba, cache_v_moba, state_hgrn, state_gdn, state_gdn_conv, cache_k_sb,
           cache_v_sb, page_table, w_in_even, w_out_even, hgrn_lb, hgrn_norm, w_in_odd, w_out_odd, gdn_conv_w,
           gdn_a_log, gdn_dt_bias, gdn_norm, ln1_g, ln1_b, ln2_g, ln2_b, ffn_wg, ffn_wu, ffn_wd, router, moe_wg,
           moe_wu, moe_wd):
    assert w_in_even.shape[0] == 1 and w_in_odd.shape[0] == 1
    bp, lp, d = x_prompt.shape
    bs, ls, _ = x_sample.shape
    mix = d // 2
    hd_a, hd_d = mix // H_A, mix // H_D
    conv_dim = gdn_conv_w.shape[-1]
    bf = lambda t: t.astype(BF16)
    cols = lambda wt, lo, n: bf(wt[:, lo:lo + n])

    wie, wio = w_in_even[0], w_in_odd[0]
    g_lo = conv_dim + 2 * H_C
    d_lo = g_lo + mix
    w = {
        "in_a": [cols(wie, i * mix, mix) for i in range(3)],
        "in_hgrn": cols(wie, 3 * mix, 4 * mix),
        "out_even": [bf(w_out_even[0][:mix]), bf(w_out_even[0][mix:])],
        "hgrn_lb": hgrn_lb, "hgrn_norm": hgrn_norm[0],
        "in_qkv_c": cols(wio, 0, conv_dim),
        "in_ab": bf(jnp.pad(wio[:, conv_dim:g_lo], ((0, 0), (0, LANES - 2 * H_C)))),
        "in_g_c": cols(wio, g_lo, mix),
        "in_d": [cols(wio, d_lo + i * mix, mix) for i in range(3)],
        "out_odd": [bf(w_out_odd[0][:mix]), bf(w_out_odd[0][mix:])],
        "conv_w": gdn_conv_w[0], "a_log": gdn_a_log[0], "dt_bias": gdn_dt_bias[0], "gdn_norm": gdn_norm[0],
        "ln1_g": ln1_g, "ln1_b": ln1_b, "ln2_g": ln2_g, "ln2_b": ln2_b,
        "ffn_wg": bf(ffn_wg), "ffn_wu": bf(ffn_wu), "ffn_wd": bf(ffn_wd),
        "router": router[0], "moe_wg": bf(moe_wg[0]), "moe_wu": bf(moe_wu[0]), "moe_wd": bf(moe_wd[0]),
    }

    def run(x, batch, past, states):
        seq = x.shape[1]
        y, k_a, v_a, s_h, s_g, cv, k_d, v_d = _layer_stack(x.reshape(batch * seq, d), batch, past, states, w)
        def cache(t, n_heads, hd):
            if t.ndim == 3:
                return t.reshape(batch, n_heads, hd, seq).transpose(0, 3, 1, 2)[None]
            return t.reshape(1, batch, seq, n_heads, hd)

        return (y.reshape(batch, seq, d), cache(k_a, H_A, hd_a), cache(v_a, H_A, hd_a), s_h[None], s_g[None], cv[None],
                cache(k_d, H_D, hd_d), cache(v_d, H_D, hd_d))

    past = (cache_k_moba, cache_v_moba, cache_k_sb, cache_v_sb, page_table)
    out_s = run(x_sample, bs, past, (state_hgrn[0], state_gdn[0], state_gdn_conv[0]))
    zeros_p = (jnp.zeros((bp,) + state_hgrn.shape[2:], F32), jnp.zeros((bp,) + state_gdn.shape[2:], F32),
               jnp.zeros((bp,) + state_gdn_conv.shape[2:], F32))
    out_p = run(x_prompt, bp, None, zeros_p)
    return (out_p[0], out_s[0]) + out_p[1:] + out_s[1:]
```

```python
import functools

import jax
import jax.numpy as jnp
from jax import lax
from jax.experimental import pallas as pl
from jax.experimental.pallas import tpu as pltpu

F32 = jnp.float32
BF16 = jnp.bfloat16

H_A, H_B, H_C, H_D = 8, 4, 4, 8
MOBA_BLOCK = 256
MOBA_TOPK = 3
CONV_W = 4
N_EXPERTS = 8
DEPTH = 2
DEEPNORM_ALPHA = (2 * DEPTH) ** 0.25
LN_EPS = 1e-5
RMS_EPS = 1e-6
NEG = -1e30
LANES = 128
VMEM_LIMIT = 56 * 1024 * 1024


def _cparams(sem):
    return pltpu.CompilerParams(dimension_semantics=sem, vmem_limit_bytes=VMEM_LIMIT)


def _layernorm(y, g, b):
    mu = jnp.mean(y, axis=-1, keepdims=True)
    yc = y - mu
    var = jnp.mean(yc * yc, axis=-1, keepdims=True)
    return yc * lax.rsqrt(var + LN_EPS) * g + b


def _split3(x):
    hi = x.astype(BF16)
    r = x - hi.astype(F32)
    mid = r.astype(BF16)
    lo = (r - mid.astype(F32)).astype(BF16)
    return hi, mid, lo


def _dot(a, b):
    return jnp.dot(a, b, preferred_element_type=F32)


def _dot_nt(a, b):
    return lax.dot_general(a, b, (((1,), (1,)), ((), ())), preferred_element_type=F32)


def _dot_tn(a, b):
    return lax.dot_general(a, b, (((0,), (0,)), ((), ())), preferred_element_type=F32)


def _dot_exact_lhs(a_bf16, b):
    b0, b1, b2 = _split3(b)
    return _dot(a_bf16, b0) + _dot(a_bf16, b1) + _dot(a_bf16, b2)


def _mm_kernel(*refs, n_in, ln):
    a_refs, w_refs, rest = refs[:n_in], refs[n_in:2 * n_in], refs[2 * n_in:]
    acc = None
    for a, w in zip(a_refs, w_refs):
        d = _dot(a[...].astype(BF16), w[...])
        acc = d if acc is None else acc + d
    if ln:
        res_ref, g_ref, b_ref, o_ref = rest
        o_ref[...] = _layernorm(DEEPNORM_ALPHA * res_ref[...] + acc, g_ref[...], b_ref[...])
    else:
        (o_ref,) = rest
        o_ref[...] = acc


def _row_tile(m, largest=512):
    for t in (1024, 512, 256, 128, 64, 32, 16, 8):
        if t <= largest and m % t == 0:
            return t
    raise ValueError(m)


def _proj_kernel(x_ref, *refs, n_w, n_t):
    w_refs, wt_refs = refs[:n_w], refs[n_w:n_w + n_t]
    o_refs, ot_refs = refs[n_w + n_t:2 * n_w + n_t], refs[2 * n_w + n_t:]
    xb = x_ref[...].astype(BF16)
    for w_ref, o_ref in zip(w_refs, o_refs):
        o_ref[...] = _dot(xb, w_ref[...])
    for wt_ref, ot_ref in zip(wt_refs, ot_refs):
        ot_ref[0] = _dot_nt(wt_ref[...], xb)


def _project(x, w_list, batch, transposed=()):
    m, d = x.shape
    tm = _row_tile(m)
    seq = m // batch
    wt_list = [w_list[t].T for t in transposed] if seq % tm == 0 else []
    nl = max(seq // tm, 1)
    const = lambda w: pl.BlockSpec(w.shape, lambda i: (0, 0))
    outs = pl.pallas_call(
        functools.partial(_proj_kernel, n_w=len(w_list), n_t=len(wt_list)),
        grid=(m // tm,),
        in_specs=[pl.BlockSpec((tm, d), lambda i: (i, 0))] + [const(w) for w in w_list] + [const(w) for w in wt_list],
        out_specs=[pl.BlockSpec((tm, w.shape[1]), lambda i: (i, 0)) for w in w_list]
        + [pl.BlockSpec((1, w.shape[0], tm), lambda i: (i // nl, 0, i % nl)) for w in wt_list],
        out_shape=[jax.ShapeDtypeStruct((m, w.shape[1]), F32) for w in w_list]
        + [jax.ShapeDtypeStruct((batch, w.shape[0], seq), F32) for w in wt_list],
        compiler_params=_cparams(("parallel",)),
        name="in_proj",
    )(x, *w_list, *wt_list)
    return outs[:len(w_list)], outs[len(w_list):]


def _matmul(a_list, w_list, *, tn=None, ln_args=None):
    m = a_list[0].shape[0]
    n = w_list[0].shape[1]
    tm = _row_tile(m, largest=1024)
    ln = ln_args is not None
    tn = n if (ln or tn is None) else tn
    assert n % tn == 0
    in_specs = [pl.BlockSpec((tm, a.shape[1]), lambda i, j: (i, 0)) for a in a_list]
    in_specs += [pl.BlockSpec((w.shape[0], tn), lambda i, j: (0, j)) for w in w_list]
    args = list(a_list) + list(w_list)
    if ln:
        res, g, b = ln_args
        in_specs += [pl.BlockSpec((tm, n), lambda i, j: (i, 0)),
                     pl.BlockSpec((1, n), lambda i, j: (0, 0)),
                     pl.BlockSpec((1, n), lambda i, j: (0, 0))]
        args += [res, g.reshape(1, n), b.reshape(1, n)]
    return pl.pallas_call(
        functools.partial(_mm_kernel, n_in=len(a_list), ln=ln),
        grid=(m // tm, n // tn),
        in_specs=in_specs,
        out_specs=pl.BlockSpec((tm, tn), lambda i, j: (i, j)),
        out_shape=jax.ShapeDtypeStruct((m, n), F32),
        compiler_params=_cparams(("parallel", "arbitrary")),
        name="matmul_ln" if ln else "matmul",
    )(*args)


def _ffn_kernel(*refs, use_comb):
    if use_comb:
        x_ref, comb_ref, wg_ref, wu_ref, wd_ref, g_ref, b_ref, o_ref, xb_ref, acc_ref = refs
    else:
        x_ref, wg_ref, wu_ref, wd_ref, g_ref, b_ref, o_ref, xb_ref, acc_ref = refs
    e, j = pl.program_id(1), pl.program_id(2)
    first = jnp.logical_and(e == 0, j == 0)
    last = jnp.logical_and(e == pl.num_programs(1) - 1, j == pl.num_programs(2) - 1)

    @pl.when(first)
    def _():
        xb_ref[...] = x_ref[...].astype(BF16)
        acc_ref[...] = jnp.zeros_like(acc_ref)

    xb = xb_ref[...]
    h = jax.nn.silu(_dot(xb, wg_ref[0])) * _dot(xb, wu_ref[0])
    if use_comb:
        lane = lax.broadcasted_iota(jnp.int32, comb_ref.shape, 1)
        h = h * jnp.sum(jnp.where(lane == e, comb_ref[...], 0.0), axis=-1, keepdims=True)
    acc_ref[...] += _dot(h.astype(BF16), wd_ref[0])

    @pl.when(last)
    def _():
        o_ref[...] = _layernorm(DEEPNORM_ALPHA * x_ref[...] + acc_ref[...], g_ref[...], b_ref[...])


def _ff_tile(ff, largest=896):
    for t in (1792, 896, 512, 256, LANES):
        if t <= largest and ff % t == 0:
            return t
    raise ValueError(ff)


def _ffn(x, comb, wg, wu, wd, g, b):
    m, d = x.shape
    n_e, _, ff = wg.shape
    tm = _row_tile(m, largest=1024)
    tf = _ff_tile(ff, largest=512)
    use_comb = comb is not None
    in_specs = [pl.BlockSpec((tm, d), lambda i, e, j: (i, 0))]
    args = [x]
    if use_comb:
        in_specs.append(pl.BlockSpec((tm, comb.shape[1]), lambda i, e, j: (i, 0)))
        args.append(comb)
    in_specs += [pl.BlockSpec((1, d, tf), lambda i, e, j: (e, 0, j)),
                 pl.BlockSpec((1, d, tf), lambda i, e, j: (e, 0, j)),
                 pl.BlockSpec((1, tf, d), lambda i, e, j: (e, j, 0)),
                 pl.BlockSpec((1, d), lambda i, e, j: (0, 0)),
                 pl.BlockSpec((1, d), lambda i, e, j: (0, 0))]
    args += [wg, wu, wd, g.reshape(1, d), b.reshape(1, d)]
    return pl.pallas_call(
        functools.partial(_ffn_kernel, use_comb=use_comb),
        grid=(m // tm, n_e, ff // tf),
        in_specs=in_specs,
        out_specs=pl.BlockSpec((tm, d), lambda i, e, j: (i, 0)),
        out_shape=jax.ShapeDtypeStruct((m, d), F32),
        scratch_shapes=[pltpu.VMEM((tm, d), BF16), pltpu.VMEM((tm, d), F32)],
        compiler_params=_cparams(("parallel", "arbitrary", "arbitrary")),
        name="moe_ffn" if use_comb else "ffn",
    )(*args)


MOE_TOKENS = 1024
MOE_SUBTILES = 1
MOE_ROWS = 128


def _moe_kernel(x_ref, comb_ref, wg_ref, wu_ref, wd_ref, g_ref, b_ref, o_ref,
                xb_ref, xc_ref, yc_ref, rank_ref, rank_t_ref, comb_t_ref):
    e, j = pl.program_id(1), pl.program_id(2)
    tm, d = x_ref.shape
    r, ts = MOE_ROWS, MOE_TOKENS
    subs = [slice(s * ts, (s + 1) * ts) for s in range(tm // ts)]
    lane = lax.broadcasted_iota(jnp.int32, (ts, LANES), 1)

    @pl.when(jnp.logical_and(e == 0, j == 0))
    def _():
        xb_ref[...] = x_ref[...].astype(BF16)
        o_ref[...] = jnp.zeros_like(o_ref)
        earlier = jnp.where(lax.broadcasted_iota(jnp.int32, (ts, ts), 1) < lax.broadcasted_iota(jnp.int32, (ts, ts), 0),
                            1.0, 0.0).astype(BF16)
        for sub in subs:
            routed = jnp.where(comb_ref[sub, :] > 0.0, 1.0, 0.0).astype(BF16)
            rank = _dot(earlier, routed)
            rank_ref[sub, :] = rank
            rank_t_ref[:, sub] = rank.T
            comb_t_ref[:, sub] = comb_ref[sub, :].T

    gate_rows = [comb_t_ref[pl.ds(e, 1), sub] for sub in subs]
    n_pass = [(jnp.sum(jnp.where(g_row > 0.0, 1, 0)) + (r - 1)) // r for g_row in gate_rows]
    base = [sum(n_pass[:s], jnp.int32(0)) * r for s in range(len(subs))]
    total_pass = sum(n_pass, jnp.int32(0))

    def passes(count, fn):
        def pair(k, carry):
            fn(k * (2 * r), 2 * r)
            return carry

        lax.fori_loop(0, count // 2, pair, 0)

        @pl.when(count % 2 == 1)
        def _():
            fn((count - 1) * r, r)

    @pl.when(j == 0)
    def _():
        for s, sub in enumerate(subs):
            rank_row = rank_t_ref[pl.ds(e, 1), sub]

            def compact(start, n_rows, s=s, sub=sub, rank_row=rank_row):
                slot = (start + lax.broadcasted_iota(jnp.int32, (n_rows, ts), 0)).astype(F32)
                pick = jnp.where(jnp.logical_and(rank_row == slot, gate_rows[s] > 0.0), 1.0, 0.0).astype(BF16)
                rows = pl.ds(pl.multiple_of(base[s] + start, r), n_rows)
                xc_ref[rows, :] = _dot(pick, xb_ref[sub, :]).astype(BF16)
                yc_ref[rows, :] = jnp.zeros((n_rows, d), F32)

            passes(n_pass[s], compact)

    def expert(start, n_rows):
        rows = pl.ds(pl.multiple_of(start, r), n_rows)
        xk = xc_ref[rows, :]
        h = jax.nn.silu(_dot(xk, wg_ref[0])) * _dot(xk, wu_ref[0])
        yc_ref[rows, :] += _dot(h.astype(BF16), wd_ref[0])

    passes(total_pass, expert)

    @pl.when(j == pl.num_programs(2) - 1)
    def _():
        for s, sub in enumerate(subs):
            rank_col = jnp.sum(jnp.where(lane == e, rank_ref[sub, :], 0.0), axis=-1, keepdims=True)
            gate_col = jnp.sum(jnp.where(lane == e, comb_ref[sub, :], 0.0), axis=-1, keepdims=True)

            def place(start, n_rows, s=s, sub=sub, rank_col=rank_col, gate_col=gate_col):
                slot = (start + lax.broadcasted_iota(jnp.int32, (ts, n_rows), 1)).astype(F32)
                put = jnp.where(jnp.logical_and(rank_col == slot, gate_col > 0.0), 1.0, 0.0).astype(BF16)
                rows = pl.ds(pl.multiple_of(base[s] + start, r), n_rows)
                o_ref[sub, :] += gate_col * _dot(put, yc_ref[rows, :].astype(BF16))

            passes(n_pass[s], place)

    @pl.when(jnp.logical_and(e == pl.num_programs(1) - 1, j == pl.num_programs(2) - 1))
    def _():
        o_ref[...] = _layernorm(DEEPNORM_ALPHA * x_ref[...] + o_ref[...], g_ref[...], b_ref[...])


def _moe(x, comb, wg, wu, wd, g, b):
    m, d = x.shape
    n_e, _, ff = wg.shape
    tm = MOE_TOKENS * MOE_SUBTILES
    if m % tm != 0:
        return _ffn(x, comb, wg, wu, wd, g, b)
    tf = _ff_tile(ff, largest=1792)
    assert MOE_TOKENS % (2 * MOE_ROWS) == 0 and comb.shape[1] == LANES and n_e <= LANES
    cap = tm + MOE_SUBTILES * MOE_ROWS
    once = pl.Buffered(1)
    return pl.pallas_call(
        _moe_kernel,
        grid=(m // tm, n_e, ff // tf),
        in_specs=[pl.BlockSpec((tm, d), lambda i, e, j: (i, 0), pipeline_mode=once),
                  pl.BlockSpec((tm, LANES), lambda i, e, j: (i, 0), pipeline_mode=once),
                  pl.BlockSpec((1, d, tf), lambda i, e, j: (e, 0, j)),
                  pl.BlockSpec((1, d, tf), lambda i, e, j: (e, 0, j)),
                  pl.BlockSpec((1, tf, d), lambda i, e, j: (e, j, 0)),
                  pl.BlockSpec((1, d), lambda i, e, j: (0, 0)),
                  pl.BlockSpec((1, d), lambda i, e, j: (0, 0))],
        out_specs=pl.BlockSpec((tm, d), lambda i, e, j: (i, 0), pipeline_mode=once),
        out_shape=jax.ShapeDtypeStruct((m, d), F32),
        scratch_shapes=[pltpu.VMEM((tm, d), BF16), pltpu.VMEM((cap, d), BF16), pltpu.VMEM((cap, d), F32),
                        pltpu.VMEM((tm, LANES), F32), pltpu.VMEM((LANES, tm), F32), pltpu.VMEM((LANES, tm), F32)],
        compiler_params=_cparams(("parallel", "arbitrary", "arbitrary")),
        name="moe_sparse",
    )(x, comb, wg, wu, wd, g.reshape(1, d), b.reshape(1, d))


def _router_kernel(x_ref, r_ref, o_ref):
    logits = _dot_mid(x_ref[...], r_ref[...])
    lane = lax.broadcasted_iota(jnp.int32, logits.shape, 1)
    logits = jnp.where(lane < N_EXPERTS, logits, -jnp.inf)
    m1 = jnp.max(logits, axis=-1, keepdims=True)
    i1 = jnp.min(jnp.where(logits == m1, lane, LANES), axis=-1, keepdims=True)
    rest = jnp.where(lane == i1, -jnp.inf, logits)
    m2 = jnp.max(rest, axis=-1, keepdims=True)
    i2 = jnp.min(jnp.where(rest == m2, lane, LANES), axis=-1, keepdims=True)
    e2 = jnp.exp(m2 - m1)
    den = 1.0 + e2
    o_ref[...] = jnp.where(lane == i1, 1.0 / den, 0.0) + jnp.where(lane == i2, e2 / den, 0.0)


def _router(x, router):
    m, d = x.shape
    tm = _row_tile(m)
    r = jnp.pad(router, ((0, 0), (0, LANES - router.shape[1])))
    return pl.pallas_call(
        _router_kernel,
        grid=(m // tm,),
        in_specs=[pl.BlockSpec((tm, d), lambda i: (i, 0)), pl.BlockSpec((d, LANES), lambda i: (0, 0))],
        out_specs=pl.BlockSpec((tm, LANES), lambda i: (i, 0)),
        out_shape=jax.ShapeDtypeStruct((m, LANES), F32),
        compiler_params=_cparams(("parallel",)),
        name="router",
    )(x, r)


def _topk_lanes(g, idx, axis=-1):
    sel = jnp.zeros(g.shape, F32)
    for _ in range(MOBA_TOPK):
        m = jnp.max(g, axis=axis, keepdims=True)
        first = jnp.min(jnp.where(g == m, idx, LANES), axis=axis, keepdims=True)
        hit = jnp.logical_and(idx == first, m > -jnp.inf)
        sel = jnp.where(hit, 1.0, sel)
        g = jnp.where(hit, -jnp.inf, g)
    return sel


def _moba_kernel(slope_ref, q_ref, k_ref, v_ref, o_ref, kb_ref, vt_ref, km_ref, m_ref, l_ref, acc_ref, sel_ref,
                 *, tq, nb, q_offset, hd):
    i = pl.program_id(2)
    blk = MOBA_BLOCK

    @pl.when(i == 0)
    def _():
        kb_ref[...] = k_ref[...].astype(BF16)
        km_ref[...] = jnp.zeros_like(km_ref)
        for j in range(nb):
            rows = slice(j * blk, (j + 1) * blk)
            vt_ref[j] = v_ref[rows, :].T.astype(BF16)
            km_ref[j:j + 1, :] = jnp.sum(k_ref[rows, :], axis=0, keepdims=True) * (1.0 / blk)

    q0 = q_offset + i * tq
    own = q0 // blk
    q = q_ref[...]
    lane = lax.broadcasted_iota(jnp.int32, (tq, LANES), 1)
    q2 = jnp.concatenate([jnp.where(lane < hd, q, 0.0), jnp.where(lane >= hd, q, 0.0)], axis=0)
    gate_t = _dot_mid(km_ref[...], q2, dot=_dot_nt)
    blk_id = lax.broadcasted_iota(jnp.int32, gate_t.shape, 0)
    sel_ref[...] = _topk_lanes(jnp.where(blk_id < own, gate_t, -jnp.inf), blk_id, axis=0)
    qb = (q2 * hd ** -0.5).astype(BF16)
    c2 = lax.broadcasted_iota(jnp.int32, (blk, 2 * tq), 1)
    d0 = (jnp.where(c2 >= tq, c2 - tq, c2) - lax.broadcasted_iota(jnp.int32, (blk, 2 * tq), 0)).astype(F32)
    slope_lane = slope_ref[...]
    c1 = lax.broadcasted_iota(jnp.int32, (1, 2 * tq), 1)
    slope = jnp.where(c1 < tq, slope_lane[:, 0:1], slope_lane[:, hd:hd + 1])
    slope_d0 = slope * d0
    first = lax.broadcasted_iota(jnp.int32, (2 * hd, tq), 0) < hd
    m_ref[...] = jnp.full(m_ref.shape, NEG, F32)
    l_ref[...] = jnp.zeros_like(l_ref)
    acc_ref[...] = jnp.zeros_like(acc_ref)

    def tiles(js, diagonal):
        scores = []
        for j in js:
            off = (q0 - j * blk).astype(F32)
            s = _dot_nt(kb_ref[pl.ds(pl.multiple_of(j * blk, blk), blk), :], qb) - (slope_d0 + slope * off)
            if diagonal:
                s = jnp.where(d0 + off >= 0.0, s, NEG)
            else:
                s = jnp.where(sel_ref[pl.ds(j, 1), :] > 0.0, s, NEG)
            scores.append(s)
        m_old = m_ref[...]
        m_new = functools.reduce(jnp.maximum, [m_old] + [jnp.max(s, axis=0, keepdims=True) for s in scores])
        alpha = jnp.exp(m_old - m_new)
        l_new = alpha * l_ref[...]
        pv = None
        for j, s in zip(js, scores):
            p = jnp.exp(s - m_new)
            l_new = l_new + jnp.sum(p, axis=0, keepdims=True)
            d = _dot(vt_ref[j], p.astype(BF16))
            pv = d if pv is None else pv + d
        m_ref[...] = m_new
        l_ref[...] = l_new
        acc_ref[...] = (jnp.where(first, alpha[:, :tq], alpha[:, tq:]) * acc_ref[...]
                        + jnp.where(first, pv[:, :tq], pv[:, tq:]))

    tiles([own], True)

    def body(t, carry):
        tiles([own - 1 - 4 * t - n for n in range(4)], False)
        return carry

    lax.fori_loop(0, own // 4, body, 0)
    rest = own % 4

    @pl.when(rest >= 2)
    def _():
        tiles([rest - 1, rest - 2], False)

    @pl.when(rest % 2 == 1)
    def _():
        tiles([own - own], False)
    l = l_ref[...]
    o_ref[...] = (acc_ref[...] / jnp.where(first, l[:, :tq], l[:, tq:])).T


def _alibi_slopes_lanes(n_heads, hd):
    slopes = jnp.exp2(-8.0 * jnp.arange(1, n_heads + 1, dtype=F32) / n_heads)
    return jnp.repeat(slopes, hd).reshape(1, n_heads * hd)


def _attn_call(kernel, q, k, v, *, batch, q_offset, n_heads, extra_in=(), extra_specs=(), extra_scratch=(), name):
    lq, lk = q.shape[0] // batch, k.shape[0] // batch
    hd = q.shape[1] // n_heads
    assert 2 * hd == LANES and lk % MOBA_BLOCK == 0
    tq = min(MOBA_BLOCK, lq)
    assert lq % tq == 0 and MOBA_BLOCK % tq == 0 and q_offset % tq == 0
    nq = lq // tq
    return pl.pallas_call(
        functools.partial(kernel, tq=tq, q_offset=q_offset, hd=hd),
        grid=(batch, n_heads // 2, nq),
        in_specs=list(extra_specs) + [
            pl.BlockSpec((tq, LANES), lambda b, h, i: (b * nq + i, h)),
            pl.BlockSpec((lk, LANES), lambda b, h, i: (b, h)),
            pl.BlockSpec((lk, LANES), lambda b, h, i: (b, h))],
        out_specs=pl.BlockSpec((tq, LANES), lambda b, h, i: (b * nq + i, h)),
        out_shape=jax.ShapeDtypeStruct(q.shape, F32),
        scratch_shapes=list(extra_scratch),
        compiler_params=_cparams(("parallel", "parallel", "arbitrary")),
        name=name,
    )(*extra_in, q, k, v)


def _moba(q, k, v, *, batch, q_offset):
    nb = k.shape[0] // batch // MOBA_BLOCK
    assert nb <= LANES
    slopes = _alibi_slopes_lanes(H_A, q.shape[1] // H_A)
    tq = min(MOBA_BLOCK, q.shape[0] // batch)
    nb_rows = -(-nb // 8) * 8
    return _attn_call(
        functools.partial(_moba_kernel, nb=nb), q, k, v, batch=batch, q_offset=q_offset, n_heads=H_A,
        extra_in=(slopes,), extra_specs=(pl.BlockSpec((1, LANES), lambda b, h, i: (0, h)),),
        extra_scratch=(pltpu.VMEM((nb * MOBA_BLOCK, LANES), BF16), pltpu.VMEM((nb, LANES, MOBA_BLOCK), BF16),
                       pltpu.VMEM((nb_rows, LANES), F32), pltpu.VMEM((1, 2 * tq), F32), pltpu.VMEM((1, 2 * tq), F32),
                       pltpu.VMEM((LANES, tq), F32), pltpu.VMEM((nb_rows, 2 * tq), F32)), name="moba")


EXP_UNDERFLOW = -110.0


def _softplus(z):
    return jnp.maximum(z, 0.0) + jnp.log(1.0 + jnp.exp(-jnp.abs(z)))


def _stick_kernel(q_ref, k_ref, v_ref, o_ref, kb_ref, vb_ref, c_ref, acc_ref, *, tq, q_offset, hd):
    i = pl.program_id(2)
    blk = MOBA_BLOCK

    @pl.when(i == 0)
    def _():
        kb_ref[...] = k_ref[...].astype(BF16)
        vb_ref[...] = v_ref[...].astype(BF16)

    q0 = q_offset + i * tq
    own = q0 // blk
    q = q_ref[...] * hd ** -0.5
    lane = lax.broadcasted_iota(jnp.int32, (tq, LANES), 1)
    qb = jnp.concatenate([jnp.where(lane < hd, q, 0.0), jnp.where(lane >= hd, q, 0.0)], axis=0).astype(BF16)
    r2 = lax.broadcasted_iota(jnp.int32, (2 * tq, blk), 0)
    d0 = jnp.where(r2 >= tq, r2 - tq, r2) - lax.broadcasted_iota(jnp.int32, (2 * tq, blk), 1)
    u = jnp.where(lax.broadcasted_iota(jnp.int32, (blk, blk), 0) > lax.broadcasted_iota(jnp.int32, (blk, blk), 1),
                  1.0, 0.0).astype(BF16)
    c_ref[...] = jnp.zeros_like(c_ref)
    acc_ref[...] = jnp.zeros_like(acc_ref)

    halves = [slice(0, tq), slice(tq, 2 * tq)]

    def tile(j, diagonal):
        start = pl.multiple_of(j * blk, blk)
        kj = kb_ref[pl.ds(start, blk), :]
        zs = [_dot_nt(qb[h], kj) for h in halves]
        log_keep = [-_softplus(z) for z in zs]
        log_beta = [z + lk for z, lk in zip(zs, log_keep)]
        if diagonal:
            before = d0[halves[0]] + (q0 - j * blk) > 0
            log_keep = [jnp.where(before, lk, 0.0) for lk in log_keep]
        his = [lk.astype(BF16) for lk in log_keep]
        los = [(lk - hi.astype(F32)).astype(BF16) for lk, hi in zip(log_keep, his)]
        later = [_dot(jnp.concatenate([hi, lo], axis=0), u) for hi, lo in zip(his, los)]
        later = [t[:tq] + t[tq:] for t in later]
        cs = [c_ref[h, :] for h in halves]
        ws = [jnp.exp(lb + lt + c) for lb, lt, c in zip(log_beta, later, cs)]
        if diagonal:
            ws = [jnp.where(before, w, 0.0) for w in ws]
        for h, c, lt, lk in zip(halves, cs, later, log_keep):
            c_ref[h, :] = c + lt[:, 0:1] + lk[:, 0:1]
        vj = vb_ref[pl.ds(start, blk), :]
        pv = [_dot(w.astype(BF16), vj) for w in ws]
        acc_ref[...] += jnp.where(lane < hd, pv[0], pv[1])

    tile(own, True)

    def more(carry):
        t, c_max = carry
        return jnp.logical_and(t <= own, c_max > EXP_UNDERFLOW)

    def body(carry):
        t, _ = carry
        tile(own - t, False)
        return t + 1, jnp.max(c_ref[...])

    lax.while_loop(more, body, (jnp.int32(1), jnp.max(c_ref[...])))
    o_ref[...] = acc_ref[...]


def _stick(q, k, v, *, batch, q_offset):
    tq = min(MOBA_BLOCK, q.shape[0] // batch)
    lk = k.shape[0] // batch
    return _attn_call(_stick_kernel, q, k, v, batch=batch, q_offset=q_offset, n_heads=H_D,
                      extra_scratch=(pltpu.VMEM((lk, LANES), BF16), pltpu.VMEM((lk, LANES), BF16),
                                     pltpu.VMEM((2 * tq, 1), F32), pltpu.VMEM((tq, LANES), F32)), name="stick")


CHUNK = 64
SUB = 16
SEQ_BLOCK = 512
EXP_CLAMP = 80.0


def _dot_mid(a, b, dot=_dot):
    a0 = a.astype(BF16)
    a1 = (a - a0.astype(F32)).astype(BF16)
    b0 = b.astype(BF16)
    b1 = (b - b0.astype(F32)).astype(BF16)
    return dot(a0, b0) + (dot(a0, b1) + dot(a1, b0))


def _incl_lower(c):
    r = lax.broadcasted_iota(jnp.int32, (c, c), 0)
    s = lax.broadcasted_iota(jnp.int32, (c, c), 1)
    return r, s


def _hgrn_kernel(q_ref, f_ref, i_ref, g_ref, lb_ref, gn_ref, s0_ref, o_ref, s_ref, st_ref, *, c, n_chunks, n_heads,
                 layer):
    l = pl.program_id(1)
    dk = LANES

    w = n_heads * dk

    @pl.when(l == 0)
    def _():
        for h in range(n_heads):
            st_ref[h * dk:(h + 1) * dk, :] = s0_ref[0, h]

    r, s = _incl_lower(c)
    tri = jnp.where(s <= r, 1.0, 0.0).astype(BF16)
    sub = min(SUB, c)
    n = n_heads * c
    shift_c, shift_s = c.bit_length() - 1, sub.bit_length() - 1
    ri = lax.broadcasted_iota(jnp.int32, (n_heads * sub, n), 0)
    ci = lax.broadcasted_iota(jnp.int32, (n_heads * sub, n), 1)
    pair = jnp.right_shift(ri, shift_s) == jnp.right_shift(ci, shift_c)
    t_loc, s_loc = jnp.bitwise_and(ri, sub - 1), jnp.bitwise_and(ci, c - 1)
    own_cols = (jnp.right_shift(lax.broadcasted_iota(jnp.int32, (n, w), 0), shift_c)
                == lax.broadcasted_iota(jnp.int32, (n, w), 1) // dk)
    gn = gn_ref[...]
    e = jnp.exp(lb_ref[...] - jnp.max(lb_ref[...], axis=0, keepdims=True))
    lb = jnp.sum(e[:layer + 1], axis=0, keepdims=True) / jnp.sum(e, axis=0, keepdims=True)

    def stack(t):
        return jnp.concatenate([t[:, h * dk:(h + 1) * dk] for h in range(n_heads)], axis=0)

    def run(chunks):
        m = len(chunks)
        rows = [pl.ds(i * c, c) if isinstance(i, int) else pl.ds(pl.multiple_of(i * c, c), c) for i in chunks]
        zs = [f_ref[r_, :] for r_ in rows]
        ks = [(1.0 - lb) * jax.nn.sigmoid(-z) for z in zs]
        qs = [jax.nn.silu(q_ref[r_, :]) for r_ in rows]
        cgs = [_dot_exact_lhs(tri, jnp.log(lb + (1.0 - lb) * jax.nn.sigmoid(z))) for z in zs]
        v4s = [stack(i_ref[r_, :]).astype(BF16) for r_ in rows]
        intra = [[] for _ in range(m)]
        for b in range(c // sub):
            lo, hi = b * sub, (b + 1) * sub
            refs = [cg[lo - 1:lo] if b else jnp.zeros((1, w), F32) for cg in cgs]
            q_sub = [stack(q[lo:hi] * jnp.exp(cg[lo:hi] - ref_pt)).astype(BF16) for q, cg, ref_pt in zip(qs, cgs, refs)]
            k_all = [stack(k * jnp.exp(jnp.minimum(ref_pt - cg, EXP_CLAMP))).astype(BF16)
                     for k, cg, ref_pt in zip(ks, cgs, refs)]
            a = [_dot_nt(q_, k_) for q_, k_ in zip(q_sub, k_all)]
            a = [jnp.where(jnp.logical_and(pair, s_loc <= t_loc + lo), a_, 0.0).astype(BF16) for a_ in a]
            for x_, (a_, v4) in enumerate(zip(a, v4s)):
                intra[x_].append(_dot(a_, v4))
        q_dec = [stack(q * jnp.exp(cg)).astype(BF16) for q, cg in zip(qs, cgs)]
        lasts = [cg[c - 1:c] for cg in cgs]
        kts = [stack(k * jnp.exp(last - cg)).astype(BF16) for k, last, cg in zip(ks, lasts, cgs)]
        for x_ in range(m):
            st = st_ref[...]
            from_state = _dot_nt(q_dec[x_], st.astype(BF16))
            v_wide = jnp.where(own_cols, jnp.concatenate([v4s[x_]] * n_heads, axis=1), 0.0)
            e_last = jnp.concatenate([jnp.broadcast_to(jnp.exp(lasts[x_][:, h * dk:(h + 1) * dk]), (dk, dk))
                                      for h in range(n_heads)], axis=0)
            st_ref[...] = st * e_last + _dot_tn(v_wide, kts[x_])
            for h in range(n_heads):
                o = from_state[h * c:(h + 1) * c, h * dk:(h + 1) * dk] + jnp.concatenate(
                    [part[h * sub:(h + 1) * sub] for part in intra[x_]], axis=0)
                o = o * lax.rsqrt(jnp.mean(o * o, axis=-1, keepdims=True) + RMS_EPS) * gn
                o_ref[rows[x_], h * dk:(h + 1) * dk] = o * jax.nn.sigmoid(g_ref[rows[x_], h * dk:(h + 1) * dk])

    per_trip = 4 if n_chunks % 4 == 0 else (2 if n_chunks % 2 == 0 else 1)

    def trip(i, carry):
        run([per_trip * i + x_ for x_ in range(per_trip)])
        return carry

    lax.fori_loop(0, n_chunks // per_trip, trip, 0)

    @pl.when(l == pl.num_programs(1) - 1)
    def _():
        for h in range(n_heads):
            s_ref[0, h] = st_ref[h * dk:(h + 1) * dk, :]


def _hgrn(hg, lb, gn, s0, *, batch, layer=0):
    t, width = hg.shape
    seq = t // batch
    w = width // 4
    n_heads = w // LANES
    c = min(CHUNK, seq)
    lblk = min(SEQ_BLOCK, seq)
    assert seq % lblk == 0 and lblk % c == 0 and c % min(SUB, c) == 0
    nl = seq // lblk
    part = lambda p: pl.BlockSpec((lblk, w), lambda b, l: (b * nl + l, p))
    state = pl.BlockSpec((1, n_heads, LANES, LANES), lambda b, l: (b, 0, 0, 0))
    o, st = pl.pallas_call(
        functools.partial(_hgrn_kernel, c=c, n_chunks=lblk // c, n_heads=n_heads, layer=layer),
        grid=(batch, nl),
        in_specs=[part(0), part(1), part(2), part(3),
                  pl.BlockSpec(lb.shape, lambda b, l: (0, 0)), pl.BlockSpec((1, LANES), lambda b, l: (0, 0)), state],
        out_specs=[pl.BlockSpec((lblk, w), lambda b, l: (b * nl + l, 0)), state],
        out_shape=[jax.ShapeDtypeStruct((t, w), F32), jax.ShapeDtypeStruct(s0.shape, F32)],
        scratch_shapes=[pltpu.VMEM((n_heads * LANES, LANES), F32)],
        compiler_params=_cparams(("parallel", "arbitrary")),
        name="hgrn2",
    )(hg, hg, hg, hg, lb, gn.reshape(1, LANES), jnp.swapaxes(s0, -1, -2))
    return o, jnp.swapaxes(st, -1, -2)


def _unit_lower_inverses(lms, n, period):
    bs = min(SUB, period)
    r, s = _incl_lower(n)
    eye = jnp.where(r == s, 1.0, 0.0)
    shift = bs.bit_length() - 1
    same = jnp.right_shift(r, shift) == jnp.right_shift(s, shift)

    def mm(a, b):
        return _dot(a.astype(BF16), b.astype(BF16))

    def neumann(mats, order):
        invs, pws, k = [eye + a for a in mats], list(mats), 2
        while k < order:
            pws = [mm(pw, pw) for pw in pws]
            invs = [inv + mm(inv, pw) for inv, pw in zip(invs, pws)]
            k *= 2
        return invs

    inv_ds = neumann([-jnp.where(same, lm, 0.0) for lm in lms], bs)
    if period == bs:
        return inv_ds
    ms = [mm(inv_d, jnp.where(same, 0.0, lm)) for inv_d, lm in zip(inv_ds, lms)]
    return [mm(a, inv_d) for a, inv_d in zip(neumann([-m for m in ms], period // bs), inv_ds)]


def _gdn_kernel(x_ref, ab_ref, g_ref, cw_ref, al_ref, dtb_ref, gn_ref, s0_ref, cb_ref, o_ref, s_ref, nb_ref,
                xx_ref, y_ref, st_ref, *, c, n_chunks, n_heads, lblk):
    l = pl.program_id(1)
    dk = LANES
    halo = 8
    w = n_heads * dk

    @pl.when(l == 0)
    def _():
        for h in range(n_heads):
            st_ref[:, h * dk:(h + 1) * dk] = s0_ref[0, h]
        xx_ref[halo - (CONV_W - 1):halo, :] = cb_ref[0]

    xx_ref[halo:halo + lblk, :] = x_ref[...]
    y = xx_ref[halo:halo + lblk, :] * cw_ref[CONV_W - 1:CONV_W, :]
    for i in range(CONV_W - 1):
        off = halo - (CONV_W - 1) + i
        y = y + xx_ref[off:off + lblk, :] * cw_ref[i:i + 1, :]
    y_ref[...] = jax.nn.silu(y)
    tail = xx_ref[halo + lblk - (CONV_W - 1):halo + lblk, :]
    xx_ref[halo - (CONV_W - 1):halo, :] = tail

    r, s = _incl_lower(c)
    tri = jnp.where(s <= r, 1.0, 0.0).astype(BF16)
    n = n_heads * c
    rr, ss = _incl_lower(n)
    shift = c.bit_length() - 1
    same_head = jnp.right_shift(rr, shift) == jnp.right_shift(ss, shift)
    incl = jnp.logical_and(same_head, ss <= rr)
    strict = jnp.logical_and(same_head, ss < rr)
    own_cols = (jnp.right_shift(lax.broadcasted_iota(jnp.int32, (n, w), 0), shift)
                == lax.broadcasted_iota(jnp.int32, (n, w), 1) // dk)
    gn = gn_ref[...]

    def l2n(t):
        return t * lax.rsqrt(jnp.sum(t * t, axis=-1, keepdims=True) + RMS_EPS)

    def stack(f):
        return jnp.concatenate([f(h) for h in range(n_heads)], axis=0)

    def own_block(t):
        return stack(lambda h: t[h * c:(h + 1) * c, h * dk:(h + 1) * dk])

    def front(i):
        rows = pl.ds(i * c, c) if isinstance(i, int) else pl.ds(pl.multiple_of(i * c, c), c)
        ab = ab_ref[rows, :]
        pre = ab + dtb_ref[...]
        log_a = -jnp.exp(al_ref[...]) * (jnp.maximum(pre, 0.0) + jnp.log(1.0 + jnp.exp(-jnp.abs(pre))))
        beta_all = jax.nn.sigmoid(ab)
        cg_all = _dot_exact_lhs(tri, log_a)
        cg_t = cg_all.T
        q = stack(lambda h: l2n(y_ref[rows, h * dk:(h + 1) * dk])) * dk ** -0.5
        k = stack(lambda h: l2n(y_ref[rows, w + h * dk:w + (h + 1) * dk]))
        v = stack(lambda h: y_ref[rows, 2 * w + h * dk:2 * w + (h + 1) * dk])
        cg = stack(lambda h: cg_all[:, h:h + 1])
        beta = stack(lambda h: beta_all[:, n_heads + h:n_heads + h + 1])
        cg_row = jnp.concatenate([cg_t[h:h + 1, :c] for h in range(n_heads)], axis=1)
        last = stack(lambda h: jnp.broadcast_to(cg_all[c - 1:c, h:h + 1], (c, 1)))
        decay = jnp.exp(jnp.where(incl, cg - cg_row, NEG))
        kk = _dot_mid(k, k, dot=_dot_nt) * decay * beta
        return jnp.where(strict, kk, 0.0), (rows, q, k, v, cg, beta, last, decay, cg_all)

    def back(side, t_inv):
        rows, q, k, v, cg, beta, last, decay, cg_all = side
        e_cg = jnp.exp(cg)
        sol = _dot_mid(t_inv, jnp.concatenate([v * beta, k * (beta * e_cg)], axis=1))
        qk = (_dot_nt(q.astype(BF16), k.astype(BF16)) * decay).astype(BF16)
        kt = (k * jnp.exp(last - cg)).astype(BF16)
        e_last = jnp.concatenate([jnp.broadcast_to(jnp.exp(cg_all[c - 1:c, h:h + 1]), (1, dk)) for h in range(n_heads)],
                                 axis=1)
        return rows, sol[:, :dk], sol[:, dk:].astype(BF16), (q * e_cg).astype(BF16), qk, kt, e_last

    def advance(prepared):
        rows, sol_v, sol_k, q_decayed, qk, kt, e_last = prepared
        st = st_ref[...]
        stb = st.astype(BF16)
        u = sol_v - own_block(_dot(sol_k, stb))
        o = own_block(_dot(q_decayed, stb)) + _dot(qk, u.astype(BF16))
        u_wide = jnp.where(own_cols, jnp.concatenate([u] * n_heads, axis=1), 0.0)
        st_ref[...] = st * e_last + _dot_tn(kt, u_wide.astype(BF16))
        o = o * lax.rsqrt(jnp.mean(o * o, axis=-1, keepdims=True) + RMS_EPS) * gn
        for h in range(n_heads):
            o_ref[rows, h * dk:(h + 1) * dk] = o[h * c:(h + 1) * c] * jax.nn.silu(g_ref[rows, h * dk:(h + 1) * dk])

    def run(chunks):
        fronts = [front(i) for i in chunks]
        inverses = _unit_lower_inverses([lm for lm, _ in fronts], n, c)
        for prepared in [back(side, t_inv) for (_, side), t_inv in zip(fronts, inverses)]:
            advance(prepared)

    per_trip = 8 if n_chunks % 8 == 0 else (4 if n_chunks % 4 == 0 else 2)

    def trip(i, carry):
        run([per_trip * i + m for m in range(per_trip)])
        return carry

    lax.fori_loop(0, n_chunks // per_trip, trip, 0)
    if n_chunks % per_trip:
        run([n_chunks - 1])

    @pl.when(l == pl.num_programs(1) - 1)
    def _():
        for h in range(n_heads):
            s_ref[0, h] = st_ref[:, h * dk:(h + 1) * dk]
        nb_ref[0] = tail


def _gdn(qkv, ab, g, conv_w, a_log, dt_bias, gn, s0, conv_buf, *, batch):
    t, width = qkv.shape
    seq = t // batch
    w = width // 3
    n_heads = w // LANES
    c = min(CHUNK, seq)
    lblk = min(SEQ_BLOCK, seq)
    assert seq % lblk == 0 and lblk % c == 0 and lblk >= CONV_W - 1
    nl = seq // lblk
    pad_lane = lambda vec: jnp.pad(vec, (0, LANES - vec.shape[0])).reshape(1, LANES)
    rows = lambda width_: pl.BlockSpec((lblk, width_), lambda b, l: (b * nl + l, 0))
    const = lambda shape: pl.BlockSpec(shape, lambda b, l: (0,) * len(shape))
    state = pl.BlockSpec((1, n_heads, LANES, LANES), lambda b, l: (b, 0, 0, 0))
    buf = pl.BlockSpec((1, CONV_W - 1, width), lambda b, l: (b, 0, 0))
    return pl.pallas_call(
        functools.partial(_gdn_kernel, c=c, n_chunks=lblk // c, n_heads=n_heads, lblk=lblk),
        grid=(batch, nl),
        in_specs=[rows(width), rows(LANES), rows(w), const((CONV_W, width)), const((1, LANES)), const((1, LANES)),
                  const((1, LANES)), state, buf],
        out_specs=[rows(w), state, buf],
        out_shape=[jax.ShapeDtypeStruct((t, w), F32), jax.ShapeDtypeStruct(s0.shape, F32),
                   jax.ShapeDtypeStruct(conv_buf.shape, F32)],
        scratch_shapes=[pltpu.VMEM((lblk + 8, width), F32), pltpu.VMEM((lblk, width), F32),
                        pltpu.VMEM((LANES, w), F32)],
        compiler_params=_cparams(("parallel", "arbitrary")),
        name="gated_deltanet",
    )(qkv, ab, g, conv_w, pad_lane(a_log), pad_lane(dt_bias), gn.reshape(1, LANES), s0, conv_buf)


PAGES_PER_STEP = 32


def _head_fold(pv, n_heads, nq, hd):
    return jnp.concatenate([pv[h * nq:(h + 1) * nq, h * hd:(h + 1) * hd] for h in range(n_heads)], axis=0)


def _row_ids(rows, nq):
    r = lax.broadcasted_iota(jnp.int32, (rows, 1), 0)
    return r // nq, r % nq


def _page_group(refs, first, count):
    return jnp.concatenate([refs[first + r][0, 0].reshape(-1, refs[first + r].shape[-1]) for r in range(count)], axis=1)


def _stick_paged_kernel(pt_ref, qbd_ref, kn_ref, vn_ref, k_hbm, v_hbm, o_ref, kbuf, vbuf, sem, *, n_heads, nq, hd,
                        n_pages):
    b = pl.program_id(0)
    rows = n_heads * nq
    page = kbuf.shape[-1]
    ppg = kbuf.shape[1]
    n_groups = n_pages // ppg
    _, row_q = _row_ids(rows, nq)
    qbd = (qbd_ref[0] * hd ** -0.5).astype(BF16)

    def copies(g, slot):
        out = []
        for r in range(ppg):
            pg = pt_ref[b, g * ppg + r]
            out.append(pltpu.make_async_copy(k_hbm.at[0, pg], kbuf.at[slot, r], sem.at[0, slot, r]))
            out.append(pltpu.make_async_copy(v_hbm.at[0, pg], vbuf.at[slot, r], sem.at[1, slot, r]))
        return out

    def group(buf, slot):
        return jnp.concatenate([buf[slot, r].reshape(n_heads * hd, page) for r in range(ppg)], axis=1).astype(BF16)

    def strict_later(n):
        return jnp.where(lax.broadcasted_iota(jnp.int32, (n, n), 0) > lax.broadcasted_iota(jnp.int32, (n, n), 1),
                         1.0, 0.0).astype(BF16)

    def weights(z, mask, carry, u):
        log_keep = -_softplus(z)
        log_beta = z + log_keep
        if mask is not None:
            log_keep = jnp.where(mask, log_keep, 0.0)
        hi = log_keep.astype(BF16)
        lo = (log_keep - hi.astype(F32)).astype(BF16)
        later = _dot(hi, u) + _dot(lo, u)
        w = jnp.exp(log_beta + later + carry)
        if mask is not None:
            w = jnp.where(mask, w, 0.0)
        return w.astype(BF16), carry + jnp.sum(log_keep, axis=-1, keepdims=True)

    for cp in copies(n_groups - 1, 0):
        cp.start()

    n = kn_ref.shape[1]
    z = _dot_nt(qbd, kn_ref[0].astype(BF16))
    col = lax.broadcasted_iota(jnp.int32, (rows, n), 1)
    w, c = weights(z, col < row_q, jnp.zeros((rows, 1), F32), strict_later(n))
    acc = _dot(w, vn_ref[0].astype(BF16))
    u = strict_later(ppg * page)

    def more(carry):
        g, c, _ = carry
        return jnp.logical_and(g >= 0, jnp.max(c) > EXP_UNDERFLOW)

    def body(carry):
        g, c, acc = carry
        slot = (n_groups - 1 - g) % 2

        @pl.when(g > 0)
        def _():
            for cp in copies(g - 1, 1 - slot):
                cp.start()

        for cp in copies(g, slot):
            cp.wait()
        w, c = weights(_dot(qbd, group(kbuf, slot)), None, c, u)
        return g - 1, c, acc + _dot_nt(w, group(vbuf, slot))

    g, _, acc = lax.while_loop(more, body, (jnp.int32(n_groups - 1), c, acc))

    @pl.when(g >= 0)
    def _():
        for cp in copies(g, (n_groups - 1 - g) % 2):
            cp.wait()

    o_ref[0] = _head_fold(acc, n_heads, nq, hd)


def _moba_paged_kernel(pt_ref, qbd_ref, kn_ref, vn_ref, *rest, pps, n_heads, nq, hd, past_len):
    k_refs, v_refs = rest[:pps], rest[pps:2 * pps]
    o_ref, mo_ref, lo_ref, acco_ref, m_ref, l_ref, acc_ref, km_ref = rest[2 * pps:]
    s = pl.program_id(1)
    rows = n_heads * nq
    page = k_refs[0].shape[-1]
    ppb = MOBA_BLOCK // page
    row_h, row_q = _row_ids(rows, nq)
    slope = jnp.exp2(-8.0 * (row_h + 1).astype(F32) / n_heads)
    lane = lax.broadcasted_iota(jnp.int32, (rows, LANES), 1)
    q_f32 = qbd_ref[0]
    qbd = (q_f32 * hd ** -0.5).astype(BF16)

    @pl.when(s == 0)
    def _():
        n = kn_ref.shape[1]
        col = lax.broadcasted_iota(jnp.int32, (rows, n), 1)
        sc = _dot_nt(qbd, kn_ref[0].astype(BF16)) - slope * (row_q - col).astype(F32)
        sc = jnp.where(col <= row_q, sc, NEG)
        m = jnp.max(sc, axis=-1, keepdims=True)
        p = jnp.exp(sc - m)
        mo_ref[...] = m
        lo_ref[...] = jnp.sum(p, axis=-1, keepdims=True)
        acco_ref[...] = _head_fold(_dot(p.astype(BF16), vn_ref[0].astype(BF16)), n_heads, nq, hd)
        m_ref[...] = jnp.full(m_ref.shape, NEG, F32)
        l_ref[...] = jnp.zeros_like(l_ref)
        km_ref[...] = jnp.zeros_like(km_ref)

    @pl.when(s > 0)
    def _():
        key = lax.broadcasted_iota(jnp.int32, (rows, MOBA_BLOCK), 1)
        m_all, l_all, km = m_ref[...], l_ref[...], km_ref[...]
        lane_k = lax.broadcasted_iota(jnp.int32, km.shape, 1)
        blocks = [(s - 1) * (pps // ppb) + bi for bi in range(pps // ppb)]
        kfs = [_page_group(k_refs, bi * ppb, ppb) for bi in range(len(blocks))]
        raw = [_dot(qbd, kf.astype(BF16)) for kf in kfs]
        means = [jnp.sum(kf, axis=-1, keepdims=True) * (1.0 / MOBA_BLOCK) for kf in kfs]
        scs = [sc - slope * (past_len + row_q - (blk * MOBA_BLOCK + key)).astype(F32) for sc, blk in zip(raw, blocks)]
        m_bs = [jnp.max(sc, axis=-1, keepdims=True) for sc in scs]
        ps = [jnp.exp(sc - m_b) for sc, m_b in zip(scs, m_bs)]
        pvs = [_dot_nt(p.astype(BF16), _page_group(v_refs, bi * ppb, ppb).astype(BF16)) for bi, p in enumerate(ps)]
        for blk, mean, m_b, p, pv in zip(blocks, means, m_bs, ps, pvs):
            km = jnp.where(lane_k == blk, mean, km)
            acc_ref[blk] = _head_fold(pv, n_heads, nq, hd)
            m_all = jnp.where(lane == blk, m_b, m_all)
            l_all = jnp.where(lane == blk, jnp.sum(p, axis=-1, keepdims=True), l_all)
        m_ref[...] = m_all
        l_ref[...] = l_all
        km_ref[...] = km

    @pl.when(s == pl.num_programs(1) - 1)
    def _():
        n_blocks = past_len // MOBA_BLOCK
        gate = _dot_mid(q_f32, km_ref[...])
        sel = _topk_lanes(jnp.where(lane < n_blocks, gate, -jnp.inf), lane) > 0.0
        m_all, l_all = m_ref[...], l_ref[...]
        m_o = mo_ref[...]
        m_tot = jnp.maximum(m_o, jnp.max(jnp.where(sel, m_all, NEG), axis=-1, keepdims=True))
        wgt = jnp.where(sel, jnp.exp(m_all - m_tot), 0.0)
        a_o = jnp.exp(m_o - m_tot)
        l_tot = lo_ref[...] * a_o + jnp.sum(wgt * l_all, axis=-1, keepdims=True)
        acc = acco_ref[...] * a_o
        for b in range(n_blocks):
            acc = acc + wgt[:, b:b + 1] * acc_ref[b]
        o_ref[0] = acc / l_tot


def _paged_attention(kind, q, k_new, v_new, k_pool, v_pool, page_table, *, n_heads):
    batch, n_pages = page_table.shape
    page, hd = k_pool.shape[2], k_pool.shape[4]
    nq = q.shape[0] // batch
    rows = n_heads * nq
    pps = PAGES_PER_STEP
    past_len = n_pages * page
    assert n_pages % pps == 0 and past_len % MOBA_BLOCK == 0 and MOBA_BLOCK % page == 0 and pps % (MOBA_BLOCK // page) == 0
    assert nq <= page and past_len // MOBA_BLOCK <= LANES and (2 * LANES) % n_heads == 0
    n_groups = n_pages // pps
    q4 = q.reshape(batch, nq, n_heads, hd).transpose(0, 2, 1, 3)
    q_bd = (q4[:, :, :, None, :] * jnp.eye(n_heads, dtype=F32)[None, :, None, :, None]).reshape(batch, rows, n_heads * hd)
    pad_new = lambda t: jnp.pad(t.reshape(batch, nq, n_heads * hd), ((0, 0), (0, page - nq), (0, 0)))
    k_t, v_t = (jnp.transpose(t, (0, 1, 3, 4, 2)) for t in (k_pool, v_pool))

    if kind == "stick":
        ppg = MOBA_BLOCK // page
        seq = lambda shape: pl.BlockSpec((1,) + shape, lambda b, pt: (b, 0, 0))
        buf = pltpu.VMEM((2, ppg, n_heads, hd, page), F32)
        out = pl.pallas_call(
            functools.partial(_stick_paged_kernel, n_heads=n_heads, nq=nq, hd=hd, n_pages=n_pages),
            grid_spec=pltpu.PrefetchScalarGridSpec(
                num_scalar_prefetch=1,
                grid=(batch,),
                in_specs=[seq((rows, n_heads * hd)), seq((page, n_heads * hd)), seq((page, n_heads * hd)),
                          pl.BlockSpec(memory_space=pl.ANY), pl.BlockSpec(memory_space=pl.ANY)],
                out_specs=seq((rows, hd)),
                scratch_shapes=[buf, buf, pltpu.SemaphoreType.DMA((2, 2, ppg))]),
            out_shape=jax.ShapeDtypeStruct((batch, rows, hd), F32),
            compiler_params=_cparams(("arbitrary",)),
            name="stick_paged",
        )(page_table, q_bd, pad_new(k_new), pad_new(v_new), k_t, v_t)
    else:
        group = lambda s: jnp.maximum(s, 1) - 1
        scratch = [pltpu.VMEM((rows, 1), F32), pltpu.VMEM((rows, 1), F32), pltpu.VMEM((rows, hd), F32),
                   pltpu.VMEM((rows, LANES), F32), pltpu.VMEM((rows, LANES), F32),
                   pltpu.VMEM((past_len // MOBA_BLOCK, rows, hd), F32), pltpu.VMEM((n_heads * hd, LANES), F32)]

        def page_spec(r):
            return pl.BlockSpec((1, 1, n_heads, hd, page), lambda b, s, pt: (0, pt[b, group(s) * pps + r], 0, 0, 0))

        per_seq = lambda shape: pl.BlockSpec((1,) + shape, lambda b, s, pt: (b, 0, 0))
        out = pl.pallas_call(
            functools.partial(_moba_paged_kernel, pps=pps, n_heads=n_heads, nq=nq, hd=hd, past_len=past_len),
            grid_spec=pltpu.PrefetchScalarGridSpec(
                num_scalar_prefetch=1,
                grid=(batch, n_groups + 1),
                in_specs=[per_seq((rows, n_heads * hd)), per_seq((page, n_heads * hd)),
                          per_seq((page, n_heads * hd))] + [page_spec(r) for r in range(pps)] * 2,
                out_specs=per_seq((rows, hd)),
                scratch_shapes=scratch),
            out_shape=jax.ShapeDtypeStruct((batch, rows, hd), F32),
            compiler_params=_cparams(("parallel", "arbitrary")),
            name="moba_paged",
        )(page_table, q_bd, pad_new(k_new), pad_new(v_new), *([k_t] * pps), *([v_t] * pps))
    return out.reshape(batch, n_heads, nq, hd).transpose(0, 2, 1, 3).reshape(batch * nq, n_heads * hd)


def _layer_stack(x, batch, past, states, w):
    s_hgrn, s_gdn, conv_buf = states

    def attend(kind, q, k, v, pools, n_heads):
        if past is None:
            return (_moba if kind == "moba" else _stick)(q, k, v, batch=batch, q_offset=0)
        return _paged_attention(kind, q, k, v, *pools, past[4], n_heads=n_heads)

    (q_a, k_a, v_a, hg), kv_a_t = _project(x, w["in_a"] + [w["in_hgrn"]], batch, transposed=(1, 2))
    o_a = attend("moba", q_a, k_a, v_a, past and past[0:2], H_A)
    o_b, s_hgrn_new = _hgrn(hg, w["hgrn_lb"], w["hgrn_norm"], s_hgrn, batch=batch, layer=0)
    x = _matmul([o_a, o_b], w["out_even"], ln_args=(x, w["ln1_g"][0], w["ln1_b"][0]))
    x = _ffn(x, None, w["ffn_wg"], w["ffn_wu"], w["ffn_wd"], w["ln2_g"][0], w["ln2_b"][0])

    (qkv_c, ab, g_c, q_d, k_d, v_d), kv_d_t = _project(x, [w["in_qkv_c"], w["in_ab"], w["in_g_c"]] + w["in_d"], batch,
                                                       transposed=(4, 5))
    o_c, s_gdn_new, conv_new = _gdn(qkv_c, ab, g_c, w["conv_w"], w["a_log"], w["dt_bias"], w["gdn_norm"],
                                    s_gdn, conv_buf, batch=batch)
    o_d = attend("stick", q_d, k_d, v_d, past and past[2:4], H_D)
    x = _matmul([o_c, o_d], w["out_odd"], ln_args=(x, w["ln1_g"][1], w["ln1_b"][1]))
    comb = _router(x, w["router"])
    x = _moe(x, comb, w["moe_wg"], w["moe_wu"], w["moe_wd"], w["ln2_g"][1], w["ln2_b"][1])
    k_a, v_a = kv_a_t or (k_a, v_a)
    k_d, v_d = kv_d_t or (k_d, v_d)
    return x, k_a, v_a, s_hgrn_new, s_gdn_new, conv_new, k_d, v_d


def kernel(x_prompt, x_sample, cache_k_moba, cache_v_moba, state_hgrn, state_gdn, state_gdn_conv, cache_k_sb,
           cache_v_sb, page_table, w_in_even, w_out_even, hgrn_lb, hgrn_norm, w_in_odd, w_out_odd, gdn_conv_w,
           gdn_a_log, gdn_dt_bias, gdn_norm, ln1_g, ln1_b, ln2_g, ln2_b, ffn_wg, ffn_wu, ffn_wd, router, moe_wg,
           moe_wu, moe_wd):
    assert w_in_even.shape[0] == 1 and w_in_odd.shape[0] == 1
    bp, lp, d = x_prompt.shape
    bs, ls, _ = x_sample.shape
    mix = d // 2
    hd_a, hd_d = mix // H_A, mix // H_D
    conv_dim = gdn_conv_w.shape[-1]
    bf = lambda t: t.astype(BF16)
    cols = lambda wt, lo, n: bf(wt[:, lo:lo + n])

    wie, wio = w_in_even[0], w_in_odd[0]
    g_lo = conv_dim + 2 * H_C
    d_lo = g_lo + mix
    w = {
        "in_a": [cols(wie, i * mix, mix) for i in range(3)],
        "in_hgrn": cols(wie, 3 * mix, 4 * mix),
        "out_even": [bf(w_out_even[0][:mix]), bf(w_out_even[0][mix:])],
        "hgrn_lb": hgrn_lb, "hgrn_norm": hgrn_norm[0],
        "in_qkv_c": cols(wio, 0, conv_dim),
        "in_ab": bf(jnp.pad(wio[:, conv_dim:g_lo], ((0, 0), (0, LANES - 2 * H_C)))),
        "in_g_c": cols(wio, g_lo, mix),
        "in_d": [cols(wio, d_lo + i * mix, mix) for i in range(3)],
        "out_odd": [bf(w_out_odd[0][:mix]), bf(w_out_odd[0][mix:])],
        "conv_w": gdn_conv_w[0], "a_log": gdn_a_log[0], "dt_bias": gdn_dt_bias[0], "gdn_norm": gdn_norm[0],
        "ln1_g": ln1_g, "ln1_b": ln1_b, "ln2_g": ln2_g, "ln2_b": ln2_b,
        "ffn_wg": bf(ffn_wg), "ffn_wu": bf(ffn_wu), "ffn_wd": bf(ffn_wd),
        "router": router[0], "moe_wg": bf(moe_wg[0]), "moe_wu": bf(moe_wu[0]), "moe_wd": bf(moe_wd[0]),
    }

    def run(x, batch, past, states):
        seq = x.shape[1]
        y, k_a, v_a, s_h, s_g, cv, k_d, v_d = _layer_stack(x.reshape(batch * seq, d), batch, past, states, w)
        def cache(t, n_heads, hd):
            if t.ndim == 3:
                return t.reshape(batch, n_heads, hd, seq).transpose(0, 3, 1, 2)[None]
            return t.reshape(1, batch, seq, n_heads, hd)

        return (y.reshape(batch, seq, d), cache(k_a, H_A, hd_a), cache(v_a, H_A, hd_a), s_h[None], s_g[None], cv[None],
                cache(k_d, H_D, hd_d), cache(v_d, H_D, hd_d))

    past = (cache_k_moba, cache_v_moba, cache_k_sb, cache_v_sb, page_table)
    out_s = run(x_sample, bs, past, (state_hgrn[0], state_gdn[0], state_gdn_conv[0]))
    zeros_p = (jnp.zeros((bp,) + state_hgrn.shape[2:], F32), jnp.zeros((bp,) + state_gdn.shape[2:], F32),
               jnp.zeros((bp,) + state_gdn_conv.shape[2:], F32))
    out_p = run(x_prompt, bp, None, zeros_p)
    return (out_p[0], out_s[0]) + out_p[1:] + out_s[1:]
```

```python
import functools

import jax
import jax.numpy as jnp
from jax import lax
from jax.experimental import pallas as pl
from jax.experimental.pallas import tpu as pltpu

F32 = jnp.float32
BF16 = jnp.bfloat16

H_A, H_B, H_C, H_D = 8, 4, 4, 8
MOBA_BLOCK = 256
MOBA_TOPK = 3
CONV_W = 4
N_EXPERTS = 8
DEPTH = 2
DEEPNORM_ALPHA = (2 * DEPTH) ** 0.25
LN_EPS = 1e-5
RMS_EPS = 1e-6
NEG = -1e30
LANES = 128
VMEM_LIMIT = 56 * 1024 * 1024


def _cparams(sem):
    return pltpu.CompilerParams(dimension_semantics=sem, vmem_limit_bytes=VMEM_LIMIT)


def _layernorm(y, g, b):
    mu = jnp.mean(y, axis=-1, keepdims=True)
    yc = y - mu
    var = jnp.mean(yc * yc, axis=-1, keepdims=True)
    return yc * lax.rsqrt(var + LN_EPS) * g + b


def _split3(x):
    hi = x.astype(BF16)
    r = x - hi.astype(F32)
    mid = r.astype(BF16)
    lo = (r - mid.astype(F32)).astype(BF16)
    return hi, mid, lo


def _dot(a, b):
    return jnp.dot(a, b, preferred_element_type=F32)


def _dot_nt(a, b):
    return lax.dot_general(a, b, (((1,), (1,)), ((), ())), preferred_element_type=F32)


def _dot_tn(a, b):
    return lax.dot_general(a, b, (((0,), (0,)), ((), ())), preferred_element_type=F32)


def _dot_exact_lhs(a_bf16, b):
    b0, b1, b2 = _split3(b)
    return _dot(a_bf16, b0) + _dot(a_bf16, b1) + _dot(a_bf16, b2)


def _mm_kernel(*refs, n_in, ln):
    a_refs, w_refs, rest = refs[:n_in], refs[n_in:2 * n_in], refs[2 * n_in:]
    acc = None
    for a, w in zip(a_refs, w_refs):
        d = _dot(a[...].astype(BF16), w[...])
        acc = d if acc is None else acc + d
    if ln:
        res_ref, g_ref, b_ref, o_ref = rest
        o_ref[...] = _layernorm(DEEPNORM_ALPHA * res_ref[...] + acc, g_ref[...], b_ref[...])
    else:
        (o_ref,) = rest
        o_ref[...] = acc


def _row_tile(m, largest=512):
    for t in (1024, 512, 256, 128, 64, 32, 16, 8):
        if t <= largest and m % t == 0:
            return t
    raise ValueError(m)


def _proj_kernel(x_ref, *refs, n_w, n_t):
    w_refs, wt_refs = refs[:n_w], refs[n_w:n_w + n_t]
    o_refs, ot_refs = refs[n_w + n_t:2 * n_w + n_t], refs[2 * n_w + n_t:]
    xb = x_ref[...].astype(BF16)
    for w_ref, o_ref in zip(w_refs, o_refs):
        o_ref[...] = _dot(xb, w_ref[...])
    for wt_ref, ot_ref in zip(wt_refs, ot_refs):
        ot_ref[0] = _dot_nt(wt_ref[...], xb)


def _project(x, w_list, batch, transposed=()):
    m, d = x.shape
    tm = _row_tile(m)
    seq = m // batch
    wt_list = [w_list[t].T for t in transposed] if seq % tm == 0 else []
    nl = max(seq // tm, 1)
    const = lambda w: pl.BlockSpec(w.shape, lambda i: (0, 0))
    outs = pl.pallas_call(
        functools.partial(_proj_kernel, n_w=len(w_list), n_t=len(wt_list)),
        grid=(m // tm,),
        in_specs=[pl.BlockSpec((tm, d), lambda i: (i, 0))] + [const(w) for w in w_list] + [const(w) for w in wt_list],
        out_specs=[pl.BlockSpec((tm, w.shape[1]), lambda i: (i, 0)) for w in w_list]
        + [pl.BlockSpec((1, w.shape[0], tm), lambda i: (i // nl, 0, i % nl)) for w in wt_list],
        out_shape=[jax.ShapeDtypeStruct((m, w.shape[1]), F32) for w in w_list]
        + [jax.ShapeDtypeStruct((batch, w.shape[0], seq), F32) for w in wt_list],
        compiler_params=_cparams(("parallel",)),
        name="in_proj",
    )(x, *w_list, *wt_list)
    return outs[:len(w_list)], outs[len(w_list):]


def _matmul(a_list, w_list, *, tn=None, ln_args=None):
    m = a_list[0].shape[0]
    n = w_list[0].shape[1]
    tm = _row_tile(m, largest=1024)
    ln = ln_args is not None
    tn = n if (ln or tn is None) else tn
    assert n % tn == 0
    in_specs = [pl.BlockSpec((tm, a.shape[1]), lambda i, j: (i, 0)) for a in a_list]
    in_specs += [pl.BlockSpec((w.shape[0], tn), lambda i, j: (0, j)) for w in w_list]
    args = list(a_list) + list(w_list)
    if ln:
        res, g, b = ln_args
        in_specs += [pl.BlockSpec((tm, n), lambda i, j: (i, 0)),
                     pl.BlockSpec((1, n), lambda i, j: (0, 0)),
                     pl.BlockSpec((1, n), lambda i, j: (0, 0))]
        args += [res, g.reshape(1, n), b.reshape(1, n)]
    return pl.pallas_call(
        functools.partial(_mm_kernel, n_in=len(a_list), ln=ln),
        grid=(m // tm, n // tn),
        in_specs=in_specs,
        out_specs=pl.BlockSpec((tm, tn), lambda i, j: (i, j)),
        out_shape=jax.ShapeDtypeStruct((m, n), F32),
        compiler_params=_cparams(("parallel", "arbitrary")),
        name="matmul_ln" if ln else "matmul",
    )(*args)


def _ffn_kernel(*refs, use_comb):
    if use_comb:
        x_ref, comb_ref, wg_ref, wu_ref, wd_ref, g_ref, b_ref, o_ref, xb_ref, acc_ref = refs
    else:
        x_ref, wg_ref, wu_ref, wd_ref, g_ref, b_ref, o_ref, xb_ref, acc_ref = refs
    e, j = pl.program_id(1), pl.program_id(2)
    first = jnp.logical_and(e == 0, j == 0)
    last = jnp.logical_and(e == pl.num_programs(1) - 1, j == pl.num_programs(2) - 1)

    @pl.when(first)
    def _():
        xb_ref[...] = x_ref[...].astype(BF16)
        acc_ref[...] = jnp.zeros_like(acc_ref)

    xb = xb_ref[...]
    h = jax.nn.silu(_dot(xb, wg_ref[0])) * _dot(xb, wu_ref[0])
    if use_comb:
        lane = lax.broadcasted_iota(jnp.int32, comb_ref.shape, 1)
        h = h * jnp.sum(jnp.where(lane == e, comb_ref[...], 0.0), axis=-1, keepdims=True)
    acc_ref[...] += _dot(h.astype(BF16), wd_ref[0])

    @pl.when(last)
    def _():
        o_ref[...] = _layernorm(DEEPNORM_ALPHA * x_ref[...] + acc_ref[...], g_ref[...], b_ref[...])


def _ff_tile(ff, largest=896):
    for t in (1792, 896, 512, 256, LANES):
        if t <= largest and ff % t == 0:
            return t
    raise ValueError(ff)


def _ffn(x, comb, wg, wu, wd, g, b):
    m, d = x.shape
    n_e, _, ff = wg.shape
    tm = _row_tile(m, largest=1024)
    tf = _ff_tile(ff, largest=512)
    use_comb = comb is not None
    in_specs = [pl.BlockSpec((tm, d), lambda i, e, j: (i, 0))]
    args = [x]
    if use_comb:
        in_specs.append(pl.BlockSpec((tm, comb.shape[1]), lambda i, e, j: (i, 0)))
        args.append(comb)
    in_specs += [pl.BlockSpec((1, d, tf), lambda i, e, j: (e, 0, j)),
                 pl.BlockSpec((1, d, tf), lambda i, e, j: (e, 0, j)),
                 pl.BlockSpec((1, tf, d), lambda i, e, j: (e, j, 0)),
                 pl.BlockSpec((1, d), lambda i, e, j: (0, 0)),
                 pl.BlockSpec((1, d), lambda i, e, j: (0, 0))]
    args += [wg, wu, wd, g.reshape(1, d), b.reshape(1, d)]
    return pl.pallas_call(
        functools.partial(_ffn_kernel, use_comb=use_comb),
        grid=(m // tm, n_e, ff // tf),
        in_specs=in_specs,
        out_specs=pl.BlockSpec((tm, d), lambda i, e, j: (i, 0)),
        out_shape=jax.ShapeDtypeStruct((m, d), F32),
        scratch_shapes=[pltpu.VMEM((tm, d), BF16), pltpu.VMEM((tm, d), F32)],
        compiler_params=_cparams(("parallel", "arbitrary", "arbitrary")),
        name="moe_ffn" if use_comb else "ffn",
    )(*args)


MOE_TOKENS = 1024
MOE_SUBTILES = 1
MOE_ROWS = 128


def _moe_kernel(x_ref, comb_ref, wg_ref, wu_ref, wd_ref, g_ref, b_ref, o_ref,
                xb_ref, xc_ref, yc_ref, rank_ref, rank_t_ref, comb_t_ref):
    e, j = pl.program_id(1), pl.program_id(2)
    tm, d = x_ref.shape
    r, ts = MOE_ROWS, MOE_TOKENS
    subs = [slice(s * ts, (s + 1) * ts) for s in range(tm // ts)]
    lane = lax.broadcasted_iota(jnp.int32, (ts, LANES), 1)

    @pl.when(jnp.logical_and(e == 0, j == 0))
    def _():
        xb_ref[...] = x_ref[...].astype(BF16)
        o_ref[...] = jnp.zeros_like(o_ref)
        earlier = jnp.where(lax.broadcasted_iota(jnp.int32, (ts, ts), 1) < lax.broadcasted_iota(jnp.int32, (ts, ts), 0),
                            1.0, 0.0).astype(BF16)
        for sub in subs:
            routed = jnp.where(comb_ref[sub, :] > 0.0, 1.0, 0.0).astype(BF16)
            rank = _dot(earlier, routed)
            rank_ref[sub, :] = rank
            rank_t_ref[:, sub] = rank.T
            comb_t_ref[:, sub] = comb_ref[sub, :].T

    gate_rows = [comb_t_ref[pl.ds(e, 1), sub] for sub in subs]
    n_pass = [(jnp.sum(jnp.where(g_row > 0.0, 1, 0)) + (r - 1)) // r for g_row in gate_rows]
    base = [sum(n_pass[:s], jnp.int32(0)) * r for s in range(len(subs))]
    total_pass = sum(n_pass, jnp.int32(0))

    def passes(count, fn):
        def pair(k, carry):
            fn(k * (2 * r), 2 * r)
            return carry

        lax.fori_loop(0, count // 2, pair, 0)

        @pl.when(count % 2 == 1)
        def _():
            fn((count - 1) * r, r)

    @pl.when(j == 0)
    def _():
        for s, sub in enumerate(subs):
            rank_row = rank_t_ref[pl.ds(e, 1), sub]

            def compact(start, n_rows, s=s, sub=sub, rank_row=rank_row):
                slot = (start + lax.broadcasted_iota(jnp.int32, (n_rows, ts), 0)).astype(F32)
                pick = jnp.where(jnp.logical_and(rank_row == slot, gate_rows[s] > 0.0), 1.0, 0.0).astype(BF16)
                rows = pl.ds(pl.multiple_of(base[s] + start, r), n_rows)
                xc_ref[rows, :] = _dot(pick, xb_ref[sub, :]).astype(BF16)
                yc_ref[rows, :] = jnp.zeros((n_rows, d), F32)

            passes(n_pass[s], compact)

    def expert(start, n_rows):
        rows = pl.ds(pl.multiple_of(start, r), n_rows)
        xk = xc_ref[rows, :]
        h = jax.nn.silu(_dot(xk, wg_ref[0])) * _dot(xk, wu_ref[0])
        yc_ref[rows, :] += _dot(h.astype(BF16), wd_ref[0])

    passes(total_pass, expert)

    @pl.when(j == pl.num_programs(2) - 1)
    def _():
        for s, sub in enumerate(subs):
            rank_col = jnp.sum(jnp.where(lane == e, rank_ref[sub, :], 0.0), axis=-1, keepdims=True)
            gate_col = jnp.sum(jnp.where(lane == e, comb_ref[sub, :], 0.0), axis=-1, keepdims=True)

            def place(start, n_rows, s=s, sub=sub, rank_col=rank_col, gate_col=gate_col):
                slot = (start + lax.broadcasted_iota(jnp.int32, (ts, n_rows), 1)).astype(F32)
                put = jnp.where(jnp.logical_and(rank_col == slot, gate_col > 0.0), 1.0, 0.0).astype(BF16)
                rows = pl.ds(pl.multiple_of(base[s] + start, r), n_rows)
                o_ref[sub, :] += gate_col * _dot(put, yc_ref[rows, :].astype(BF16))

            passes(n_pass[s], place)

    @pl.when(jnp.logical_and(e == pl.num_programs(1) - 1, j == pl.num_programs(2) - 1))
    def _():
        o_ref[...] = _layernorm(DEEPNORM_ALPHA * x_ref[...] + o_ref[...], g_ref[...], b_ref[...])


def _moe(x, comb, wg, wu, wd, g, b):
    m, d = x.shape
    n_e, _, ff = wg.shape
    tm = MOE_TOKENS * MOE_SUBTILES
    if m % tm != 0:
        return _ffn(x, comb, wg, wu, wd, g, b)
    tf = _ff_tile(ff, largest=1792)
    assert MOE_TOKENS % (2 * MOE_ROWS) == 0 and comb.shape[1] == LANES and n_e <= LANES
    cap = tm + MOE_SUBTILES * MOE_ROWS
    return pl.pallas_call(
        _moe_kernel,
        grid=(m // tm, n_e, ff // tf),
        in_specs=[pl.BlockSpec((tm, d), lambda i, e, j: (i, 0)),
                  pl.BlockSpec((tm, LANES), lambda i, e, j: (i, 0)),
                  pl.BlockSpec((1, d, tf), lambda i, e, j: (e, 0, j)),
                  pl.BlockSpec((1, d, tf), lambda i, e, j: (e, 0, j)),
                  pl.BlockSpec((1, tf, d), lambda i, e, j: (e, j, 0)),
                  pl.BlockSpec((1, d), lambda i, e, j: (0, 0)),
                  pl.BlockSpec((1, d), lambda i, e, j: (0, 0))],
        out_specs=pl.BlockSpec((tm, d), lambda i, e, j: (i, 0)),
        out_shape=jax.ShapeDtypeStruct((m, d), F32),
        scratch_shapes=[pltpu.VMEM((tm, d), BF16), pltpu.VMEM((cap, d), BF16), pltpu.VMEM((cap, d), F32),
                        pltpu.VMEM((tm, LANES), F32), pltpu.VMEM((LANES, tm), F32), pltpu.VMEM((LANES, tm), F32)],
        compiler_params=_cparams(("parallel", "arbitrary", "arbitrary")),
        name="moe_sparse",
    )(x, comb, wg, wu, wd, g.reshape(1, d), b.reshape(1, d))


def _router_kernel(x_ref, r_ref, o_ref):
    logits = _dot_mid(x_ref[...], r_ref[...])
    lane = lax.broadcasted_iota(jnp.int32, logits.shape, 1)
    logits = jnp.where(lane < N_EXPERTS, logits, -jnp.inf)
    m1 = jnp.max(logits, axis=-1, keepdims=True)
    i1 = jnp.min(jnp.where(logits == m1, lane, LANES), axis=-1, keepdims=True)
    rest = jnp.where(lane == i1, -jnp.inf, logits)
    m2 = jnp.max(rest, axis=-1, keepdims=True)
    i2 = jnp.min(jnp.where(rest == m2, lane, LANES), axis=-1, keepdims=True)
    e2 = jnp.exp(m2 - m1)
    den = 1.0 + e2
    o_ref[...] = jnp.where(lane == i1, 1.0 / den, 0.0) + jnp.where(lane == i2, e2 / den, 0.0)


def _router(x, router):
    m, d = x.shape
    tm = _row_tile(m)
    r = jnp.pad(router, ((0, 0), (0, LANES - router.shape[1])))
    return pl.pallas_call(
        _router_kernel,
        grid=(m // tm,),
        in_specs=[pl.BlockSpec((tm, d), lambda i: (i, 0)), pl.BlockSpec((d, LANES), lambda i: (0, 0))],
        out_specs=pl.BlockSpec((tm, LANES), lambda i: (i, 0)),
        out_shape=jax.ShapeDtypeStruct((m, LANES), F32),
        compiler_params=_cparams(("parallel",)),
        name="router",
    )(x, r)


def _topk_lanes(g, idx, axis=-1):
    sel = jnp.zeros(g.shape, F32)
    for _ in range(MOBA_TOPK):
        m = jnp.max(g, axis=axis, keepdims=True)
        first = jnp.min(jnp.where(g == m, idx, LANES), axis=axis, keepdims=True)
        hit = jnp.logical_and(idx == first, m > -jnp.inf)
        sel = jnp.where(hit, 1.0, sel)
        g = jnp.where(hit, -jnp.inf, g)
    return sel


def _moba_kernel(slope_ref, q_ref, k_ref, v_ref, o_ref, kb_ref, vt_ref, km_ref, m_ref, l_ref, acc_ref, sel_ref,
                 *, tq, nb, q_offset, hd):
    i = pl.program_id(2)
    blk = MOBA_BLOCK

    @pl.when(i == 0)
    def _():
        kb_ref[...] = k_ref[...].astype(BF16)
        km_ref[...] = jnp.zeros_like(km_ref)
        for j in range(nb):
            rows = slice(j * blk, (j + 1) * blk)
            vt_ref[j] = v_ref[rows, :].T.astype(BF16)
            km_ref[j:j + 1, :] = jnp.sum(k_ref[rows, :], axis=0, keepdims=True) * (1.0 / blk)

    q0 = q_offset + i * tq
    own = q0 // blk
    q = q_ref[...]
    lane = lax.broadcasted_iota(jnp.int32, (tq, LANES), 1)
    q2 = jnp.concatenate([jnp.where(lane < hd, q, 0.0), jnp.where(lane >= hd, q, 0.0)], axis=0)
    gate_t = _dot_mid(km_ref[...], q2, dot=_dot_nt)
    blk_id = lax.broadcasted_iota(jnp.int32, gate_t.shape, 0)
    sel_ref[...] = _topk_lanes(jnp.where(blk_id < own, gate_t, -jnp.inf), blk_id, axis=0)
    qb = (q2 * hd ** -0.5).astype(BF16)
    c2 = lax.broadcasted_iota(jnp.int32, (blk, 2 * tq), 1)
    d0 = (jnp.where(c2 >= tq, c2 - tq, c2) - lax.broadcasted_iota(jnp.int32, (blk, 2 * tq), 0)).astype(F32)
    slope_lane = slope_ref[...]
    c1 = lax.broadcasted_iota(jnp.int32, (1, 2 * tq), 1)
    slope = jnp.where(c1 < tq, slope_lane[:, 0:1], slope_lane[:, hd:hd + 1])
    slope_d0 = slope * d0
    first = lax.broadcasted_iota(jnp.int32, (2 * hd, tq), 0) < hd
    m_ref[...] = jnp.full(m_ref.shape, NEG, F32)
    l_ref[...] = jnp.zeros_like(l_ref)
    acc_ref[...] = jnp.zeros_like(acc_ref)

    def tiles(js, diagonal):
        scores = []
        for j in js:
            off = (q0 - j * blk).astype(F32)
            s = _dot_nt(kb_ref[pl.ds(pl.multiple_of(j * blk, blk), blk), :], qb) - (slope_d0 + slope * off)
            if diagonal:
                s = jnp.where(d0 + off >= 0.0, s, NEG)
            else:
                s = jnp.where(sel_ref[pl.ds(j, 1), :] > 0.0, s, NEG)
            scores.append(s)
        m_old = m_ref[...]
        m_new = functools.reduce(jnp.maximum, [m_old] + [jnp.max(s, axis=0, keepdims=True) for s in scores])
        alpha = jnp.exp(m_old - m_new)
        l_new = alpha * l_ref[...]
        pv = None
        for j, s in zip(js, scores):
            p = jnp.exp(s - m_new)
            l_new = l_new + jnp.sum(p, axis=0, keepdims=True)
            d = _dot(vt_ref[j], p.astype(BF16))
            pv = d if pv is None else pv + d
        m_ref[...] = m_new
        l_ref[...] = l_new
        acc_ref[...] = (jnp.where(first, alpha[:, :tq], alpha[:, tq:]) * acc_ref[...]
                        + jnp.where(first, pv[:, :tq], pv[:, tq:]))

    tiles([own], True)

    def body(t, carry):
        tiles([own - 1 - 4 * t - n for n in range(4)], False)
        return carry

    lax.fori_loop(0, own // 4, body, 0)
    rest = own % 4

    @pl.when(rest >= 2)
    def _():
        tiles([rest - 1, rest - 2], False)

    @pl.when(rest % 2 == 1)
    def _():
        tiles([own - own], False)
    l = l_ref[...]
    o_ref[...] = (acc_ref[...] / jnp.where(first, l[:, :tq], l[:, tq:])).T


def _alibi_slopes_lanes(n_heads, hd):
    slopes = jnp.exp2(-8.0 * jnp.arange(1, n_heads + 1, dtype=F32) / n_heads)
    return jnp.repeat(slopes, hd).reshape(1, n_heads * hd)


def _attn_call(kernel, q, k, v, *, batch, q_offset, n_heads, extra_in=(), extra_specs=(), extra_scratch=(), name):
    lq, lk = q.shape[0] // batch, k.shape[0] // batch
    hd = q.shape[1] // n_heads
    assert 2 * hd == LANES and lk % MOBA_BLOCK == 0
    tq = min(MOBA_BLOCK, lq)
    assert lq % tq == 0 and MOBA_BLOCK % tq == 0 and q_offset % tq == 0
    nq = lq // tq
    return pl.pallas_call(
        functools.partial(kernel, tq=tq, q_offset=q_offset, hd=hd),
        grid=(batch, n_heads // 2, nq),
        in_specs=list(extra_specs) + [
            pl.BlockSpec((tq, LANES), lambda b, h, i: (b * nq + i, h)),
            pl.BlockSpec((lk, LANES), lambda b, h, i: (b, h)),
            pl.BlockSpec((lk, LANES), lambda b, h, i: (b, h))],
        out_specs=pl.BlockSpec((tq, LANES), lambda b, h, i: (b * nq + i, h)),
        out_shape=jax.ShapeDtypeStruct(q.shape, F32),
        scratch_shapes=list(extra_scratch),
        compiler_params=_cparams(("parallel", "parallel", "arbitrary")),
        name=name,
    )(*extra_in, q, k, v)


def _moba(q, k, v, *, batch, q_offset):
    nb = k.shape[0] // batch // MOBA_BLOCK
    assert nb <= LANES
    slopes = _alibi_slopes_lanes(H_A, q.shape[1] // H_A)
    tq = min(MOBA_BLOCK, q.shape[0] // batch)
    nb_rows = -(-nb // 8) * 8
    return _attn_call(
        functools.partial(_moba_kernel, nb=nb), q, k, v, batch=batch, q_offset=q_offset, n_heads=H_A,
        extra_in=(slopes,), extra_specs=(pl.BlockSpec((1, LANES), lambda b, h, i: (0, h)),),
        extra_scratch=(pltpu.VMEM((nb * MOBA_BLOCK, LANES), BF16), pltpu.VMEM((nb, LANES, MOBA_BLOCK), BF16),
                       pltpu.VMEM((nb_rows, LANES), F32), pltpu.VMEM((1, 2 * tq), F32), pltpu.VMEM((1, 2 * tq), F32),
                       pltpu.VMEM((LANES, tq), F32), pltpu.VMEM((nb_rows, 2 * tq), F32)), name="moba")


EXP_UNDERFLOW = -110.0


def _softplus(z):
    return jnp.maximum(z, 0.0) + jnp.log(1.0 + jnp.exp(-jnp.abs(z)))


def _stick_kernel(q_ref, k_ref, v_ref, o_ref, kb_ref, vb_ref, c_ref, acc_ref, *, tq, q_offset, hd):
    i = pl.program_id(2)
    blk = MOBA_BLOCK

    @pl.when(i == 0)
    def _():
        kb_ref[...] = k_ref[...].astype(BF16)
        vb_ref[...] = v_ref[...].astype(BF16)

    q0 = q_offset + i * tq
    own = q0 // blk
    q = q_ref[...] * hd ** -0.5
    lane = lax.broadcasted_iota(jnp.int32, (tq, LANES), 1)
    qb = jnp.concatenate([jnp.where(lane < hd, q, 0.0), jnp.where(lane >= hd, q, 0.0)], axis=0).astype(BF16)
    r2 = lax.broadcasted_iota(jnp.int32, (2 * tq, blk), 0)
    d0 = jnp.where(r2 >= tq, r2 - tq, r2) - lax.broadcasted_iota(jnp.int32, (2 * tq, blk), 1)
    u = jnp.where(lax.broadcasted_iota(jnp.int32, (blk, blk), 0) > lax.broadcasted_iota(jnp.int32, (blk, blk), 1),
                  1.0, 0.0).astype(BF16)
    c_ref[...] = jnp.zeros_like(c_ref)
    acc_ref[...] = jnp.zeros_like(acc_ref)

    halves = [slice(0, tq), slice(tq, 2 * tq)]

    def tile(j, diagonal):
        start = pl.multiple_of(j * blk, blk)
        kj = kb_ref[pl.ds(start, blk), :]
        zs = [_dot_nt(qb[h], kj) for h in halves]
        log_keep = [-_softplus(z) for z in zs]
        log_beta = [z + lk for z, lk in zip(zs, log_keep)]
        if diagonal:
            before = d0[halves[0]] + (q0 - j * blk) > 0
            log_keep = [jnp.where(before, lk, 0.0) for lk in log_keep]
        his = [lk.astype(BF16) for lk in log_keep]
        los = [(lk - hi.astype(F32)).astype(BF16) for lk, hi in zip(log_keep, his)]
        later = [_dot(jnp.concatenate([hi, lo], axis=0), u) for hi, lo in zip(his, los)]
        later = [t[:tq] + t[tq:] for t in later]
        cs = [c_ref[h, :] for h in halves]
        ws = [jnp.exp(lb + lt + c) for lb, lt, c in zip(log_beta, later, cs)]
        if diagonal:
            ws = [jnp.where(before, w, 0.0) for w in ws]
        for h, c, lt, lk in zip(halves, cs, later, log_keep):
            c_ref[h, :] = c + lt[:, 0:1] + lk[:, 0:1]
        vj = vb_ref[pl.ds(start, blk), :]
        pv = [_dot(w.astype(BF16), vj) for w in ws]
        acc_ref[...] += jnp.where(lane < hd, pv[0], pv[1])

    tile(own, True)

    def more(carry):
        t, c_max = carry
        return jnp.logical_and(t <= own, c_max > EXP_UNDERFLOW)

    def body(carry):
        t, _ = carry
        tile(own - t, False)
        return t + 1, jnp.max(c_ref[...])

    lax.while_loop(more, body, (jnp.int32(1), jnp.max(c_ref[...])))
    o_ref[...] = acc_ref[...]


def _stick(q, k, v, *, batch, q_offset):
    tq = min(MOBA_BLOCK, q.shape[0] // batch)
    lk = k.shape[0] // batch
    return _attn_call(_stick_kernel, q, k, v, batch=batch, q_offset=q_offset, n_heads=H_D,
                      extra_scratch=(pltpu.VMEM((lk, LANES), BF16), pltpu.VMEM((lk, LANES), BF16),
                                     pltpu.VMEM((2 * tq, 1), F32), pltpu.VMEM((tq, LANES), F32)), name="stick")


CHUNK = 64
SUB = 16
SEQ_BLOCK = 512
EXP_CLAMP = 80.0


def _dot_mid(a, b, dot=_dot):
    a0 = a.astype(BF16)
    a1 = (a - a0.astype(F32)).astype(BF16)
    b0 = b.astype(BF16)
    b1 = (b - b0.astype(F32)).astype(BF16)
    return dot(a0, b0) + (dot(a0, b1) + dot(a1, b0))


def _incl_lower(c):
    r = lax.broadcasted_iota(jnp.int32, (c, c), 0)
    s = lax.broadcasted_iota(jnp.int32, (c, c), 1)
    return r, s


def _hgrn_kernel(q_ref, f_ref, i_ref, g_ref, lb_ref, gn_ref, s0_ref, o_ref, s_ref, st_ref, *, c, n_chunks, n_heads,
                 layer):
    l = pl.program_id(1)
    dk = LANES

    w = n_heads * dk

    @pl.when(l == 0)
    def _():
        for h in range(n_heads):
            st_ref[h * dk:(h + 1) * dk, :] = s0_ref[0, h]

    r, s = _incl_lower(c)
    tri = jnp.where(s <= r, 1.0, 0.0).astype(BF16)
    sub = min(SUB, c)
    n = n_heads * c
    shift_c, shift_s = c.bit_length() - 1, sub.bit_length() - 1
    ri = lax.broadcasted_iota(jnp.int32, (n_heads * sub, n), 0)
    ci = lax.broadcasted_iota(jnp.int32, (n_heads * sub, n), 1)
    pair = jnp.right_shift(ri, shift_s) == jnp.right_shift(ci, shift_c)
    t_loc, s_loc = jnp.bitwise_and(ri, sub - 1), jnp.bitwise_and(ci, c - 1)
    own_cols = (jnp.right_shift(lax.broadcasted_iota(jnp.int32, (n, w), 0), shift_c)
                == lax.broadcasted_iota(jnp.int32, (n, w), 1) // dk)
    gn = gn_ref[...]
    e = jnp.exp(lb_ref[...] - jnp.max(lb_ref[...], axis=0, keepdims=True))
    lb = jnp.sum(e[:layer + 1], axis=0, keepdims=True) / jnp.sum(e, axis=0, keepdims=True)

    def stack(t):
        return jnp.concatenate([t[:, h * dk:(h + 1) * dk] for h in range(n_heads)], axis=0)

    def run(chunks):
        m = len(chunks)
        rows = [pl.ds(i * c, c) if isinstance(i, int) else pl.ds(pl.multiple_of(i * c, c), c) for i in chunks]
        zs = [f_ref[r_, :] for r_ in rows]
        ks = [(1.0 - lb) * jax.nn.sigmoid(-z) for z in zs]
        qs = [jax.nn.silu(q_ref[r_, :]) for r_ in rows]
        cgs = [_dot_exact_lhs(tri, jnp.log(lb + (1.0 - lb) * jax.nn.sigmoid(z))) for z in zs]
        v4s = [stack(i_ref[r_, :]).astype(BF16) for r_ in rows]
        intra = [[] for _ in range(m)]
        for b in range(c // sub):
            lo, hi = b * sub, (b + 1) * sub
            refs = [cg[lo - 1:lo] if b else jnp.zeros((1, w), F32) for cg in cgs]
            q_sub = [stack(q[lo:hi] * jnp.exp(cg[lo:hi] - ref_pt)).astype(BF16) for q, cg, ref_pt in zip(qs, cgs, refs)]
            k_all = [stack(k * jnp.exp(jnp.minimum(ref_pt - cg, EXP_CLAMP))).astype(BF16)
                     for k, cg, ref_pt in zip(ks, cgs, refs)]
            a = [_dot_nt(q_, k_) for q_, k_ in zip(q_sub, k_all)]
            a = [jnp.where(jnp.logical_and(pair, s_loc <= t_loc + lo), a_, 0.0).astype(BF16) for a_ in a]
            for x_, (a_, v4) in enumerate(zip(a, v4s)):
                intra[x_].append(_dot(a_, v4))
        q_dec = [stack(q * jnp.exp(cg)).astype(BF16) for q, cg in zip(qs, cgs)]
        lasts = [cg[c - 1:c] for cg in cgs]
        kts = [stack(k * jnp.exp(last - cg)).astype(BF16) for k, last, cg in zip(ks, lasts, cgs)]
        for x_ in range(m):
            st = st_ref[...]
            from_state = _dot_nt(q_dec[x_], st.astype(BF16))
            v_wide = jnp.where(own_cols, jnp.concatenate([v4s[x_]] * n_heads, axis=1), 0.0)
            e_last = jnp.concatenate([jnp.broadcast_to(jnp.exp(lasts[x_][:, h * dk:(h + 1) * dk]), (dk, dk))
                                      for h in range(n_heads)], axis=0)
            st_ref[...] = st * e_last + _dot_tn(v_wide, kts[x_])
            for h in range(n_heads):
                o = from_state[h * c:(h + 1) * c, h * dk:(h + 1) * dk] + jnp.concatenate(
                    [part[h * sub:(h + 1) * sub] for part in intra[x_]], axis=0)
                o = o * lax.rsqrt(jnp.mean(o * o, axis=-1, keepdims=True) + RMS_EPS) * gn
                o_ref[rows[x_], h * dk:(h + 1) * dk] = o * jax.nn.sigmoid(g_ref[rows[x_], h * dk:(h + 1) * dk])

    per_trip = 4 if n_chunks % 4 == 0 else (2 if n_chunks % 2 == 0 else 1)

    def trip(i, carry):
        run([per_trip * i + x_ for x_ in range(per_trip)])
        return carry

    lax.fori_loop(0, n_chunks // per_trip, trip, 0)

    @pl.when(l == pl.num_programs(1) - 1)
    def _():
        for h in range(n_heads):
            s_ref[0, h] = st_ref[h * dk:(h + 1) * dk, :]


def _hgrn(hg, lb, gn, s0, *, batch, layer=0):
    t, width = hg.shape
    seq = t // batch
    w = width // 4
    n_heads = w // LANES
    c = min(CHUNK, seq)
    lblk = min(SEQ_BLOCK, seq)
    assert seq % lblk == 0 and lblk % c == 0 and c % min(SUB, c) == 0
    nl = seq // lblk
    part = lambda p: pl.BlockSpec((lblk, w), lambda b, l: (b * nl + l, p))
    state = pl.BlockSpec((1, n_heads, LANES, LANES), lambda b, l: (b, 0, 0, 0))
    o, st = pl.pallas_call(
        functools.partial(_hgrn_kernel, c=c, n_chunks=lblk // c, n_heads=n_heads, layer=layer),
        grid=(batch, nl),
        in_specs=[part(0), part(1), part(2), part(3),
                  pl.BlockSpec(lb.shape, lambda b, l: (0, 0)), pl.BlockSpec((1, LANES), lambda b, l: (0, 0)), state],
        out_specs=[pl.BlockSpec((lblk, w), lambda b, l: (b * nl + l, 0)), state],
        out_shape=[jax.ShapeDtypeStruct((t, w), F32), jax.ShapeDtypeStruct(s0.shape, F32)],
        scratch_shapes=[pltpu.VMEM((n_heads * LANES, LANES), F32)],
        compiler_params=_cparams(("parallel", "arbitrary")),
        name="hgrn2",
    )(hg, hg, hg, hg, lb, gn.reshape(1, LANES), jnp.swapaxes(s0, -1, -2))
    return o, jnp.swapaxes(st, -1, -2)


def _unit_lower_inverses(lms, n, period):
    bs = min(SUB, period)
    r, s = _incl_lower(n)
    eye = jnp.where(r == s, 1.0, 0.0)
    shift = bs.bit_length() - 1
    same = jnp.right_shift(r, shift) == jnp.right_shift(s, shift)

    def mm(a, b):
        return _dot(a.astype(BF16), b.astype(BF16))

    def neumann(mats, order):
        invs, pws, k = [eye + a for a in mats], list(mats), 2
        while k < order:
            pws = [mm(pw, pw) for pw in pws]
            invs = [inv + mm(inv, pw) for inv, pw in zip(invs, pws)]
            k *= 2
        return invs

    inv_ds = neumann([-jnp.where(same, lm, 0.0) for lm in lms], bs)
    if period == bs:
        return inv_ds
    ms = [mm(inv_d, jnp.where(same, 0.0, lm)) for inv_d, lm in zip(inv_ds, lms)]
    return [mm(a, inv_d) for a, inv_d in zip(neumann([-m for m in ms], period // bs), inv_ds)]


def _gdn_kernel(x_ref, ab_ref, g_ref, cw_ref, al_ref, dtb_ref, gn_ref, s0_ref, cb_ref, o_ref, s_ref, nb_ref,
                xx_ref, y_ref, st_ref, *, c, n_chunks, n_heads, lblk):
    l = pl.program_id(1)
    dk = LANES
    halo = 8
    w = n_heads * dk

    @pl.when(l == 0)
    def _():
        for h in range(n_heads):
            st_ref[:, h * dk:(h + 1) * dk] = s0_ref[0, h]
        xx_ref[halo - (CONV_W - 1):halo, :] = cb_ref[0]

    xx_ref[halo:halo + lblk, :] = x_ref[...]
    y = xx_ref[halo:halo + lblk, :] * cw_ref[CONV_W - 1:CONV_W, :]
    for i in range(CONV_W - 1):
        off = halo - (CONV_W - 1) + i
        y = y + xx_ref[off:off + lblk, :] * cw_ref[i:i + 1, :]
    y_ref[...] = jax.nn.silu(y)
    tail = xx_ref[halo + lblk - (CONV_W - 1):halo + lblk, :]
    xx_ref[halo - (CONV_W - 1):halo, :] = tail

    r, s = _incl_lower(c)
    tri = jnp.where(s <= r, 1.0, 0.0).astype(BF16)
    n = n_heads * c
    rr, ss = _incl_lower(n)
    shift = c.bit_length() - 1
    same_head = jnp.right_shift(rr, shift) == jnp.right_shift(ss, shift)
    incl = jnp.logical_and(same_head, ss <= rr)
    strict = jnp.logical_and(same_head, ss < rr)
    own_cols = (jnp.right_shift(lax.broadcasted_iota(jnp.int32, (n, w), 0), shift)
                == lax.broadcasted_iota(jnp.int32, (n, w), 1) // dk)
    gn = gn_ref[...]

    def l2n(t):
        return t * lax.rsqrt(jnp.sum(t * t, axis=-1, keepdims=True) + RMS_EPS)

    def stack(f):
        return jnp.concatenate([f(h) for h in range(n_heads)], axis=0)

    def own_block(t):
        return stack(lambda h: t[h * c:(h + 1) * c, h * dk:(h + 1) * dk])

    def front(i):
        rows = pl.ds(i * c, c) if isinstance(i, int) else pl.ds(pl.multiple_of(i * c, c), c)
        ab = ab_ref[rows, :]
        pre = ab + dtb_ref[...]
        log_a = -jnp.exp(al_ref[...]) * (jnp.maximum(pre, 0.0) + jnp.log(1.0 + jnp.exp(-jnp.abs(pre))))
        beta_all = jax.nn.sigmoid(ab)
        cg_all = _dot_exact_lhs(tri, log_a)
        cg_t = cg_all.T
        q = stack(lambda h: l2n(y_ref[rows, h * dk:(h + 1) * dk])) * dk ** -0.5
        k = stack(lambda h: l2n(y_ref[rows, w + h * dk:w + (h + 1) * dk]))
        v = stack(lambda h: y_ref[rows, 2 * w + h * dk:2 * w + (h + 1) * dk])
        cg = stack(lambda h: cg_all[:, h:h + 1])
        beta = stack(lambda h: beta_all[:, n_heads + h:n_heads + h + 1])
        cg_row = jnp.concatenate([cg_t[h:h + 1, :c] for h in range(n_heads)], axis=1)
        last = stack(lambda h: jnp.broadcast_to(cg_all[c - 1:c, h:h + 1], (c, 1)))
        decay = jnp.exp(jnp.where(incl, cg - cg_row, NEG))
        kk = _dot_mid(k, k, dot=_dot_nt) * decay * beta
        return jnp.where(strict, kk, 0.0), (rows, q, k, v, cg, beta, last, decay, cg_all)

    def back(side, t_inv):
        rows, q, k, v, cg, beta, last, decay, cg_all = side
        e_cg = jnp.exp(cg)
        sol = _dot_mid(t_inv, jnp.concatenate([v * beta, k * (beta * e_cg)], axis=1))
        qk = (_dot_nt(q.astype(BF16), k.astype(BF16)) * decay).astype(BF16)
        kt = (k * jnp.exp(last - cg)).astype(BF16)
        e_last = jnp.concatenate([jnp.broadcast_to(jnp.exp(cg_all[c - 1:c, h:h + 1]), (1, dk)) for h in range(n_heads)],
                                 axis=1)
        return rows, sol[:, :dk], sol[:, dk:].astype(BF16), (q * e_cg).astype(BF16), qk, kt, e_last

    def advance(prepared):
        rows, sol_v, sol_k, q_decayed, qk, kt, e_last = prepared
        st = st_ref[...]
        stb = st.astype(BF16)
        u = sol_v - own_block(_dot(sol_k, stb))
        o = own_block(_dot(q_decayed, stb)) + _dot(qk, u.astype(BF16))
        u_wide = jnp.where(own_cols, jnp.concatenate([u] * n_heads, axis=1), 0.0)
        st_ref[...] = st * e_last + _dot_tn(kt, u_wide.astype(BF16))
        o = o * lax.rsqrt(jnp.mean(o * o, axis=-1, keepdims=True) + RMS_EPS) * gn
        for h in range(n_heads):
            o_ref[rows, h * dk:(h + 1) * dk] = o[h * c:(h + 1) * c] * jax.nn.silu(g_ref[rows, h * dk:(h + 1) * dk])

    def run(chunks):
        fronts = [front(i) for i in chunks]
        inverses = _unit_lower_inverses([lm for lm, _ in fronts], n, c)
        for prepared in [back(side, t_inv) for (_, side), t_inv in zip(fronts, inverses)]:
            advance(prepared)

    per_trip = 8 if n_chunks % 8 == 0 else (4 if n_chunks % 4 == 0 else 2)

    def trip(i, carry):
        run([per_trip * i + m for m in range(per_trip)])
        return carry

    lax.fori_loop(0, n_chunks // per_trip, trip, 0)
    if n_chunks % per_trip:
        run([n_chunks - 1])

    @pl.when(l == pl.num_programs(1) - 1)
    def _():
        for h in range(n_heads):
            s_ref[0, h] = st_ref[:, h * dk:(h + 1) * dk]
        nb_ref[0] = tail


def _gdn(qkv, ab, g, conv_w, a_log, dt_bias, gn, s0, conv_buf, *, batch):
    t, width = qkv.shape
    seq = t // batch
    w = width // 3
    n_heads = w // LANES
    c = min(CHUNK, seq)
    lblk = min(SEQ_BLOCK, seq)
    assert seq % lblk == 0 and lblk % c == 0 and lblk >= CONV_W - 1
    nl = seq // lblk
    pad_lane = lambda vec: jnp.pad(vec, (0, LANES - vec.shape[0])).reshape(1, LANES)
    rows = lambda width_: pl.BlockSpec((lblk, width_), lambda b, l: (b * nl + l, 0))
    const = lambda shape: pl.BlockSpec(shape, lambda b, l: (0,) * len(shape))
    state = pl.BlockSpec((1, n_heads, LANES, LANES), lambda b, l: (b, 0, 0, 0))
    buf = pl.BlockSpec((1, CONV_W - 1, width), lambda b, l: (b, 0, 0))
    return pl.pallas_call(
        functools.partial(_gdn_kernel, c=c, n_chunks=lblk // c, n_heads=n_heads, lblk=lblk),
        grid=(batch, nl),
        in_specs=[rows(width), rows(LANES), rows(w), const((CONV_W, width)), const((1, LANES)), const((1, LANES)),
                  const((1, LANES)), state, buf],
        out_specs=[rows(w), state, buf],
        out_shape=[jax.ShapeDtypeStruct((t, w), F32), jax.ShapeDtypeStruct(s0.shape, F32),
                   jax.ShapeDtypeStruct(conv_buf.shape, F32)],
        scratch_shapes=[pltpu.VMEM((lblk + 8, width), F32), pltpu.VMEM((lblk, width), F32),
                        pltpu.VMEM((LANES, w), F32)],
        compiler_params=_cparams(("parallel", "arbitrary")),
        name="gated_deltanet",
    )(qkv, ab, g, conv_w, pad_lane(a_log), pad_lane(dt_bias), gn.reshape(1, LANES), s0, conv_buf)


PAGES_PER_STEP = 32


def _head_fold(pv, n_heads, nq, hd):
    return jnp.concatenate([pv[h * nq:(h + 1) * nq, h * hd:(h + 1) * hd] for h in range(n_heads)], axis=0)


def _row_ids(rows, nq):
    r = lax.broadcasted_iota(jnp.int32, (rows, 1), 0)
    return r // nq, r % nq


def _page_group(refs, first, count):
    return jnp.concatenate([refs[first + r][0, 0].reshape(-1, refs[first + r].shape[-1]) for r in range(count)], axis=1)


def _stick_paged_kernel(pt_ref, qbd_ref, kn_ref, vn_ref, k_hbm, v_hbm, o_ref, kbuf, vbuf, sem, *, n_heads, nq, hd,
                        n_pages):
    b = pl.program_id(0)
    rows = n_heads * nq
    page = kbuf.shape[-1]
    ppg = kbuf.shape[1]
    n_groups = n_pages // ppg
    _, row_q = _row_ids(rows, nq)
    qbd = (qbd_ref[0] * hd ** -0.5).astype(BF16)

    def copies(g, slot):
        out = []
        for r in range(ppg):
            pg = pt_ref[b, g * ppg + r]
            out.append(pltpu.make_async_copy(k_hbm.at[0, pg], kbuf.at[slot, r], sem.at[0, slot, r]))
            out.append(pltpu.make_async_copy(v_hbm.at[0, pg], vbuf.at[slot, r], sem.at[1, slot, r]))
        return out

    def group(buf, slot):
        return jnp.concatenate([buf[slot, r].reshape(n_heads * hd, page) for r in range(ppg)], axis=1).astype(BF16)

    def strict_later(n):
        return jnp.where(lax.broadcasted_iota(jnp.int32, (n, n), 0) > lax.broadcasted_iota(jnp.int32, (n, n), 1),
                         1.0, 0.0).astype(BF16)

    def weights(z, mask, carry, u):
        log_keep = -_softplus(z)
        log_beta = z + log_keep
        if mask is not None:
            log_keep = jnp.where(mask, log_keep, 0.0)
        hi = log_keep.astype(BF16)
        lo = (log_keep - hi.astype(F32)).astype(BF16)
        later = _dot(hi, u) + _dot(lo, u)
        w = jnp.exp(log_beta + later + carry)
        if mask is not None:
            w = jnp.where(mask, w, 0.0)
        return w.astype(BF16), carry + jnp.sum(log_keep, axis=-1, keepdims=True)

    for cp in copies(n_groups - 1, 0):
        cp.start()

    n = kn_ref.shape[1]
    z = _dot_nt(qbd, kn_ref[0].astype(BF16))
    col = lax.broadcasted_iota(jnp.int32, (rows, n), 1)
    w, c = weights(z, col < row_q, jnp.zeros((rows, 1), F32), strict_later(n))
    acc = _dot(w, vn_ref[0].astype(BF16))
    u = strict_later(ppg * page)

    def more(carry):
        g, c, _ = carry
        return jnp.logical_and(g >= 0, jnp.max(c) > EXP_UNDERFLOW)

    def body(carry):
        g, c, acc = carry
        slot = (n_groups - 1 - g) % 2

        @pl.when(g > 0)
        def _():
            for cp in copies(g - 1, 1 - slot):
                cp.start()

        for cp in copies(g, slot):
            cp.wait()
        w, c = weights(_dot(qbd, group(kbuf, slot)), None, c, u)
        return g - 1, c, acc + _dot_nt(w, group(vbuf, slot))

    g, _, acc = lax.while_loop(more, body, (jnp.int32(n_groups - 1), c, acc))

    @pl.when(g >= 0)
    def _():
        for cp in copies(g, (n_groups - 1 - g) % 2):
            cp.wait()

    o_ref[0] = _head_fold(acc, n_heads, nq, hd)


def _moba_paged_kernel(pt_ref, qbd_ref, kn_ref, vn_ref, *rest, pps, n_heads, nq, hd, past_len):
    k_refs, v_refs = rest[:pps], rest[pps:2 * pps]
    o_ref, mo_ref, lo_ref, acco_ref, m_ref, l_ref, acc_ref, km_ref = rest[2 * pps:]
    s = pl.program_id(1)
    rows = n_heads * nq
    page = k_refs[0].shape[-1]
    ppb = MOBA_BLOCK // page
    row_h, row_q = _row_ids(rows, nq)
    slope = jnp.exp2(-8.0 * (row_h + 1).astype(F32) / n_heads)
    lane = lax.broadcasted_iota(jnp.int32, (rows, LANES), 1)
    q_f32 = qbd_ref[0]
    qbd = (q_f32 * hd ** -0.5).astype(BF16)

    @pl.when(s == 0)
    def _():
        n = kn_ref.shape[1]
        col = lax.broadcasted_iota(jnp.int32, (rows, n), 1)
        sc = _dot_nt(qbd, kn_ref[0].astype(BF16)) - slope * (row_q - col).astype(F32)
        sc = jnp.where(col <= row_q, sc, NEG)
        m = jnp.max(sc, axis=-1, keepdims=True)
        p = jnp.exp(sc - m)
        mo_ref[...] = m
        lo_ref[...] = jnp.sum(p, axis=-1, keepdims=True)
        acco_ref[...] = _head_fold(_dot(p.astype(BF16), vn_ref[0].astype(BF16)), n_heads, nq, hd)
        m_ref[...] = jnp.full(m_ref.shape, NEG, F32)
        l_ref[...] = jnp.zeros_like(l_ref)
        km_ref[...] = jnp.zeros_like(km_ref)

    @pl.when(s > 0)
    def _():
        key = lax.broadcasted_iota(jnp.int32, (rows, MOBA_BLOCK), 1)
        m_all, l_all, km = m_ref[...], l_ref[...], km_ref[...]
        lane_k = lax.broadcasted_iota(jnp.int32, km.shape, 1)
        blocks = [(s - 1) * (pps // ppb) + bi for bi in range(pps // ppb)]
        kfs = [_page_group(k_refs, bi * ppb, ppb) for bi in range(len(blocks))]
        raw = [_dot(qbd, kf.astype(BF16)) for kf in kfs]
        means = [jnp.sum(kf, axis=-1, keepdims=True) * (1.0 / MOBA_BLOCK) for kf in kfs]
        scs = [sc - slope * (past_len + row_q - (blk * MOBA_BLOCK + key)).astype(F32) for sc, blk in zip(raw, blocks)]
        m_bs = [jnp.max(sc, axis=-1, keepdims=True) for sc in scs]
        ps = [jnp.exp(sc - m_b) for sc, m_b in zip(scs, m_bs)]
        pvs = [_dot_nt(p.astype(BF16), _page_group(v_refs, bi * ppb, ppb).astype(BF16)) for bi, p in enumerate(ps)]
        for blk, mean, m_b, p, pv in zip(blocks, means, m_bs, ps, pvs):
            km = jnp.where(lane_k == blk, mean, km)
            acc_ref[blk] = _head_fold(pv, n_heads, nq, hd)
            m_all = jnp.where(lane == blk, m_b, m_all)
            l_all = jnp.where(lane == blk, jnp.sum(p, axis=-1, keepdims=True), l_all)
        m_ref[...] = m_all
        l_ref[...] = l_all
        km_ref[...] = km

    @pl.when(s == pl.num_programs(1) - 1)
    def _():
        n_blocks = past_len // MOBA_BLOCK
        gate = _dot_mid(q_f32, km_ref[...])
        sel = _topk_lanes(jnp.where(lane < n_blocks, gate, -jnp.inf), lane) > 0.0
        m_all, l_all = m_ref[...], l_ref[...]
        m_o = mo_ref[...]
        m_tot = jnp.maximum(m_o, jnp.max(jnp.where(sel, m_all, NEG), axis=-1, keepdims=True))
        wgt = jnp.where(sel, jnp.exp(m_all - m_tot), 0.0)
        a_o = jnp.exp(m_o - m_tot)
        l_tot = lo_ref[...] * a_o + jnp.sum(wgt * l_all, axis=-1, keepdims=True)
        acc = acco_ref[...] * a_o
        for b in range(n_blocks):
            acc = acc + wgt[:, b:b + 1] * acc_ref[b]
        o_ref[0] = acc / l_tot


def _paged_attention(kind, q, k_new, v_new, k_pool, v_pool, page_table, *, n_heads):
    batch, n_pages = page_table.shape
    page, hd = k_pool.shape[2], k_pool.shape[4]
    nq = q.shape[0] // batch
    rows = n_heads * nq
    pps = PAGES_PER_STEP
    past_len = n_pages * page
    assert n_pages % pps == 0 and past_len % MOBA_BLOCK == 0 and MOBA_BLOCK % page == 0 and pps % (MOBA_BLOCK // page) == 0
    assert nq <= page and past_len // MOBA_BLOCK <= LANES and (2 * LANES) % n_heads == 0
    n_groups = n_pages // pps
    q4 = q.reshape(batch, nq, n_heads, hd).transpose(0, 2, 1, 3)
    q_bd = (q4[:, :, :, None, :] * jnp.eye(n_heads, dtype=F32)[None, :, None, :, None]).reshape(batch, rows, n_heads * hd)
    pad_new = lambda t: jnp.pad(t.reshape(batch, nq, n_heads * hd), ((0, 0), (0, page - nq), (0, 0)))
    k_t, v_t = (jnp.transpose(t, (0, 1, 3, 4, 2)) for t in (k_pool, v_pool))

    if kind == "stick":
        ppg = MOBA_BLOCK // page
        seq = lambda shape: pl.BlockSpec((1,) + shape, lambda b, pt: (b, 0, 0))
        buf = pltpu.VMEM((2, ppg, n_heads, hd, page), F32)
        out = pl.pallas_call(
            functools.partial(_stick_paged_kernel, n_heads=n_heads, nq=nq, hd=hd, n_pages=n_pages),
            grid_spec=pltpu.PrefetchScalarGridSpec(
                num_scalar_prefetch=1,
                grid=(batch,),
                in_specs=[seq((rows, n_heads * hd)), seq((page, n_heads * hd)), seq((page, n_heads * hd)),
                          pl.BlockSpec(memory_space=pl.ANY), pl.BlockSpec(memory_space=pl.ANY)],
                out_specs=seq((rows, hd)),
                scratch_shapes=[buf, buf, pltpu.SemaphoreType.DMA((2, 2, ppg))]),
            out_shape=jax.ShapeDtypeStruct((batch, rows, hd), F32),
            compiler_params=_cparams(("arbitrary",)),
            name="stick_paged",
        )(page_table, q_bd, pad_new(k_new), pad_new(v_new), k_t, v_t)
    else:
        group = lambda s: jnp.maximum(s, 1) - 1
        scratch = [pltpu.VMEM((rows, 1), F32), pltpu.VMEM((rows, 1), F32), pltpu.VMEM((rows, hd), F32),
                   pltpu.VMEM((rows, LANES), F32), pltpu.VMEM((rows, LANES), F32),
                   pltpu.VMEM((past_len // MOBA_BLOCK, rows, hd), F32), pltpu.VMEM((n_heads * hd, LANES), F32)]

        def page_spec(r):
            return pl.BlockSpec((1, 1, n_heads, hd, page), lambda b, s, pt: (0, pt[b, group(s) * pps + r], 0, 0, 0))

        per_seq = lambda shape: pl.BlockSpec((1,) + shape, lambda b, s, pt: (b, 0, 0))
        out = pl.pallas_call(
            functools.partial(_moba_paged_kernel, pps=pps, n_heads=n_heads, nq=nq, hd=hd, past_len=past_len),
            grid_spec=pltpu.PrefetchScalarGridSpec(
                num_scalar_prefetch=1,
                grid=(batch, n_groups + 1),
                in_specs=[per_seq((rows, n_heads * hd)), per_seq((page, n_heads * hd)),
                          per_seq((page, n_heads * hd))] + [page_spec(r) for r in range(pps)] * 2,
                out_specs=per_seq((rows, hd)),
                scratch_shapes=scratch),
            out_shape=jax.ShapeDtypeStruct((batch, rows, hd), F32),
            compiler_params=_cparams(("parallel", "arbitrary")),
            name="moba_paged",
        )(page_table, q_bd, pad_new(k_new), pad_new(v_new), *([k_t] * pps), *([v_t] * pps))
    return out.reshape(batch, n_heads, nq, hd).transpose(0, 2, 1, 3).reshape(batch * nq, n_heads * hd)


def _layer_stack(x, batch, past, states, w):
    s_hgrn, s_gdn, conv_buf = states

    def attend(kind, q, k, v, pools, n_heads):
        if past is None:
            return (_moba if kind == "moba" else _stick)(q, k, v, batch=batch, q_offset=0)
        return _paged_attention(kind, q, k, v, *pools, past[4], n_heads=n_heads)

    (q_a, k_a, v_a, hg), kv_a_t = _project(x, w["in_a"] + [w["in_hgrn"]], batch, transposed=(1, 2))
    o_a = attend("moba", q_a, k_a, v_a, past and past[0:2], H_A)
    o_b, s_hgrn_new = _hgrn(hg, w["hgrn_lb"], w["hgrn_norm"], s_hgrn, batch=batch, layer=0)
    x = _matmul([o_a, o_b], w["out_even"], ln_args=(x, w["ln1_g"][0], w["ln1_b"][0]))
    x = _ffn(x, None, w["ffn_wg"], w["ffn_wu"], w["ffn_wd"], w["ln2_g"][0], w["ln2_b"][0])

    (qkv_c, ab, g_c, q_d, k_d, v_d), kv_d_t = _project(x, [w["in_qkv_c"], w["in_ab"], w["in_g_c"]] + w["in_d"], batch,
                                                       transposed=(4, 5))
    o_c, s_gdn_new, conv_new = _gdn(qkv_c, ab, g_c, w["conv_w"], w["a_log"], w["dt_bias"], w["gdn_norm"],
                                    s_gdn, conv_buf, batch=batch)
    o_d = attend("stick", q_d, k_d, v_d, past and past[2:4], H_D)
    x = _matmul([o_c, o_d], w["out_odd"], ln_args=(x, w["ln1_g"][1], w["ln1_b"][1]))
    comb = _router(x, w["router"])
    x = _moe(x, comb, w["moe_wg"], w["moe_wu"], w["moe_wd"], w["ln2_g"][1], w["ln2_b"][1])
    k_a, v_a = kv_a_t or (k_a, v_a)
    k_d, v_d = kv_d_t or (k_d, v_d)
    return x, k_a, v_a, s_hgrn_new, s_gdn_new, conv_new, k_d, v_d


def kernel(x_prompt, x_sample, cache_k_moba, cache_v_moba, state_hgrn, state_gdn, state_gdn_conv, cache_k_sb,
           cache_v_sb, page_table, w_in_even, w_out_even, hgrn_lb, hgrn_norm, w_in_odd, w_out_odd, gdn_conv_w,
           gdn_a_log, gdn_dt_bias, gdn_norm, ln1_g, ln1_b, ln2_g, ln2_b, ffn_wg, ffn_wu, ffn_wd, router, moe_wg,
           moe_wu, moe_wd):
    assert w_in_even.shape[0] == 1 and w_in_odd.shape[0] == 1
    bp, lp, d = x_prompt.shape
    bs, ls, _ = x_sample.shape
    mix = d // 2
    hd_a, hd_d = mix // H_A, mix // H_D
    conv_dim = gdn_conv_w.shape[-1]
    bf = lambda t: t.astype(BF16)
    cols = lambda wt, lo, n: bf(wt[:, lo:lo + n])

    wie, wio = w_in_even[0], w_in_odd[0]
    g_lo = conv_dim + 2 * H_C
    d_lo = g_lo + mix
    w = {
        "in_a": [cols(wie, i * mix, mix) for i in range(3)],
        "in_hgrn": cols(wie, 3 * mix, 4 * mix),
        "out_even": [bf(w_out_even[0][:mix]), bf(w_out_even[0][mix:])],
        "hgrn_lb": hgrn_lb, "hgrn_norm": hgrn_norm[0],
        "in_qkv_c": cols(wio, 0, conv_dim),
        "in_ab": bf(jnp.pad(wio[:, conv_dim:g_lo], ((0, 0), (0, LANES - 2 * H_C)))),
        "in_g_c": cols(wio, g_lo, mix),
        "in_d": [cols(wio, d_lo + i * mix, mix) for i in range(3)],
        "out_odd": [bf(w_out_odd[0][:mix]), bf(w_out_odd[0][mix:])],
        "conv_w": gdn_conv_w[0], "a_log": gdn_a_log[0], "dt_bias": gdn_dt_bias[0], "gdn_norm": gdn_norm[0],
        "ln1_g": ln1_g, "ln1_b": ln1_b, "ln2_g": ln2_g, "ln2_b": ln2_b,
        "ffn_wg": bf(ffn_wg), "ffn_wu": bf(ffn_wu), "ffn_wd": bf(ffn_wd),
        "router": router[0], "moe_wg": bf(moe_wg[0]), "moe_wu": bf(moe_wu[0]), "moe_wd": bf(moe_wd[0]),
    }

    def run(x, batch, past, states):
        seq = x.shape[1]
        y, k_a, v_a, s_h, s_g, cv, k_d, v_d = _layer_stack(x.reshape(batch * seq, d), batch, past, states, w)
        def cache(t, n_heads, hd):
            if t.ndim == 3:
                return t.reshape(batch, n_heads, hd, seq).transpose(0, 3, 1, 2)[None]
            return t.reshape(1, batch, seq, n_heads, hd)

        return (y.reshape(batch, seq, d), cache(k_a, H_A, hd_a), cache(v_a, H_A, hd_a), s_h[None], s_g[None], cv[None],
                cache(k_d, H_D, hd_d), cache(v_d, H_D, hd_d))

    past = (cache_k_moba, cache_v_moba, cache_k_sb, cache_v_sb, page_table)
    out_s = run(x_sample, bs, past, (state_hgrn[0], state_gdn[0], state_gdn_conv[0]))
    zeros_p = (jnp.zeros((bp,) + state_hgrn.shape[2:], F32), jnp.zeros((bp,) + state_gdn.shape[2:], F32),
               jnp.zeros((bp,) + state_gdn_conv.shape[2:], F32))
    out_p = run(x_prompt, bp, None, zeros_p)
    return (out_p[0], out_s[0]) + out_p[1:] + out_s[1:]
```
